```python
import math
import jax, jax.numpy as jnp
from jax import lax
import numpy as np

D_MODEL = 1024
BATCH = 2
SEQ = 16384
DEPTH = 4

BRANCH_WIDTH = 512
N_BRANCHES = 3
HEAD_DIM = 64
NSA_HEADS = BRANCH_WIDTH // HEAD_DIM
NSA_KV_GROUPS = 2
NSA_GROUP = NSA_HEADS // NSA_KV_GROUPS
NSA_WIDTH = NSA_HEADS * HEAD_DIM
CMP_LEN = 32
CMP_STRIDE = 16
SLC_BLOCK = 64
SLC_TOPN = 16
WINDOW = 512
CMP_HIDDEN = 256
Q_BLOCK = 128
CONV_WIDTH = BRANCH_WIDTH
CONV_K = 3
RET_HEADS = 4
RET_HEAD_DIM = BRANCH_WIDTH // RET_HEADS
RET_WIDTH = RET_HEADS * RET_HEAD_DIM
RET_CHUNK = 128
ROPE_BASE = 10000.0
D_FF = 3584
N_EXPERTS = 8
TOP_K = 2
N_DENSE = (DEPTH + 1) // 2
N_MOE = DEPTH // 2
LN_EPS = 1e-5
ALPHA = (2.0 * DEPTH) ** 0.25
BETA = (8.0 * DEPTH) ** -0.25
SPLIT_SIZES = (NSA_WIDTH, 6 * NSA_KV_GROUPS * HEAD_DIM, 3 * NSA_HEADS,
               CONV_WIDTH, CONV_WIDTH, CONV_WIDTH,
               RET_WIDTH, RET_WIDTH, RET_WIDTH, RET_WIDTH,
               N_BRANCHES * D_MODEL)
IN_COLS = sum(SPLIT_SIZES)

kernel_name = "hybrid_nsa_conv_retention_moe_deepnorm"


def split_points():
    pts, acc = [], 0
    for s in SPLIT_SIZES[:-1]:
        acc += s
        pts.append(acc)
    return pts


def layer_norm(x, g, b):
    xf = x.astype(jnp.float32)
    mu = xf.mean(-1, keepdims=True)
    var = jnp.square(xf - mu).mean(-1, keepdims=True)
    return ((xf - mu) * lax.rsqrt(var + LN_EPS)).astype(x.dtype) * g + b


def masked_softmax(s, mask):
    s = jnp.where(mask, s.astype(jnp.float32), -jnp.inf)
    m = jnp.max(s, axis=-1, keepdims=True)
    m = jnp.where(jnp.isfinite(m), m, 0.0)
    p = jnp.exp(s - m)
    return p / jnp.maximum(p.sum(-1, keepdims=True), 1e-30)


def nsa_compress(kv, pos, w1, b1, w2, b2):
    b, t, g, dh = kv.shape
    chunks = kv.reshape(b, t // CMP_STRIDE, CMP_STRIDE, g, dh)
    blocks = jnp.concatenate([chunks[:, :-1], chunks[:, 1:]], axis=2) + pos[:, None, :]
    nc = blocks.shape[1]
    flat = blocks.transpose(0, 1, 3, 2, 4).reshape(b, nc, g, CMP_LEN * dh)
    hid = jax.nn.gelu(flat @ w1 + b1)
    return hid @ w2 + b2


def nsa_attention(q, k_cmp, v_cmp, k_slc, v_slc, k_win, v_win, gates):
    b, t, h, dh = q.shape
    G, R = NSA_KV_GROUPS, NSA_GROUP
    qg = q.reshape(b, t, G, R, dh).transpose(0, 2, 3, 1, 4) * (dh ** -0.5)
    kc = k_cmp.transpose(0, 2, 1, 3)
    vc = v_cmp.transpose(0, 2, 1, 3)
    nc = kc.shape[2]
    n_sel = t // SLC_BLOCK
    topn = min(SLC_TOPN, n_sel)
    ratio = SLC_BLOCK // CMP_STRIDE
    ks = k_slc.transpose(0, 2, 1, 3).reshape(b, G, n_sel, SLC_BLOCK, dh)
    vs = v_slc.transpose(0, 2, 1, 3).reshape(b, G, n_sel, SLC_BLOCK, dh)
    pad_w = ((0, 0), (0, 0), (WINDOW, 0), (0, 0))
    kw = jnp.pad(k_win.transpose(0, 2, 1, 3), pad_w)
    vw = jnp.pad(v_win.transpose(0, 2, 1, 3), pad_w)
    cmp_end = jnp.arange(nc) * CMP_STRIDE + CMP_LEN - 1
    right_pad = ratio * (n_sel + 1) - 1 - nc
    b_idx = jnp.arange(b)[:, None, None, None]
    g_idx = jnp.arange(G)[None, :, None, None]
    j_blk = jnp.arange(n_sel)

    def block_fn(qb):
        q0 = qb * Q_BLOCK
        tq = q0 + jnp.arange(Q_BLOCK)
        qq = lax.dynamic_slice_in_dim(qg, q0, Q_BLOCK, axis=3)
        s_c = jnp.einsum('bgrqd,bgnd->bgrqn', qq, kc)
        p_c = masked_softmax(s_c, cmp_end[None, :] <= tq[:, None])
        o_c = jnp.einsum('bgrqn,bgnd->bgrqd', p_c.astype(vc.dtype), vc)
        p_grp = p_c.sum(axis=2)
        p_pad = jnp.pad(p_grp, ((0, 0), (0, 0), (0, 0), (1, right_pad)))
        P = p_pad.reshape(b, G, Q_BLOCK, n_sel + 1, ratio)
        score = (P[..., :n_sel, 0] + 2.0 * (P[..., :n_sel, 1] + P[..., :n_sel, 2] + P[..., :n_sel, 3])
                 + P[..., 1:, 0])
        cur = tq // SLC_BLOCK
        valid = j_blk[None, :] <= cur[:, None]
        forced = (j_blk[None, :] == 0) | (j_blk[None, :] == cur[:, None]) | (j_blk[None, :] == cur[:, None] - 1)
        score = jnp.where(forced, 1e9, score)
        score = jnp.where(valid, score, -1.0)
        _, idx = lax.top_k(score, topn)
        k_sel = ks[b_idx, g_idx, idx]
        v_sel = vs[b_idx, g_idx, idx]
        s_s = jnp.einsum('bgrqd,bgqnkd->bgrqnk', qq, k_sel).reshape(b, G, R, Q_BLOCK, -1)
        kpos_s = idx[..., None] * SLC_BLOCK + jnp.arange(SLC_BLOCK)
        m_s = (kpos_s <= tq[None, None, :, None, None]).reshape(b, G, 1, Q_BLOCK, -1)
        p_s = masked_softmax(s_s, m_s).reshape(b, G, R, Q_BLOCK, topn, SLC_BLOCK)
        o_s = jnp.einsum('bgrqnk,bgqnkd->bgrqd', p_s.astype(v_sel.dtype), v_sel)
        kwb = lax.dynamic_slice_in_dim(kw, q0, Q_BLOCK + WINDOW, axis=2)
        vwb = lax.dynamic_slice_in_dim(vw, q0, Q_BLOCK + WINDOW, axis=2)
        kpos_w = q0 - WINDOW + jnp.arange(Q_BLOCK + WINDOW)
        m_w = ((kpos_w[None, :] <= tq[:, None]) & (kpos_w[None, :] > tq[:, None] - WINDOW)
               & (kpos_w[None, :] >= 0))
        s_w = jnp.einsum('bgrqd,bgkd->bgrqk', qq, kwb)
        p_w = masked_softmax(s_w, m_w)
        o_w = jnp.einsum('bgrqk,bgkd->bgrqd', p_w.astype(vwb.dtype), vwb)
        return o_c, o_s, o_w

    o_c, o_s, o_w = lax.map(block_fn, jnp.arange(t // Q_BLOCK))

    def to_bthd(o):
        return o.transpose(1, 0, 4, 2, 3, 5).reshape(b, t, h, dh)

    y = gates[..., 0:1] * to_bthd(o_c) + gates[..., 1:2] * to_bthd(o_s) + gates[..., 2:3] * to_bthd(o_w)
    return y.reshape(b, t, h * dh)


def short_conv(bx, cx, hx, w_conv):
    u = cx * hx
    y = lax.conv_general_dilated(u, w_conv[:, None, :], window_strides=(1,), padding=((CONV_K - 1, 0),),
                                 dimension_numbers=('NWC', 'WIO', 'NWC'), feature_group_count=CONV_WIDTH)
    return bx * y


def rotary(x, pos):
    d = x.shape[-1]
    inv = ROPE_BASE ** (-jnp.arange(0, d, 2, dtype=jnp.float32) / d)
    ang = pos.astype(jnp.float32)[:, None] * inv[None, :]
    cos = jnp.cos(ang)[None, :, None, :].astype(x.dtype)
    sin = jnp.sin(ang)[None, :, None, :].astype(x.dtype)
    x1, x2 = x[..., : d // 2], x[..., d // 2:]
    return jnp.concatenate([x1 * cos - x2 * sin, x1 * sin + x2 * cos], axis=-1)


def retention(q, k, v):
    b, t, h, d = q.shape
    C = RET_CHUNK
    n = t // C
    log_g = jnp.log(1.0 - 2.0 ** (-5.0 - jnp.arange(h, dtype=jnp.float32)))
    idx = jnp.arange(C, dtype=jnp.float32)
    diff = idx[:, None] - idx[None, :]
    decay_in = jnp.where(diff >= 0, jnp.exp(log_g[:, None, None] * jnp.maximum(diff, 0.0)), 0.0)
    q_dec = jnp.exp(log_g[:, None] * (idx + 1.0))[None, :, :, None]
    k_dec = jnp.exp(log_g[:, None] * (C - 1.0 - idx))[None, :, :, None]
    chunk_dec = jnp.exp(log_g * C)[None, :, None, None]

    def chunks(a):
        return a.astype(jnp.float32).reshape(b, n, C, h, d).transpose(1, 0, 3, 2, 4)

    def step(state, inp):
        qi, ki, vi = inp
        inner = jnp.einsum('bhqd,bhkd->bhqk', qi, ki) * decay_in
        o = jnp.einsum('bhqk,bhkd->bhqd', inner, vi) + jnp.einsum('bhqd,bhde->bhqe', qi, state) * q_dec
        state = state * chunk_dec + jnp.einsum('bhkd,bhke->bhde', ki * k_dec, vi)
        return state, o

    state0 = jnp.zeros((b, h, d, d), jnp.float32)
    _, o = lax.scan(step, state0, (chunks(q), chunks(k), chunks(v)))
    return o.transpose(1, 0, 3, 2, 4).reshape(b, t, h, d).astype(q.dtype)


def group_norm(o, g, bias):
    of = o.astype(jnp.float32)
    mu = of.mean(-1, keepdims=True)
    var = jnp.square(of - mu).mean(-1, keepdims=True)
    y = ((of - mu) * lax.rsqrt(var + LN_EPS)).astype(o.dtype)
    return y.reshape(o.shape[0], o.shape[1], -1) * g + bias


def token_mixer(x, w_in, cmp_pos, cmp_w1, cmp_b1, cmp_w2, cmp_b2, conv_w, gn_g, gn_b, w_br, w_out):
    b, t, _ = x.shape
    hcat = x @ w_in
    (q, kv, ngate, cb, cc, ch, rq, rk, rv, rg, mgate) = jnp.split(hcat, split_points(), axis=-1)
    q = q.reshape(b, t, NSA_HEADS, HEAD_DIM)
    kv = kv.reshape(b, t, 3, 2, NSA_KV_GROUPS, HEAD_DIM)
    k_cmp = nsa_compress(kv[:, :, 0, 0], cmp_pos[0], cmp_w1[0], cmp_b1[0], cmp_w2[0], cmp_b2[0])
    v_cmp = nsa_compress(kv[:, :, 0, 1], cmp_pos[1], cmp_w1[1], cmp_b1[1], cmp_w2[1], cmp_b2[1])
    nsa_g = jax.nn.sigmoid(ngate).reshape(b, t, NSA_HEADS, 3)
    y_a = nsa_attention(q, k_cmp, v_cmp, kv[:, :, 1, 0], kv[:, :, 1, 1], kv[:, :, 2, 0], kv[:, :, 2, 1], nsa_g)
    y_b = short_conv(cb, cc, ch, conv_w)
    pos = jnp.arange(t)
    rq = rotary(rq.reshape(b, t, RET_HEADS, RET_HEAD_DIM), pos)
    rk = rotary(rk.reshape(b, t, RET_HEADS, RET_HEAD_DIM), pos) * (RET_HEAD_DIM ** -0.5)
    o_r = retention(rq, rk, rv.reshape(b, t, RET_HEADS, RET_HEAD_DIM))
    y_c = jax.nn.silu(rg) * group_norm(o_r, gn_g, gn_b)
    ys = jnp.stack([y_a, y_b, y_c], axis=2)
    proj = jnp.einsum('btcw,cwd->btcd', ys, w_br)
    gates = jax.nn.sigmoid(mgate).reshape(b, t, N_BRANCHES, D_MODEL)
    return jnp.sum(gates * proj, axis=2) @ w_out


def swiglu(x, w_gate, w_up, w_down):
    return (jax.nn.silu(x @ w_gate) * (x @ w_up)) @ w_down


def moe_swiglu(x, w_router, w_gate, w_up, w_down):
    logits = (x @ w_router).astype(jnp.float32)
    top_v, top_i = lax.top_k(logits, TOP_K)
    probs = jax.nn.softmax(top_v, axis=-1)
    combine = jnp.sum(jax.nn.one_hot(top_i, N_EXPERTS, dtype=jnp.float32) * probs[..., None], axis=-2)
    combine = combine.astype(x.dtype)
    out = jnp.zeros_like(x)
    for e in range(N_EXPERTS):
        out = out + combine[..., e:e + 1] * swiglu(x, w_gate[e], w_up[e], w_down[e])
    return out


def setup_inputs(seed: int = 0) -> dict:
    key = jax.random.key(seed)
    ks = jax.random.split(key, 24)
    f32 = jnp.float32

    def nrm(k, shape, scale):
        return jax.random.normal(k, shape, f32) * scale

    L = DEPTH
    return {
        "x": nrm(ks[0], (BATCH, SEQ, D_MODEL), 1.0),
        "w_in": nrm(ks[1], (L, D_MODEL, IN_COLS), D_MODEL ** -0.5),
        "cmp_pos": nrm(ks[2], (L, 2, CMP_LEN, HEAD_DIM), 0.1),
        "cmp_w1": nrm(ks[3], (L, 2, CMP_LEN * HEAD_DIM, CMP_HIDDEN), (CMP_LEN * HEAD_DIM) ** -0.5),
        "cmp_b1": nrm(ks[4], (L, 2, CMP_HIDDEN), 0.01),
        "cmp_w2": nrm(ks[5], (L, 2, CMP_HIDDEN, HEAD_DIM), CMP_HIDDEN ** -0.5),
        "cmp_b2": nrm(ks[6], (L, 2, HEAD_DIM), 0.01),
        "conv_w": nrm(ks[7], (L, CONV_K, CONV_WIDTH), CONV_K ** -0.5),
        "ret_gn_g": 1.0 + nrm(ks[8], (L, RET_WIDTH), 0.01),
        "ret_gn_b": nrm(ks[9], (L, RET_WIDTH), 0.01),
        "w_br": nrm(ks[10], (L, N_BRANCHES, BRANCH_WIDTH, D_MODEL), BETA * BRANCH_WIDTH ** -0.5),
        "w_out": nrm(ks[11], (L, D_MODEL, D_MODEL), BETA * D_MODEL ** -0.5),
        "ln1_g": 1.0 + nrm(ks[12], (L, D_MODEL), 0.01),
        "ln1_b": nrm(ks[13], (L, D_MODEL), 0.01),
        "ln2_g": 1.0 + nrm(ks[14], (L, D_MODEL), 0.01),
        "ln2_b": nrm(ks[15], (L, D_MODEL), 0.01),
        "ffn_gate": nrm(ks[16], (N_DENSE, D_MODEL, D_FF), D_MODEL ** -0.5),
        "ffn_up": nrm(ks[17], (N_DENSE, D_MODEL, D_FF), D_MODEL ** -0.5),
        "ffn_down": nrm(ks[18], (N_DENSE, D_FF, D_MODEL), BETA * D_FF ** -0.5),
        "moe_router": nrm(ks[19], (N_MOE, D_MODEL, N_EXPERTS), D_MODEL ** -0.5),
        "moe_gate": nrm(ks[20], (N_MOE, N_EXPERTS, D_MODEL, D_FF), D_MODEL ** -0.5),
        "moe_up": nrm(ks[21], (N_MOE, N_EXPERTS, D_MODEL, D_FF), D_MODEL ** -0.5),
        "moe_down": nrm(ks[22], (N_MOE, N_EXPERTS, D_FF, D_MODEL), BETA * D_FF ** -0.5),
    }


def reference(x, w_in, cmp_pos, cmp_w1, cmp_b1, cmp_w2, cmp_b2, conv_w, ret_gn_g, ret_gn_b, w_br, w_out,
              ln1_g, ln1_b, ln2_g, ln2_b, ffn_gate, ffn_up, ffn_down, moe_router, moe_gate, moe_up, moe_down):
    for l in range(DEPTH):
        m = token_mixer(x, w_in[l], cmp_pos[l], cmp_w1[l], cmp_b1[l], cmp_w2[l], cmp_b2[l], conv_w[l],
                        ret_gn_g[l], ret_gn_b[l], w_br[l], w_out[l])
        x = layer_norm(ALPHA * x + m, ln1_g[l], ln1_b[l])
        if l % 2 == 0:
            f = swiglu(x, ffn_gate[l // 2], ffn_up[l // 2], ffn_down[l // 2])
        else:
            f = moe_swiglu(x, moe_router[l // 2], moe_gate[l // 2], moe_up[l // 2], moe_down[l // 2])
        x = layer_norm(ALPHA * x + f, ln2_g[l], ln2_b[l])
    return x
```

```python
import functools
import math

import jax
import jax.numpy as jnp
from jax import lax
from jax.experimental import pallas as pl
from jax.experimental.pallas import tpu as pltpu

F32 = jnp.float32
BF16 = jnp.bfloat16

D_MODEL = 1024
DEPTH = 4
BRANCH_WIDTH = 512
HEAD_DIM = 64
NSA_HEADS = 8
NSA_KV_GROUPS = 2
NSA_GROUP = 4
CMP_LEN = 32
CMP_STRIDE = 16
SLC_BLOCK = 64
SLC_TOPN = 16
WINDOW = 512
CMP_HIDDEN = 256
Q_BLOCK = 128
CONV_K = 3
RET_HEADS = 4
RET_HEAD_DIM = 128
RET_CHUNK = 128
ROPE_BASE = 10000.0
D_FF = 3584
N_EXPERTS = 8
TOP_K = 2
LN_EPS = 1e-5
ALPHA = (2.0 * DEPTH) ** 0.25

NAT_MG = 0
NAT_CB = 3072
NAT_CC = 3584
NAT_CH = 4096
NAT_RQ = 4608
NAT_RK = 5120
NAT_RV = 5632
NAT_RG = 6144
NAT_KP = 6656
NAT_KC = 6912
NAT_VC = 7040
NAT_COLS = 7168
TR_Q = 0
TR_VS = 512
TR_VW = 640
TR_NG = 768
TR_ROWS = 800

NEG = -1e30
SEL_TK = 512
VMEM_LIMIT = 48 * 1024 * 1024


def _cparams(n_axes, vmem=VMEM_LIMIT):
    return pltpu.CompilerParams(dimension_semantics=("arbitrary",) * n_axes, vmem_limit_bytes=vmem)


def _mm_nat_kernel(x_ref, w_ref, o_ref):
    o_ref[...] = jnp.dot(x_ref[...].astype(BF16), w_ref[...], preferred_element_type=F32).astype(o_ref.dtype)


def mm_nat(x, w, tm, tn, out_dtype=F32):
    m, k = x.shape
    n = w.shape[1]
    return pl.pallas_call(
        _mm_nat_kernel,
        grid=(m // tm, n // tn),
        in_specs=[pl.BlockSpec((tm, k), lambda i, j: (i, 0)), pl.BlockSpec((k, tn), lambda i, j: (0, j))],
        out_specs=pl.BlockSpec((tm, tn), lambda i, j: (i, j)),
        out_shape=jax.ShapeDtypeStruct((m, n), out_dtype),
        compiler_params=_cparams(2),
        name="mm_nat",
    )(x, w)


def _mm_nt_kernel(w_ref, x_ref, o_ref):
    o_ref[0] = lax.dot_general(w_ref[...], x_ref[0].astype(BF16), (((1,), (1,)), ((), ())),
                               preferred_element_type=F32)


def mm_nt(w_t, x, tm):
    b, t, k = x.shape
    r = w_t.shape[0]
    return pl.pallas_call(
        _mm_nt_kernel,
        grid=(b, t // tm),
        in_specs=[pl.BlockSpec((r, k), lambda i, j: (0, 0)), pl.BlockSpec((1, tm, k), lambda i, j: (i, j, 0))],
        out_specs=pl.BlockSpec((1, r, tm), lambda i, j: (i, 0, j)),
        out_shape=jax.ShapeDtypeStruct((b, r, t), F32),
        compiler_params=_cparams(2),
        name="mm_nt",
    )(w_t, x)


def _cmp_kernel(c_ref, pos_ref, w1_ref, b1_ref, w2_ref, w2t_ref, b2_ref, b2t_ref, on_ref, ot_ref):
    half = CMP_STRIDE * HEAD_DIM
    c = c_ref[0, 0, 0].astype(BF16)
    w1 = w1_ref[0]
    a = jnp.dot(c, w1[:half], preferred_element_type=F32)
    bm = jnp.dot(c, w1[half:], preferred_element_type=F32)
    nch = a.shape[0]
    bm = pltpu.roll(bm, nch - 1, 0)
    posb = jnp.dot(pos_ref[0].astype(BF16), w1, preferred_element_type=F32)[0:1]
    hid = jax.nn.gelu(a + bm + posb + b1_ref[0])
    hb = hid.astype(BF16)
    on_ref[0, 0, 0] = jnp.dot(hb, w2_ref[0], preferred_element_type=F32) + b2_ref[0]
    ot_ref[0, 0, 0] = lax.dot_general(w2t_ref[0], hb, (((1,), (1,)), ((), ())),
                                      preferred_element_type=F32) + b2t_ref[0]


def nsa_compress_kv(chunks, pos, w1, b1, w2, b2):
    _, b, g, nch, cw = chunks.shape
    pos_flat = jnp.broadcast_to(pos.reshape(2, 1, CMP_LEN * HEAD_DIM), (2, 8, CMP_LEN * HEAD_DIM))
    w1b = w1.astype(BF16)
    w2b = w2.astype(BF16)
    w2t = jnp.swapaxes(w2, 1, 2).astype(BF16)
    b1r = b1.reshape(2, 1, CMP_HIDDEN)
    b2r = b2.reshape(2, 1, HEAD_DIM)
    b2t = b2.reshape(2, HEAD_DIM, 1)
    sel = lambda kv, i, j: (kv, 0, 0)
    return pl.pallas_call(
        _cmp_kernel,
        grid=(2, b, g),
        in_specs=[
            pl.BlockSpec((1, 1, 1, nch, cw), lambda kv, i, j: (kv, i, j, 0, 0)),
            pl.BlockSpec((1, 8, CMP_LEN * HEAD_DIM), sel),
            pl.BlockSpec((1, CMP_LEN * HEAD_DIM, CMP_HIDDEN), sel),
            pl.BlockSpec((1, 1, CMP_HIDDEN), sel),
            pl.BlockSpec((1, CMP_HIDDEN, HEAD_DIM), sel),
            pl.BlockSpec((1, HEAD_DIM, CMP_HIDDEN), sel),
            pl.BlockSpec((1, 1, HEAD_DIM), sel),
            pl.BlockSpec((1, HEAD_DIM, 1), sel),
        ],
        out_specs=[
            pl.BlockSpec((1, 1, 1, nch, HEAD_DIM), lambda kv, i, j: (kv, i, j, 0, 0)),
            pl.BlockSpec((1, 1, 1, HEAD_DIM, nch), lambda kv, i, j: (kv, i, j, 0, 0)),
        ],
        out_shape=[
            jax.ShapeDtypeStruct((2, b, g, nch, HEAD_DIM), F32),
            jax.ShapeDtypeStruct((2, b, g, HEAD_DIM, nch), F32),
        ],
        compiler_params=_cparams(3),
        name="nsa_compress",
    )(chunks, pos_flat, w1b, b1r, w2b, w2t, b2r, b2t)


def _online_step(s, vt, m, l, acc):
    m_new = jnp.maximum(m, jnp.max(s, axis=0, keepdims=True))
    alpha = jnp.exp(m - m_new)
    p = jnp.exp(s - m_new)
    l_new = alpha * l + jnp.sum(p, axis=0, keepdims=True)
    acc_new = alpha * acc + jnp.dot(vt, p.astype(BF16), preferred_element_type=F32)
    return m_new, l_new, acc_new


def _nsa_kernel(qt_ref, ng_ref, kc_ref, vct_ref, kp_ref, vst_ref, vwt_ref, wsel_ref, o_ref, bias_ref):
    nq = Q_BLOCK
    ncol = NSA_GROUP * nq
    qb = pl.program_id(2)
    q0 = qb * nq
    n_sel = bias_ref.shape[0]
    ncp = kc_ref.shape[2]

    qt = qt_ref[0] * (HEAD_DIM ** -0.5)
    qs = jnp.concatenate([qt[r * HEAD_DIM:(r + 1) * HEAD_DIM, :] for r in range(NSA_GROUP)], axis=1)
    zq = jnp.zeros_like(qs)
    q_c = qs.astype(BF16)
    q_slc = jnp.concatenate([qs, zq], axis=0).astype(BF16)
    q_win = jnp.concatenate([zq, qs], axis=0).astype(BF16)
    tq = q0 + (lax.broadcasted_iota(jnp.int32, (1, ncol), 1) & (nq - 1))

    s_c = jnp.dot(kc_ref[0, 0], q_c, preferred_element_type=F32)
    n_io = lax.broadcasted_iota(jnp.int32, (ncp, ncol), 0)
    mask_c = (n_io * CMP_STRIDE + (CMP_LEN - 1)) <= tq
    m_c = jnp.max(jnp.where(mask_c, s_c, NEG), axis=0, keepdims=True)
    m_c = jnp.where(m_c > 0.5 * NEG, m_c, 0.0)
    e_c = jnp.where(mask_c, jnp.exp(s_c - m_c), 0.0)
    l_c = jnp.sum(e_c, axis=0, keepdims=True)
    p_c = e_c / jnp.maximum(l_c, 1e-30)
    o_c = jnp.dot(vct_ref[0, 0], p_c.astype(BF16), preferred_element_type=F32)

    p_grp = p_c[:, 0:nq]
    for r in range(1, NSA_GROUP):
        p_grp = p_grp + p_c[:, r * nq:(r + 1) * nq]
    p_hi = p_grp.astype(BF16)
    res = p_grp - p_hi.astype(F32)
    p_mid = res.astype(BF16)
    p_lo = (res - p_mid.astype(F32)).astype(BF16)
    wsel = wsel_ref[...]
    score = (jnp.dot(wsel, p_hi, preferred_element_type=F32)
             + jnp.dot(wsel, p_mid, preferred_element_type=F32)
             + jnp.dot(wsel, p_lo, preferred_element_type=F32))
    j_io = lax.broadcasted_iota(jnp.int32, (n_sel, nq), 0).astype(F32)
    tq1 = q0 + lax.broadcasted_iota(jnp.int32, (1, nq), 1)
    cur = jnp.right_shift(tq1, 6).astype(F32)
    forced = (j_io == 0) | (j_io == cur) | (j_io == cur - 1)
    score = jnp.where(forced, 1e9, score)
    score = jnp.where(j_io <= cur, score, -1.0)
    bias = jnp.full((n_sel, nq), NEG, F32)
    for _ in range(SLC_TOPN):
        mx = jnp.max(score, axis=0, keepdims=True)
        first = jnp.min(jnp.where(score == mx, j_io, float(n_sel)), axis=0, keepdims=True)
        hit = j_io == first
        bias = jnp.where(hit, 0.0, bias)
        score = jnp.where(hit, -3e38, score)
    bias_ref[...] = bias

    blocks_per_tile = SEL_TK // SLC_BLOCK

    def sel_scores(j):
        k0 = pl.multiple_of(j * SEL_TK, SEL_TK)
        kt = kp_ref[0, pl.ds(k0, SEL_TK), :]
        s = jnp.dot(kt, q_slc, preferred_element_type=F32)
        b8 = bias_ref[pl.ds(pl.multiple_of(j * blocks_per_tile, blocks_per_tile), blocks_per_tile), :]
        bt = jnp.concatenate([jnp.broadcast_to(b8[c:c + 1, :], (SLC_BLOCK, nq)) for c in range(blocks_per_tile)],
                             axis=0)
        s = s + jnp.concatenate([bt] * NSA_GROUP, axis=1)
        vt = vst_ref[0, :, pl.ds(k0, SEL_TK)]
        return s, vt, k0

    def sel_body(j, carry):
        s, vt, _ = sel_scores(j)
        return _online_step(s, vt, *carry)

    init = (jnp.full((1, ncol), NEG, F32), jnp.zeros((1, ncol), F32), jnp.zeros((HEAD_DIM, ncol), F32))
    n_full = q0 // SEL_TK
    carry = lax.fori_loop(0, n_full, sel_body, init)
    s, vt, k0 = sel_scores(n_full)
    kpos = k0 + lax.broadcasted_iota(jnp.int32, (SEL_TK, ncol), 0)
    s = jnp.where(kpos <= tq, s, NEG)
    _, l_s, acc_s = _online_step(s, vt, *carry)
    o_s = acc_s / l_s

    carry = init
    n_wt = WINDOW // nq + 1
    for i in range(n_wt - 1, -1, -1):
        start = q0 - WINDOW + i * nq
        ok = start >= 0
        st = pl.multiple_of(jnp.maximum(start, 0), nq)
        kt = kp_ref[0, pl.ds(st, nq), :]
        s = jnp.dot(kt, q_win, preferred_element_type=F32)
        kpos = start + lax.broadcasted_iota(jnp.int32, (nq, ncol), 0)
        if i == n_wt - 1:
            s = jnp.where(kpos <= tq, s, NEG)
        elif i == 0:
            s = jnp.where((kpos > tq - WINDOW) & ok, s, NEG)
        else:
            s = jnp.where(ok, s, NEG)
        vt = vwt_ref[0, :, pl.ds(st, nq)]
        carry = _online_step(s, vt, *carry)
    _, l_w, acc_w = carry
    o_w = acc_w / l_w

    gts = jax.nn.sigmoid(ng_ref[0])
    rows = []
    for r in range(NSA_GROUP):
        sl = slice(r * nq, (r + 1) * nq)
        rows.append(gts[3 * r:3 * r + 1, :] * o_c[:, sl] + gts[3 * r + 1:3 * r + 2, :] * o_s[:, sl]
                    + gts[3 * r + 2:3 * r + 3, :] * o_w[:, sl])
    y_t = jnp.concatenate(rows, axis=0)
    o_ref[0] = y_t.T


def nsa_attention(proj_t, kc, vct, kpack, vt_pack, wsel):
    b, _, t = proj_t.shape
    ncp = kc.shape[2]
    n_sel = t // SLC_BLOCK
    gq = NSA_GROUP * HEAD_DIM
    return pl.pallas_call(
        _nsa_kernel,
        grid=(b, NSA_KV_GROUPS, t // Q_BLOCK),
        in_specs=[
            pl.BlockSpec((1, gq, Q_BLOCK), lambda i, g, q: (i, g, q)),
            pl.BlockSpec((1, 16, Q_BLOCK), lambda i, g, q: (i, TR_NG // 16 + g, q)),
            pl.BlockSpec((1, 1, ncp, HEAD_DIM), lambda i, g, q: (i, g, 0, 0)),
            pl.BlockSpec((1, 1, HEAD_DIM, ncp), lambda i, g, q: (i, g, 0, 0)),
            pl.BlockSpec((1, t, 2 * HEAD_DIM), lambda i, g, q: (i, 0, g)),
            pl.BlockSpec((1, HEAD_DIM, t), lambda i, g, q: (i, g, 0)),
            pl.BlockSpec((1, HEAD_DIM, t), lambda i, g, q: (i, NSA_KV_GROUPS + g, 0)),
            pl.BlockSpec((n_sel, ncp), lambda i, g, q: (0, 0)),
        ],
        out_specs=pl.BlockSpec((1, Q_BLOCK, gq), lambda i, g, q: (i, q, g)),
        out_shape=jax.ShapeDtypeStruct((b, t, NSA_HEADS * HEAD_DIM), F32),
        scratch_shapes=[pltpu.VMEM((n_sel, Q_BLOCK), F32)],
        compiler_params=_cparams(3),
        name="nsa_attention",
    )(proj_t, proj_t, kc, vct, kpack, vt_pack, vt_pack, wsel)


def _selection_weights(n_sel, ncp):
    j = jnp.arange(n_sel)[:, None]
    n = jnp.arange(ncp)[None, :]
    d = n - 4 * j
    w = jnp.where((d == -1) | (d == 3), 1.0, 0.0) + jnp.where((d >= 0) & (d <= 2), 2.0, 0.0)
    w = jnp.where(n < ncp - 1, w, 0.0)
    return w.astype(BF16)


def _conv_kernel(cb_ref, cc_ref, ch_ref, pc_ref, ph_ref, w_ref, o_ref):
    i = pl.program_id(1)
    u = cc_ref[0] * ch_ref[0]
    prev = pc_ref[0] * ph_ref[0]
    prev = jnp.where(i > 0, prev, 0.0)
    tm = u.shape[0]
    row = lax.broadcasted_iota(jnp.int32, u.shape, 0)
    p1 = prev[7:8, :]
    p2 = prev[6:7, :]
    u1 = jnp.where(row == 0, p1, pltpu.roll(u, 1, 0))
    u2 = jnp.where(row == 0, p2, jnp.where(row == 1, p1, pltpu.roll(u, 2, 0)))
    w = w_ref[...]
    o_ref[0] = cb_ref[0] * (w[0:1, :] * u2 + w[1:2, :] * u1 + w[2:3, :] * u)


def short_conv(hn, conv_w, tm):
    b, t, _ = hn.shape
    w = BRANCH_WIDTH
    col = lambda c: (lambda i, j: (i, j, c // w))
    hal = lambda c: (lambda i, j: (i, jnp.maximum(j * (tm // 8) - 1, 0), c // w))
    wpad = jnp.zeros((8, w), F32).at[:CONV_K].set(conv_w)
    return pl.pallas_call(
        _conv_kernel,
        grid=(b, t // tm),
        in_specs=[
            pl.BlockSpec((1, tm, w), col(NAT_CB)),
            pl.BlockSpec((1, tm, w), col(NAT_CC)),
            pl.BlockSpec((1, tm, w), col(NAT_CH)),
            pl.BlockSpec((1, 8, w), hal(NAT_CC)),
            pl.BlockSpec((1, 8, w), hal(NAT_CH)),
            pl.BlockSpec((8, w), lambda i, j: (0, 0)),
        ],
        out_specs=pl.BlockSpec((1, tm, w), lambda i, j: (i, j, 0)),
        out_shape=jax.ShapeDtypeStruct((b, t, w), F32),
        compiler_params=_cparams(2),
        name="short_conv",
    )(hn, hn, hn, hn, hn, wpad)


def _ret_kernel(rq_ref, rk_ref, rv_ref, rg_ref, cos_ref, sin_ref, dec_ref, qd_ref, kd_ref, cd_ref,
                gg_ref, gb_ref, o_ref, st_ref):
    c = RET_CHUNK
    hd = RET_HEAD_DIM

    @pl.when(pl.program_id(1) == 0)
    def _():
        st_ref[...] = jnp.zeros_like(st_ref)

    n_chunks = rq_ref.shape[1] // c

    def chunk(ci, _):
        r0 = pl.multiple_of(ci * c, c)
        cos = cos_ref[pl.ds(r0, c), :]
        sin = sin_ref[pl.ds(r0, c), :]
        for h in range(RET_HEADS):
            hs = slice(h * hd, (h + 1) * hd)
            q = rq_ref[0, pl.ds(r0, c), hs]
            k = rk_ref[0, pl.ds(r0, c), hs]
            v = rv_ref[0, pl.ds(r0, c), hs]
            q = q * cos + pltpu.roll(q, hd // 2, 1) * sin
            k = (k * cos + pltpu.roll(k, hd // 2, 1) * sin) * (hd ** -0.5)
            qb = q.astype(BF16)
            kb = k.astype(BF16)
            vb = v.astype(BF16)
            inner = lax.dot_general(qb, kb, (((1,), (1,)), ((), ())), preferred_element_type=F32) * dec_ref[h]
            st = st_ref[h]
            o = (jnp.dot(inner.astype(BF16), vb, preferred_element_type=F32)
                 + jnp.dot(qb, st.astype(BF16), preferred_element_type=F32) * qd_ref[h])
            kdt = (k * kd_ref[h]).T.astype(BF16)
            st_ref[h] = st * cd_ref[h] + jnp.dot(kdt, vb, preferred_element_type=F32)
            mu = jnp.mean(o, axis=-1, keepdims=True)
            d = o - mu
            var = jnp.mean(d * d, axis=-1, keepdims=True)
            y = d * lax.rsqrt(var + LN_EPS) * gg_ref[:, hs] + gb_ref[:, hs]
            o_ref[0, pl.ds(r0, c), hs] = jax.nn.silu(rg_ref[0, pl.ds(r0, c), hs]) * y
        return 0

    lax.fori_loop(0, n_chunks, chunk, 0)


def retention_branch(hn, gn_g, gn_b, tr):
    b, t, _ = hn.shape
    w = BRANCH_WIDTH
    hd = RET_HEAD_DIM
    c = RET_CHUNK
    inv = ROPE_BASE ** (-jnp.arange(0, hd, 2, dtype=F32) / hd)
    ang = jnp.arange(t, dtype=F32)[:, None] * inv[None, :]
    cos2 = jnp.concatenate([jnp.cos(ang), jnp.cos(ang)], axis=1)
    sin2 = jnp.concatenate([-jnp.sin(ang), jnp.sin(ang)], axis=1)
    log_g = jnp.log(1.0 - 2.0 ** (-5.0 - jnp.arange(RET_HEADS, dtype=F32)))
    idx = jnp.arange(c, dtype=F32)
    diff = idx[:, None] - idx[None, :]
    decay_in = jnp.where(diff >= 0, jnp.exp(log_g[:, None, None] * jnp.maximum(diff, 0.0)), 0.0)
    ones = jnp.ones((RET_HEADS, c, hd), F32)
    q_dec = jnp.exp(log_g[:, None] * (idx + 1.0))[:, :, None] * ones
    k_dec = jnp.exp(log_g[:, None] * (c - 1.0 - idx))[:, :, None] * ones
    c_dec = jnp.exp(log_g * c)[:, None, None] * ones
    col = lambda cc: (lambda i, j: (i, j, cc // w))
    full3 = lambda i, j: (0, 0, 0)
    return pl.pallas_call(
        _ret_kernel,
        grid=(b, t // tr),
        in_specs=[
            pl.BlockSpec((1, tr, w), col(NAT_RQ)),
            pl.BlockSpec((1, tr, w), col(NAT_RK)),
            pl.BlockSpec((1, tr, w), col(NAT_RV)),
            pl.BlockSpec((1, tr, w), col(NAT_RG)),
            pl.BlockSpec((tr, hd), lambda i, j: (j, 0)),
            pl.BlockSpec((tr, hd), lambda i, j: (j, 0)),
            pl.BlockSpec((RET_HEADS, c, c), full3),
            pl.BlockSpec((RET_HEADS, c, hd), full3),
            pl.BlockSpec((RET_HEADS, c, hd), full3),
            pl.BlockSpec((RET_HEADS, c, hd), full3),
            pl.BlockSpec((1, w), lambda i, j: (0, 0)),
            pl.BlockSpec((1, w), lambda i, j: (0, 0)),
        ],
        out_specs=pl.BlockSpec((1, tr, w), lambda i, j: (i, j, 0)),
        out_shape=jax.ShapeDtypeStruct((b, t, w), F32),
        scratch_shapes=[pltpu.VMEM((RET_HEADS, hd, hd), F32)],
        compiler_params=_cparams(2),
        name="retention",
    )(hn, hn, hn, hn, cos2, sin2, decay_in, q_dec, k_dec, c_dec, gn_g.reshape(1, w), gn_b.reshape(1, w))


def _layer_norm(z, g, b):
    mu = jnp.mean(z, axis=-1, keepdims=True)
    d = z - mu
    var = jnp.mean(d * d, axis=-1, keepdims=True)
    return d * lax.rsqrt(var + LN_EPS) * g + b


def _merge_kernel(x_ref, ya_ref, yb_ref, yc_ref, mg_ref, wbr_ref, wout_ref, g_ref, b_ref, o_ref):
    mix = None
    for c, y_ref in enumerate((ya_ref, yb_ref, yc_ref)):
        proj = jnp.dot(y_ref[...].astype(BF16), wbr_ref[c], preferred_element_type=F32)
        gate = jax.nn.sigmoid(mg_ref[:, c * D_MODEL:(c + 1) * D_MODEL])
        mix = gate * proj if mix is None else mix + gate * proj
    m = jnp.dot(mix.astype(BF16), wout_ref[...], preferred_element_type=F32)
    o_ref[...] = _layer_norm(ALPHA * x_ref[...] + m, g_ref[...], b_ref[...])


def merge_branches(x, ya, yb, yc, hn, w_br, w_out, g, b, tm):
    m = x.shape[0]
    d = D_MODEL
    w = BRANCH_WIDTH
    row = lambda i: (i, 0)
    return pl.pallas_call(
        _merge_kernel,
        grid=(m // tm,),
        in_specs=[
            pl.BlockSpec((tm, d), row),
            pl.BlockSpec((tm, w), row),
            pl.BlockSpec((tm, w), row),
            pl.BlockSpec((tm, w), row),
            pl.BlockSpec((tm, 3 * d), row),
            pl.BlockSpec((3, w, d), lambda i: (0, 0, 0)),
            pl.BlockSpec((d, d), lambda i: (0, 0)),
            pl.BlockSpec((1, d), lambda i: (0, 0)),
            pl.BlockSpec((1, d), lambda i: (0, 0)),
        ],
        out_specs=pl.BlockSpec((tm, d), row),
        out_shape=jax.ShapeDtypeStruct((m, d), F32),
        compiler_params=_cparams(1),
        name="merge",
    )(x, ya, yb, yc, hn, w_br.astype(BF16), w_out.astype(BF16), g.reshape(1, d), b.reshape(1, d))


def _mlp_kernel(te_ref, tv_ref, x_ref, wg_ref, wu_ref, wd_ref, sc_ref, o_ref):
    i = pl.program_id(0)
    f = pl.program_id(1)

    @pl.when(tv_ref[i] > 0)
    def _():
        x = x_ref[...]
        gte = jnp.dot(x, wg_ref[0], preferred_element_type=F32)
        up = jnp.dot(x, wu_ref[0], preferred_element_type=F32)
        h = (jax.nn.silu(gte) * up).astype(BF16)
        part = jnp.dot(h, wd_ref[0], preferred_element_type=F32)

        @pl.when(f == 0)
        def _():
            o_ref[...] = part

        @pl.when(f > 0)
        def _():
            o_ref[...] = o_ref[...] + part

        @pl.when(f == pl.num_programs(1) - 1)
        def _():
            o_ref[...] = o_ref[...] * sc_ref[...]

    @pl.when((tv_ref[i] == 0) & (f == 0))
    def _():
        o_ref[...] = jnp.zeros_like(o_ref)


def grouped_swiglu(xs, w_gate, w_up, w_down, scale, tile_expert, tile_valid, tm, tf):
    n, d = xs.shape
    ff = w_gate.shape[2]
    grid_spec = pltpu.PrefetchScalarGridSpec(
        num_scalar_prefetch=2,
        grid=(n // tm, ff // tf),
        in_specs=[
            pl.BlockSpec((tm, d), lambda i, f, te, tv: (i, 0)),
            pl.BlockSpec((1, d, tf), lambda i, f, te, tv: (te[i], 0, f)),
            pl.BlockSpec((1, d, tf), lambda i, f, te, tv: (te[i], 0, f)),
            pl.BlockSpec((1, tf, d), lambda i, f, te, tv: (te[i], f, 0)),
            pl.BlockSpec((tm, 1), lambda i, f, te, tv: (i, 0)),
        ],
        out_specs=pl.BlockSpec((tm, d), lambda i, f, te, tv: (i, 0)),
    )
    return pl.pallas_call(
        _mlp_kernel,
        grid_spec=grid_spec,
        out_shape=jax.ShapeDtypeStruct((n, d), F32),
        compiler_params=_cparams(2),
        name="grouped_swiglu",
    )(tile_expert, tile_valid, xs, w_gate, w_up, w_down, scale)


def _add_ln_kernel(*refs):
    x_ref, *adds, g_ref, b_ref, o_ref = refs
    z = ALPHA * x_ref[...]
    for a in adds:
        z = z + a[...]
    o_ref[...] = _layer_norm(z, g_ref[...], b_ref[...])


def add_layer_norm(x, adds, g, b, tm):
    m, d = x.shape
    row = lambda i: (i, 0)
    return pl.pallas_call(
        _add_ln_kernel,
        grid=(m // tm,),
        in_specs=[pl.BlockSpec((tm, d), row)] * (1 + len(adds)) + [pl.BlockSpec((1, d), lambda i: (0, 0))] * 2,
        out_specs=pl.BlockSpec((tm, d), row),
        out_shape=jax.ShapeDtypeStruct((m, d), F32),
        compiler_params=_cparams(1),
        name="add_layer_norm",
    )(x, *adds, g.reshape(1, d), b.reshape(1, d))


def _router_kernel(x_ref, wh_ref, wl_ref, o_ref):
    x = x_ref[...]
    xh = x.astype(BF16)
    xl = (x - xh.astype(F32)).astype(BF16)
    wh = wh_ref[...]
    o_ref[...] = (jnp.dot(xh, wh, preferred_element_type=F32) + jnp.dot(xl, wh, preferred_element_type=F32)
                  + jnp.dot(xh, wl_ref[...], preferred_element_type=F32))


def router_logits(x, w_router, tm):
    m, d = x.shape
    wp = jnp.zeros((d, 128), F32).at[:, :N_EXPERTS].set(w_router)
    wh = wp.astype(BF16)
    wl = (wp - wh.astype(F32)).astype(BF16)
    return pl.pallas_call(
        _router_kernel,
        grid=(m // tm,),
        in_specs=[pl.BlockSpec((tm, d), lambda i: (i, 0)), pl.BlockSpec((d, 128), lambda i: (0, 0)),
                  pl.BlockSpec((d, 128), lambda i: (0, 0))],
        out_specs=pl.BlockSpec((tm, 128), lambda i: (i, 0)),
        out_shape=jax.ShapeDtypeStruct((m, 128), F32),
        compiler_params=_cparams(1),
        name="router",
    )(x, wh, wl)


def dense_ffn(x, w_gate, w_up, w_down, g, b):
    m = x.shape[0]
    tm = min(1024, m)
    nt = m // tm
    ys = grouped_swiglu(x.astype(BF16), w_gate[None].astype(BF16), w_up[None].astype(BF16),
                        w_down[None].astype(BF16), jnp.ones((m, 1), F32),
                        jnp.zeros((nt,), jnp.int32), jnp.ones((nt,), jnp.int32), tm, 512)
    return add_layer_norm(x, [ys], g, b, min(1024, m))


def moe_ffn(x, w_router, w_gate, w_up, w_down, g, b):
    m = x.shape[0]
    tm = 512
    logits = router_logits(x, w_router, min(1024, m))[:, :N_EXPERTS]
    top_v, top_i = lax.top_k(logits, TOP_K)
    probs = jax.nn.softmax(top_v, axis=-1)
    e_flat = top_i.reshape(-1)
    onehot = (e_flat[:, None] == jnp.arange(N_EXPERTS)[None, :]).astype(jnp.int32)
    csum = jnp.cumsum(onehot, axis=0)
    counts = csum[-1]
    rank = jnp.take_along_axis(csum, e_flat[:, None], axis=1)[:, 0] - 1
    padded = ((counts + tm - 1) // tm) * tm
    ends = jnp.cumsum(padded)
    starts = ends - padded
    pos = starts[e_flat] + rank
    n_slots = TOP_K * m + N_EXPERTS * tm
    n_tiles = n_slots // tm
    tok = jnp.arange(TOP_K * m, dtype=jnp.int32) // TOP_K
    src = jnp.zeros((n_slots,), jnp.int32).at[pos].set(tok)
    scale = jnp.zeros((n_slots,), F32).at[pos].set(probs.reshape(-1))
    tile_start = jnp.arange(n_tiles, dtype=jnp.int32) * tm
    tile_expert = jnp.minimum(jnp.searchsorted(ends, tile_start, side="right"), N_EXPERTS - 1).astype(jnp.int32)
    tile_valid = (tile_start < ends[-1]).astype(jnp.int32)
    xs = jnp.take(x.astype(BF16), src, axis=0)
    ys = grouped_swiglu(xs, w_gate.astype(BF16), w_up.astype(BF16), w_down.astype(BF16), scale[:, None],
                        tile_expert, tile_valid, tm, 512)
    pos2 = pos.reshape(m, TOP_K)
    y0 = jnp.take(ys, pos2[:, 0], axis=0)
    y1 = jnp.take(ys, pos2[:, 1], axis=0)
    return add_layer_norm(x, [y0, y1], g, b, min(1024, m))


def _pack_in_weights(w):
    d = w.shape[0]
    o = 0
    q = w[:, o:o + 512]; o += 512
    kv = w[:, o:o + 768].reshape(d, 3, 2, NSA_KV_GROUPS, HEAD_DIM); o += 768
    ng = w[:, o:o + 24]; o += 24
    rest = w[:, o:o + 7 * 512]; o += 7 * 512
    mg = w[:, o:]
    kpack = jnp.concatenate([kv[:, 1, 0], kv[:, 2, 0]], axis=-1).reshape(d, 2 * NSA_KV_GROUPS * HEAD_DIM)
    kc = kv[:, 0, 0].reshape(d, NSA_KV_GROUPS * HEAD_DIM)
    vc = kv[:, 0, 1].reshape(d, NSA_KV_GROUPS * HEAD_DIM)
    w_nat = jnp.concatenate([mg, rest, kpack, kc, vc], axis=1).astype(BF16)
    vs = kv[:, 1, 1].reshape(d, NSA_KV_GROUPS * HEAD_DIM)
    vw = kv[:, 2, 1].reshape(d, NSA_KV_GROUPS * HEAD_DIM)
    ngp = jnp.pad(ng.reshape(d, NSA_KV_GROUPS, NSA_GROUP * 3), ((0, 0), (0, 0), (0, 4))).reshape(d, 32)
    w_t = jnp.concatenate([q, vs, vw, ngp], axis=1).T.astype(BF16)
    return w_nat, w_t


def token_mixer_ln(x, w_in, cmp_pos, cmp_w1, cmp_b1, cmp_w2, cmp_b2, conv_w, gn_g, gn_b, w_br, w_out, ln_g, ln_b):
    b, t, d = x.shape
    m = b * t
    w_nat, w_t = _pack_in_weights(w_in)
    x2 = x.reshape(m, d)
    hn = mm_nat(x2, w_nat, min(1024, m), 512).reshape(b, t, NAT_COLS)
    proj_t = mm_nt(w_t, x, min(1024, t))
    nch = t // CMP_STRIDE
    raw = hn[:, :, NAT_KC:NAT_KC + 2 * NSA_KV_GROUPS * HEAD_DIM].reshape(b, t, 2, NSA_KV_GROUPS, HEAD_DIM)
    chunks = raw.transpose(2, 0, 3, 1, 4).reshape(2, b, NSA_KV_GROUPS, nch, CMP_STRIDE * HEAD_DIM)
    cn, ct = nsa_compress_kv(chunks, cmp_pos, cmp_w1, cmp_b1, cmp_w2, cmp_b2)
    kc = cn[0].astype(BF16)
    vct = ct[1].astype(BF16)
    kpack = hn[:, :, NAT_KP:NAT_KP + 2 * NSA_KV_GROUPS * HEAD_DIM].astype(BF16)
    vt_pack = proj_t[:, TR_VS:TR_NG, :].astype(BF16)
    wsel = _selection_weights(t // SLC_BLOCK, nch)
    y_a = nsa_attention(proj_t, kc, vct, kpack, vt_pack, wsel)
    y_b = short_conv(hn, conv_w, min(1024, t))
    y_c = retention_branch(hn, gn_g, gn_b, min(1024, t))
    out = merge_branches(x2, y_a.reshape(m, -1), y_b.reshape(m, -1), y_c.reshape(m, -1), hn.reshape(m, NAT_COLS),
                         w_br, w_out, ln_g, ln_b, min(512, m))
    return out


def kernel(x, w_in, cmp_pos, cmp_w1, cmp_b1, cmp_w2, cmp_b2, conv_w, ret_gn_g, ret_gn_b, w_br, w_out,
           ln1_g, ln1_b, ln2_g, ln2_b, ffn_gate, ffn_up, ffn_down, moe_router, moe_gate, moe_up, moe_down):
    b, t, d = x.shape
    for l in range(DEPTH):
        x2 = token_mixer_ln(x, w_in[l], cmp_pos[l], cmp_w1[l], cmp_b1[l], cmp_w2[l], cmp_b2[l], conv_w[l],
                            ret_gn_g[l], ret_gn_b[l], w_br[l], w_out[l], ln1_g[l], ln1_b[l])
        if l % 2 == 0:
            x2 = dense_ffn(x2, ffn_gate[l // 2], ffn_up[l // 2], ffn_down[l // 2], ln2_g[l], ln2_b[l])
        else:
            x2 = moe_ffn(x2, moe_router[l // 2], moe_gate[l // 2], moe_up[l // 2], moe_down[l // 2],
                         ln2_g[l], ln2_b[l])
        x = x2.reshape(b, t, d)
    return x
```

```python
import functools
import math

import jax
import jax.numpy as jnp
from jax import lax
from jax.experimental import pallas as pl
from jax.experimental.pallas import tpu as pltpu

F32 = jnp.float32
BF16 = jnp.bfloat16

D_MODEL = 1024
DEPTH = 4
BRANCH_WIDTH = 512
HEAD_DIM = 64
NSA_HEADS = 8
NSA_KV_GROUPS = 2
NSA_GROUP = 4
CMP_LEN = 32
CMP_STRIDE = 16
SLC_BLOCK = 64
SLC_TOPN = 16
WINDOW = 512
CMP_HIDDEN = 256
Q_BLOCK = 128
CONV_K = 3
RET_HEADS = 4
RET_HEAD_DIM = 128
RET_CHUNK = 128
ROPE_BASE = 10000.0
D_FF = 3584
N_EXPERTS = 8
TOP_K = 2
LN_EPS = 1e-5
ALPHA = (2.0 * DEPTH) ** 0.25

NAT_MG = 0
NAT_CB = 3072
NAT_CC = 3584
NAT_CH = 4096
NAT_RQ = 4608
NAT_RK = 5120
NAT_RV = 5632
NAT_RG = 6144
NAT_KP = 6656
NAT_KC = 6912
NAT_VC = 7040
NAT_COLS = 7168
TR_Q = 0
TR_VS = 512
TR_VW = 640
TR_NG = 768
TR_ROWS = 800

NEG = -1e30
SEL_TK = 1024
LOG2E = 1.4426950408889634
VT_ROWS = 80
VMEM_LIMIT = 48 * 1024 * 1024


def _cparams(n_axes, vmem=VMEM_LIMIT):
    return pltpu.CompilerParams(dimension_semantics=("arbitrary",) * n_axes, vmem_limit_bytes=vmem)


def _mm_nat_kernel(x_ref, w_ref, o_ref):
    o_ref[...] = jnp.dot(x_ref[...].astype(BF16), w_ref[...], preferred_element_type=F32).astype(o_ref.dtype)


def mm_nat(x, w, tm, tn, out_dtype=F32):
    m, k = x.shape
    n = w.shape[1]
    return pl.pallas_call(
        _mm_nat_kernel,
        grid=(m // tm, n // tn),
        in_specs=[pl.BlockSpec((tm, k), lambda i, j: (i, 0)), pl.BlockSpec((k, tn), lambda i, j: (0, j))],
        out_specs=pl.BlockSpec((tm, tn), lambda i, j: (i, j)),
        out_shape=jax.ShapeDtypeStruct((m, n), out_dtype),
        compiler_params=_cparams(2),
        name="mm_nat",
    )(x, w)


def _mm_nt_kernel(w_ref, x_ref, o_ref):
    o_ref[0] = lax.dot_general(w_ref[...], x_ref[0].astype(BF16), (((1,), (1,)), ((), ())),
                               preferred_element_type=F32)


def mm_nt(w_t, x, tm):
    b, t, k = x.shape
    r = w_t.shape[0]
    return pl.pallas_call(
        _mm_nt_kernel,
        grid=(b, t // tm),
        in_specs=[pl.BlockSpec((r, k), lambda i, j: (0, 0)), pl.BlockSpec((1, tm, k), lambda i, j: (i, j, 0))],
        out_specs=pl.BlockSpec((1, r, tm), lambda i, j: (i, 0, j)),
        out_shape=jax.ShapeDtypeStruct((b, r, t), F32),
        compiler_params=_cparams(2),
        name="mm_nt",
    )(w_t, x)


def _cmp_kernel(c_ref, pos_ref, w1_ref, b1_ref, w2_ref, w2t_ref, b2_ref, b2t_ref, on_ref, ot_ref):
    half = CMP_STRIDE * HEAD_DIM
    c = c_ref[0, 0, 0].astype(BF16)
    w1 = w1_ref[0]
    a = jnp.dot(c, w1[:half], preferred_element_type=F32)
    bm = jnp.dot(c, w1[half:], preferred_element_type=F32)
    nch = a.shape[0]
    bm = pltpu.roll(bm, nch - 1, 0)
    posb = jnp.dot(pos_ref[0].astype(BF16), w1, preferred_element_type=F32)[0:1]
    hid = jax.nn.gelu(a + bm + posb + b1_ref[0])
    hb = hid.astype(BF16)
    on_ref[0, 0, 0] = jnp.dot(hb, w2_ref[0], preferred_element_type=F32) + b2_ref[0]
    ot_ref[0, 0, 0] = lax.dot_general(w2t_ref[0], hb, (((1,), (1,)), ((), ())),
                                      preferred_element_type=F32) + b2t_ref[0]


def nsa_compress_kv(chunks, pos, w1, b1, w2, b2):
    _, b, g, nch, cw = chunks.shape
    pos_flat = jnp.broadcast_to(pos.reshape(2, 1, CMP_LEN * HEAD_DIM), (2, 8, CMP_LEN * HEAD_DIM))
    w1b = w1.astype(BF16)
    w2b = w2.astype(BF16)
    w2t = jnp.swapaxes(w2, 1, 2).astype(BF16)
    b1r = b1.reshape(2, 1, CMP_HIDDEN)
    b2r = b2.reshape(2, 1, HEAD_DIM)
    b2t = b2.reshape(2, HEAD_DIM, 1)
    sel = lambda kv, i, j: (kv, 0, 0)
    return pl.pallas_call(
        _cmp_kernel,
        grid=(2, b, g),
        in_specs=[
            pl.BlockSpec((1, 1, 1, nch, cw), lambda kv, i, j: (kv, i, j, 0, 0)),
            pl.BlockSpec((1, 8, CMP_LEN * HEAD_DIM), sel),
            pl.BlockSpec((1, CMP_LEN * HEAD_DIM, CMP_HIDDEN), sel),
            pl.BlockSpec((1, 1, CMP_HIDDEN), sel),
            pl.BlockSpec((1, CMP_HIDDEN, HEAD_DIM), sel),
            pl.BlockSpec((1, HEAD_DIM, CMP_HIDDEN), sel),
            pl.BlockSpec((1, 1, HEAD_DIM), sel),
            pl.BlockSpec((1, HEAD_DIM, 1), sel),
        ],
        out_specs=[
            pl.BlockSpec((1, 1, 1, nch, HEAD_DIM), lambda kv, i, j: (kv, i, j, 0, 0)),
            pl.BlockSpec((1, 1, 1, HEAD_DIM, nch), lambda kv, i, j: (kv, i, j, 0, 0)),
        ],
        out_shape=[
            jax.ShapeDtypeStruct((2, b, g, nch, HEAD_DIM), F32),
            jax.ShapeDtypeStruct((2, b, g, HEAD_DIM, nch), F32),
        ],
        compiler_params=_cparams(3),
        name="nsa_compress",
    )(chunks, pos_flat, w1b, b1r, w2b, w2t, b2r, b2t)


def _accumulate(s, mt, vt, m, acc):
    m_new = jnp.maximum(m, mt)
    alpha = jnp.exp2(m - m_new)
    p = jnp.exp2(s - m_new).astype(BF16)
    return m_new, alpha * acc + jnp.dot(vt, p, preferred_element_type=F32)


def _nsa_kernel(qt_ref, ng_ref, kc_ref, vct_ref, kp_ref, vst_ref, vwt_ref, wsel_ref, e_ref, tri_ref, wpat_ref,
                o_ref, bias_ref, w_ref, sbuf_ref):
    nq = Q_BLOCK
    ncol = NSA_GROUP * nq
    qb = pl.program_id(2)
    q0 = pl.multiple_of(qb * nq, nq)
    n_sel = bias_ref.shape[0]
    ncp = kc_ref.shape[2]
    kw = 2 * HEAD_DIM

    qt = qt_ref[0] * (HEAD_DIM ** -0.5 * LOG2E)
    qs = jnp.concatenate([qt[r * HEAD_DIM:(r + 1) * HEAD_DIM, :] for r in range(NSA_GROUP)], axis=1)
    zq = jnp.zeros_like(qs)
    q_c = qs.astype(BF16)
    q_slc = jnp.concatenate([qs, zq], axis=0).astype(BF16)
    q_win = jnp.concatenate([zq, qs], axis=0).astype(BF16)
    tq = q0 + (lax.broadcasted_iota(jnp.int32, (1, ncol), 1) & (nq - 1))

    s_c = jnp.dot(kc_ref[0, 0], q_c, preferred_element_type=F32)
    n_io = lax.broadcasted_iota(jnp.int32, (ncp, ncol), 0)
    mask_c = (n_io * CMP_STRIDE + (CMP_LEN - 1)) <= tq
    m_c = jnp.max(jnp.where(mask_c, s_c, NEG), axis=0, keepdims=True)
    m_c = jnp.where(m_c > 0.5 * NEG, m_c, 0.0)
    e_c = jnp.where(mask_c, jnp.exp2(s_c - m_c), 0.0)
    l_c = jnp.sum(e_c, axis=0, keepdims=True)
    p_c = e_c / jnp.maximum(l_c, 1e-30)
    o_c = jnp.dot(vct_ref[0, 0], p_c.astype(BF16), preferred_element_type=F32)

    p_grp = p_c[:, 0:nq]
    for r in range(1, NSA_GROUP):
        p_grp = p_grp + p_c[:, r * nq:(r + 1) * nq]
    p_hi = p_grp.astype(BF16)
    res = p_grp - p_hi.astype(F32)
    p_mid = res.astype(BF16)
    p_lo = (res - p_mid.astype(F32)).astype(BF16)
    wsel = wsel_ref[...]
    score = (jnp.dot(wsel, p_hi, preferred_element_type=F32)
             + jnp.dot(wsel, p_mid, preferred_element_type=F32)
             + jnp.dot(wsel, p_lo, preferred_element_type=F32))
    j_io = lax.broadcasted_iota(jnp.int32, (n_sel, nq), 0).astype(F32)
    tq1 = q0 + lax.broadcasted_iota(jnp.int32, (1, nq), 1)
    cur = jnp.right_shift(tq1, 6).astype(F32)
    forced = (j_io == 0) | (j_io == cur) | (j_io == cur - 1)
    score = jnp.where(forced, 1e9, score)
    score = jnp.where(j_io <= cur, score, -1.0)
    bias = jnp.full((n_sel, nq), NEG, F32)
    for _ in range(SLC_TOPN):
        mx = jnp.max(score, axis=0, keepdims=True)
        first = jnp.min(jnp.where(score == mx, j_io, float(n_sel)), axis=0, keepdims=True)
        hit = j_io == first
        bias = jnp.where(hit, 0.0, bias)
        score = jnp.where(hit, -3e38, score)
    first_diag = (2 * qb).astype(F32)
    bias_ref[...] = jnp.where(j_io < first_diag, bias, NEG).astype(BF16)

    wlen = WINDOW + nq
    wstart = pl.multiple_of(jnp.maximum(q0 - WINDOW, 0), nq)
    woff = pl.multiple_of(jnp.maximum(WINDOW - q0, 0), nq)
    s_w = jnp.dot(kp_ref[0, pl.ds(wstart, wlen), :], q_win, preferred_element_type=F32)
    s_w = s_w + jnp.concatenate([wpat_ref[pl.ds(woff, wlen), :]] * NSA_GROUP, axis=1)
    p_w = jnp.exp2(s_w - jnp.max(s_w, axis=0, keepdims=True)).astype(BF16)
    acc_w = jnp.dot(vwt_ref[0, :, pl.ds(wstart, wlen)], p_w, preferred_element_type=F32)
    o_w = acc_w[0:HEAD_DIM] / acc_w[HEAD_DIM:HEAD_DIM + 1]

    for wslot in range(2):
        w_ref[wslot, 0:kw, :] = q_slc
        w_ref[wslot, kw:, :] = jnp.zeros((w_ref.shape[1] - kw, ncol), BF16)
    bpt = SEL_TK // SLC_BLOCK

    n_tiles_total = kp_ref.shape[1] // SEL_TK

    def produce(j, slot):
        jc = jnp.minimum(j, n_tiles_total - 1)
        k0 = pl.multiple_of(jc * SEL_TK, SEL_TK)
        b16 = bias_ref[pl.ds(pl.multiple_of(jc * bpt, bpt), bpt), :]
        w_ref[slot, kw:kw + bpt, :] = jnp.concatenate([b16] * NSA_GROUP, axis=1)
        lhs = jnp.concatenate([kp_ref[0, pl.ds(k0, SEL_TK), :], e_ref[...]], axis=1)
        s = jnp.dot(lhs, w_ref[slot], preferred_element_type=F32)
        sbuf_ref[slot] = s
        return jnp.max(s, axis=0, keepdims=True)

    def consume(j, slot, mt, m, acc):
        jc = jnp.minimum(j, n_tiles_total - 1)
        k0 = pl.multiple_of(jc * SEL_TK, SEL_TK)
        return _accumulate(sbuf_ref[slot], mt, vst_ref[0, :, pl.ds(k0, SEL_TK)], m, acc)

    def pair(i, carry, last):
        mt_a, m, acc = carry
        mt_b = produce(2 * i + 1, 1)
        m, acc = consume(2 * i, 0, mt_a, m, acc)
        if last:
            return consume(2 * i + 1, 1, mt_b, m, acc)
        mt_a = produce(2 * i + 2, 0)
        m, acc = consume(2 * i + 1, 1, mt_b, m, acc)
        return mt_a, m, acc

    n_main = (q0 + (SEL_TK - 1)) // SEL_TK
    n_pairs = jnp.maximum((n_main + 1) // 2, 1)
    carry = (produce(0, 0), jnp.full((1, ncol), NEG, F32), jnp.zeros((vst_ref.shape[1], ncol), F32))
    carry = lax.fori_loop(0, n_pairs - 1, lambda i, c: pair(i, c, False), carry)
    m_s, acc_s = pair(n_pairs - 1, carry, True)
    s_d = jnp.dot(kp_ref[0, pl.ds(q0, nq), :], q_slc, preferred_element_type=F32) + tri_ref[...]
    _, acc_s = _accumulate(s_d, jnp.max(s_d, axis=0, keepdims=True), vst_ref[0, :, pl.ds(q0, nq)], m_s, acc_s)
    o_s = acc_s[0:HEAD_DIM] / acc_s[HEAD_DIM:HEAD_DIM + 1]

    gts = jax.nn.sigmoid(ng_ref[0])
    rows = []
    for r in range(NSA_GROUP):
        sl = slice(r * nq, (r + 1) * nq)
        rows.append(gts[3 * r:3 * r + 1, :] * o_c[:, sl] + gts[3 * r + 1:3 * r + 2, :] * o_s[:, sl]
                    + gts[3 * r + 2:3 * r + 3, :] * o_w[:, sl])
    y_t = jnp.concatenate(rows, axis=0)
    o_ref[0] = y_t.T


def _nsa_constants():
    nq = Q_BLOCK
    ql = jnp.arange(nq)[None, :]
    r = jnp.arange(SEL_TK)[:, None]
    e = (r // SLC_BLOCK == jnp.arange(2 * HEAD_DIM)[None, :]).astype(BF16)
    rd = jnp.arange(nq)[:, None]
    tri = jnp.tile(jnp.where(rd <= ql, 0.0, NEG).astype(F32), (1, NSA_GROUP))
    rw = jnp.arange(2 * WINDOW + nq)[:, None]
    wpat = jnp.where((rw > ql) & (rw <= ql + WINDOW), 0.0, NEG).astype(F32)
    return e, tri, wpat


def nsa_attention(proj_t, kc, vct, kpack, vt_pack, wsel):
    b, _, t = proj_t.shape
    ncp = kc.shape[2]
    n_sel = t // SLC_BLOCK
    gq = NSA_GROUP * HEAD_DIM
    e, tri, wpat = _nsa_constants()
    const2 = lambda i, g, q: (0, 0)
    return pl.pallas_call(
        _nsa_kernel,
        grid=(b, NSA_KV_GROUPS, t // Q_BLOCK),
        in_specs=[
            pl.BlockSpec((1, gq, Q_BLOCK), lambda i, g, q: (i, g, q)),
            pl.BlockSpec((1, 16, Q_BLOCK), lambda i, g, q: (i, TR_NG // 16 + g, q)),
            pl.BlockSpec((1, 1, ncp, HEAD_DIM), lambda i, g, q: (i, g, 0, 0)),
            pl.BlockSpec((1, 1, HEAD_DIM, ncp), lambda i, g, q: (i, g, 0, 0)),
            pl.BlockSpec((1, t, 2 * HEAD_DIM), lambda i, g, q: (i, 0, g)),
            pl.BlockSpec((1, VT_ROWS, t), lambda i, g, q: (i, g, 0)),
            pl.BlockSpec((1, VT_ROWS, t), lambda i, g, q: (i, NSA_KV_GROUPS + g, 0)),
            pl.BlockSpec((n_sel, ncp), const2),
            pl.BlockSpec(e.shape, const2),
            pl.BlockSpec(tri.shape, const2),
            pl.BlockSpec(wpat.shape, const2),
        ],
        out_specs=pl.BlockSpec((1, Q_BLOCK, gq), lambda i, g, q: (i, q, g)),
        out_shape=jax.ShapeDtypeStruct((b, t, NSA_HEADS * HEAD_DIM), F32),
        scratch_shapes=[pltpu.VMEM((n_sel, Q_BLOCK), BF16), pltpu.VMEM((2, 4 * HEAD_DIM, NSA_GROUP * Q_BLOCK), BF16),
                        pltpu.VMEM((2, SEL_TK, NSA_GROUP * Q_BLOCK), F32)],
        compiler_params=_cparams(3),
        name="nsa_attention",
    )(proj_t, proj_t, kc, vct, kpack, vt_pack, vt_pack, wsel, e, tri, wpat)


def _selection_weights(n_sel, ncp):
    j = jnp.arange(n_sel)[:, None]
    n = jnp.arange(ncp)[None, :]
    d = n - 4 * j
    w = jnp.where((d == -1) | (d == 3), 1.0, 0.0) + jnp.where((d >= 0) & (d <= 2), 2.0, 0.0)
    w = jnp.where(n < ncp - 1, w, 0.0)
    return w.astype(BF16)


def _conv_kernel(cb_ref, cc_ref, ch_ref, pc_ref, ph_ref, w_ref, o_ref):
    i = pl.program_id(1)
    u = cc_ref[0] * ch_ref[0]
    prev = pc_ref[0] * ph_ref[0]
    prev = jnp.where(i > 0, prev, 0.0)
    tm = u.shape[0]
    row = lax.broadcasted_iota(jnp.int32, u.shape, 0)
    p1 = prev[7:8, :]
    p2 = prev[6:7, :]
    u1 = jnp.where(row == 0, p1, pltpu.roll(u, 1, 0))
    u2 = jnp.where(row == 0, p2, jnp.where(row == 1, p1, pltpu.roll(u, 2, 0)))
    w = w_ref[...]
    o_ref[0] = cb_ref[0] * (w[0:1, :] * u2 + w[1:2, :] * u1 + w[2:3, :] * u)


def short_conv(hn, conv_w, tm):
    b, t, _ = hn.shape
    w = BRANCH_WIDTH
    col = lambda c: (lambda i, j: (i, j, c // w))
    hal = lambda c: (lambda i, j: (i, jnp.maximum(j * (tm // 8) - 1, 0), c // w))
    wpad = jnp.zeros((8, w), F32).at[:CONV_K].set(conv_w)
    return pl.pallas_call(
        _conv_kernel,
        grid=(b, t // tm),
        in_specs=[
            pl.BlockSpec((1, tm, w), col(NAT_CB)),
            pl.BlockSpec((1, tm, w), col(NAT_CC)),
            pl.BlockSpec((1, tm, w), col(NAT_CH)),
            pl.BlockSpec((1, 8, w), hal(NAT_CC)),
            pl.BlockSpec((1, 8, w), hal(NAT_CH)),
            pl.BlockSpec((8, w), lambda i, j: (0, 0)),
        ],
        out_specs=pl.BlockSpec((1, tm, w), lambda i, j: (i, j, 0)),
        out_shape=jax.ShapeDtypeStruct((b, t, w), F32),
        compiler_params=_cparams(2),
        name="short_conv",
    )(hn, hn, hn, hn, hn, wpad)


def _ret_kernel(rq_ref, rk_ref, rv_ref, rg_ref, cos_ref, sin_ref, dec_ref, qd_ref, kd_ref, cd_ref,
                gg_ref, gb_ref, o_ref, st_ref):
    c = RET_CHUNK
    hd = RET_HEAD_DIM

    @pl.when(pl.program_id(1) == 0)
    def _():
        st_ref[...] = jnp.zeros_like(st_ref)

    n_chunks = rq_ref.shape[1] // c

    def chunk(ci, _):
        r0 = pl.multiple_of(ci * c, c)
        cos = cos_ref[pl.ds(r0, c), :]
        sin = sin_ref[pl.ds(r0, c), :]
        for h in range(RET_HEADS):
            hs = slice(h * hd, (h + 1) * hd)
            q = rq_ref[0, pl.ds(r0, c), hs]
            k = rk_ref[0, pl.ds(r0, c), hs]
            v = rv_ref[0, pl.ds(r0, c), hs]
            q = q * cos + pltpu.roll(q, hd // 2, 1) * sin
            k = (k * cos + pltpu.roll(k, hd // 2, 1) * sin) * (hd ** -0.5)
            qb = q.astype(BF16)
            kb = k.astype(BF16)
            vb = v.astype(BF16)
            inner = lax.dot_general(qb, kb, (((1,), (1,)), ((), ())), preferred_element_type=F32) * dec_ref[h]
            st = st_ref[h]
            o = (jnp.dot(inner.astype(BF16), vb, preferred_element_type=F32)
                 + jnp.dot(qb, st.astype(BF16), preferred_element_type=F32) * qd_ref[h])
            kdt = (k * kd_ref[h]).T.astype(BF16)
            st_ref[h] = st * cd_ref[h] + jnp.dot(kdt, vb, preferred_element_type=F32)
            mu = jnp.mean(o, axis=-1, keepdims=True)
            d = o - mu
            var = jnp.mean(d * d, axis=-1, keepdims=True)
            y = d * lax.rsqrt(var + LN_EPS) * gg_ref[:, hs] + gb_ref[:, hs]
            o_ref[0, pl.ds(r0, c), hs] = jax.nn.silu(rg_ref[0, pl.ds(r0, c), hs]) * y
        return 0

    lax.fori_loop(0, n_chunks, chunk, 0)


def retention_branch(hn, gn_g, gn_b, tr):
    b, t, _ = hn.shape
    w = BRANCH_WIDTH
    hd = RET_HEAD_DIM
    c = RET_CHUNK
    inv = ROPE_BASE ** (-jnp.arange(0, hd, 2, dtype=F32) / hd)
    ang = jnp.arange(t, dtype=F32)[:, None] * inv[None, :]
    cos2 = jnp.concatenate([jnp.cos(ang), jnp.cos(ang)], axis=1)
    sin2 = jnp.concatenate([-jnp.sin(ang), jnp.sin(ang)], axis=1)
    log_g = jnp.log(1.0 - 2.0 ** (-5.0 - jnp.arange(RET_HEADS, dtype=F32)))
    idx = jnp.arange(c, dtype=F32)
    diff = idx[:, None] - idx[None, :]
    decay_in = jnp.where(diff >= 0, jnp.exp(log_g[:, None, None] * jnp.maximum(diff, 0.0)), 0.0)
    ones = jnp.ones((RET_HEADS, c, hd), F32)
    q_dec = jnp.exp(log_g[:, None] * (idx + 1.0))[:, :, None] * ones
    k_dec = jnp.exp(log_g[:, None] * (c - 1.0 - idx))[:, :, None] * ones
    c_dec = jnp.exp(log_g * c)[:, None, None] * ones
    col = lambda cc: (lambda i, j: (i, j, cc // w))
    full3 = lambda i, j: (0, 0, 0)
    return pl.pallas_call(
        _ret_kernel,
        grid=(b, t // tr),
        in_specs=[
            pl.BlockSpec((1, tr, w), col(NAT_RQ)),
            pl.BlockSpec((1, tr, w), col(NAT_RK)),
            pl.BlockSpec((1, tr, w), col(NAT_RV)),
            pl.BlockSpec((1, tr, w), col(NAT_RG)),
            pl.BlockSpec((tr, hd), lambda i, j: (j, 0)),
            pl.BlockSpec((tr, hd), lambda i, j: (j, 0)),
            pl.BlockSpec((RET_HEADS, c, c), full3),
            pl.BlockSpec((RET_HEADS, c, hd), full3),
            pl.BlockSpec((RET_HEADS, c, hd), full3),
            pl.BlockSpec((RET_HEADS, c, hd), full3),
            pl.BlockSpec((1, w), lambda i, j: (0, 0)),
            pl.BlockSpec((1, w), lambda i, j: (0, 0)),
        ],
        out_specs=pl.BlockSpec((1, tr, w), lambda i, j: (i, j, 0)),
        out_shape=jax.ShapeDtypeStruct((b, t, w), F32),
        scratch_shapes=[pltpu.VMEM((RET_HEADS, hd, hd), F32)],
        compiler_params=_cparams(2),
        name="retention",
    )(hn, hn, hn, hn, cos2, sin2, decay_in, q_dec, k_dec, c_dec, gn_g.reshape(1, w), gn_b.reshape(1, w))


def _layer_norm(z, g, b):
    mu = jnp.mean(z, axis=-1, keepdims=True)
    d = z - mu
    var = jnp.mean(d * d, axis=-1, keepdims=True)
    return d * lax.rsqrt(var + LN_EPS) * g + b


def _merge_kernel(x_ref, ya_ref, yb_ref, yc_ref, mg_ref, wbr_ref, wout_ref, g_ref, b_ref, o_ref):
    mix = None
    for c, y_ref in enumerate((ya_ref, yb_ref, yc_ref)):
        proj = jnp.dot(y_ref[...].astype(BF16), wbr_ref[c], preferred_element_type=F32)
        gate = jax.nn.sigmoid(mg_ref[:, c * D_MODEL:(c + 1) * D_MODEL])
        mix = gate * proj if mix is None else mix + gate * proj
    m = jnp.dot(mix.astype(BF16), wout_ref[...], preferred_element_type=F32)
    o_ref[...] = _layer_norm(ALPHA * x_ref[...] + m, g_ref[...], b_ref[...])


def merge_branches(x, ya, yb, yc, hn, w_br, w_out, g, b, tm):
    m = x.shape[0]
    d = D_MODEL
    w = BRANCH_WIDTH
    row = lambda i: (i, 0)
    return pl.pallas_call(
        _merge_kernel,
        grid=(m // tm,),
        in_specs=[
            pl.BlockSpec((tm, d), row),
            pl.BlockSpec((tm, w), row),
            pl.BlockSpec((tm, w), row),
            pl.BlockSpec((tm, w), row),
            pl.BlockSpec((tm, 3 * d), row),
            pl.BlockSpec((3, w, d), lambda i: (0, 0, 0)),
            pl.BlockSpec((d, d), lambda i: (0, 0)),
            pl.BlockSpec((1, d), lambda i: (0, 0)),
            pl.BlockSpec((1, d), lambda i: (0, 0)),
        ],
        out_specs=pl.BlockSpec((tm, d), row),
        out_shape=jax.ShapeDtypeStruct((m, d), F32),
        compiler_params=_cparams(1),
        name="merge",
    )(x, ya, yb, yc, hn, w_br.astype(BF16), w_out.astype(BF16), g.reshape(1, d), b.reshape(1, d))


def _mlp_kernel(te_ref, tv_ref, x_ref, wg_ref, wu_ref, wd_ref, sc_ref, o_ref):
    i = pl.program_id(0)
    f = pl.program_id(1)

    @pl.when(tv_ref[i] > 0)
    def _():
        x = x_ref[...]
        gte = jnp.dot(x, wg_ref[0], preferred_element_type=F32)
        up = jnp.dot(x, wu_ref[0], preferred_element_type=F32)
        h = (jax.nn.silu(gte) * up).astype(BF16)
        part = jnp.dot(h, wd_ref[0], preferred_element_type=F32)

        @pl.when(f == 0)
        def _():
            o_ref[...] = part

        @pl.when(f > 0)
        def _():
            o_ref[...] = o_ref[...] + part

        @pl.when(f == pl.num_programs(1) - 1)
        def _():
            o_ref[...] = o_ref[...] * sc_ref[...]

    @pl.when((tv_ref[i] == 0) & (f == 0))
    def _():
        o_ref[...] = jnp.zeros_like(o_ref)


def grouped_swiglu(xs, w_gate, w_up, w_down, scale, tile_expert, tile_valid, tm, tf):
    n, d = xs.shape
    ff = w_gate.shape[2]
    grid_spec = pltpu.PrefetchScalarGridSpec(
        num_scalar_prefetch=2,
        grid=(n // tm, ff // tf),
        in_specs=[
            pl.BlockSpec((tm, d), lambda i, f, te, tv: (i, 0)),
            pl.BlockSpec((1, d, tf), lambda i, f, te, tv: (te[i], 0, f)),
            pl.BlockSpec((1, d, tf), lambda i, f, te, tv: (te[i], 0, f)),
            pl.BlockSpec((1, tf, d), lambda i, f, te, tv: (te[i], f, 0)),
            pl.BlockSpec((tm, 1), lambda i, f, te, tv: (i, 0)),
        ],
        out_specs=pl.BlockSpec((tm, d), lambda i, f, te, tv: (i, 0)),
    )
    return pl.pallas_call(
        _mlp_kernel,
        grid_spec=grid_spec,
        out_shape=jax.ShapeDtypeStruct((n, d), F32),
        compiler_params=_cparams(2),
        name="grouped_swiglu",
    )(tile_expert, tile_valid, xs, w_gate, w_up, w_down, scale)


def _add_ln_kernel(*refs):
    x_ref, *adds, g_ref, b_ref, o_ref = refs
    z = ALPHA * x_ref[...]
    for a in adds:
        z = z + a[...]
    o_ref[...] = _layer_norm(z, g_ref[...], b_ref[...])


def add_layer_norm(x, adds, g, b, tm):
    m, d = x.shape
    row = lambda i: (i, 0)
    return pl.pallas_call(
        _add_ln_kernel,
        grid=(m // tm,),
        in_specs=[pl.BlockSpec((tm, d), row)] * (1 + len(adds)) + [pl.BlockSpec((1, d), lambda i: (0, 0))] * 2,
        out_specs=pl.BlockSpec((tm, d), row),
        out_shape=jax.ShapeDtypeStruct((m, d), F32),
        compiler_params=_cparams(1),
        name="add_layer_norm",
    )(x, *adds, g.reshape(1, d), b.reshape(1, d))


def _router_kernel(x_ref, wh_ref, wl_ref, o_ref):
    x = x_ref[...]
    xh = x.astype(BF16)
    xl = (x - xh.astype(F32)).astype(BF16)
    wh = wh_ref[...]
    o_ref[...] = (jnp.dot(xh, wh, preferred_element_type=F32) + jnp.dot(xl, wh, preferred_element_type=F32)
                  + jnp.dot(xh, wl_ref[...], preferred_element_type=F32))


def router_logits(x, w_router, tm):
    m, d = x.shape
    wp = jnp.zeros((d, 128), F32).at[:, :N_EXPERTS].set(w_router)
    wh = wp.astype(BF16)
    wl = (wp - wh.astype(F32)).astype(BF16)
    return pl.pallas_call(
        _router_kernel,
        grid=(m // tm,),
        in_specs=[pl.BlockSpec((tm, d), lambda i: (i, 0)), pl.BlockSpec((d, 128), lambda i: (0, 0)),
                  pl.BlockSpec((d, 128), lambda i: (0, 0))],
        out_specs=pl.BlockSpec((tm, 128), lambda i: (i, 0)),
        out_shape=jax.ShapeDtypeStruct((m, 128), F32),
        compiler_params=_cparams(1),
        name="router",
    )(x, wh, wl)


def dense_ffn(x, w_gate, w_up, w_down, g, b):
    m = x.shape[0]
    tm = min(1024, m)
    nt = m // tm
    ys = grouped_swiglu(x.astype(BF16), w_gate[None].astype(BF16), w_up[None].astype(BF16),
                        w_down[None].astype(BF16), jnp.ones((m, 1), F32),
                        jnp.zeros((nt,), jnp.int32), jnp.ones((nt,), jnp.int32), tm, 512)
    return add_layer_norm(x, [ys], g, b, min(1024, m))


def moe_ffn(x, w_router, w_gate, w_up, w_down, g, b):
    m = x.shape[0]
    tm = 512
    logits = router_logits(x, w_router, min(1024, m))[:, :N_EXPERTS]
    top_v, top_i = lax.top_k(logits, TOP_K)
    probs = jax.nn.softmax(top_v, axis=-1)
    e_flat = top_i.reshape(-1)
    onehot = (e_flat[:, None] == jnp.arange(N_EXPERTS)[None, :]).astype(jnp.int32)
    csum = jnp.cumsum(onehot, axis=0)
    counts = csum[-1]
    rank = jnp.take_along_axis(csum, e_flat[:, None], axis=1)[:, 0] - 1
    padded = ((counts + tm - 1) // tm) * tm
    ends = jnp.cumsum(padded)
    starts = ends - padded
    pos = starts[e_flat] + rank
    n_slots = TOP_K * m + N_EXPERTS * tm
    n_tiles = n_slots // tm
    tok = jnp.arange(TOP_K * m, dtype=jnp.int32) // TOP_K
    src = jnp.zeros((n_slots,), jnp.int32).at[pos].set(tok)
    scale = jnp.zeros((n_slots,), F32).at[pos].set(probs.reshape(-1))
    tile_start = jnp.arange(n_tiles, dtype=jnp.int32) * tm
    tile_expert = jnp.minimum(jnp.searchsorted(ends, tile_start, side="right"), N_EXPERTS - 1).astype(jnp.int32)
    tile_valid = (tile_start < ends[-1]).astype(jnp.int32)
    xs = jnp.take(x.astype(BF16), src, axis=0)
    ys = grouped_swiglu(xs, w_gate.astype(BF16), w_up.astype(BF16), w_down.astype(BF16), scale[:, None],
                        tile_expert, tile_valid, tm, 512)
    pos2 = pos.reshape(m, TOP_K)
    y0 = jnp.take(ys, pos2[:, 0], axis=0)
    y1 = jnp.take(ys, pos2[:, 1], axis=0)
    return add_layer_norm(x, [y0, y1], g, b, min(1024, m))


def _pack_in_weights(w):
    d = w.shape[0]
    o = 0
    q = w[:, o:o + 512]; o += 512
    kv = w[:, o:o + 768].reshape(d, 3, 2, NSA_KV_GROUPS, HEAD_DIM); o += 768
    ng = w[:, o:o + 24]; o += 24
    rest = w[:, o:o + 7 * 512]; o += 7 * 512
    mg = w[:, o:]
    kpack = jnp.concatenate([kv[:, 1, 0], kv[:, 2, 0]], axis=-1).reshape(d, 2 * NSA_KV_GROUPS * HEAD_DIM)
    kc = kv[:, 0, 0].reshape(d, NSA_KV_GROUPS * HEAD_DIM)
    vc = kv[:, 0, 1].reshape(d, NSA_KV_GROUPS * HEAD_DIM)
    w_nat = jnp.concatenate([mg, rest, kpack, kc, vc], axis=1).astype(BF16)
    vs = kv[:, 1, 1].reshape(d, NSA_KV_GROUPS * HEAD_DIM)
    vw = kv[:, 2, 1].reshape(d, NSA_KV_GROUPS * HEAD_DIM)
    ngp = jnp.pad(ng.reshape(d, NSA_KV_GROUPS, NSA_GROUP * 3), ((0, 0), (0, 0), (0, 4))).reshape(d, 32)
    w_t = jnp.concatenate([q, vs, vw, ngp], axis=1).T.astype(BF16)
    return w_nat, w_t


def token_mixer_ln(x, w_in, cmp_pos, cmp_w1, cmp_b1, cmp_w2, cmp_b2, conv_w, gn_g, gn_b, w_br, w_out, ln_g, ln_b):
    b, t, d = x.shape
    m = b * t
    w_nat, w_t = _pack_in_weights(w_in)
    x2 = x.reshape(m, d)
    hn = mm_nat(x2, w_nat, min(1024, m), 512).reshape(b, t, NAT_COLS)
    proj_t = mm_nt(w_t, x, min(1024, t))
    nch = t // CMP_STRIDE
    raw = hn[:, :, NAT_KC:NAT_KC + 2 * NSA_KV_GROUPS * HEAD_DIM].reshape(b, t, 2, NSA_KV_GROUPS, HEAD_DIM)
    chunks = raw.transpose(2, 0, 3, 1, 4).reshape(2, b, NSA_KV_GROUPS, nch, CMP_STRIDE * HEAD_DIM)
    cn, ct = nsa_compress_kv(chunks, cmp_pos, cmp_w1, cmp_b1, cmp_w2, cmp_b2)
    kc = cn[0].astype(BF16)
    vct = ct[1].astype(BF16)
    kpack = hn[:, :, NAT_KP:NAT_KP + 2 * NSA_KV_GROUPS * HEAD_DIM].astype(BF16)
    vt4 = proj_t[:, TR_VS:TR_NG, :].astype(BF16).reshape(b, 4, HEAD_DIM, t)
    ones_pad = jnp.zeros((b, 4, VT_ROWS - HEAD_DIM, t), BF16).at[:, :, 0].set(1.0)
    vt_pack = jnp.concatenate([vt4, ones_pad], axis=2).reshape(b, 4 * VT_ROWS, t)
    wsel = _selection_weights(t // SLC_BLOCK, nch)
    y_a = nsa_attention(proj_t, kc, vct, kpack, vt_pack, wsel)
    y_b = short_conv(hn, conv_w, min(1024, t))
    y_c = retention_branch(hn, gn_g, gn_b, min(1024, t))
    out = merge_branches(x2, y_a.reshape(m, -1), y_b.reshape(m, -1), y_c.reshape(m, -1), hn.reshape(m, NAT_COLS),
                         w_br, w_out, ln_g, ln_b, min(512, m))
    return out


def kernel(x, w_in, cmp_pos, cmp_w1, cmp_b1, cmp_w2, cmp_b2, conv_w, ret_gn_g, ret_gn_b, w_br, w_out,
           ln1_g, ln1_b, ln2_g, ln2_b, ffn_gate, ffn_up, ffn_down, moe_router, moe_gate, moe_up, moe_down):
    b, t, d = x.shape
    for l in range(DEPTH):
        x2 = token_mixer_ln(x, w_in[l], cmp_pos[l], cmp_w1[l], cmp_b1[l], cmp_w2[l], cmp_b2[l], conv_w[l],
                            ret_gn_g[l], ret_gn_b[l], w_br[l], w_out[l], ln1_g[l], ln1_b[l])
        if l % 2 == 0:
            x2 = dense_ffn(x2, ffn_gate[l // 2], ffn_up[l // 2], ffn_down[l // 2], ln2_g[l], ln2_b[l])
        else:
            x2 = moe_ffn(x2, moe_router[l // 2], moe_gate[l // 2], moe_up[l // 2], moe_down[l // 2],
                         ln2_g[l], ln2_b[l])
        x = x2.reshape(b, t, d)
    return x
```

```python
import functools
import math

import jax
import jax.numpy as jnp
from jax import lax
from jax.experimental import pallas as pl
from jax.experimental.pallas import tpu as pltpu

F32 = jnp.float32
BF16 = jnp.bfloat16

D_MODEL = 1024
DEPTH = 4
BRANCH_WIDTH = 512
HEAD_DIM = 64
NSA_HEADS = 8
NSA_KV_GROUPS = 2
NSA_GROUP = 4
CMP_LEN = 32
CMP_STRIDE = 16
SLC_BLOCK = 64
SLC_TOPN = 16
WINDOW = 512
CMP_HIDDEN = 256
Q_BLOCK = 128
CONV_K = 3
RET_HEADS = 4
RET_HEAD_DIM = 128
RET_CHUNK = 128
ROPE_BASE = 10000.0
D_FF = 3584
N_EXPERTS = 8
TOP_K = 2
LN_EPS = 1e-5
ALPHA = (2.0 * DEPTH) ** 0.25

NAT_MG = 0
NAT_CB = 3072
NAT_CC = 3584
NAT_CH = 4096
NAT_RQ = 4608
NAT_RK = 5120
NAT_RV = 5632
NAT_RG = 6144
NAT_KP = 6656
NAT_KC = 6912
NAT_VC = 7040
NAT_COLS = 7168
TR_Q = 0
TR_VS = 512
TR_VW = 640
TR_NG = 768
TR_ROWS = 800

NEG = -1e30
SEL_TK = 1024
LOG2E = 1.4426950408889634
NSA_QSUB = 2
CMP_CLASS_ROWS = 256
PICKED = -3e38
VT_ROWS = 80
VMEM_LIMIT = 48 * 1024 * 1024


def _cparams(n_axes, vmem=VMEM_LIMIT):
    return pltpu.CompilerParams(dimension_semantics=("arbitrary",) * n_axes, vmem_limit_bytes=vmem)


def _mm_nat_kernel(x_ref, w_ref, o_ref, xb_ref):
    @pl.when(pl.program_id(1) == 0)
    def _():
        xb_ref[...] = x_ref[...].astype(BF16)

    o_ref[...] = jnp.dot(xb_ref[...], w_ref[...], preferred_element_type=F32).astype(o_ref.dtype)


def mm_nat(x, w, tm, tn, out_dtype=F32):
    m, k = x.shape
    n = w.shape[1]
    return pl.pallas_call(
        _mm_nat_kernel,
        grid=(m // tm, n // tn),
        in_specs=[pl.BlockSpec((tm, k), lambda i, j: (i, 0)), pl.BlockSpec((k, tn), lambda i, j: (0, j))],
        out_specs=pl.BlockSpec((tm, tn), lambda i, j: (i, j)),
        out_shape=jax.ShapeDtypeStruct((m, n), out_dtype),
        scratch_shapes=[pltpu.VMEM((tm, k), BF16)],
        compiler_params=_cparams(2),
        name="mm_nat",
    )(x, w)


def _mm_nt_kernel(w_ref, x_ref, o_ref):
    o_ref[0] = lax.dot_general(w_ref[...], x_ref[0].astype(BF16), (((1,), (1,)), ((), ())),
                               preferred_element_type=F32)


def mm_nt(w_t, x, tm):
    b, t, k = x.shape
    r = w_t.shape[0]
    return pl.pallas_call(
        _mm_nt_kernel,
        grid=(b, t // tm),
        in_specs=[pl.BlockSpec((r, k), lambda i, j: (0, 0)), pl.BlockSpec((1, tm, k), lambda i, j: (i, j, 0))],
        out_specs=pl.BlockSpec((1, r, tm), lambda i, j: (i, 0, j)),
        out_shape=jax.ShapeDtypeStruct((b, r, t), F32),
        compiler_params=_cparams(2),
        name="mm_nt",
    )(w_t, x)


def _cmp_kernel(c_ref, pos_ref, w1_ref, b1_ref, w2_ref, w2t_ref, b2_ref, b2t_ref, on_ref, ot_ref):
    half = CMP_STRIDE * HEAD_DIM
    c = c_ref[0, 0, 0].astype(BF16)
    w1 = w1_ref[0]
    a = jnp.dot(c, w1[:half], preferred_element_type=F32)
    bm = jnp.dot(c, w1[half:], preferred_element_type=F32)
    nch = a.shape[0]
    bm = pltpu.roll(bm, nch - 1, 0)
    posb = jnp.dot(pos_ref[0].astype(BF16), w1, preferred_element_type=F32)[0:1]
    hid = jax.nn.gelu(a + bm + posb + b1_ref[0])
    hb = hid.astype(BF16)
    on_ref[0, 0, 0] = jnp.dot(hb, w2_ref[0], preferred_element_type=F32) + b2_ref[0]
    ot_ref[0, 0, 0] = lax.dot_general(w2t_ref[0], hb, (((1,), (1,)), ((), ())),
                                      preferred_element_type=F32) + b2t_ref[0]


def nsa_compress_kv(chunks, pos, w1, b1, w2, b2):
    _, b, g, nch, cw = chunks.shape
    pos_flat = jnp.broadcast_to(pos.reshape(2, 1, CMP_LEN * HEAD_DIM), (2, 8, CMP_LEN * HEAD_DIM))
    w1b = w1.astype(BF16)
    w2b = w2.astype(BF16)
    w2t = jnp.swapaxes(w2, 1, 2).astype(BF16)
    b1r = b1.reshape(2, 1, CMP_HIDDEN)
    b2r = b2.reshape(2, 1, HEAD_DIM)
    b2t = b2.reshape(2, HEAD_DIM, 1)
    sel = lambda kv, i, j: (kv, 0, 0)
    return pl.pallas_call(
        _cmp_kernel,
        grid=(2, b, g),
        in_specs=[
            pl.BlockSpec((1, 1, 1, nch, cw), lambda kv, i, j: (kv, i, j, 0, 0)),
            pl.BlockSpec((1, 8, CMP_LEN * HEAD_DIM), sel),
            pl.BlockSpec((1, CMP_LEN * HEAD_DIM, CMP_HIDDEN), sel),
            pl.BlockSpec((1, 1, CMP_HIDDEN), sel),
            pl.BlockSpec((1, CMP_HIDDEN, HEAD_DIM), sel),
            pl.BlockSpec((1, HEAD_DIM, CMP_HIDDEN), sel),
            pl.BlockSpec((1, 1, HEAD_DIM), sel),
            pl.BlockSpec((1, HEAD_DIM, 1), sel),
        ],
        out_specs=[
            pl.BlockSpec((1, 1, 1, nch, HEAD_DIM), lambda kv, i, j: (kv, i, j, 0, 0)),
            pl.BlockSpec((1, 1, 1, HEAD_DIM, nch), lambda kv, i, j: (kv, i, j, 0, 0)),
        ],
        out_shape=[
            jax.ShapeDtypeStruct((2, b, g, nch, HEAD_DIM), F32),
            jax.ShapeDtypeStruct((2, b, g, HEAD_DIM, nch), F32),
        ],
        compiler_params=_cparams(3),
        name="nsa_compress",
    )(chunks, pos_flat, w1b, b1r, w2b, w2t, b2r, b2t)


def _accumulate(s, mt, vt, m, acc):
    m_new = jnp.maximum(m, mt)
    alpha = jnp.exp2(m - m_new)
    p = jnp.exp2(s - m_new).astype(BF16)
    return m_new, alpha * acc + jnp.dot(vt, p, preferred_element_type=F32)


def _nsa_kernel(qt_ref, ng_ref, kc_ref, vct_ref, kp_ref, vst_ref, vwt_ref, wsel_ref, e_ref, tri_ref, wpat_ref,
                cpat_ref, o_ref, bias_ref, w_ref, sbuf_ref, oc_ref):
    nq = Q_BLOCK
    nsub = qt_ref.shape[2] // nq
    ncol = NSA_GROUP * nq
    n_sel = bias_ref.shape[1]
    ncp = kc_ref.shape[2]
    kw = 2 * HEAD_DIM
    bpt = SEL_TK // SLC_BLOCK
    n_tiles_total = kp_ref.shape[1] // SEL_TK
    qb_first = pl.program_id(2) * nsub
    wlen = WINDOW + nq

    def prepare(u):
        qb = qb_first + u
        q0 = pl.multiple_of(qb * nq, nq)
        qt = qt_ref[0, :, u * nq:(u + 1) * nq] * (HEAD_DIM ** -0.5 * LOG2E)
        qs = jnp.concatenate([qt[r * HEAD_DIM:(r + 1) * HEAD_DIM, :] for r in range(NSA_GROUP)], axis=1)
        zq = jnp.zeros_like(qs)
        q_c = qs.astype(BF16)
        q_slc = jnp.concatenate([qs, zq], axis=0).astype(BF16)
        q_win = jnp.concatenate([zq, qs], axis=0).astype(BF16)
        tq = q0 + (lax.broadcasted_iota(jnp.int32, (1, ncol), 1) & (nq - 1))

        coff = pl.multiple_of(ncp - qb * (nq // CMP_STRIDE), nq // CMP_STRIDE)
        tq1 = q0 + lax.broadcasted_iota(jnp.int32, (1, nq), 1)
        cur = jnp.right_shift(tq1, 6).astype(F32)
        first_diag = (2 * qb).astype(F32)
        crow = min(CMP_CLASS_ROWS, ncp)
        cls = qb // (crow // (nq // CMP_STRIDE))
        for c in range(ncp // crow):
            nc_c = crow * (c + 1)
            ns_c = nc_c * CMP_STRIDE // SLC_BLOCK

            @pl.when(cls == c)
            def _(nc_c=nc_c, ns_c=ns_c):
                s_c = (jnp.dot(kc_ref[0, 0, 0:nc_c, :], q_c, preferred_element_type=F32)
                       + jnp.concatenate([cpat_ref[pl.ds(coff, nc_c), :]] * NSA_GROUP, axis=1))
                m_c = jnp.max(s_c, axis=0, keepdims=True)
                e_c = jnp.exp2(s_c - m_c)
                l_c = jnp.sum(e_c, axis=0, keepdims=True)
                p_c = e_c * jnp.where(tq >= CMP_LEN - 1, 1.0 / l_c, 0.0)
                oc_ref[u] = jnp.dot(vct_ref[0, 0, :, 0:nc_c], p_c.astype(BF16), preferred_element_type=F32)

                p_grp = p_c[:, 0:nq]
                for r in range(1, NSA_GROUP):
                    p_grp = p_grp + p_c[:, r * nq:(r + 1) * nq]
                p_hi = p_grp.astype(BF16)
                res = p_grp - p_hi.astype(F32)
                p_mid = res.astype(BF16)
                p_lo = (res - p_mid.astype(F32)).astype(BF16)
                p3 = jnp.concatenate([p_hi, p_mid, p_lo], axis=1)
                sc3 = jnp.dot(wsel_ref[0:ns_c, 0:nc_c], p3, preferred_element_type=F32)
                score = sc3[:, 0:nq] + sc3[:, nq:2 * nq] + sc3[:, 2 * nq:3 * nq]
                j_io = lax.broadcasted_iota(jnp.int32, (ns_c, nq), 0).astype(F32)
                score = jnp.where(j_io <= cur, score, -1.0)
                score = jnp.where(j_io == 0.0, PICKED, score)
                score = jnp.where(j_io == cur, PICKED, score)
                score = jnp.where(j_io == cur - 1.0, PICKED, score)
                for _ in range(SLC_TOPN - 3):
                    mx = jnp.max(score, axis=0, keepdims=True)
                    first = jnp.min(jnp.where(score == mx, j_io, float(ns_c)), axis=0, keepdims=True)
                    score = jnp.where(j_io == first, PICKED, score)
                bias = jnp.where(j_io < first_diag, jnp.where(score == PICKED, 0.0, NEG), NEG)
                bias_ref[u, 0:ns_c, :] = bias.astype(BF16)
                if ns_c < n_sel:
                    bias_ref[u, ns_c:, :] = jnp.full((n_sel - ns_c, nq), NEG, BF16)

        o_c = oc_ref[u]

        wstart = pl.multiple_of(jnp.maximum(q0 - WINDOW, 0), nq)
        woff = pl.multiple_of(jnp.maximum(WINDOW - q0, 0), nq)
        s_w = jnp.dot(kp_ref[0, pl.ds(wstart, wlen), :], q_win, preferred_element_type=F32)
        s_w = s_w + jnp.concatenate([wpat_ref[pl.ds(woff, wlen), :]] * NSA_GROUP, axis=1)
        p_w = jnp.exp2(s_w - jnp.max(s_w, axis=0, keepdims=True)).astype(BF16)
        acc_w = jnp.dot(vwt_ref[0, :, pl.ds(wstart, wlen)], p_w, preferred_element_type=F32)
        o_w = acc_w[0:HEAD_DIM] / acc_w[HEAD_DIM:HEAD_DIM + 1]

        for slot in range(2):
            w_ref[slot, u, 0:kw, :] = q_slc
            w_ref[slot, u, kw:, :] = jnp.zeros((w_ref.shape[2] - kw, ncol), BF16)
        return q0, q_slc, o_c, o_w

    prepared = [prepare(u) for u in range(nsub)]

    def tile_start(j):
        jc = jnp.minimum(j, n_tiles_total - 1)
        return jc, pl.multiple_of(jc * SEL_TK, SEL_TK)

    def produce(j, slot):
        jc, k0 = tile_start(j)
        lhs = jnp.concatenate([kp_ref[0, pl.ds(k0, SEL_TK), :], e_ref[...]], axis=1)
        mts = []
        for u in range(nsub):
            b16 = bias_ref[u, pl.ds(pl.multiple_of(jc * bpt, bpt), bpt), :]
            w_ref[slot, u, kw:kw + bpt, :] = jnp.concatenate([b16] * NSA_GROUP, axis=1)
            s = jnp.dot(lhs, w_ref[slot, u], preferred_element_type=F32)
            sbuf_ref[slot, u] = s
            mts.append(jnp.max(s, axis=0, keepdims=True))
        return tuple(mts)

    def consume(j, slot, mts, state):
        _, k0 = tile_start(j)
        vt = vst_ref[0, :, pl.ds(k0, SEL_TK)]
        return tuple(_accumulate(sbuf_ref[slot, u], mts[u], vt, *state[u]) for u in range(nsub))

    def pair(i, carry, last):
        mt_a, state = carry
        mt_b = produce(2 * i + 1, 1)
        state = consume(2 * i, 0, mt_a, state)
        if last:
            return consume(2 * i + 1, 1, mt_b, state)
        mt_a = produce(2 * i + 2, 0)
        state = consume(2 * i + 1, 1, mt_b, state)
        return mt_a, state

    n_main = ((qb_first + nsub - 1) * nq + (SEL_TK - 1)) // SEL_TK
    n_pairs = jnp.maximum((n_main + 1) // 2, 1)
    state = tuple((jnp.full((1, ncol), NEG, F32), jnp.zeros((vst_ref.shape[1], ncol), F32)) for _ in range(nsub))
    carry = lax.fori_loop(0, n_pairs - 1, lambda i, c: pair(i, c, False), (produce(0, 0), state))
    state = pair(n_pairs - 1, carry, True)

    for u in range(nsub):
        q0, q_slc, o_c, o_w = prepared[u]
        s_d = jnp.dot(kp_ref[0, pl.ds(q0, nq), :], q_slc, preferred_element_type=F32) + tri_ref[...]
        _, acc_s = _accumulate(s_d, jnp.max(s_d, axis=0, keepdims=True), vst_ref[0, :, pl.ds(q0, nq)], *state[u])
        o_s = acc_s[0:HEAD_DIM] / acc_s[HEAD_DIM:HEAD_DIM + 1]
        gts = jax.nn.sigmoid(ng_ref[0, :, u * nq:(u + 1) * nq])
        rows = []
        for r in range(NSA_GROUP):
            sl = slice(r * nq, (r + 1) * nq)
            rows.append(gts[3 * r:3 * r + 1, :] * o_c[:, sl] + gts[3 * r + 1:3 * r + 2, :] * o_s[:, sl]
                        + gts[3 * r + 2:3 * r + 3, :] * o_w[:, sl])
        y_t = jnp.concatenate(rows, axis=0)
        o_ref[0, u * nq:(u + 1) * nq, :] = y_t.T


def _nsa_constants(ncp):
    nq = Q_BLOCK
    ql = jnp.arange(nq)[None, :]
    r = jnp.arange(SEL_TK)[:, None]
    e = (r // SLC_BLOCK == jnp.arange(2 * HEAD_DIM)[None, :]).astype(BF16)
    rd = jnp.arange(nq)[:, None]
    tri = jnp.tile(jnp.where(rd <= ql, 0.0, NEG).astype(F32), (1, NSA_GROUP))
    rw = jnp.arange(2 * WINDOW + nq)[:, None]
    wpat = jnp.where((rw > ql) & (rw <= ql + WINDOW), 0.0, NEG).astype(F32)
    rc = jnp.arange(2 * ncp)[:, None] - ncp
    cpat = jnp.where(CMP_STRIDE * rc + (CMP_LEN - 1) <= ql, 0.0, NEG).astype(F32)
    return e, tri, wpat, cpat


def nsa_attention(proj_t, kc, vct, kpack, vt_pack, wsel):
    b, _, t = proj_t.shape
    ncp = kc.shape[2]
    n_sel = t // SLC_BLOCK
    gq = NSA_GROUP * HEAD_DIM
    qstep = NSA_QSUB * Q_BLOCK
    e, tri, wpat, cpat = _nsa_constants(ncp)
    const2 = lambda i, g, q: (0, 0)
    return pl.pallas_call(
        _nsa_kernel,
        grid=(b, NSA_KV_GROUPS, t // qstep),
        in_specs=[
            pl.BlockSpec((1, gq, qstep), lambda i, g, q: (i, g, q)),
            pl.BlockSpec((1, 16, qstep), lambda i, g, q: (i, TR_NG // 16 + g, q)),
            pl.BlockSpec((1, 1, ncp, HEAD_DIM), lambda i, g, q: (i, g, 0, 0)),
            pl.BlockSpec((1, 1, HEAD_DIM, ncp), lambda i, g, q: (i, g, 0, 0)),
            pl.BlockSpec((1, t, 2 * HEAD_DIM), lambda i, g, q: (i, 0, g)),
            pl.BlockSpec((1, VT_ROWS, t), lambda i, g, q: (i, g, 0)),
            pl.BlockSpec((1, VT_ROWS, t), lambda i, g, q: (i, NSA_KV_GROUPS + g, 0)),
            pl.BlockSpec((n_sel, ncp), const2),
            pl.BlockSpec(e.shape, const2),
            pl.BlockSpec(tri.shape, const2),
            pl.BlockSpec(wpat.shape, const2),
            pl.BlockSpec(cpat.shape, const2),
        ],
        out_specs=pl.BlockSpec((1, qstep, gq), lambda i, g, q: (i, q, g)),
        out_shape=jax.ShapeDtypeStruct((b, t, NSA_HEADS * HEAD_DIM), F32),
        scratch_shapes=[pltpu.VMEM((NSA_QSUB, n_sel, Q_BLOCK), BF16),
                        pltpu.VMEM((2, NSA_QSUB, 4 * HEAD_DIM, NSA_GROUP * Q_BLOCK), BF16),
                        pltpu.VMEM((2, NSA_QSUB, SEL_TK, NSA_GROUP * Q_BLOCK), F32),
                        pltpu.VMEM((NSA_QSUB, HEAD_DIM, NSA_GROUP * Q_BLOCK), F32)],
        compiler_params=_cparams(3),
        name="nsa_attention",
    )(proj_t, proj_t, kc, vct, kpack, vt_pack, vt_pack, wsel, e, tri, wpat, cpat)


def _selection_weights(n_sel, ncp):
    j = jnp.arange(n_sel)[:, None]
    n = jnp.arange(ncp)[None, :]
    d = n - 4 * j
    w = jnp.where((d == -1) | (d == 3), 1.0, 0.0) + jnp.where((d >= 0) & (d <= 2), 2.0, 0.0)
    w = jnp.where(n < ncp - 1, w, 0.0)
    return w.astype(BF16)


def _conv_kernel(cb_ref, cc_ref, ch_ref, pc_ref, ph_ref, w_ref, o_ref):
    i = pl.program_id(1)
    u = cc_ref[0] * ch_ref[0]
    prev = pc_ref[0] * ph_ref[0]
    prev = jnp.where(i > 0, prev, 0.0)
    tm = u.shape[0]
    row = lax.broadcasted_iota(jnp.int32, u.shape, 0)
    p1 = prev[7:8, :]
    p2 = prev[6:7, :]
    u1 = jnp.where(row == 0, p1, pltpu.roll(u, 1, 0))
    u2 = jnp.where(row == 0, p2, jnp.where(row == 1, p1, pltpu.roll(u, 2, 0)))
    w = w_ref[...]
    o_ref[0] = cb_ref[0] * (w[0:1, :] * u2 + w[1:2, :] * u1 + w[2:3, :] * u)


def short_conv(hn, conv_w, tm):
    b, t, _ = hn.shape
    w = BRANCH_WIDTH
    col = lambda c: (lambda i, j: (i, j, c // w))
    hal = lambda c: (lambda i, j: (i, jnp.maximum(j * (tm // 8) - 1, 0), c // w))
    wpad = jnp.zeros((8, w), F32).at[:CONV_K].set(conv_w)
    return pl.pallas_call(
        _conv_kernel,
        grid=(b, t // tm),
        in_specs=[
            pl.BlockSpec((1, tm, w), col(NAT_CB)),
            pl.BlockSpec((1, tm, w), col(NAT_CC)),
            pl.BlockSpec((1, tm, w), col(NAT_CH)),
            pl.BlockSpec((1, 8, w), hal(NAT_CC)),
            pl.BlockSpec((1, 8, w), hal(NAT_CH)),
            pl.BlockSpec((8, w), lambda i, j: (0, 0)),
        ],
        out_specs=pl.BlockSpec((1, tm, w), lambda i, j: (i, j, 0)),
        out_shape=jax.ShapeDtypeStruct((b, t, w), F32),
        compiler_params=_cparams(2),
        name="short_conv",
    )(hn, hn, hn, hn, hn, wpad)


def _ret_kernel(rq_ref, rk_ref, rv_ref, rg_ref, cos_ref, sin_ref, dec_ref, qd_ref, kd_ref, cd_ref,
                gg_ref, gb_ref, o_ref, st_ref):
    c = RET_CHUNK
    hd = RET_HEAD_DIM

    @pl.when(pl.program_id(1) == 0)
    def _():
        st_ref[...] = jnp.zeros_like(st_ref)

    n_chunks = rq_ref.shape[1] // c

    def chunk(ci, _):
        r0 = pl.multiple_of(ci * c, c)
        cos = cos_ref[pl.ds(r0, c), :]
        sin = sin_ref[pl.ds(r0, c), :]
        for h in range(RET_HEADS):
            hs = slice(h * hd, (h + 1) * hd)
            q = rq_ref[0, pl.ds(r0, c), hs]
            k = rk_ref[0, pl.ds(r0, c), hs]
            v = rv_ref[0, pl.ds(r0, c), hs]
            q = q * cos + pltpu.roll(q, hd // 2, 1) * sin
            k = (k * cos + pltpu.roll(k, hd // 2, 1) * sin) * (hd ** -0.5)
            qb = q.astype(BF16)
            kb = k.astype(BF16)
            vb = v.astype(BF16)
            inner = lax.dot_general(qb, kb, (((1,), (1,)), ((), ())), preferred_element_type=F32) * dec_ref[h]
            st = st_ref[h]
            o = (jnp.dot(inner.astype(BF16), vb, preferred_element_type=F32)
                 + jnp.dot(qb, st.astype(BF16), preferred_element_type=F32) * qd_ref[h])
            kdt = (k * kd_ref[h]).T.astype(BF16)
            st_ref[h] = st * cd_ref[h] + jnp.dot(kdt, vb, preferred_element_type=F32)
            mu = jnp.mean(o, axis=-1, keepdims=True)
            d = o - mu
            var = jnp.mean(d * d, axis=-1, keepdims=True)
            y = d * lax.rsqrt(var + LN_EPS) * gg_ref[:, hs] + gb_ref[:, hs]
            o_ref[0, pl.ds(r0, c), hs] = jax.nn.silu(rg_ref[0, pl.ds(r0, c), hs]) * y
        return 0

    lax.fori_loop(0, n_chunks, chunk, 0)


def retention_branch(hn, gn_g, gn_b, tr):
    b, t, _ = hn.shape
    w = BRANCH_WIDTH
    hd = RET_HEAD_DIM
    c = RET_CHUNK
    inv = ROPE_BASE ** (-jnp.arange(0, hd, 2, dtype=F32) / hd)
    ang = jnp.arange(t, dtype=F32)[:, None] * inv[None, :]
    cos2 = jnp.concatenate([jnp.cos(ang), jnp.cos(ang)], axis=1)
    sin2 = jnp.concatenate([-jnp.sin(ang), jnp.sin(ang)], axis=1)
    log_g = jnp.log(1.0 - 2.0 ** (-5.0 - jnp.arange(RET_HEADS, dtype=F32)))
    idx = jnp.arange(c, dtype=F32)
    diff = idx[:, None] - idx[None, :]
    decay_in = jnp.where(diff >= 0, jnp.exp(log_g[:, None, None] * jnp.maximum(diff, 0.0)), 0.0)
    ones = jnp.ones((RET_HEADS, c, hd), F32)
    q_dec = jnp.exp(log_g[:, None] * (idx + 1.0))[:, :, None] * ones
    k_dec = jnp.exp(log_g[:, None] * (c - 1.0 - idx))[:, :, None] * ones
    c_dec = jnp.exp(log_g * c)[:, None, None] * ones
    col = lambda cc: (lambda i, j: (i, j, cc // w))
    full3 = lambda i, j: (0, 0, 0)
    return pl.pallas_call(
        _ret_kernel,
        grid=(b, t // tr),
        in_specs=[
            pl.BlockSpec((1, tr, w), col(NAT_RQ)),
            pl.BlockSpec((1, tr, w), col(NAT_RK)),
            pl.BlockSpec((1, tr, w), col(NAT_RV)),
            pl.BlockSpec((1, tr, w), col(NAT_RG)),
            pl.BlockSpec((tr, hd), lambda i, j: (j, 0)),
            pl.BlockSpec((tr, hd), lambda i, j: (j, 0)),
            pl.BlockSpec((RET_HEADS, c, c), full3),
            pl.BlockSpec((RET_HEADS, c, hd), full3),
            pl.BlockSpec((RET_HEADS, c, hd), full3),
            pl.BlockSpec((RET_HEADS, c, hd), full3),
            pl.BlockSpec((1, w), lambda i, j: (0, 0)),
            pl.BlockSpec((1, w), lambda i, j: (0, 0)),
        ],
        out_specs=pl.BlockSpec((1, tr, w), lambda i, j: (i, j, 0)),
        out_shape=jax.ShapeDtypeStruct((b, t, w), F32),
        scratch_shapes=[pltpu.VMEM((RET_HEADS, hd, hd), F32)],
        compiler_params=_cparams(2),
        name="retention",
    )(hn, hn, hn, hn, cos2, sin2, decay_in, q_dec, k_dec, c_dec, gn_g.reshape(1, w), gn_b.reshape(1, w))


def _layer_norm(z, g, b):
    mu = jnp.mean(z, axis=-1, keepdims=True)
    d = z - mu
    var = jnp.mean(d * d, axis=-1, keepdims=True)
    return d * lax.rsqrt(var + LN_EPS) * g + b


def _merge_kernel(x_ref, ya_ref, yb_ref, yc_ref, mg_ref, wbr_ref, wout_ref, g_ref, b_ref, o_ref):
    mix = None
    for c, y_ref in enumerate((ya_ref, yb_ref, yc_ref)):
        proj = jnp.dot(y_ref[...].astype(BF16), wbr_ref[c], preferred_element_type=F32)
        gate = jax.nn.sigmoid(mg_ref[:, c * D_MODEL:(c + 1) * D_MODEL])
        mix = gate * proj if mix is None else mix + gate * proj
    m = jnp.dot(mix.astype(BF16), wout_ref[...], preferred_element_type=F32)
    o_ref[...] = _layer_norm(ALPHA * x_ref[...] + m, g_ref[...], b_ref[...])


def merge_branches(x, ya, yb, yc, hn, w_br, w_out, g, b, tm):
    m = x.shape[0]
    d = D_MODEL
    w = BRANCH_WIDTH
    row = lambda i: (i, 0)
    return pl.pallas_call(
        _merge_kernel,
        grid=(m // tm,),
        in_specs=[
            pl.BlockSpec((tm, d), row),
            pl.BlockSpec((tm, w), row),
            pl.BlockSpec((tm, w), row),
            pl.BlockSpec((tm, w), row),
            pl.BlockSpec((tm, 3 * d), row),
            pl.BlockSpec((3, w, d), lambda i: (0, 0, 0)),
            pl.BlockSpec((d, d), lambda i: (0, 0)),
            pl.BlockSpec((1, d), lambda i: (0, 0)),
            pl.BlockSpec((1, d), lambda i: (0, 0)),
        ],
        out_specs=pl.BlockSpec((tm, d), row),
        out_shape=jax.ShapeDtypeStruct((m, d), F32),
        compiler_params=_cparams(1),
        name="merge",
    )(x, ya, yb, yc, hn, w_br.astype(BF16), w_out.astype(BF16), g.reshape(1, d), b.reshape(1, d))


def _mlp_kernel(te_ref, tv_ref, x_ref, wg_ref, wu_ref, wd_ref, o_ref):
    i = pl.program_id(0)
    f = pl.program_id(1)

    @pl.when(tv_ref[i] > 0)
    def _():
        x = x_ref[...]
        gte = jnp.dot(x, wg_ref[0], preferred_element_type=F32)
        up = jnp.dot(x, wu_ref[0], preferred_element_type=F32)
        h = (jax.nn.silu(gte) * up).astype(BF16)
        part = jnp.dot(h, wd_ref[0], preferred_element_type=F32)

        @pl.when(f == 0)
        def _():
            o_ref[...] = part

        @pl.when(f > 0)
        def _():
            o_ref[...] = o_ref[...] + part

    @pl.when((tv_ref[i] == 0) & (f == 0))
    def _():
        o_ref[...] = jnp.zeros_like(o_ref)


def grouped_swiglu(xs, w_gate, w_up, w_down, tile_expert, tile_valid, tm, tf):
    n, d = xs.shape
    ff = w_gate.shape[2]
    grid_spec = pltpu.PrefetchScalarGridSpec(
        num_scalar_prefetch=2,
        grid=(n // tm, ff // tf),
        in_specs=[
            pl.BlockSpec((tm, d), lambda i, f, te, tv: (i, 0)),
            pl.BlockSpec((1, d, tf), lambda i, f, te, tv: (te[i], 0, f)),
            pl.BlockSpec((1, d, tf), lambda i, f, te, tv: (te[i], 0, f)),
            pl.BlockSpec((1, tf, d), lambda i, f, te, tv: (te[i], f, 0)),
        ],
        out_specs=pl.BlockSpec((tm, d), lambda i, f, te, tv: (i, 0)),
    )
    return pl.pallas_call(
        _mlp_kernel,
        grid_spec=grid_spec,
        out_shape=jax.ShapeDtypeStruct((n, d), F32),
        compiler_params=_cparams(2),
        name="grouped_swiglu",
    )(tile_expert, tile_valid, xs, w_gate, w_up, w_down)


def _add_ln_kernel(*refs):
    x_ref, sc_ref, *adds, g_ref, b_ref, o_ref = refs
    z = ALPHA * x_ref[...]
    for k, a in enumerate(adds):
        z = z + sc_ref[:, k:k + 1] * a[...]
    o_ref[...] = _layer_norm(z, g_ref[...], b_ref[...])


def add_layer_norm(x, adds, scales, g, b, tm):
    m, d = x.shape
    row = lambda i: (i, 0)
    return pl.pallas_call(
        _add_ln_kernel,
        grid=(m // tm,),
        in_specs=([pl.BlockSpec((tm, d), row), pl.BlockSpec((tm, len(adds)), row)]
                  + [pl.BlockSpec((tm, d), row)] * len(adds) + [pl.BlockSpec((1, d), lambda i: (0, 0))] * 2),
        out_specs=pl.BlockSpec((tm, d), row),
        out_shape=jax.ShapeDtypeStruct((m, d), F32),
        compiler_params=_cparams(1),
        name="add_layer_norm",
    )(x, scales, *adds, g.reshape(1, d), b.reshape(1, d))


def _router_kernel(x_ref, wh_ref, wl_ref, o_ref):
    x = x_ref[...]
    xh = x.astype(BF16)
    xl = (x - xh.astype(F32)).astype(BF16)
    wh = wh_ref[...]
    o_ref[...] = (jnp.dot(xh, wh, preferred_element_type=F32) + jnp.dot(xl, wh, preferred_element_type=F32)
                  + jnp.dot(xh, wl_ref[...], preferred_element_type=F32))


def router_logits(x, w_router, tm):
    m, d = x.shape
    wp = jnp.zeros((d, 128), F32).at[:, :N_EXPERTS].set(w_router)
    wh = wp.astype(BF16)
    wl = (wp - wh.astype(F32)).astype(BF16)
    return pl.pallas_call(
        _router_kernel,
        grid=(m // tm,),
        in_specs=[pl.BlockSpec((tm, d), lambda i: (i, 0)), pl.BlockSpec((d, 128), lambda i: (0, 0)),
                  pl.BlockSpec((d, 128), lambda i: (0, 0))],
        out_specs=pl.BlockSpec((tm, 128), lambda i: (i, 0)),
        out_shape=jax.ShapeDtypeStruct((m, 128), F32),
        compiler_params=_cparams(1),
        name="router",
    )(x, wh, wl)


def dense_ffn(x, w_gate, w_up, w_down, g, b):
    m = x.shape[0]
    tm = min(1024, m)
    nt = m // tm
    ys = grouped_swiglu(x.astype(BF16), w_gate[None].astype(BF16), w_up[None].astype(BF16),
                        w_down[None].astype(BF16),
                        jnp.zeros((nt,), jnp.int32), jnp.ones((nt,), jnp.int32), tm, 896)
    return add_layer_norm(x, [ys], jnp.ones((m, 1), F32), g, b, min(1024, m))


def moe_ffn(x, w_router, w_gate, w_up, w_down, g, b):
    m = x.shape[0]
    tm = 512
    logits = router_logits(x, w_router, min(1024, m))[:, :N_EXPERTS]
    top_v, top_i = lax.top_k(logits, TOP_K)
    probs = jax.nn.softmax(top_v, axis=-1)
    e_flat = top_i.reshape(-1)
    onehot = (e_flat[:, None] == jnp.arange(N_EXPERTS)[None, :]).astype(jnp.int32)
    csum = jnp.cumsum(onehot, axis=0)
    counts = csum[-1]
    rank = jnp.take_along_axis(csum, e_flat[:, None], axis=1)[:, 0] - 1
    padded = ((counts + tm - 1) // tm) * tm
    ends = jnp.cumsum(padded)
    starts = ends - padded
    pos = starts[e_flat] + rank
    n_slots = TOP_K * m + N_EXPERTS * tm
    n_tiles = n_slots // tm
    tok = jnp.arange(TOP_K * m, dtype=jnp.int32) // TOP_K
    src = jnp.zeros((n_slots,), jnp.int32).at[pos].set(tok)
    tile_start = jnp.arange(n_tiles, dtype=jnp.int32) * tm
    tile_expert = jnp.minimum(jnp.searchsorted(ends, tile_start, side="right"), N_EXPERTS - 1).astype(jnp.int32)
    tile_valid = (tile_start < ends[-1]).astype(jnp.int32)
    xs = jnp.take(x.astype(BF16), src, axis=0)
    ys = grouped_swiglu(xs, w_gate.astype(BF16), w_up.astype(BF16), w_down.astype(BF16),
                        tile_expert, tile_valid, tm, 1792)
    pos2 = pos.reshape(m, TOP_K)
    y0 = jnp.take(ys, pos2[:, 0], axis=0)
    y1 = jnp.take(ys, pos2[:, 1], axis=0)
    return add_layer_norm(x, [y0, y1], probs, g, b, min(1024, m))


def _pack_in_weights(w):
    d = w.shape[0]
    o = 0
    q = w[:, o:o + 512]; o += 512
    kv = w[:, o:o + 768].reshape(d, 3, 2, NSA_KV_GROUPS, HEAD_DIM); o += 768
    ng = w[:, o:o + 24]; o += 24
    rest = w[:, o:o + 7 * 512]; o += 7 * 512
    mg = w[:, o:]
    kpack = jnp.concatenate([kv[:, 1, 0], kv[:, 2, 0]], axis=-1).reshape(d, 2 * NSA_KV_GROUPS * HEAD_DIM)
    kc = kv[:, 0, 0].reshape(d, NSA_KV_GROUPS * HEAD_DIM)
    vc = kv[:, 0, 1].reshape(d, NSA_KV_GROUPS * HEAD_DIM)
    w_nat = jnp.concatenate([mg, rest, kpack, kc, vc], axis=1).astype(BF16)
    vs = kv[:, 1, 1].reshape(d, NSA_KV_GROUPS * HEAD_DIM)
    vw = kv[:, 2, 1].reshape(d, NSA_KV_GROUPS * HEAD_DIM)
    ngp = jnp.pad(ng.reshape(d, NSA_KV_GROUPS, NSA_GROUP * 3), ((0, 0), (0, 0), (0, 4))).reshape(d, 32)
    w_t = jnp.concatenate([q, vs, vw, ngp], axis=1).T.astype(BF16)
    return w_nat, w_t


def token_mixer_ln(x, w_in, cmp_pos, cmp_w1, cmp_b1, cmp_w2, cmp_b2, conv_w, gn_g, gn_b, w_br, w_out, ln_g, ln_b):
    b, t, d = x.shape
    m = b * t
    w_nat, w_t = _pack_in_weights(w_in)
    x2 = x.reshape(m, d)
    hn = mm_nat(x2, w_nat, min(1024, m), 1024).reshape(b, t, NAT_COLS)
    proj_t = mm_nt(w_t, x, min(1024, t))
    nch = t // CMP_STRIDE
    raw = hn[:, :, NAT_KC:NAT_KC + 2 * NSA_KV_GROUPS * HEAD_DIM].reshape(b, t, 2, NSA_KV_GROUPS, HEAD_DIM)
    chunks = raw.transpose(2, 0, 3, 1, 4).reshape(2, b, NSA_KV_GROUPS, nch, CMP_STRIDE * HEAD_DIM)
    cn, ct = nsa_compress_kv(chunks, cmp_pos, cmp_w1, cmp_b1, cmp_w2, cmp_b2)
    kc = cn[0].astype(BF16)
    vct = ct[1].astype(BF16)
    kpack = hn[:, :, NAT_KP:NAT_KP + 2 * NSA_KV_GROUPS * HEAD_DIM].astype(BF16)
    vt4 = proj_t[:, TR_VS:TR_NG, :].astype(BF16).reshape(b, 4, HEAD_DIM, t)
    ones_pad = jnp.zeros((b, 4, VT_ROWS - HEAD_DIM, t), BF16).at[:, :, 0].set(1.0)
    vt_pack = jnp.concatenate([vt4, ones_pad], axis=2).reshape(b, 4 * VT_ROWS, t)
    wsel = _selection_weights(t // SLC_BLOCK, nch)
    y_a = nsa_attention(proj_t, kc, vct, kpack, vt_pack, wsel)
    y_b = short_conv(hn, conv_w, min(1024, t))
    y_c = retention_branch(hn, gn_g, gn_b, min(1024, t))
    out = merge_branches(x2, y_a.reshape(m, -1), y_b.reshape(m, -1), y_c.reshape(m, -1), hn.reshape(m, NAT_COLS),
                         w_br, w_out, ln_g, ln_b, min(512, m))
    return out


def kernel(x, w_in, cmp_pos, cmp_w1, cmp_b1, cmp_w2, cmp_b2, conv_w, ret_gn_g, ret_gn_b, w_br, w_out,
           ln1_g, ln1_b, ln2_g, ln2_b, ffn_gate, ffn_up, ffn_down, moe_router, moe_gate, moe_up, moe_down):
    b, t, d = x.shape
    for l in range(DEPTH):
        x2 = token_mixer_ln(x, w_in[l], cmp_pos[l], cmp_w1[l], cmp_b1[l], cmp_w2[l], cmp_b2[l], conv_w[l],
                            ret_gn_g[l], ret_gn_b[l], w_br[l], w_out[l], ln1_g[l], ln1_b[l])
        if l % 2 == 0:
            x2 = dense_ffn(x2, ffn_gate[l // 2], ffn_up[l // 2], ffn_down[l // 2], ln2_g[l], ln2_b[l])
        else:
            x2 = moe_ffn(x2, moe_router[l // 2], moe_gate[l // 2], moe_up[l // 2], moe_down[l // 2],
                         ln2_g[l], ln2_b[l])
        x = x2.reshape(b, t, d)
    return x
```

```python
import functools
import math

import jax
import jax.numpy as jnp
from jax import lax
from jax.experimental import pallas as pl
from jax.experimental.pallas import tpu as pltpu

F32 = jnp.float32
BF16 = jnp.bfloat16

D_MODEL = 1024
DEPTH = 4
BRANCH_WIDTH = 512
HEAD_DIM = 64
NSA_HEADS = 8
NSA_KV_GROUPS = 2
NSA_GROUP = 4
CMP_LEN = 32
CMP_STRIDE = 16
SLC_BLOCK = 64
SLC_TOPN = 16
WINDOW = 512
CMP_HIDDEN = 256
Q_BLOCK = 128
CONV_K = 3
CONV_HALO = 16
RET_HEADS = 4
RET_HEAD_DIM = 128
RET_CHUNK = 128
ROPE_BASE = 10000.0
D_FF = 3584
N_EXPERTS = 8
TOP_K = 2
LN_EPS = 1e-5
ALPHA = (2.0 * DEPTH) ** 0.25

NAT_MG = 0
NAT_CB = 3072
NAT_CC = 3584
NAT_CH = 4096
NAT_RQ = 4608
NAT_RK = 5120
NAT_RV = 5632
NAT_RG = 6144
NAT_KP = 6656
NAT_KC = 6912
NAT_VC = 7040
NAT_COLS = 7168
TR_Q = 0
TR_VS = 512
TR_VW = 640
TR_NG = 768
TR_ROWS = 800

NEG = -1e30
SEL_TK = 1024
LOG2E = 1.4426950408889634
NSA_QSUB = 2
CMP_CLASS_ROWS = 256
PICKED = -3e38
VT_ROWS = 80
VMEM_LIMIT = 48 * 1024 * 1024


def _cparams(n_axes, vmem=VMEM_LIMIT):
    return pltpu.CompilerParams(dimension_semantics=("arbitrary",) * n_axes, vmem_limit_bytes=vmem)


def _mm_nat_kernel(x_ref, w_ref, o_ref, xb_ref):
    @pl.when(pl.program_id(1) == 0)
    def _():
        xb_ref[...] = x_ref[...].astype(BF16)

    o_ref[...] = jnp.dot(xb_ref[...], w_ref[...], preferred_element_type=F32).astype(o_ref.dtype)


def mm_nat(x, w, tm, tn, out_dtype=F32):
    m, k = x.shape
    n = w.shape[1]
    return pl.pallas_call(
        _mm_nat_kernel,
        grid=(m // tm, n // tn),
        in_specs=[pl.BlockSpec((tm, k), lambda i, j: (i, 0)), pl.BlockSpec((k, tn), lambda i, j: (0, j))],
        out_specs=pl.BlockSpec((tm, tn), lambda i, j: (i, j)),
        out_shape=jax.ShapeDtypeStruct((m, n), out_dtype),
        scratch_shapes=[pltpu.VMEM((tm, k), BF16)],
        compiler_params=_cparams(2),
        name="mm_nat",
    )(x, w)


def _mm_nt_kernel(w_ref, x_ref, o_ref):
    o_ref[0] = lax.dot_general(w_ref[...], x_ref[0].astype(BF16), (((1,), (1,)), ((), ())),
                               preferred_element_type=F32)


def mm_nt(w_t, x, tm):
    b, t, k = x.shape
    r = w_t.shape[0]
    return pl.pallas_call(
        _mm_nt_kernel,
        grid=(b, t // tm),
        in_specs=[pl.BlockSpec((r, k), lambda i, j: (0, 0)), pl.BlockSpec((1, tm, k), lambda i, j: (i, j, 0))],
        out_specs=pl.BlockSpec((1, r, tm), lambda i, j: (i, 0, j)),
        out_shape=jax.ShapeDtypeStruct((b, r, t), F32),
        compiler_params=_cparams(2),
        name="mm_nt",
    )(w_t, x)


def _cmp_kernel(c_ref, pos_ref, w1_ref, b1_ref, w2_ref, w2t_ref, b2_ref, b2t_ref, on_ref, ot_ref):
    half = CMP_STRIDE * HEAD_DIM
    c = c_ref[0, 0, 0].astype(BF16)
    w1 = w1_ref[0]
    a = jnp.dot(c, w1[:half], preferred_element_type=F32)
    bm = jnp.dot(c, w1[half:], preferred_element_type=F32)
    nch = a.shape[0]
    bm = pltpu.roll(bm, nch - 1, 0)
    posb = jnp.dot(pos_ref[0].astype(BF16), w1, preferred_element_type=F32)[0:1]
    hid = jax.nn.gelu(a + bm + posb + b1_ref[0])
    hb = hid.astype(BF16)
    on_ref[0, 0, 0] = jnp.dot(hb, w2_ref[0], preferred_element_type=F32) + b2_ref[0]
    ot_ref[0, 0, 0] = lax.dot_general(w2t_ref[0], hb, (((1,), (1,)), ((), ())),
                                      preferred_element_type=F32) + b2t_ref[0]


def nsa_compress_kv(chunks, pos, w1, b1, w2, b2):
    _, b, g, nch, cw = chunks.shape
    pos_flat = jnp.broadcast_to(pos.reshape(2, 1, CMP_LEN * HEAD_DIM), (2, 8, CMP_LEN * HEAD_DIM))
    w1b = w1.astype(BF16)
    w2b = w2.astype(BF16)
    w2t = jnp.swapaxes(w2, 1, 2).astype(BF16)
    b1r = b1.reshape(2, 1, CMP_HIDDEN)
    b2r = b2.reshape(2, 1, HEAD_DIM)
    b2t = b2.reshape(2, HEAD_DIM, 1)
    sel = lambda kv, i, j: (kv, 0, 0)
    return pl.pallas_call(
        _cmp_kernel,
        grid=(2, b, g),
        in_specs=[
            pl.BlockSpec((1, 1, 1, nch, cw), lambda kv, i, j: (kv, i, j, 0, 0)),
            pl.BlockSpec((1, 8, CMP_LEN * HEAD_DIM), sel),
            pl.BlockSpec((1, CMP_LEN * HEAD_DIM, CMP_HIDDEN), sel),
            pl.BlockSpec((1, 1, CMP_HIDDEN), sel),
            pl.BlockSpec((1, CMP_HIDDEN, HEAD_DIM), sel),
            pl.BlockSpec((1, HEAD_DIM, CMP_HIDDEN), sel),
            pl.BlockSpec((1, 1, HEAD_DIM), sel),
            pl.BlockSpec((1, HEAD_DIM, 1), sel),
        ],
        out_specs=[
            pl.BlockSpec((1, 1, 1, nch, HEAD_DIM), lambda kv, i, j: (kv, i, j, 0, 0)),
            pl.BlockSpec((1, 1, 1, HEAD_DIM, nch), lambda kv, i, j: (kv, i, j, 0, 0)),
        ],
        out_shape=[
            jax.ShapeDtypeStruct((2, b, g, nch, HEAD_DIM), F32),
            jax.ShapeDtypeStruct((2, b, g, HEAD_DIM, nch), F32),
        ],
        compiler_params=_cparams(3),
        name="nsa_compress",
    )(chunks, pos_flat, w1b, b1r, w2b, w2t, b2r, b2t)


def _accumulate(s, mt, vt, m, acc):
    m_new = jnp.maximum(m, mt)
    alpha = jnp.exp2(m - m_new)
    p = jnp.exp2(s - m_new).astype(BF16)
    return m_new, alpha * acc + jnp.dot(vt, p, preferred_element_type=F32)


def _nsa_kernel(qt_ref, ng_ref, kc_ref, vct_ref, kp_ref, vst_ref, vwt_ref, wsel_ref, e_ref, tri_ref, wpat_ref,
                cpat_ref, o_ref, bias_ref, w_ref, sbuf_ref, oc_ref):
    nq = Q_BLOCK
    nsub = qt_ref.shape[2] // nq
    ncol = NSA_GROUP * nq
    n_sel = bias_ref.shape[1]
    ncp = kc_ref.shape[2]
    kw = 2 * HEAD_DIM
    bpt = SEL_TK // SLC_BLOCK
    n_tiles_total = kp_ref.shape[1] // SEL_TK
    qb_first = pl.program_id(2) * nsub
    wlen = WINDOW + nq

    def prepare(u):
        qb = qb_first + u
        q0 = pl.multiple_of(qb * nq, nq)
        qt = qt_ref[0, :, u * nq:(u + 1) * nq] * (HEAD_DIM ** -0.5 * LOG2E)
        qs = jnp.concatenate([qt[r * HEAD_DIM:(r + 1) * HEAD_DIM, :] for r in range(NSA_GROUP)], axis=1)
        zq = jnp.zeros_like(qs)
        q_c = qs.astype(BF16)
        q_slc = jnp.concatenate([qs, zq], axis=0).astype(BF16)
        q_win = jnp.concatenate([zq, qs], axis=0).astype(BF16)
        tq = q0 + (lax.broadcasted_iota(jnp.int32, (1, ncol), 1) & (nq - 1))

        coff = pl.multiple_of(ncp - qb * (nq // CMP_STRIDE), nq // CMP_STRIDE)
        tq1 = q0 + lax.broadcasted_iota(jnp.int32, (1, nq), 1)
        cur = jnp.right_shift(tq1, 6).astype(F32)
        first_diag = (2 * qb).astype(F32)
        crow = min(CMP_CLASS_ROWS, ncp)
        cls = qb // (crow // (nq // CMP_STRIDE))
        for c in range(ncp // crow):
            nc_c = crow * (c + 1)
            ns_c = nc_c * CMP_STRIDE // SLC_BLOCK

            @pl.when(cls == c)
            def _(nc_c=nc_c, ns_c=ns_c):
                s_c = (jnp.dot(kc_ref[0, 0, 0:nc_c, :], q_c, preferred_element_type=F32)
                       + jnp.concatenate([cpat_ref[pl.ds(coff, nc_c), :]] * NSA_GROUP, axis=1))
                m_c = jnp.max(s_c, axis=0, keepdims=True)
                e_c = jnp.exp2(s_c - m_c)
                l_c = jnp.sum(e_c, axis=0, keepdims=True)
                p_c = e_c * jnp.where(tq >= CMP_LEN - 1, 1.0 / l_c, 0.0)
                oc_ref[u] = jnp.dot(vct_ref[0, 0, :, 0:nc_c], p_c.astype(BF16), preferred_element_type=F32)

                p_grp = p_c[:, 0:nq]
                for r in range(1, NSA_GROUP):
                    p_grp = p_grp + p_c[:, r * nq:(r + 1) * nq]
                p_hi = p_grp.astype(BF16)
                res = p_grp - p_hi.astype(F32)
                p_mid = res.astype(BF16)
                p_lo = (res - p_mid.astype(F32)).astype(BF16)
                p3 = jnp.concatenate([p_hi, p_mid, p_lo], axis=1)
                sc3 = jnp.dot(wsel_ref[0:ns_c, 0:nc_c], p3, preferred_element_type=F32)
                score = sc3[:, 0:nq] + sc3[:, nq:2 * nq] + sc3[:, 2 * nq:3 * nq]
                j_io = lax.broadcasted_iota(jnp.int32, (ns_c, nq), 0).astype(F32)
                score = jnp.where(j_io <= cur, score, -1.0)
                score = jnp.where(j_io == 0.0, PICKED, score)
                score = jnp.where(j_io == cur, PICKED, score)
                score = jnp.where(j_io == cur - 1.0, PICKED, score)
                for _ in range(SLC_TOPN - 3):
                    mx = jnp.max(score, axis=0, keepdims=True)
                    first = jnp.min(jnp.where(score == mx, j_io, float(ns_c)), axis=0, keepdims=True)
                    score = jnp.where(j_io == first, PICKED, score)
                bias = jnp.where(j_io < first_diag, jnp.where(score == PICKED, 0.0, NEG), NEG)
                bias_ref[u, 0:ns_c, :] = bias.astype(BF16)
                if ns_c < n_sel:
                    bias_ref[u, ns_c:, :] = jnp.full((n_sel - ns_c, nq), NEG, BF16)

        o_c = oc_ref[u]

        wstart = pl.multiple_of(jnp.maximum(q0 - WINDOW, 0), nq)
        woff = pl.multiple_of(jnp.maximum(WINDOW - q0, 0), nq)
        s_w = jnp.dot(kp_ref[0, pl.ds(wstart, wlen), :], q_win, preferred_element_type=F32)
        s_w = s_w + jnp.concatenate([wpat_ref[pl.ds(woff, wlen), :]] * NSA_GROUP, axis=1)
        p_w = jnp.exp2(s_w - jnp.max(s_w, axis=0, keepdims=True)).astype(BF16)
        acc_w = jnp.dot(vwt_ref[0, :, pl.ds(wstart, wlen)], p_w, preferred_element_type=F32)
        o_w = acc_w[0:HEAD_DIM] / acc_w[HEAD_DIM:HEAD_DIM + 1]

        for slot in range(2):
            w_ref[slot, u, 0:kw, :] = q_slc
            w_ref[slot, u, kw:, :] = jnp.zeros((w_ref.shape[2] - kw, ncol), BF16)
        return q0, q_slc, o_c, o_w

    prepared = [prepare(u) for u in range(nsub)]

    def tile_start(j):
        jc = jnp.minimum(j, n_tiles_total - 1)
        return jc, pl.multiple_of(jc * SEL_TK, SEL_TK)

    def produce(j, slot):
        jc, k0 = tile_start(j)
        lhs = jnp.concatenate([kp_ref[0, pl.ds(k0, SEL_TK), :], e_ref[...]], axis=1)
        mts = []
        for u in range(nsub):
            b16 = bias_ref[u, pl.ds(pl.multiple_of(jc * bpt, bpt), bpt), :]
            w_ref[slot, u, kw:kw + bpt, :] = jnp.concatenate([b16] * NSA_GROUP, axis=1)
            s = jnp.dot(lhs, w_ref[slot, u], preferred_element_type=F32)
            sbuf_ref[slot, u] = s
            mts.append(jnp.max(s, axis=0, keepdims=True))
        return tuple(mts)

    def consume(j, slot, mts, state):
        _, k0 = tile_start(j)
        vt = vst_ref[0, :, pl.ds(k0, SEL_TK)]
        return tuple(_accumulate(sbuf_ref[slot, u], mts[u], vt, *state[u]) for u in range(nsub))

    def pair(i, carry, last):
        mt_a, state = carry
        mt_b = produce(2 * i + 1, 1)
        state = consume(2 * i, 0, mt_a, state)
        if last:
            return consume(2 * i + 1, 1, mt_b, state)
        mt_a = produce(2 * i + 2, 0)
        state = consume(2 * i + 1, 1, mt_b, state)
        return mt_a, state

    n_main = ((qb_first + nsub - 1) * nq + (SEL_TK - 1)) // SEL_TK
    n_pairs = jnp.maximum((n_main + 1) // 2, 1)
    state = tuple((jnp.full((1, ncol), NEG, F32), jnp.zeros((vst_ref.shape[1], ncol), F32)) for _ in range(nsub))
    carry = lax.fori_loop(0, n_pairs - 1, lambda i, c: pair(i, c, False), (produce(0, 0), state))
    state = pair(n_pairs - 1, carry, True)

    for u in range(nsub):
        q0, q_slc, o_c, o_w = prepared[u]
        s_d = jnp.dot(kp_ref[0, pl.ds(q0, nq), :], q_slc, preferred_element_type=F32) + tri_ref[...]
        _, acc_s = _accumulate(s_d, jnp.max(s_d, axis=0, keepdims=True), vst_ref[0, :, pl.ds(q0, nq)], *state[u])
        o_s = acc_s[0:HEAD_DIM] / acc_s[HEAD_DIM:HEAD_DIM + 1]
        gts = jax.nn.sigmoid(ng_ref[0, :, u * nq:(u + 1) * nq])
        rows = []
        for r in range(NSA_GROUP):
            sl = slice(r * nq, (r + 1) * nq)
            rows.append(gts[3 * r:3 * r + 1, :] * o_c[:, sl] + gts[3 * r + 1:3 * r + 2, :] * o_s[:, sl]
                        + gts[3 * r + 2:3 * r + 3, :] * o_w[:, sl])
        y_t = jnp.concatenate(rows, axis=0)
        o_ref[0, u * nq:(u + 1) * nq, :] = y_t.T


def _nsa_constants(ncp):
    nq = Q_BLOCK
    ql = jnp.arange(nq)[None, :]
    r = jnp.arange(SEL_TK)[:, None]
    e = (r // SLC_BLOCK == jnp.arange(2 * HEAD_DIM)[None, :]).astype(BF16)
    rd = jnp.arange(nq)[:, None]
    tri = jnp.tile(jnp.where(rd <= ql, 0.0, NEG).astype(F32), (1, NSA_GROUP))
    rw = jnp.arange(2 * WINDOW + nq)[:, None]
    wpat = jnp.where((rw > ql) & (rw <= ql + WINDOW), 0.0, NEG).astype(F32)
    rc = jnp.arange(2 * ncp)[:, None] - ncp
    cpat = jnp.where(CMP_STRIDE * rc + (CMP_LEN - 1) <= ql, 0.0, NEG).astype(F32)
    return e, tri, wpat, cpat


def nsa_attention(proj_t, kc, vct, kpack, vt_pack, wsel):
    b, _, t = proj_t.shape
    ncp = kc.shape[2]
    n_sel = t // SLC_BLOCK
    gq = NSA_GROUP * HEAD_DIM
    qstep = NSA_QSUB * Q_BLOCK
    e, tri, wpat, cpat = _nsa_constants(ncp)
    const2 = lambda i, g, q: (0, 0)
    return pl.pallas_call(
        _nsa_kernel,
        grid=(b, NSA_KV_GROUPS, t // qstep),
        in_specs=[
            pl.BlockSpec((1, gq, qstep), lambda i, g, q: (i, g, q)),
            pl.BlockSpec((1, 16, qstep), lambda i, g, q: (i, TR_NG // 16 + g, q)),
            pl.BlockSpec((1, 1, ncp, HEAD_DIM), lambda i, g, q: (i, g, 0, 0)),
            pl.BlockSpec((1, 1, HEAD_DIM, ncp), lambda i, g, q: (i, g, 0, 0)),
            pl.BlockSpec((1, t, 2 * HEAD_DIM), lambda i, g, q: (i, 0, g)),
            pl.BlockSpec((1, VT_ROWS, t), lambda i, g, q: (i, g, 0)),
            pl.BlockSpec((1, VT_ROWS, t), lambda i, g, q: (i, NSA_KV_GROUPS + g, 0)),
            pl.BlockSpec((n_sel, ncp), const2),
            pl.BlockSpec(e.shape, const2),
            pl.BlockSpec(tri.shape, const2),
            pl.BlockSpec(wpat.shape, const2),
            pl.BlockSpec(cpat.shape, const2),
        ],
        out_specs=pl.BlockSpec((1, qstep, gq), lambda i, g, q: (i, q, g)),
        out_shape=jax.ShapeDtypeStruct((b, t, NSA_HEADS * HEAD_DIM), F32),
        scratch_shapes=[pltpu.VMEM((NSA_QSUB, n_sel, Q_BLOCK), BF16),
                        pltpu.VMEM((2, NSA_QSUB, 4 * HEAD_DIM, NSA_GROUP * Q_BLOCK), BF16),
                        pltpu.VMEM((2, NSA_QSUB, SEL_TK, NSA_GROUP * Q_BLOCK), F32),
                        pltpu.VMEM((NSA_QSUB, HEAD_DIM, NSA_GROUP * Q_BLOCK), F32)],
        compiler_params=_cparams(3),
        name="nsa_attention",
    )(proj_t, proj_t, kc, vct, kpack, vt_pack, vt_pack, wsel, e, tri, wpat, cpat)


def _selection_weights(n_sel, ncp):
    j = jnp.arange(n_sel)[:, None]
    n = jnp.arange(ncp)[None, :]
    d = n - 4 * j
    w = jnp.where((d == -1) | (d == 3), 1.0, 0.0) + jnp.where((d >= 0) & (d <= 2), 2.0, 0.0)
    w = jnp.where(n < ncp - 1, w, 0.0)
    return w.astype(BF16)


def _conv_kernel(cb_ref, cc_ref, ch_ref, pc_ref, ph_ref, w_ref, o_ref):
    i = pl.program_id(1)
    u = cc_ref[0].astype(F32) * ch_ref[0].astype(F32)
    prev = pc_ref[0].astype(F32) * ph_ref[0].astype(F32)
    prev = jnp.where(i > 0, prev, 0.0)
    row = lax.broadcasted_iota(jnp.int32, u.shape, 0)
    p1 = prev[CONV_HALO - 1:CONV_HALO, :]
    p2 = prev[CONV_HALO - 2:CONV_HALO - 1, :]
    u1 = jnp.where(row == 0, p1, pltpu.roll(u, 1, 0))
    u2 = jnp.where(row == 0, p2, jnp.where(row == 1, p1, pltpu.roll(u, 2, 0)))
    w = w_ref[...]
    o_ref[0] = cb_ref[0].astype(F32) * (w[0:1, :] * u2 + w[1:2, :] * u1 + w[2:3, :] * u)


def short_conv(hn, conv_w, tm):
    b, t, _ = hn.shape
    w = BRANCH_WIDTH
    col = lambda c: (lambda i, j: (i, j, c // w))
    hal = lambda c: (lambda i, j: (i, jnp.maximum(j * (tm // CONV_HALO) - 1, 0), c // w))
    wpad = jnp.zeros((8, w), F32).at[:CONV_K].set(conv_w)
    return pl.pallas_call(
        _conv_kernel,
        grid=(b, t // tm),
        in_specs=[
            pl.BlockSpec((1, tm, w), col(NAT_CB)),
            pl.BlockSpec((1, tm, w), col(NAT_CC)),
            pl.BlockSpec((1, tm, w), col(NAT_CH)),
            pl.BlockSpec((1, CONV_HALO, w), hal(NAT_CC)),
            pl.BlockSpec((1, CONV_HALO, w), hal(NAT_CH)),
            pl.BlockSpec((8, w), lambda i, j: (0, 0)),
        ],
        out_specs=pl.BlockSpec((1, tm, w), lambda i, j: (i, j, 0)),
        out_shape=jax.ShapeDtypeStruct((b, t, w), F32),
        compiler_params=_cparams(2),
        name="short_conv",
    )(hn, hn, hn, hn, hn, wpad)


def _ret_kernel(rq_ref, rk_ref, rv_ref, rg_ref, cos_ref, sin_ref, dec_ref, qd_ref, kd_ref, cd_ref,
                gg_ref, gb_ref, o_ref, st_ref):
    c = RET_CHUNK
    hd = RET_HEAD_DIM

    @pl.when(pl.program_id(1) == 0)
    def _():
        st_ref[...] = jnp.zeros_like(st_ref)

    n_chunks = rq_ref.shape[1] // c

    def chunk(ci, _):
        r0 = pl.multiple_of(ci * c, c)
        cos = cos_ref[pl.ds(r0, c), :]
        sin = sin_ref[pl.ds(r0, c), :]
        for h in range(RET_HEADS):
            hs = slice(h * hd, (h + 1) * hd)
            q = rq_ref[0, pl.ds(r0, c), hs].astype(F32)
            k = rk_ref[0, pl.ds(r0, c), hs].astype(F32)
            v = rv_ref[0, pl.ds(r0, c), hs]
            q = q * cos + pltpu.roll(q, hd // 2, 1) * sin
            k = (k * cos + pltpu.roll(k, hd // 2, 1) * sin) * (hd ** -0.5)
            qb = q.astype(BF16)
            kb = k.astype(BF16)
            vb = v.astype(BF16)
            inner = lax.dot_general(qb, kb, (((1,), (1,)), ((), ())), preferred_element_type=F32) * dec_ref[h]
            st = st_ref[h]
            o = (jnp.dot(inner.astype(BF16), vb, preferred_element_type=F32)
                 + jnp.dot(qb, st.astype(BF16), preferred_element_type=F32) * qd_ref[h])
            kdt = (k * kd_ref[h]).T.astype(BF16)
            st_ref[h] = st * cd_ref[h] + jnp.dot(kdt, vb, preferred_element_type=F32)
            mu = jnp.mean(o, axis=-1, keepdims=True)
            d = o - mu
            var = jnp.mean(d * d, axis=-1, keepdims=True)
            y = d * lax.rsqrt(var + LN_EPS) * gg_ref[:, hs] + gb_ref[:, hs]
            o_ref[0, pl.ds(r0, c), hs] = jax.nn.silu(rg_ref[0, pl.ds(r0, c), hs].astype(F32)) * y
        return 0

    lax.fori_loop(0, n_chunks, chunk, 0)


def retention_branch(hn, gn_g, gn_b, tr):
    b, t, _ = hn.shape
    w = BRANCH_WIDTH
    hd = RET_HEAD_DIM
    c = RET_CHUNK
    inv = ROPE_BASE ** (-jnp.arange(0, hd, 2, dtype=F32) / hd)
    ang = jnp.arange(t, dtype=F32)[:, None] * inv[None, :]
    cos2 = jnp.concatenate([jnp.cos(ang), jnp.cos(ang)], axis=1)
    sin2 = jnp.concatenate([-jnp.sin(ang), jnp.sin(ang)], axis=1)
    log_g = jnp.log(1.0 - 2.0 ** (-5.0 - jnp.arange(RET_HEADS, dtype=F32)))
    idx = jnp.arange(c, dtype=F32)
    diff = idx[:, None] - idx[None, :]
    decay_in = jnp.where(diff >= 0, jnp.exp(log_g[:, None, None] * jnp.maximum(diff, 0.0)), 0.0)
    ones = jnp.ones((RET_HEADS, c, hd), F32)
    q_dec = jnp.exp(log_g[:, None] * (idx + 1.0))[:, :, None] * ones
    k_dec = jnp.exp(log_g[:, None] * (c - 1.0 - idx))[:, :, None] * ones
    c_dec = jnp.exp(log_g * c)[:, None, None] * ones
    col = lambda cc: (lambda i, j: (i, j, cc // w))
    full3 = lambda i, j: (0, 0, 0)
    return pl.pallas_call(
        _ret_kernel,
        grid=(b, t // tr),
        in_specs=[
            pl.BlockSpec((1, tr, w), col(NAT_RQ)),
            pl.BlockSpec((1, tr, w), col(NAT_RK)),
            pl.BlockSpec((1, tr, w), col(NAT_RV)),
            pl.BlockSpec((1, tr, w), col(NAT_RG)),
            pl.BlockSpec((tr, hd), lambda i, j: (j, 0)),
            pl.BlockSpec((tr, hd), lambda i, j: (j, 0)),
            pl.BlockSpec((RET_HEADS, c, c), full3),
            pl.BlockSpec((RET_HEADS, c, hd), full3),
            pl.BlockSpec((RET_HEADS, c, hd), full3),
            pl.BlockSpec((RET_HEADS, c, hd), full3),
            pl.BlockSpec((1, w), lambda i, j: (0, 0)),
            pl.BlockSpec((1, w), lambda i, j: (0, 0)),
        ],
        out_specs=pl.BlockSpec((1, tr, w), lambda i, j: (i, j, 0)),
        out_shape=jax.ShapeDtypeStruct((b, t, w), F32),
        scratch_shapes=[pltpu.VMEM((RET_HEADS, hd, hd), F32)],
        compiler_params=_cparams(2),
        name="retention",
    )(hn, hn, hn, hn, cos2, sin2, decay_in, q_dec, k_dec, c_dec, gn_g.reshape(1, w), gn_b.reshape(1, w))


def _layer_norm(z, g, b):
    mu = jnp.mean(z, axis=-1, keepdims=True)
    d = z - mu
    var = jnp.mean(d * d, axis=-1, keepdims=True)
    return d * lax.rsqrt(var + LN_EPS) * g + b


def _merge_kernel(x_ref, ya_ref, yb_ref, yc_ref, mg_ref, wbr_ref, wout_ref, g_ref, b_ref, o_ref):
    mix = None
    for c, y_ref in enumerate((ya_ref, yb_ref, yc_ref)):
        proj = jnp.dot(y_ref[...].astype(BF16), wbr_ref[c], preferred_element_type=F32)
        gate = jax.nn.sigmoid(mg_ref[:, c * D_MODEL:(c + 1) * D_MODEL].astype(F32))
        mix = gate * proj if mix is None else mix + gate * proj
    m = jnp.dot(mix.astype(BF16), wout_ref[...], preferred_element_type=F32)
    o_ref[...] = _layer_norm(ALPHA * x_ref[...] + m, g_ref[...], b_ref[...])


def merge_branches(x, ya, yb, yc, hn, w_br, w_out, g, b, tm):
    m = x.shape[0]
    d = D_MODEL
    w = BRANCH_WIDTH
    row = lambda i: (i, 0)
    return pl.pallas_call(
        _merge_kernel,
        grid=(m // tm,),
        in_specs=[
            pl.BlockSpec((tm, d), row),
            pl.BlockSpec((tm, w), row),
            pl.BlockSpec((tm, w), row),
            pl.BlockSpec((tm, w), row),
            pl.BlockSpec((tm, 3 * d), row),
            pl.BlockSpec((3, w, d), lambda i: (0, 0, 0)),
            pl.BlockSpec((d, d), lambda i: (0, 0)),
            pl.BlockSpec((1, d), lambda i: (0, 0)),
            pl.BlockSpec((1, d), lambda i: (0, 0)),
        ],
        out_specs=pl.BlockSpec((tm, d), row),
        out_shape=jax.ShapeDtypeStruct((m, d), F32),
        compiler_params=_cparams(1),
        name="merge",
    )(x, ya, yb, yc, hn, w_br.astype(BF16), w_out.astype(BF16), g.reshape(1, d), b.reshape(1, d))


def _mlp_kernel(te_ref, tv_ref, x_ref, wg_ref, wu_ref, wd_ref, o_ref):
    i = pl.program_id(0)
    f = pl.program_id(1)

    @pl.when(tv_ref[i] > 0)
    def _():
        x = x_ref[...]
        gte = jnp.dot(x, wg_ref[0], preferred_element_type=F32)
        up = jnp.dot(x, wu_ref[0], preferred_element_type=F32)
        h = (jax.nn.silu(gte) * up).astype(BF16)
        part = jnp.dot(h, wd_ref[0], preferred_element_type=F32)

        @pl.when(f == 0)
        def _():
            o_ref[...] = part

        @pl.when(f > 0)
        def _():
            o_ref[...] = o_ref[...] + part

    @pl.when((tv_ref[i] == 0) & (f == 0))
    def _():
        o_ref[...] = jnp.zeros_like(o_ref)


def grouped_swiglu(xs, w_gate, w_up, w_down, tile_expert, tile_valid, tm, tf):
    n, d = xs.shape
    ff = w_gate.shape[2]
    grid_spec = pltpu.PrefetchScalarGridSpec(
        num_scalar_prefetch=2,
        grid=(n // tm, ff // tf),
        in_specs=[
            pl.BlockSpec((tm, d), lambda i, f, te, tv: (i, 0)),
            pl.BlockSpec((1, d, tf), lambda i, f, te, tv: (te[i], 0, f)),
            pl.BlockSpec((1, d, tf), lambda i, f, te, tv: (te[i], 0, f)),
            pl.BlockSpec((1, tf, d), lambda i, f, te, tv: (te[i], f, 0)),
        ],
        out_specs=pl.BlockSpec((tm, d), lambda i, f, te, tv: (i, 0)),
    )
    return pl.pallas_call(
        _mlp_kernel,
        grid_spec=grid_spec,
        out_shape=jax.ShapeDtypeStruct((n, d), F32),
        compiler_params=_cparams(2),
        name="grouped_swiglu",
    )(tile_expert, tile_valid, xs, w_gate, w_up, w_down)


def _add_ln_kernel(*refs):
    x_ref, sc_ref, *adds, g_ref, b_ref, o_ref = refs
    z = ALPHA * x_ref[...]
    for k, a in enumerate(adds):
        z = z + sc_ref[:, k:k + 1] * a[...]
    o_ref[...] = _layer_norm(z, g_ref[...], b_ref[...])


def add_layer_norm(x, adds, scales, g, b, tm):
    m, d = x.shape
    row = lambda i: (i, 0)
    return pl.pallas_call(
        _add_ln_kernel,
        grid=(m // tm,),
        in_specs=([pl.BlockSpec((tm, d), row), pl.BlockSpec((tm, len(adds)), row)]
                  + [pl.BlockSpec((tm, d), row)] * len(adds) + [pl.BlockSpec((1, d), lambda i: (0, 0))] * 2),
        out_specs=pl.BlockSpec((tm, d), row),
        out_shape=jax.ShapeDtypeStruct((m, d), F32),
        compiler_params=_cparams(1),
        name="add_layer_norm",
    )(x, scales, *adds, g.reshape(1, d), b.reshape(1, d))


def _router_kernel(x_ref, wh_ref, wl_ref, o_ref):
    x = x_ref[...]
    xh = x.astype(BF16)
    xl = (x - xh.astype(F32)).astype(BF16)
    wh = wh_ref[...]
    o_ref[...] = (jnp.dot(xh, wh, preferred_element_type=F32) + jnp.dot(xl, wh, preferred_element_type=F32)
                  + jnp.dot(xh, wl_ref[...], preferred_element_type=F32))


def router_logits(x, w_router, tm):
    m, d = x.shape
    wp = jnp.zeros((d, 128), F32).at[:, :N_EXPERTS].set(w_router)
    wh = wp.astype(BF16)
    wl = (wp - wh.astype(F32)).astype(BF16)
    return pl.pallas_call(
        _router_kernel,
        grid=(m // tm,),
        in_specs=[pl.BlockSpec((tm, d), lambda i: (i, 0)), pl.BlockSpec((d, 128), lambda i: (0, 0)),
                  pl.BlockSpec((d, 128), lambda i: (0, 0))],
        out_specs=pl.BlockSpec((tm, 128), lambda i: (i, 0)),
        out_shape=jax.ShapeDtypeStruct((m, 128), F32),
        compiler_params=_cparams(1),
        name="router",
    )(x, wh, wl)


def dense_ffn(x, w_gate, w_up, w_down, g, b):
    m = x.shape[0]
    tm = min(512, m)
    nt = m // tm
    ys = grouped_swiglu(x.astype(BF16), w_gate[None].astype(BF16), w_up[None].astype(BF16),
                        w_down[None].astype(BF16),
                        jnp.zeros((nt,), jnp.int32), jnp.ones((nt,), jnp.int32), tm, 1792)
    return add_layer_norm(x, [ys], jnp.ones((m, 1), F32), g, b, min(1024, m))


def moe_ffn(x, w_router, w_gate, w_up, w_down, g, b):
    m = x.shape[0]
    tm = 512
    logits = router_logits(x, w_router, min(1024, m))[:, :N_EXPERTS]
    top_v, top_i = lax.top_k(logits, TOP_K)
    probs = jax.nn.softmax(top_v, axis=-1)
    e_flat = top_i.reshape(-1)
    onehot = (e_flat[:, None] == jnp.arange(N_EXPERTS)[None, :]).astype(jnp.int32)
    csum = jnp.cumsum(onehot, axis=0)
    counts = csum[-1]
    rank = jnp.take_along_axis(csum, e_flat[:, None], axis=1)[:, 0] - 1
    padded = ((counts + tm - 1) // tm) * tm
    ends = jnp.cumsum(padded)
    starts = ends - padded
    pos = starts[e_flat] + rank
    n_slots = TOP_K * m + N_EXPERTS * tm
    n_tiles = n_slots // tm
    tok = jnp.arange(TOP_K * m, dtype=jnp.int32) // TOP_K
    src = jnp.zeros((n_slots,), jnp.int32).at[pos].set(tok)
    tile_start = jnp.arange(n_tiles, dtype=jnp.int32) * tm
    tile_expert = jnp.minimum(jnp.searchsorted(ends, tile_start, side="right"), N_EXPERTS - 1).astype(jnp.int32)
    tile_valid = (tile_start < ends[-1]).astype(jnp.int32)
    xs = jnp.take(x.astype(BF16), src, axis=0)
    ys = grouped_swiglu(xs, w_gate.astype(BF16), w_up.astype(BF16), w_down.astype(BF16),
                        tile_expert, tile_valid, tm, 1792)
    pos2 = pos.reshape(m, TOP_K)
    y0 = jnp.take(ys, pos2[:, 0], axis=0)
    y1 = jnp.take(ys, pos2[:, 1], axis=0)
    return add_layer_norm(x, [y0, y1], probs, g, b, min(1024, m))


def _pack_in_weights(w):
    d = w.shape[0]
    o = 0
    q = w[:, o:o + 512]; o += 512
    kv = w[:, o:o + 768].reshape(d, 3, 2, NSA_KV_GROUPS, HEAD_DIM); o += 768
    ng = w[:, o:o + 24]; o += 24
    rest = w[:, o:o + 7 * 512]; o += 7 * 512
    mg = w[:, o:]
    kpack = jnp.concatenate([kv[:, 1, 0], kv[:, 2, 0]], axis=-1).reshape(d, 2 * NSA_KV_GROUPS * HEAD_DIM)
    kc = kv[:, 0, 0].reshape(d, NSA_KV_GROUPS * HEAD_DIM)
    vc = kv[:, 0, 1].reshape(d, NSA_KV_GROUPS * HEAD_DIM)
    w_nat = jnp.concatenate([mg, rest, kpack, kc, vc], axis=1).astype(BF16)
    vs = kv[:, 1, 1].reshape(d, NSA_KV_GROUPS * HEAD_DIM)
    vw = kv[:, 2, 1].reshape(d, NSA_KV_GROUPS * HEAD_DIM)
    ngp = jnp.pad(ng.reshape(d, NSA_KV_GROUPS, NSA_GROUP * 3), ((0, 0), (0, 0), (0, 4))).reshape(d, 32)
    w_t = jnp.concatenate([q, vs, vw, ngp], axis=1).T.astype(BF16)
    return w_nat, w_t


def token_mixer_ln(x, w_in, cmp_pos, cmp_w1, cmp_b1, cmp_w2, cmp_b2, conv_w, gn_g, gn_b, w_br, w_out, ln_g, ln_b):
    b, t, d = x.shape
    m = b * t
    w_nat, w_t = _pack_in_weights(w_in)
    x2 = x.reshape(m, d)
    hn = mm_nat(x2, w_nat, min(1024, m), 1024, BF16).reshape(b, t, NAT_COLS)
    proj_t = mm_nt(w_t, x, min(1024, t))
    nch = t // CMP_STRIDE
    raw = hn[:, :, NAT_KC:NAT_KC + 2 * NSA_KV_GROUPS * HEAD_DIM].reshape(b, t, 2, NSA_KV_GROUPS, HEAD_DIM)
    chunks = raw.transpose(2, 0, 3, 1, 4).reshape(2, b, NSA_KV_GROUPS, nch, CMP_STRIDE * HEAD_DIM)
    cn, ct = nsa_compress_kv(chunks, cmp_pos, cmp_w1, cmp_b1, cmp_w2, cmp_b2)
    kc = cn[0].astype(BF16)
    vct = ct[1].astype(BF16)
    kpack = hn[:, :, NAT_KP:NAT_KP + 2 * NSA_KV_GROUPS * HEAD_DIM]
    vt4 = proj_t[:, TR_VS:TR_NG, :].astype(BF16).reshape(b, 4, HEAD_DIM, t)
    ones_pad = jnp.zeros((b, 4, VT_ROWS - HEAD_DIM, t), BF16).at[:, :, 0].set(1.0)
    vt_pack = jnp.concatenate([vt4, ones_pad], axis=2).reshape(b, 4 * VT_ROWS, t)
    wsel = _selection_weights(t // SLC_BLOCK, nch)
    y_a = nsa_attention(proj_t, kc, vct, kpack, vt_pack, wsel)
    y_b = short_conv(hn, conv_w, min(1024, t))
    y_c = retention_branch(hn, gn_g, gn_b, min(1024, t))
    out = merge_branches(x2, y_a.reshape(m, -1), y_b.reshape(m, -1), y_c.reshape(m, -1), hn.reshape(m, NAT_COLS),
                         w_br, w_out, ln_g, ln_b, min(512, m))
    return out


def kernel(x, w_in, cmp_pos, cmp_w1, cmp_b1, cmp_w2, cmp_b2, conv_w, ret_gn_g, ret_gn_b, w_br, w_out,
           ln1_g, ln1_b, ln2_g, ln2_b, ffn_gate, ffn_up, ffn_down, moe_router, moe_gate, moe_up, moe_down):
    b, t, d = x.shape
    for l in range(DEPTH):
        x2 = token_mixer_ln(x, w_in[l], cmp_pos[l], cmp_w1[l], cmp_b1[l], cmp_w2[l], cmp_b2[l], conv_w[l],
                            ret_gn_g[l], ret_gn_b[l], w_br[l], w_out[l], ln1_g[l], ln1_b[l])
        if l % 2 == 0:
            x2 = dense_ffn(x2, ffn_gate[l // 2], ffn_up[l // 2], ffn_down[l // 2], ln2_g[l], ln2_b[l])
        else:
            x2 = moe_ffn(x2, moe_router[l // 2], moe_gate[l // 2], moe_up[l // 2], moe_down[l // 2],
                         ln2_g[l], ln2_b[l])
        x = x2.reshape(b, t, d)
    return x
```

```python
import functools
import math

import jax
import jax.numpy as jnp
from jax import lax
from jax.experimental import pallas as pl
from jax.experimental.pallas import tpu as pltpu

F32 = jnp.float32
BF16 = jnp.bfloat16

D_MODEL = 1024
DEPTH = 4
BRANCH_WIDTH = 512
HEAD_DIM = 64
NSA_HEADS = 8
NSA_KV_GROUPS = 2
NSA_GROUP = 4
CMP_LEN = 32
CMP_STRIDE = 16
SLC_BLOCK = 64
SLC_TOPN = 16
WINDOW = 512
CMP_HIDDEN = 256
Q_BLOCK = 128
CONV_K = 3
CONV_HALO = 16
RET_HEADS = 4
RET_HEAD_DIM = 128
RET_CHUNK = 128
ROPE_BASE = 10000.0
D_FF = 3584
N_EXPERTS = 8
TOP_K = 2
LN_EPS = 1e-5
ALPHA = (2.0 * DEPTH) ** 0.25

NAT_MG = 0
NAT_CB = 3072
NAT_CC = 3584
NAT_CH = 4096
NAT_RQ = 4608
NAT_RK = 5120
NAT_RV = 5632
NAT_RG = 6144
NAT_KP = 6656
NAT_KC = 6912
NAT_VC = 7040
NAT_COLS = 7168
TR_Q = 0
TR_VS = 512
TR_VW = 640
TR_NG = 768
TR_ROWS = 800

NEG = -1e30
SEL_TK = 1024
LOG2E = 1.4426950408889634
NSA_QSUB = 2
CMP_CLASS_ROWS = 256
PICKED = -3e38
VT_ROWS = 80
VMEM_LIMIT = 48 * 1024 * 1024


def _cparams(n_axes, vmem=VMEM_LIMIT):
    return pltpu.CompilerParams(dimension_semantics=("arbitrary",) * n_axes, vmem_limit_bytes=vmem)


def _mm_nat_kernel(x_ref, w_ref, o_ref, xb_ref):
    @pl.when(pl.program_id(1) == 0)
    def _():
        xb_ref[...] = x_ref[...].astype(BF16)

    o_ref[...] = jnp.dot(xb_ref[...], w_ref[...], preferred_element_type=F32).astype(o_ref.dtype)


def mm_nat(x, w, tm, tn, out_dtype=F32):
    m, k = x.shape
    n = w.shape[1]
    return pl.pallas_call(
        _mm_nat_kernel,
        grid=(m // tm, n // tn),
        in_specs=[pl.BlockSpec((tm, k), lambda i, j: (i, 0)), pl.BlockSpec((k, tn), lambda i, j: (0, j))],
        out_specs=pl.BlockSpec((tm, tn), lambda i, j: (i, j)),
        out_shape=jax.ShapeDtypeStruct((m, n), out_dtype),
        scratch_shapes=[pltpu.VMEM((tm, k), BF16)],
        compiler_params=_cparams(2),
        name="mm_nat",
    )(x, w)


def _mm_nt_kernel(w_ref, x_ref, o_ref):
    o_ref[0] = lax.dot_general(w_ref[...], x_ref[0].astype(BF16), (((1,), (1,)), ((), ())),
                               preferred_element_type=F32)


def mm_nt(w_t, x, tm):
    b, t, k = x.shape
    r = w_t.shape[0]
    return pl.pallas_call(
        _mm_nt_kernel,
        grid=(b, t // tm),
        in_specs=[pl.BlockSpec((r, k), lambda i, j: (0, 0)), pl.BlockSpec((1, tm, k), lambda i, j: (i, j, 0))],
        out_specs=pl.BlockSpec((1, r, tm), lambda i, j: (i, 0, j)),
        out_shape=jax.ShapeDtypeStruct((b, r, t), F32),
        compiler_params=_cparams(2),
        name="mm_nt",
    )(w_t, x)


def _cmp_kernel(c_ref, pos_ref, w1_ref, b1_ref, w2_ref, w2t_ref, b2_ref, b2t_ref, on_ref, ot_ref):
    half = CMP_STRIDE * HEAD_DIM
    c = c_ref[0, 0, 0].astype(BF16)
    w1 = w1_ref[0]
    a = jnp.dot(c, w1[:half], preferred_element_type=F32)
    bm = jnp.dot(c, w1[half:], preferred_element_type=F32)
    nch = a.shape[0]
    bm = pltpu.roll(bm, nch - 1, 0)
    posb = jnp.dot(pos_ref[0].astype(BF16), w1, preferred_element_type=F32)[0:1]
    hid = jax.nn.gelu(a + bm + posb + b1_ref[0])
    hb = hid.astype(BF16)
    on_ref[0, 0, 0] = jnp.dot(hb, w2_ref[0], preferred_element_type=F32) + b2_ref[0]
    ot_ref[0, 0, 0] = lax.dot_general(w2t_ref[0], hb, (((1,), (1,)), ((), ())),
                                      preferred_element_type=F32) + b2t_ref[0]


def nsa_compress_kv(chunks, pos, w1, b1, w2, b2):
    _, b, g, nch, cw = chunks.shape
    pos_flat = jnp.broadcast_to(pos.reshape(2, 1, CMP_LEN * HEAD_DIM), (2, 8, CMP_LEN * HEAD_DIM))
    w1b = w1.astype(BF16)
    w2b = w2.astype(BF16)
    w2t = jnp.swapaxes(w2, 1, 2).astype(BF16)
    b1r = b1.reshape(2, 1, CMP_HIDDEN)
    b2r = b2.reshape(2, 1, HEAD_DIM)
    b2t = b2.reshape(2, HEAD_DIM, 1)
    sel = lambda kv, i, j: (kv, 0, 0)
    return pl.pallas_call(
        _cmp_kernel,
        grid=(2, b, g),
        in_specs=[
            pl.BlockSpec((1, 1, 1, nch, cw), lambda kv, i, j: (kv, i, j, 0, 0)),
            pl.BlockSpec((1, 8, CMP_LEN * HEAD_DIM), sel),
            pl.BlockSpec((1, CMP_LEN * HEAD_DIM, CMP_HIDDEN), sel),
            pl.BlockSpec((1, 1, CMP_HIDDEN), sel),
            pl.BlockSpec((1, CMP_HIDDEN, HEAD_DIM), sel),
            pl.BlockSpec((1, HEAD_DIM, CMP_HIDDEN), sel),
            pl.BlockSpec((1, 1, HEAD_DIM), sel),
            pl.BlockSpec((1, HEAD_DIM, 1), sel),
        ],
        out_specs=[
            pl.BlockSpec((1, 1, 1, nch, HEAD_DIM), lambda kv, i, j: (kv, i, j, 0, 0)),
            pl.BlockSpec((1, 1, 1, HEAD_DIM, nch), lambda kv, i, j: (kv, i, j, 0, 0)),
        ],
        out_shape=[
            jax.ShapeDtypeStruct((2, b, g, nch, HEAD_DIM), F32),
            jax.ShapeDtypeStruct((2, b, g, HEAD_DIM, nch), F32),
        ],
        compiler_params=_cparams(3),
        name="nsa_compress",
    )(chunks, pos_flat, w1b, b1r, w2b, w2t, b2r, b2t)


def _accumulate(s, mt, vt, m, acc):
    m_new = jnp.maximum(m, mt)
    alpha = jnp.exp2(m - m_new)
    p = jnp.exp2(s - m_new).astype(BF16)
    return m_new, alpha * acc + jnp.dot(vt, p, preferred_element_type=F32)


def _nsa_kernel(qt_ref, ng_ref, kc_ref, vct_ref, kp_ref, vst_ref, vwt_ref, wsel_ref, e_ref, tri_ref, wpat_ref,
                cpat_ref, o_ref, bias_ref, w_ref, sbuf_ref, oc_ref):
    nq = Q_BLOCK
    nsub = qt_ref.shape[2] // nq
    ncol = NSA_GROUP * nq
    n_sel = bias_ref.shape[1]
    ncp = kc_ref.shape[2]
    kw = 2 * HEAD_DIM
    bpt = SEL_TK // SLC_BLOCK
    n_tiles_total = kp_ref.shape[1] // SEL_TK
    qb_first = pl.program_id(2) * nsub
    wlen = WINDOW + nq

    def prepare(u):
        qb = qb_first + u
        q0 = pl.multiple_of(qb * nq, nq)
        qt = qt_ref[0, :, u * nq:(u + 1) * nq] * (HEAD_DIM ** -0.5 * LOG2E)
        qs = jnp.concatenate([qt[r * HEAD_DIM:(r + 1) * HEAD_DIM, :] for r in range(NSA_GROUP)], axis=1)
        zq = jnp.zeros_like(qs)
        q_c = qs.astype(BF16)
        q_slc = jnp.concatenate([qs, zq], axis=0).astype(BF16)
        q_win = jnp.concatenate([zq, qs], axis=0).astype(BF16)
        tq = q0 + (lax.broadcasted_iota(jnp.int32, (1, ncol), 1) & (nq - 1))

        coff = pl.multiple_of(ncp - qb * (nq // CMP_STRIDE), nq // CMP_STRIDE)
        tq1 = q0 + lax.broadcasted_iota(jnp.int32, (1, nq), 1)
        cur = jnp.right_shift(tq1, 6).astype(F32)
        first_diag = (2 * qb).astype(F32)
        crow = min(CMP_CLASS_ROWS, ncp)
        cls = qb // (crow // (nq // CMP_STRIDE))
        for c in range(ncp // crow):
            nc_c = crow * (c + 1)
            ns_c = nc_c * CMP_STRIDE // SLC_BLOCK

            @pl.when(cls == c)
            def _(nc_c=nc_c, ns_c=ns_c):
                s_c = (jnp.dot(kc_ref[0, 0, 0:nc_c, :], q_c, preferred_element_type=F32)
                       + jnp.concatenate([cpat_ref[pl.ds(coff, nc_c), :]] * NSA_GROUP, axis=1))
                m_c = jnp.max(s_c, axis=0, keepdims=True)
                e_c = jnp.exp2(s_c - m_c)
                l_c = jnp.sum(e_c, axis=0, keepdims=True)
                p_c = e_c * jnp.where(tq >= CMP_LEN - 1, 1.0 / l_c, 0.0)
                oc_ref[u] = jnp.dot(vct_ref[0, 0, :, 0:nc_c], p_c.astype(BF16), preferred_element_type=F32)

                p_grp = p_c[:, 0:nq]
                for r in range(1, NSA_GROUP):
                    p_grp = p_grp + p_c[:, r * nq:(r + 1) * nq]
                p_hi = p_grp.astype(BF16)
                res = p_grp - p_hi.astype(F32)
                p_mid = res.astype(BF16)
                p_lo = (res - p_mid.astype(F32)).astype(BF16)
                p3 = jnp.concatenate([p_hi, p_mid, p_lo], axis=1)
                sc3 = jnp.dot(wsel_ref[0:ns_c, 0:nc_c], p3, preferred_element_type=F32)
                score = sc3[:, 0:nq] + sc3[:, nq:2 * nq] + sc3[:, 2 * nq:3 * nq]
                j_io = lax.broadcasted_iota(jnp.int32, (ns_c, nq), 0).astype(F32)
                score = jnp.where(j_io <= cur, score, -1.0)
                score = jnp.where(j_io == 0.0, PICKED, score)
                score = jnp.where(j_io == cur, PICKED, score)
                score = jnp.where(j_io == cur - 1.0, PICKED, score)
                for _ in range(SLC_TOPN - 3):
                    mx = jnp.max(score, axis=0, keepdims=True)
                    first = jnp.min(jnp.where(score == mx, j_io, float(ns_c)), axis=0, keepdims=True)
                    score = jnp.where(j_io == first, PICKED, score)
                bias = jnp.where(j_io < first_diag, jnp.where(score == PICKED, 0.0, NEG), NEG)
                bias_ref[u, 0:ns_c, :] = bias.astype(BF16)
                if ns_c < n_sel:
                    bias_ref[u, ns_c:, :] = jnp.full((n_sel - ns_c, nq), NEG, BF16)

        o_c = oc_ref[u]

        wstart = pl.multiple_of(jnp.maximum(q0 - WINDOW, 0), nq)
        woff = pl.multiple_of(jnp.maximum(WINDOW - q0, 0), nq)
        s_w = jnp.dot(kp_ref[0, pl.ds(wstart, wlen), :], q_win, preferred_element_type=F32)
        s_w = s_w + jnp.concatenate([wpat_ref[pl.ds(woff, wlen), :]] * NSA_GROUP, axis=1)
        p_w = jnp.exp2(s_w - jnp.max(s_w, axis=0, keepdims=True)).astype(BF16)
        acc_w = jnp.dot(vwt_ref[0, :, pl.ds(wstart, wlen)], p_w, preferred_element_type=F32)
        o_w = acc_w[0:HEAD_DIM] / acc_w[HEAD_DIM:HEAD_DIM + 1]

        for slot in range(2):
            w_ref[slot, u, 0:kw, :] = q_slc
            w_ref[slot, u, kw:, :] = jnp.zeros((w_ref.shape[2] - kw, ncol), BF16)
        return q0, q_slc, o_c, o_w

    prepared = [prepare(u) for u in range(nsub)]

    def tile_start(j):
        jc = jnp.minimum(j, n_tiles_total - 1)
        return jc, pl.multiple_of(jc * SEL_TK, SEL_TK)

    def produce(j, slot):
        jc, k0 = tile_start(j)
        lhs = jnp.concatenate([kp_ref[0, pl.ds(k0, SEL_TK), :], e_ref[...]], axis=1)
        mts = []
        for u in range(nsub):
            b16 = bias_ref[u, pl.ds(pl.multiple_of(jc * bpt, bpt), bpt), :]
            w_ref[slot, u, kw:kw + bpt, :] = jnp.concatenate([b16] * NSA_GROUP, axis=1)
            s = jnp.dot(lhs, w_ref[slot, u], preferred_element_type=F32)
            sbuf_ref[slot, u] = s
            mts.append(jnp.max(s, axis=0, keepdims=True))
        return tuple(mts)

    def consume(j, slot, mts, state):
        _, k0 = tile_start(j)
        vt = vst_ref[0, :, pl.ds(k0, SEL_TK)]
        return tuple(_accumulate(sbuf_ref[slot, u], mts[u], vt, *state[u]) for u in range(nsub))

    def pair(i, carry, last):
        mt_a, state = carry
        mt_b = produce(2 * i + 1, 1)
        state = consume(2 * i, 0, mt_a, state)
        if last:
            return consume(2 * i + 1, 1, mt_b, state)
        mt_a = produce(2 * i + 2, 0)
        state = consume(2 * i + 1, 1, mt_b, state)
        return mt_a, state

    n_main = ((qb_first + nsub - 1) * nq + (SEL_TK - 1)) // SEL_TK
    n_pairs = jnp.maximum((n_main + 1) // 2, 1)
    state = tuple((jnp.full((1, ncol), NEG, F32), jnp.zeros((vst_ref.shape[1], ncol), F32)) for _ in range(nsub))
    carry = lax.fori_loop(0, n_pairs - 1, lambda i, c: pair(i, c, False), (produce(0, 0), state))
    state = pair(n_pairs - 1, carry, True)

    for u in range(nsub):
        q0, q_slc, o_c, o_w = prepared[u]
        s_d = jnp.dot(kp_ref[0, pl.ds(q0, nq), :], q_slc, preferred_element_type=F32) + tri_ref[...]
        _, acc_s = _accumulate(s_d, jnp.max(s_d, axis=0, keepdims=True), vst_ref[0, :, pl.ds(q0, nq)], *state[u])
        o_s = acc_s[0:HEAD_DIM] / acc_s[HEAD_DIM:HEAD_DIM + 1]
        gts = jax.nn.sigmoid(ng_ref[0, :, u * nq:(u + 1) * nq])
        rows = []
        for r in range(NSA_GROUP):
            sl = slice(r * nq, (r + 1) * nq)
            rows.append(gts[3 * r:3 * r + 1, :] * o_c[:, sl] + gts[3 * r + 1:3 * r + 2, :] * o_s[:, sl]
                        + gts[3 * r + 2:3 * r + 3, :] * o_w[:, sl])
        y_t = jnp.concatenate(rows, axis=0)
        o_ref[0, u * nq:(u + 1) * nq, :] = y_t.T.astype(o_ref.dtype)


def _nsa_constants(ncp):
    nq = Q_BLOCK
    ql = jnp.arange(nq)[None, :]
    r = jnp.arange(SEL_TK)[:, None]
    e = (r // SLC_BLOCK == jnp.arange(2 * HEAD_DIM)[None, :]).astype(BF16)
    rd = jnp.arange(nq)[:, None]
    tri = jnp.tile(jnp.where(rd <= ql, 0.0, NEG).astype(F32), (1, NSA_GROUP))
    rw = jnp.arange(2 * WINDOW + nq)[:, None]
    wpat = jnp.where((rw > ql) & (rw <= ql + WINDOW), 0.0, NEG).astype(F32)
    rc = jnp.arange(2 * ncp)[:, None] - ncp
    cpat = jnp.where(CMP_STRIDE * rc + (CMP_LEN - 1) <= ql, 0.0, NEG).astype(F32)
    return e, tri, wpat, cpat


def nsa_attention(proj_t, kc, vct, hn, vt_pack, wsel):
    b, _, t = proj_t.shape
    ncp = kc.shape[2]
    n_sel = t // SLC_BLOCK
    gq = NSA_GROUP * HEAD_DIM
    qstep = NSA_QSUB * Q_BLOCK
    e, tri, wpat, cpat = _nsa_constants(ncp)
    const2 = lambda i, g, q: (0, 0)
    return pl.pallas_call(
        _nsa_kernel,
        grid=(b, NSA_KV_GROUPS, t // qstep),
        in_specs=[
            pl.BlockSpec((1, gq, qstep), lambda i, g, q: (i, g, q)),
            pl.BlockSpec((1, 16, qstep), lambda i, g, q: (i, TR_NG // 16 + g, q)),
            pl.BlockSpec((1, 1, ncp, HEAD_DIM), lambda i, g, q: (i, g, 0, 0)),
            pl.BlockSpec((1, 1, HEAD_DIM, ncp), lambda i, g, q: (i, g, 0, 0)),
            pl.BlockSpec((1, t, 2 * HEAD_DIM), lambda i, g, q: (i, 0, NAT_KP // (2 * HEAD_DIM) + g)),
            pl.BlockSpec((1, VT_ROWS, t), lambda i, g, q: (i, g, 0)),
            pl.BlockSpec((1, VT_ROWS, t), lambda i, g, q: (i, NSA_KV_GROUPS + g, 0)),
            pl.BlockSpec((n_sel, ncp), const2),
            pl.BlockSpec(e.shape, const2),
            pl.BlockSpec(tri.shape, const2),
            pl.BlockSpec(wpat.shape, const2),
            pl.BlockSpec(cpat.shape, const2),
        ],
        out_specs=pl.BlockSpec((1, qstep, gq), lambda i, g, q: (i, q, g)),
        out_shape=jax.ShapeDtypeStruct((b, t, NSA_HEADS * HEAD_DIM), BF16),
        scratch_shapes=[pltpu.VMEM((NSA_QSUB, n_sel, Q_BLOCK), BF16),
                        pltpu.VMEM((2, NSA_QSUB, 4 * HEAD_DIM, NSA_GROUP * Q_BLOCK), BF16),
                        pltpu.VMEM((2, NSA_QSUB, SEL_TK, NSA_GROUP * Q_BLOCK), F32),
                        pltpu.VMEM((NSA_QSUB, HEAD_DIM, NSA_GROUP * Q_BLOCK), F32)],
        compiler_params=_cparams(3),
        name="nsa_attention",
    )(proj_t, proj_t, kc, vct, hn, vt_pack, vt_pack, wsel, e, tri, wpat, cpat)


def _selection_weights(n_sel, ncp):
    j = jnp.arange(n_sel)[:, None]
    n = jnp.arange(ncp)[None, :]
    d = n - 4 * j
    w = jnp.where((d == -1) | (d == 3), 1.0, 0.0) + jnp.where((d >= 0) & (d <= 2), 2.0, 0.0)
    w = jnp.where(n < ncp - 1, w, 0.0)
    return w.astype(BF16)


def _ret_kernel(rq_ref, rk_ref, rv_ref, rg_ref, cos_ref, sin_ref, dec_ref, qd_ref, kd_ref, cd_ref,
                gg_ref, gb_ref, o_ref, st_ref):
    c = RET_CHUNK
    hd = RET_HEAD_DIM

    @pl.when(pl.program_id(1) == 0)
    def _():
        st_ref[...] = jnp.zeros_like(st_ref)

    n_chunks = rq_ref.shape[1] // c

    def chunk(ci, _):
        r0 = pl.multiple_of(ci * c, c)
        cos = cos_ref[pl.ds(r0, c), :]
        sin = sin_ref[pl.ds(r0, c), :]
        for h in range(RET_HEADS):
            hs = slice(h * hd, (h + 1) * hd)
            q = rq_ref[0, pl.ds(r0, c), hs].astype(F32)
            k = rk_ref[0, pl.ds(r0, c), hs].astype(F32)
            v = rv_ref[0, pl.ds(r0, c), hs]
            q = q * cos + pltpu.roll(q, hd // 2, 1) * sin
            k = (k * cos + pltpu.roll(k, hd // 2, 1) * sin) * (hd ** -0.5)
            qb = q.astype(BF16)
            kb = k.astype(BF16)
            vb = v.astype(BF16)
            inner = lax.dot_general(qb, kb, (((1,), (1,)), ((), ())), preferred_element_type=F32) * dec_ref[h]
            st = st_ref[h]
            o = (jnp.dot(inner.astype(BF16), vb, preferred_element_type=F32)
                 + jnp.dot(qb, st.astype(BF16), preferred_element_type=F32) * qd_ref[h])
            kdt = (k * kd_ref[h]).T.astype(BF16)
            st_ref[h] = st * cd_ref[h] + jnp.dot(kdt, vb, preferred_element_type=F32)
            mu = jnp.mean(o, axis=-1, keepdims=True)
            d = o - mu
            var = jnp.mean(d * d, axis=-1, keepdims=True)
            y = d * lax.rsqrt(var + LN_EPS) * gg_ref[:, hs] + gb_ref[:, hs]
            o_ref[0, pl.ds(r0, c), hs] = (jax.nn.silu(rg_ref[0, pl.ds(r0, c), hs].astype(F32)) * y).astype(o_ref.dtype)
        return 0

    lax.fori_loop(0, n_chunks, chunk, 0)


def retention_branch(hn, gn_g, gn_b, tr):
    b, t, _ = hn.shape
    w = BRANCH_WIDTH
    hd = RET_HEAD_DIM
    c = RET_CHUNK
    inv = ROPE_BASE ** (-jnp.arange(0, hd, 2, dtype=F32) / hd)
    ang = jnp.arange(t, dtype=F32)[:, None] * inv[None, :]
    cos2 = jnp.concatenate([jnp.cos(ang), jnp.cos(ang)], axis=1)
    sin2 = jnp.concatenate([-jnp.sin(ang), jnp.sin(ang)], axis=1)
    log_g = jnp.log(1.0 - 2.0 ** (-5.0 - jnp.arange(RET_HEADS, dtype=F32)))
    idx = jnp.arange(c, dtype=F32)
    diff = idx[:, None] - idx[None, :]
    decay_in = jnp.where(diff >= 0, jnp.exp(log_g[:, None, None] * jnp.maximum(diff, 0.0)), 0.0)
    ones = jnp.ones((RET_HEADS, c, hd), F32)
    q_dec = jnp.exp(log_g[:, None] * (idx + 1.0))[:, :, None] * ones
    k_dec = jnp.exp(log_g[:, None] * (c - 1.0 - idx))[:, :, None] * ones
    c_dec = jnp.exp(log_g * c)[:, None, None] * ones
    col = lambda cc: (lambda i, j: (i, j, cc // w))
    full3 = lambda i, j: (0, 0, 0)
    return pl.pallas_call(
        _ret_kernel,
        grid=(b, t // tr),
        in_specs=[
            pl.BlockSpec((1, tr, w), col(NAT_RQ)),
            pl.BlockSpec((1, tr, w), col(NAT_RK)),
            pl.BlockSpec((1, tr, w), col(NAT_RV)),
            pl.BlockSpec((1, tr, w), col(NAT_RG)),
            pl.BlockSpec((tr, hd), lambda i, j: (j, 0)),
            pl.BlockSpec((tr, hd), lambda i, j: (j, 0)),
            pl.BlockSpec((RET_HEADS, c, c), full3),
            pl.BlockSpec((RET_HEADS, c, hd), full3),
            pl.BlockSpec((RET_HEADS, c, hd), full3),
            pl.BlockSpec((RET_HEADS, c, hd), full3),
            pl.BlockSpec((1, w), lambda i, j: (0, 0)),
            pl.BlockSpec((1, w), lambda i, j: (0, 0)),
        ],
        out_specs=pl.BlockSpec((1, tr, w), lambda i, j: (i, j, 0)),
        out_shape=jax.ShapeDtypeStruct((b, t, w), BF16),
        scratch_shapes=[pltpu.VMEM((RET_HEADS, hd, hd), F32)],
        compiler_params=_cparams(2),
        name="retention",
    )(hn, hn, hn, hn, cos2, sin2, decay_in, q_dec, k_dec, c_dec, gn_g.reshape(1, w), gn_b.reshape(1, w))


def _layer_norm(z, g, b):
    mu = jnp.mean(z, axis=-1, keepdims=True)
    d = z - mu
    var = jnp.mean(d * d, axis=-1, keepdims=True)
    return d * lax.rsqrt(var + LN_EPS) * g + b


def _merge_kernel(tiles_per_seq, x_ref, ya_ref, cb_ref, cc_ref, ch_ref, pc_ref, ph_ref, cw_ref, yc_ref, mg_ref,
                  wbr_ref, wout_ref, g_ref, b_ref, o_ref, ob_ref):
    first = (pl.program_id(0) % tiles_per_seq) == 0
    u = cc_ref[...].astype(F32) * ch_ref[...].astype(F32)
    prev = pc_ref[...].astype(F32) * ph_ref[...].astype(F32)
    prev = jnp.where(first, 0.0, prev)
    row = lax.broadcasted_iota(jnp.int32, u.shape, 0)
    p1 = prev[CONV_HALO - 1:CONV_HALO, :]
    p2 = prev[CONV_HALO - 2:CONV_HALO - 1, :]
    u1 = jnp.where(row == 0, p1, pltpu.roll(u, 1, 0))
    u2 = jnp.where(row == 0, p2, jnp.where(row == 1, p1, pltpu.roll(u, 2, 0)))
    cw = cw_ref[...]
    y_b = cb_ref[...].astype(F32) * (cw[0:1, :] * u2 + cw[1:2, :] * u1 + cw[2:3, :] * u)

    mix = None
    for c, y in enumerate((ya_ref[...], y_b.astype(BF16), yc_ref[...])):
        proj = jnp.dot(y, wbr_ref[c], preferred_element_type=F32)
        gate = jax.nn.sigmoid(mg_ref[:, c * D_MODEL:(c + 1) * D_MODEL].astype(F32))
        mix = gate * proj if mix is None else mix + gate * proj
    m = jnp.dot(mix.astype(BF16), wout_ref[...], preferred_element_type=F32)
    out = _layer_norm(ALPHA * x_ref[...] + m, g_ref[...], b_ref[...])
    o_ref[...] = out
    ob_ref[...] = out.astype(BF16)


def merge_branches(x, ya, yc, hn, conv_w, w_br, w_out, g, b, tm, t):
    m = x.shape[0]
    d = D_MODEL
    w = BRANCH_WIDTH
    row = lambda i: (i, 0)
    col = lambda c: (lambda i: (i, c // w))
    hal = lambda c: (lambda i: (jnp.maximum(i * (tm // CONV_HALO) - 1, 0), c // w))
    wpad = jnp.zeros((8, w), F32).at[:CONV_K].set(conv_w)
    return pl.pallas_call(
        functools.partial(_merge_kernel, t // tm),
        grid=(m // tm,),
        in_specs=[
            pl.BlockSpec((tm, d), row),
            pl.BlockSpec((tm, w), row),
            pl.BlockSpec((tm, w), col(NAT_CB)),
            pl.BlockSpec((tm, w), col(NAT_CC)),
            pl.BlockSpec((tm, w), col(NAT_CH)),
            pl.BlockSpec((CONV_HALO, w), hal(NAT_CC)),
            pl.BlockSpec((CONV_HALO, w), hal(NAT_CH)),
            pl.BlockSpec((8, w), lambda i: (0, 0)),
            pl.BlockSpec((tm, w), row),
            pl.BlockSpec((tm, 3 * d), row),
            pl.BlockSpec((3, w, d), lambda i: (0, 0, 0)),
            pl.BlockSpec((d, d), lambda i: (0, 0)),
            pl.BlockSpec((1, d), lambda i: (0, 0)),
            pl.BlockSpec((1, d), lambda i: (0, 0)),
        ],
        out_specs=[pl.BlockSpec((tm, d), row), pl.BlockSpec((tm, d), row)],
        out_shape=[jax.ShapeDtypeStruct((m, d), F32), jax.ShapeDtypeStruct((m, d), BF16)],
        compiler_params=_cparams(1),
        name="merge",
    )(x, ya, hn, hn, hn, hn, hn, wpad, yc, hn, w_br.astype(BF16), w_out.astype(BF16), g.reshape(1, d),
      b.reshape(1, d))


def _mlp_kernel(te_ref, tv_ref, x_ref, wg_ref, wu_ref, wd_ref, o_ref, acc_ref):
    i = pl.program_id(0)
    f = pl.program_id(1)

    @pl.when(tv_ref[i] > 0)
    def _():
        x = x_ref[...]
        gte = jnp.dot(x, wg_ref[0], preferred_element_type=F32)
        up = jnp.dot(x, wu_ref[0], preferred_element_type=F32)
        h = (jax.nn.silu(gte) * up).astype(BF16)
        part = jnp.dot(h, wd_ref[0], preferred_element_type=F32)

        @pl.when(f == 0)
        def _():
            acc_ref[...] = part

        @pl.when(f > 0)
        def _():
            acc_ref[...] = acc_ref[...] + part

        @pl.when(f == pl.num_programs(1) - 1)
        def _():
            o_ref[...] = acc_ref[...].astype(o_ref.dtype)

    @pl.when((tv_ref[i] == 0) & (f == 0))
    def _():
        o_ref[...] = jnp.zeros_like(o_ref)


def grouped_swiglu(xs, w_gate, w_up, w_down, tile_expert, tile_valid, tm, tf):
    n, d = xs.shape
    ff = w_gate.shape[2]
    grid_spec = pltpu.PrefetchScalarGridSpec(
        num_scalar_prefetch=2,
        grid=(n // tm, ff // tf),
        in_specs=[
            pl.BlockSpec((tm, d), lambda i, f, te, tv: (i, 0)),
            pl.BlockSpec((1, d, tf), lambda i, f, te, tv: (te[i], 0, f)),
            pl.BlockSpec((1, d, tf), lambda i, f, te, tv: (te[i], 0, f)),
            pl.BlockSpec((1, tf, d), lambda i, f, te, tv: (te[i], f, 0)),
        ],
        out_specs=pl.BlockSpec((tm, d), lambda i, f, te, tv: (i, 0)),
        scratch_shapes=[pltpu.VMEM((tm, d), F32)],
    )
    return pl.pallas_call(
        _mlp_kernel,
        grid_spec=grid_spec,
        out_shape=jax.ShapeDtypeStruct((n, d), BF16),
        compiler_params=_cparams(2),
        name="grouped_swiglu",
    )(tile_expert, tile_valid, xs, w_gate, w_up, w_down)


def _add_ln_kernel(*refs):
    x_ref, sc_ref, *adds, g_ref, b_ref, o_ref = refs
    z = ALPHA * x_ref[...]
    for k, a in enumerate(adds):
        z = z + sc_ref[:, k:k + 1] * a[...].astype(F32)
    o_ref[...] = _layer_norm(z, g_ref[...], b_ref[...])


def add_layer_norm(x, adds, scales, g, b, tm):
    m, d = x.shape
    row = lambda i: (i, 0)
    return pl.pallas_call(
        _add_ln_kernel,
        grid=(m // tm,),
        in_specs=([pl.BlockSpec((tm, d), row), pl.BlockSpec((tm, len(adds)), row)]
                  + [pl.BlockSpec((tm, d), row)] * len(adds) + [pl.BlockSpec((1, d), lambda i: (0, 0))] * 2),
        out_specs=pl.BlockSpec((tm, d), row),
        out_shape=jax.ShapeDtypeStruct((m, d), F32),
        compiler_params=_cparams(1),
        name="add_layer_norm",
    )(x, scales, *adds, g.reshape(1, d), b.reshape(1, d))


def _router_kernel(x_ref, wh_ref, wl_ref, o_ref):
    x = x_ref[...]
    xh = x.astype(BF16)
    xl = (x - xh.astype(F32)).astype(BF16)
    wh = wh_ref[...]
    o_ref[...] = (jnp.dot(xh, wh, preferred_element_type=F32) + jnp.dot(xl, wh, preferred_element_type=F32)
                  + jnp.dot(xh, wl_ref[...], preferred_element_type=F32))


def router_logits(x, w_router, tm):
    m, d = x.shape
    wp = jnp.zeros((d, 128), F32).at[:, :N_EXPERTS].set(w_router)
    wh = wp.astype(BF16)
    wl = (wp - wh.astype(F32)).astype(BF16)
    return pl.pallas_call(
        _router_kernel,
        grid=(m // tm,),
        in_specs=[pl.BlockSpec((tm, d), lambda i: (i, 0)), pl.BlockSpec((d, 128), lambda i: (0, 0)),
                  pl.BlockSpec((d, 128), lambda i: (0, 0))],
        out_specs=pl.BlockSpec((tm, 128), lambda i: (i, 0)),
        out_shape=jax.ShapeDtypeStruct((m, 128), F32),
        compiler_params=_cparams(1),
        name="router",
    )(x, wh, wl)


def _dense_ffn_kernel(xb_ref, x_ref, wg_ref, wu_ref, wd_ref, g_ref, b_ref, o_ref, acc_ref):
    f = pl.program_id(1)
    xb = xb_ref[...]
    gte = jnp.dot(xb, wg_ref[...], preferred_element_type=F32)
    up = jnp.dot(xb, wu_ref[...], preferred_element_type=F32)
    h = (jax.nn.silu(gte) * up).astype(BF16)
    part = jnp.dot(h, wd_ref[...], preferred_element_type=F32)

    @pl.when(f == 0)
    def _():
        acc_ref[...] = part

    @pl.when(f > 0)
    def _():
        acc_ref[...] = acc_ref[...] + part

    @pl.when(f == pl.num_programs(1) - 1)
    def _():
        o_ref[...] = _layer_norm(ALPHA * x_ref[...] + acc_ref[...], g_ref[...], b_ref[...])


def dense_ffn(x, xb, w_gate, w_up, w_down, g, b):
    m, d = x.shape
    ff = w_gate.shape[1]
    tm = min(512, m)
    tf = 1792
    row = lambda i, f: (i, 0)
    return pl.pallas_call(
        _dense_ffn_kernel,
        grid=(m // tm, ff // tf),
        in_specs=[
            pl.BlockSpec((tm, d), row),
            pl.BlockSpec((tm, d), row),
            pl.BlockSpec((d, tf), lambda i, f: (0, f)),
            pl.BlockSpec((d, tf), lambda i, f: (0, f)),
            pl.BlockSpec((tf, d), lambda i, f: (f, 0)),
            pl.BlockSpec((1, d), lambda i, f: (0, 0)),
            pl.BlockSpec((1, d), lambda i, f: (0, 0)),
        ],
        out_specs=pl.BlockSpec((tm, d), row),
        out_shape=jax.ShapeDtypeStruct((m, d), F32),
        scratch_shapes=[pltpu.VMEM((tm, d), F32)],
        compiler_params=_cparams(2),
        name="dense_ffn",
    )(xb, x, w_gate.astype(BF16), w_up.astype(BF16), w_down.astype(BF16), g.reshape(1, d), b.reshape(1, d))


def moe_ffn(x, xb, w_router, w_gate, w_up, w_down, g, b):
    m = x.shape[0]
    tm = 512
    logits = router_logits(x, w_router, min(1024, m))[:, :N_EXPERTS]
    top_v, top_i = lax.top_k(logits, TOP_K)
    probs = jax.nn.softmax(top_v, axis=-1)
    e_flat = top_i.reshape(-1)
    onehot = (e_flat[:, None] == jnp.arange(N_EXPERTS)[None, :]).astype(jnp.int32)
    csum = jnp.cumsum(onehot, axis=0)
    counts = csum[-1]
    rank = jnp.take_along_axis(csum, e_flat[:, None], axis=1)[:, 0] - 1
    padded = ((counts + tm - 1) // tm) * tm
    ends = jnp.cumsum(padded)
    starts = ends - padded
    pos = starts[e_flat] + rank
    n_slots = TOP_K * m + N_EXPERTS * tm
    n_tiles = n_slots // tm
    tok = jnp.arange(TOP_K * m, dtype=jnp.int32) // TOP_K
    src = jnp.zeros((n_slots,), jnp.int32).at[pos].set(tok)
    tile_start = jnp.arange(n_tiles, dtype=jnp.int32) * tm
    tile_expert = jnp.minimum(jnp.searchsorted(ends, tile_start, side="right"), N_EXPERTS - 1).astype(jnp.int32)
    tile_valid = (tile_start < ends[-1]).astype(jnp.int32)
    xs = jnp.take(xb, src, axis=0)
    ys = grouped_swiglu(xs, w_gate.astype(BF16), w_up.astype(BF16), w_down.astype(BF16),
                        tile_expert, tile_valid, tm, 1792)
    pos2 = pos.reshape(m, TOP_K)
    y0 = jnp.take(ys, pos2[:, 0], axis=0)
    y1 = jnp.take(ys, pos2[:, 1], axis=0)
    return add_layer_norm(x, [y0, y1], probs, g, b, min(1024, m))


def _pack_in_weights(w):
    d = w.shape[0]
    o = 0
    q = w[:, o:o + 512]; o += 512
    kv = w[:, o:o + 768].reshape(d, 3, 2, NSA_KV_GROUPS, HEAD_DIM); o += 768
    ng = w[:, o:o + 24]; o += 24
    rest = w[:, o:o + 7 * 512]; o += 7 * 512
    mg = w[:, o:]
    kpack = jnp.concatenate([kv[:, 1, 0], kv[:, 2, 0]], axis=-1).reshape(d, 2 * NSA_KV_GROUPS * HEAD_DIM)
    kc = kv[:, 0, 0].reshape(d, NSA_KV_GROUPS * HEAD_DIM)
    vc = kv[:, 0, 1].reshape(d, NSA_KV_GROUPS * HEAD_DIM)
    w_nat = jnp.concatenate([mg, rest, kpack, kc, vc], axis=1).astype(BF16)
    vs = kv[:, 1, 1].reshape(d, NSA_KV_GROUPS * HEAD_DIM)
    vw = kv[:, 2, 1].reshape(d, NSA_KV_GROUPS * HEAD_DIM)
    ngp = jnp.pad(ng.reshape(d, NSA_KV_GROUPS, NSA_GROUP * 3), ((0, 0), (0, 0), (0, 4))).reshape(d, 32)
    w_t = jnp.concatenate([q, vs, vw, ngp], axis=1).T.astype(BF16)
    return w_nat, w_t


def token_mixer_ln(x, w_in, cmp_pos, cmp_w1, cmp_b1, cmp_w2, cmp_b2, conv_w, gn_g, gn_b, w_br, w_out, ln_g, ln_b):
    b, t, d = x.shape
    m = b * t
    w_nat, w_t = _pack_in_weights(w_in)
    x2 = x.reshape(m, d)
    hn = mm_nat(x2, w_nat, min(1024, m), 1024, BF16).reshape(b, t, NAT_COLS)
    proj_t = mm_nt(w_t, x, min(1024, t))
    nch = t // CMP_STRIDE
    raw = hn[:, :, NAT_KC:NAT_KC + 2 * NSA_KV_GROUPS * HEAD_DIM].reshape(b, t, 2, NSA_KV_GROUPS, HEAD_DIM)
    chunks = raw.transpose(2, 0, 3, 1, 4).reshape(2, b, NSA_KV_GROUPS, nch, CMP_STRIDE * HEAD_DIM)
    cn, ct = nsa_compress_kv(chunks, cmp_pos, cmp_w1, cmp_b1, cmp_w2, cmp_b2)
    kc = cn[0].astype(BF16)
    vct = ct[1].astype(BF16)
    vt4 = proj_t[:, TR_VS:TR_NG, :].astype(BF16).reshape(b, 4, HEAD_DIM, t)
    ones_pad = jnp.zeros((b, 4, VT_ROWS - HEAD_DIM, t), BF16).at[:, :, 0].set(1.0)
    vt_pack = jnp.concatenate([vt4, ones_pad], axis=2).reshape(b, 4 * VT_ROWS, t)
    wsel = _selection_weights(t // SLC_BLOCK, nch)
    y_a = nsa_attention(proj_t, kc, vct, hn, vt_pack, wsel)
    y_c = retention_branch(hn, gn_g, gn_b, min(1024, t))
    return merge_branches(x2, y_a.reshape(m, -1), y_c.reshape(m, -1), hn.reshape(m, NAT_COLS), conv_w,
                          w_br, w_out, ln_g, ln_b, min(512, t), t)


def kernel(x, w_in, cmp_pos, cmp_w1, cmp_b1, cmp_w2, cmp_b2, conv_w, ret_gn_g, ret_gn_b, w_br, w_out,
           ln1_g, ln1_b, ln2_g, ln2_b, ffn_gate, ffn_up, ffn_down, moe_router, moe_gate, moe_up, moe_down):
    b, t, d = x.shape
    for l in range(DEPTH):
        x2, x2b = token_mixer_ln(x, w_in[l], cmp_pos[l], cmp_w1[l], cmp_b1[l], cmp_w2[l], cmp_b2[l], conv_w[l],
                                 ret_gn_g[l], ret_gn_b[l], w_br[l], w_out[l], ln1_g[l], ln1_b[l])
        if l % 2 == 0:
            x2 = dense_ffn(x2, x2b, ffn_gate[l // 2], ffn_up[l // 2], ffn_down[l // 2], ln2_g[l], ln2_b[l])
        else:
            x2 = moe_ffn(x2, x2b, moe_router[l // 2], moe_gate[l // 2], moe_up[l // 2], moe_down[l // 2],
                         ln2_g[l], ln2_b[l])
        x = x2.reshape(b, t, d)
    return x
```

```python
import functools
import math

import jax
import jax.numpy as jnp
from jax import lax
from jax.experimental import pallas as pl
from jax.experimental.pallas import tpu as pltpu

F32 = jnp.float32
BF16 = jnp.bfloat16

D_MODEL = 1024
DEPTH = 4
BRANCH_WIDTH = 512
HEAD_DIM = 64
NSA_HEADS = 8
NSA_KV_GROUPS = 2
NSA_GROUP = 4
CMP_LEN = 32
CMP_STRIDE = 16
SLC_BLOCK = 64
SLC_TOPN = 16
WINDOW = 512
CMP_HIDDEN = 256
Q_BLOCK = 128
CONV_K = 3
CONV_HALO = 16
RET_HEADS = 4
RET_HEAD_DIM = 128
RET_CHUNK = 128
ROPE_BASE = 10000.0
D_FF = 3584
N_EXPERTS = 8
TOP_K = 2
LN_EPS = 1e-5
ALPHA = (2.0 * DEPTH) ** 0.25

NAT_MG = 0
NAT_CB = 3072
NAT_CC = 3584
NAT_CH = 4096
NAT_RQ = 4608
NAT_RK = 5120
NAT_RV = 5632
NAT_RG = 6144
NAT_KP = 6656
NAT_KC = 6912
NAT_VC = 7040
NAT_COLS = 7168
TR_Q = 0
TR_VS = 512
TR_VW = 640
TR_NG = 768
TR_ROWS = 800

NEG = -1e30
SEL_TK = 1024
LOG2E = 1.4426950408889634
NSA_QSUB = 2
CMP_CLASS_ROWS = 256
PG_PAD = 8
PICKED = -3e38
VT_ROWS = 80
VMEM_LIMIT = 48 * 1024 * 1024


def _cparams(n_axes, vmem=VMEM_LIMIT):
    return pltpu.CompilerParams(dimension_semantics=("arbitrary",) * n_axes, vmem_limit_bytes=vmem)


def _mm_nat_kernel(x_ref, w_ref, o_ref, xb_ref):
    @pl.when(pl.program_id(1) == 0)
    def _():
        xb_ref[...] = x_ref[...].astype(BF16)

    o_ref[...] = jnp.dot(xb_ref[...], w_ref[...], preferred_element_type=F32).astype(o_ref.dtype)


def mm_nat(x, w, tm, tn, out_dtype=F32):
    m, k = x.shape
    n = w.shape[1]
    return pl.pallas_call(
        _mm_nat_kernel,
        grid=(m // tm, n // tn),
        in_specs=[pl.BlockSpec((tm, k), lambda i, j: (i, 0)), pl.BlockSpec((k, tn), lambda i, j: (0, j))],
        out_specs=pl.BlockSpec((tm, tn), lambda i, j: (i, j)),
        out_shape=jax.ShapeDtypeStruct((m, n), out_dtype),
        scratch_shapes=[pltpu.VMEM((tm, k), BF16)],
        compiler_params=_cparams(2),
        name="mm_nat",
    )(x, w)


def _mm_nt_kernel(w_ref, x_ref, o_ref):
    o_ref[0] = lax.dot_general(w_ref[...], x_ref[0].astype(BF16), (((1,), (1,)), ((), ())),
                               preferred_element_type=F32)


def mm_nt(w_t, x, tm):
    b, t, k = x.shape
    r = w_t.shape[0]
    return pl.pallas_call(
        _mm_nt_kernel,
        grid=(b, t // tm),
        in_specs=[pl.BlockSpec((r, k), lambda i, j: (0, 0)), pl.BlockSpec((1, tm, k), lambda i, j: (i, j, 0))],
        out_specs=pl.BlockSpec((1, r, tm), lambda i, j: (i, 0, j)),
        out_shape=jax.ShapeDtypeStruct((b, r, t), F32),
        compiler_params=_cparams(2),
        name="mm_nt",
    )(w_t, x)


def _cmp_kernel(c_ref, pos_ref, w1_ref, b1_ref, w2_ref, w2t_ref, b2_ref, b2t_ref, on_ref, ot_ref):
    half = CMP_STRIDE * HEAD_DIM
    c = c_ref[0, 0, 0].astype(BF16)
    w1 = w1_ref[0]
    a = jnp.dot(c, w1[:half], preferred_element_type=F32)
    bm = jnp.dot(c, w1[half:], preferred_element_type=F32)
    nch = a.shape[0]
    bm = pltpu.roll(bm, nch - 1, 0)
    posb = jnp.dot(pos_ref[0].astype(BF16), w1, preferred_element_type=F32)[0:1]
    hid = jax.nn.gelu(a + bm + posb + b1_ref[0])
    hb = hid.astype(BF16)
    on_ref[0, 0, 0] = jnp.dot(hb, w2_ref[0], preferred_element_type=F32) + b2_ref[0]
    ot_ref[0, 0, 0] = lax.dot_general(w2t_ref[0], hb, (((1,), (1,)), ((), ())),
                                      preferred_element_type=F32) + b2t_ref[0]


def nsa_compress_kv(chunks, pos, w1, b1, w2, b2):
    _, b, g, nch, cw = chunks.shape
    pos_flat = jnp.broadcast_to(pos.reshape(2, 1, CMP_LEN * HEAD_DIM), (2, 8, CMP_LEN * HEAD_DIM))
    w1b = w1.astype(BF16)
    w2b = w2.astype(BF16)
    w2t = jnp.swapaxes(w2, 1, 2).astype(BF16)
    b1r = b1.reshape(2, 1, CMP_HIDDEN)
    b2r = b2.reshape(2, 1, HEAD_DIM)
    b2t = b2.reshape(2, HEAD_DIM, 1)
    sel = lambda kv, i, j: (kv, 0, 0)
    return pl.pallas_call(
        _cmp_kernel,
        grid=(2, b, g),
        in_specs=[
            pl.BlockSpec((1, 1, 1, nch, cw), lambda kv, i, j: (kv, i, j, 0, 0)),
            pl.BlockSpec((1, 8, CMP_LEN * HEAD_DIM), sel),
            pl.BlockSpec((1, CMP_LEN * HEAD_DIM, CMP_HIDDEN), sel),
            pl.BlockSpec((1, 1, CMP_HIDDEN), sel),
            pl.BlockSpec((1, CMP_HIDDEN, HEAD_DIM), sel),
            pl.BlockSpec((1, HEAD_DIM, CMP_HIDDEN), sel),
            pl.BlockSpec((1, 1, HEAD_DIM), sel),
            pl.BlockSpec((1, HEAD_DIM, 1), sel),
        ],
        out_specs=[
            pl.BlockSpec((1, 1, 1, nch, HEAD_DIM), lambda kv, i, j: (kv, i, j, 0, 0)),
            pl.BlockSpec((1, 1, 1, HEAD_DIM, nch), lambda kv, i, j: (kv, i, j, 0, 0)),
        ],
        out_shape=[
            jax.ShapeDtypeStruct((2, b, g, nch, HEAD_DIM), F32),
            jax.ShapeDtypeStruct((2, b, g, HEAD_DIM, nch), F32),
        ],
        compiler_params=_cparams(3),
        name="nsa_compress",
    )(chunks, pos_flat, w1b, b1r, w2b, w2t, b2r, b2t)


def _accumulate(s, mt, vt, m, acc):
    m_new = jnp.maximum(m, mt)
    alpha = jnp.exp2(m - m_new)
    p = jnp.exp2(s - m_new).astype(BF16)
    return m_new, alpha * acc + jnp.dot(vt, p, preferred_element_type=F32)


def _nsa_kernel(qt_ref, ng_ref, kc_ref, vct_ref, kp_ref, vst_ref, vwt_ref, e_ref, tri_ref, wpat_ref,
                cpat_ref, o_ref, bias_ref, w_ref, sbuf_ref, oc_ref, pg_ref):
    nq = Q_BLOCK
    nsub = qt_ref.shape[2] // nq
    ncol = NSA_GROUP * nq
    n_sel = bias_ref.shape[1]
    ncp = kc_ref.shape[2]
    kw = 2 * HEAD_DIM
    bpt = SEL_TK // SLC_BLOCK
    n_tiles_total = kp_ref.shape[1] // SEL_TK
    qb_first = pl.program_id(2) * nsub
    wlen = WINDOW + nq

    def prepare(u):
        qb = qb_first + u
        q0 = pl.multiple_of(qb * nq, nq)
        qt = qt_ref[0, :, u * nq:(u + 1) * nq] * (HEAD_DIM ** -0.5 * LOG2E)
        qs = jnp.concatenate([qt[r * HEAD_DIM:(r + 1) * HEAD_DIM, :] for r in range(NSA_GROUP)], axis=1)
        zq = jnp.zeros_like(qs)
        q_c = qs.astype(BF16)
        q_slc = jnp.concatenate([qs, zq], axis=0).astype(BF16)
        q_win = jnp.concatenate([zq, qs], axis=0).astype(BF16)
        tq = q0 + (lax.broadcasted_iota(jnp.int32, (1, ncol), 1) & (nq - 1))

        coff = pl.multiple_of(ncp - qb * (nq // CMP_STRIDE), nq // CMP_STRIDE)
        tq1 = q0 + lax.broadcasted_iota(jnp.int32, (1, nq), 1)
        cur = jnp.right_shift(tq1, 6).astype(F32)
        first_diag = (2 * qb).astype(F32)
        pg_ref[u, 0:PG_PAD, :] = jnp.zeros((PG_PAD, nq), F32)
        crow = min(CMP_CLASS_ROWS, ncp)
        cls = qb // (crow // (nq // CMP_STRIDE))
        for c in range(ncp // crow):
            nc_c = crow * (c + 1)
            ns_c = nc_c * CMP_STRIDE // SLC_BLOCK

            @pl.when(cls == c)
            def _(nc_c=nc_c, ns_c=ns_c):
                s_c = (jnp.dot(kc_ref[0, 0, 0:nc_c, :], q_c, preferred_element_type=F32)
                       + jnp.concatenate([cpat_ref[pl.ds(coff, nc_c), :]] * NSA_GROUP, axis=1))
                m_c = jnp.max(s_c, axis=0, keepdims=True)
                e_c = jnp.exp2(s_c - m_c)
                l_c = jnp.sum(e_c, axis=0, keepdims=True)
                p_c = e_c * jnp.where(tq >= CMP_LEN - 1, 1.0 / l_c, 0.0)
                oc_ref[u] = jnp.dot(vct_ref[0, 0, :, 0:nc_c], p_c.astype(BF16), preferred_element_type=F32)

                p_grp = p_c[:, 0:nq]
                for r in range(1, NSA_GROUP):
                    p_grp = p_grp + p_c[:, r * nq:(r + 1) * nq]
                pg_ref[u, PG_PAD:PG_PAD + nc_c, :] = p_grp
                ratio = SLC_BLOCK // CMP_STRIDE
                taps = [pg_ref[u, pl.ds(PG_PAD - 1 + k, ns_c, stride=ratio), :] for k in range(ratio + 1)]
                score = taps[0] + 2.0 * (taps[1] + taps[2] + taps[3]) + taps[4]
                j_io = lax.broadcasted_iota(jnp.int32, (ns_c, nq), 0).astype(F32)
                score = jnp.where(j_io <= cur, score, -1.0)
                score = jnp.where(j_io == 0.0, PICKED, score)
                score = jnp.where(j_io == cur, PICKED, score)
                score = jnp.where(j_io == cur - 1.0, PICKED, score)
                for _ in range(SLC_TOPN - 3):
                    mx = jnp.max(score, axis=0, keepdims=True)
                    first = jnp.min(jnp.where(score == mx, j_io, float(ns_c)), axis=0, keepdims=True)
                    score = jnp.where(j_io == first, PICKED, score)
                bias = jnp.where(j_io < first_diag, jnp.where(score == PICKED, 0.0, NEG), NEG)
                bias_ref[u, 0:ns_c, :] = bias.astype(BF16)
                if ns_c < n_sel:
                    bias_ref[u, ns_c:, :] = jnp.full((n_sel - ns_c, nq), NEG, BF16)

        o_c = oc_ref[u]

        wstart = pl.multiple_of(jnp.maximum(q0 - WINDOW, 0), nq)
        woff = pl.multiple_of(jnp.maximum(WINDOW - q0, 0), nq)
        s_w = jnp.dot(kp_ref[0, pl.ds(wstart, wlen), :], q_win, preferred_element_type=F32)
        s_w = s_w + jnp.concatenate([wpat_ref[pl.ds(woff, wlen), :]] * NSA_GROUP, axis=1)
        p_w = jnp.exp2(s_w - jnp.max(s_w, axis=0, keepdims=True)).astype(BF16)
        acc_w = jnp.dot(vwt_ref[0, :, pl.ds(wstart, wlen)], p_w, preferred_element_type=F32)
        o_w = acc_w[0:HEAD_DIM] / acc_w[HEAD_DIM:HEAD_DIM + 1]

        for slot in range(2):
            w_ref[slot, u, 0:kw, :] = q_slc
            w_ref[slot, u, kw:, :] = jnp.zeros((w_ref.shape[2] - kw, ncol), BF16)
        return q0, q_slc, o_c, o_w

    prepared = [prepare(u) for u in range(nsub)]

    def tile_start(j):
        jc = jnp.minimum(j, n_tiles_total - 1)
        return jc, pl.multiple_of(jc * SEL_TK, SEL_TK)

    def produce(j, slot):
        jc, k0 = tile_start(j)
        lhs = jnp.concatenate([kp_ref[0, pl.ds(k0, SEL_TK), :], e_ref[...]], axis=1)
        mts = []
        for u in range(nsub):
            b16 = bias_ref[u, pl.ds(pl.multiple_of(jc * bpt, bpt), bpt), :]
            w_ref[slot, u, kw:kw + bpt, :] = jnp.concatenate([b16] * NSA_GROUP, axis=1)
            s = jnp.dot(lhs, w_ref[slot, u], preferred_element_type=F32)
            sbuf_ref[slot, u] = s
            mts.append(jnp.max(s, axis=0, keepdims=True))
        return tuple(mts)

    def consume(j, slot, mts, state):
        _, k0 = tile_start(j)
        vt = vst_ref[0, :, pl.ds(k0, SEL_TK)]
        return tuple(_accumulate(sbuf_ref[slot, u], mts[u], vt, *state[u]) for u in range(nsub))

    def pair(i, carry, last):
        mt_a, state = carry
        mt_b = produce(2 * i + 1, 1)
        state = consume(2 * i, 0, mt_a, state)
        if last:
            return consume(2 * i + 1, 1, mt_b, state)
        mt_a = produce(2 * i + 2, 0)
        state = consume(2 * i + 1, 1, mt_b, state)
        return mt_a, state

    n_main = ((qb_first + nsub - 1) * nq + (SEL_TK - 1)) // SEL_TK
    n_pairs = jnp.maximum((n_main + 1) // 2, 1)
    state = tuple((jnp.full((1, ncol), NEG, F32), jnp.zeros((vst_ref.shape[1], ncol), F32)) for _ in range(nsub))
    carry = lax.fori_loop(0, n_pairs - 1, lambda i, c: pair(i, c, False), (produce(0, 0), state))
    state = pair(n_pairs - 1, carry, True)

    for u in range(nsub):
        q0, q_slc, o_c, o_w = prepared[u]
        s_d = jnp.dot(kp_ref[0, pl.ds(q0, nq), :], q_slc, preferred_element_type=F32) + tri_ref[...]
        _, acc_s = _accumulate(s_d, jnp.max(s_d, axis=0, keepdims=True), vst_ref[0, :, pl.ds(q0, nq)], *state[u])
        o_s = acc_s[0:HEAD_DIM] / acc_s[HEAD_DIM:HEAD_DIM + 1]
        gts = jax.nn.sigmoid(ng_ref[0, :, u * nq:(u + 1) * nq])
        rows = []
        for r in range(NSA_GROUP):
            sl = slice(r * nq, (r + 1) * nq)
            rows.append(gts[3 * r:3 * r + 1, :] * o_c[:, sl] + gts[3 * r + 1:3 * r + 2, :] * o_s[:, sl]
                        + gts[3 * r + 2:3 * r + 3, :] * o_w[:, sl])
        y_t = jnp.concatenate(rows, axis=0)
        o_ref[0, u * nq:(u + 1) * nq, :] = y_t.T.astype(o_ref.dtype)


def _nsa_constants(ncp):
    nq = Q_BLOCK
    ql = jnp.arange(nq)[None, :]
    r = jnp.arange(SEL_TK)[:, None]
    e = (r // SLC_BLOCK == jnp.arange(2 * HEAD_DIM)[None, :]).astype(BF16)
    rd = jnp.arange(nq)[:, None]
    tri = jnp.tile(jnp.where(rd <= ql, 0.0, NEG).astype(F32), (1, NSA_GROUP))
    rw = jnp.arange(2 * WINDOW + nq)[:, None]
    wpat = jnp.where((rw > ql) & (rw <= ql + WINDOW), 0.0, NEG).astype(F32)
    rc = jnp.arange(2 * ncp)[:, None] - ncp
    cpat = jnp.where(CMP_STRIDE * rc + (CMP_LEN - 1) <= ql, 0.0, NEG).astype(F32)
    return e, tri, wpat, cpat


def nsa_attention(proj_t, kc, vct, hn, vt_pack):
    b, _, t = proj_t.shape
    ncp = kc.shape[2]
    n_sel = t // SLC_BLOCK
    gq = NSA_GROUP * HEAD_DIM
    qstep = NSA_QSUB * Q_BLOCK
    e, tri, wpat, cpat = _nsa_constants(ncp)
    const2 = lambda i, g, q: (0, 0)
    return pl.pallas_call(
        _nsa_kernel,
        grid=(b, NSA_KV_GROUPS, t // qstep),
        in_specs=[
            pl.BlockSpec((1, gq, qstep), lambda i, g, q: (i, g, q)),
            pl.BlockSpec((1, 16, qstep), lambda i, g, q: (i, TR_NG // 16 + g, q)),
            pl.BlockSpec((1, 1, ncp, HEAD_DIM), lambda i, g, q: (i, g, 0, 0)),
            pl.BlockSpec((1, 1, HEAD_DIM, ncp), lambda i, g, q: (i, g, 0, 0)),
            pl.BlockSpec((1, t, 2 * HEAD_DIM), lambda i, g, q: (i, 0, NAT_KP // (2 * HEAD_DIM) + g)),
            pl.BlockSpec((1, VT_ROWS, t), lambda i, g, q: (i, g, 0)),
            pl.BlockSpec((1, VT_ROWS, t), lambda i, g, q: (i, NSA_KV_GROUPS + g, 0)),
            pl.BlockSpec(e.shape, const2),
            pl.BlockSpec(tri.shape, const2),
            pl.BlockSpec(wpat.shape, const2),
            pl.BlockSpec(cpat.shape, const2),
        ],
        out_specs=pl.BlockSpec((1, qstep, gq), lambda i, g, q: (i, q, g)),
        out_shape=jax.ShapeDtypeStruct((b, t, NSA_HEADS * HEAD_DIM), BF16),
        scratch_shapes=[pltpu.VMEM((NSA_QSUB, n_sel, Q_BLOCK), BF16),
                        pltpu.VMEM((2, NSA_QSUB, 4 * HEAD_DIM, NSA_GROUP * Q_BLOCK), BF16),
                        pltpu.VMEM((2, NSA_QSUB, SEL_TK, NSA_GROUP * Q_BLOCK), F32),
                        pltpu.VMEM((NSA_QSUB, HEAD_DIM, NSA_GROUP * Q_BLOCK), F32),
                        pltpu.VMEM((NSA_QSUB, PG_PAD + ncp, Q_BLOCK), F32)],
        compiler_params=_cparams(3),
        name="nsa_attention",
    )(proj_t, proj_t, kc, vct, hn, vt_pack, vt_pack, e, tri, wpat, cpat)


def _ret_kernel(rq_ref, rk_ref, rv_ref, rg_ref, cos_ref, sin_ref, dec_ref, qd_ref, kd_ref, cd_ref,
                gg_ref, gb_ref, o_ref, st_ref):
    c = RET_CHUNK
    hd = RET_HEAD_DIM

    @pl.when(pl.program_id(1) == 0)
    def _():
        st_ref[...] = jnp.zeros_like(st_ref)

    n_chunks = rq_ref.shape[1] // c

    def chunk(ci, _):
        r0 = pl.multiple_of(ci * c, c)
        cos = cos_ref[pl.ds(r0, c), :]
        sin = sin_ref[pl.ds(r0, c), :]
        for h in range(RET_HEADS):
            hs = slice(h * hd, (h + 1) * hd)
            q = rq_ref[0, pl.ds(r0, c), hs].astype(F32)
            k = rk_ref[0, pl.ds(r0, c), hs].astype(F32)
            v = rv_ref[0, pl.ds(r0, c), hs]
            q = q * cos + pltpu.roll(q, hd // 2, 1) * sin
            k = (k * cos + pltpu.roll(k, hd // 2, 1) * sin) * (hd ** -0.5)
            qb = q.astype(BF16)
            kb = k.astype(BF16)
            vb = v.astype(BF16)
            inner = lax.dot_general(qb, kb, (((1,), (1,)), ((), ())), preferred_element_type=F32) * dec_ref[h]
            st = st_ref[h]
            o = (jnp.dot(inner.astype(BF16), vb, preferred_element_type=F32)
                 + jnp.dot(qb, st.astype(BF16), preferred_element_type=F32) * qd_ref[h])
            kdt = (k * kd_ref[h]).T.astype(BF16)
            st_ref[h] = st * cd_ref[h] + jnp.dot(kdt, vb, preferred_element_type=F32)
            mu = jnp.mean(o, axis=-1, keepdims=True)
            d = o - mu
            var = jnp.mean(d * d, axis=-1, keepdims=True)
            y = d * lax.rsqrt(var + LN_EPS) * gg_ref[:, hs] + gb_ref[:, hs]
            o_ref[0, pl.ds(r0, c), hs] = (jax.nn.silu(rg_ref[0, pl.ds(r0, c), hs].astype(F32)) * y).astype(o_ref.dtype)
        return 0

    lax.fori_loop(0, n_chunks, chunk, 0)


def retention_branch(hn, gn_g, gn_b, tr):
    b, t, _ = hn.shape
    w = BRANCH_WIDTH
    hd = RET_HEAD_DIM
    c = RET_CHUNK
    inv = ROPE_BASE ** (-jnp.arange(0, hd, 2, dtype=F32) / hd)
    ang = jnp.arange(t, dtype=F32)[:, None] * inv[None, :]
    cos2 = jnp.concatenate([jnp.cos(ang), jnp.cos(ang)], axis=1)
    sin2 = jnp.concatenate([-jnp.sin(ang), jnp.sin(ang)], axis=1)
    log_g = jnp.log(1.0 - 2.0 ** (-5.0 - jnp.arange(RET_HEADS, dtype=F32)))
    idx = jnp.arange(c, dtype=F32)
    diff = idx[:, None] - idx[None, :]
    decay_in = jnp.where(diff >= 0, jnp.exp(log_g[:, None, None] * jnp.maximum(diff, 0.0)), 0.0)
    ones = jnp.ones((RET_HEADS, c, hd), F32)
    q_dec = jnp.exp(log_g[:, None] * (idx + 1.0))[:, :, None] * ones
    k_dec = jnp.exp(log_g[:, None] * (c - 1.0 - idx))[:, :, None] * ones
    c_dec = jnp.exp(log_g * c)[:, None, None] * ones
    col = lambda cc: (lambda i, j: (i, j, cc // w))
    full3 = lambda i, j: (0, 0, 0)
    return pl.pallas_call(
        _ret_kernel,
        grid=(b, t // tr),
        in_specs=[
            pl.BlockSpec((1, tr, w), col(NAT_RQ)),
            pl.BlockSpec((1, tr, w), col(NAT_RK)),
            pl.BlockSpec((1, tr, w), col(NAT_RV)),
            pl.BlockSpec((1, tr, w), col(NAT_RG)),
            pl.BlockSpec((tr, hd), lambda i, j: (j, 0)),
            pl.BlockSpec((tr, hd), lambda i, j: (j, 0)),
            pl.BlockSpec((RET_HEADS, c, c), full3),
            pl.BlockSpec((RET_HEADS, c, hd), full3),
            pl.BlockSpec((RET_HEADS, c, hd), full3),
            pl.BlockSpec((RET_HEADS, c, hd), full3),
            pl.BlockSpec((1, w), lambda i, j: (0, 0)),
            pl.BlockSpec((1, w), lambda i, j: (0, 0)),
        ],
        out_specs=pl.BlockSpec((1, tr, w), lambda i, j: (i, j, 0)),
        out_shape=jax.ShapeDtypeStruct((b, t, w), BF16),
        scratch_shapes=[pltpu.VMEM((RET_HEADS, hd, hd), F32)],
        compiler_params=_cparams(2),
        name="retention",
    )(hn, hn, hn, hn, cos2, sin2, decay_in, q_dec, k_dec, c_dec, gn_g.reshape(1, w), gn_b.reshape(1, w))


def _layer_norm(z, g, b):
    mu = jnp.mean(z, axis=-1, keepdims=True)
    d = z - mu
    var = jnp.mean(d * d, axis=-1, keepdims=True)
    return d * lax.rsqrt(var + LN_EPS) * g + b


def _merge_kernel(tiles_per_seq, x_ref, ya_ref, cb_ref, cc_ref, ch_ref, pc_ref, ph_ref, cw_ref, yc_ref, mg_ref,
                  wbr_ref, wout_ref, g_ref, b_ref, o_ref, ob_ref):
    first = (pl.program_id(0) % tiles_per_seq) == 0
    u = cc_ref[...].astype(F32) * ch_ref[...].astype(F32)
    prev = pc_ref[...].astype(F32) * ph_ref[...].astype(F32)
    prev = jnp.where(first, 0.0, prev)
    row = lax.broadcasted_iota(jnp.int32, u.shape, 0)
    p1 = prev[CONV_HALO - 1:CONV_HALO, :]
    p2 = prev[CONV_HALO - 2:CONV_HALO - 1, :]
    u1 = jnp.where(row == 0, p1, pltpu.roll(u, 1, 0))
    u2 = jnp.where(row == 0, p2, jnp.where(row == 1, p1, pltpu.roll(u, 2, 0)))
    cw = cw_ref[...]
    y_b = cb_ref[...].astype(F32) * (cw[0:1, :] * u2 + cw[1:2, :] * u1 + cw[2:3, :] * u)

    mix = None
    for c, y in enumerate((ya_ref[...], y_b.astype(BF16), yc_ref[...])):
        proj = jnp.dot(y, wbr_ref[c], preferred_element_type=F32)
        gate = jax.nn.sigmoid(mg_ref[:, c * D_MODEL:(c + 1) * D_MODEL].astype(F32))
        mix = gate * proj if mix is None else mix + gate * proj
    m = jnp.dot(mix.astype(BF16), wout_ref[...], preferred_element_type=F32)
    out = _layer_norm(ALPHA * x_ref[...] + m, g_ref[...], b_ref[...])
    o_ref[...] = out
    ob_ref[...] = out.astype(BF16)


def merge_branches(x, ya, yc, hn, conv_w, w_br, w_out, g, b, tm, t):
    m = x.shape[0]
    d = D_MODEL
    w = BRANCH_WIDTH
    row = lambda i: (i, 0)
    col = lambda c: (lambda i: (i, c // w))
    hal = lambda c: (lambda i: (jnp.maximum(i * (tm // CONV_HALO) - 1, 0), c // w))
    wpad = jnp.zeros((8, w), F32).at[:CONV_K].set(conv_w)
    return pl.pallas_call(
        functools.partial(_merge_kernel, t // tm),
        grid=(m // tm,),
        in_specs=[
            pl.BlockSpec((tm, d), row),
            pl.BlockSpec((tm, w), row),
            pl.BlockSpec((tm, w), col(NAT_CB)),
            pl.BlockSpec((tm, w), col(NAT_CC)),
            pl.BlockSpec((tm, w), col(NAT_CH)),
            pl.BlockSpec((CONV_HALO, w), hal(NAT_CC)),
            pl.BlockSpec((CONV_HALO, w), hal(NAT_CH)),
            pl.BlockSpec((8, w), lambda i: (0, 0)),
            pl.BlockSpec((tm, w), row),
            pl.BlockSpec((tm, 3 * d), row),
            pl.BlockSpec((3, w, d), lambda i: (0, 0, 0)),
            pl.BlockSpec((d, d), lambda i: (0, 0)),
            pl.BlockSpec((1, d), lambda i: (0, 0)),
            pl.BlockSpec((1, d), lambda i: (0, 0)),
        ],
        out_specs=[pl.BlockSpec((tm, d), row), pl.BlockSpec((tm, d), row)],
        out_shape=[jax.ShapeDtypeStruct((m, d), F32), jax.ShapeDtypeStruct((m, d), BF16)],
        compiler_params=_cparams(1),
        name="merge",
    )(x, ya, hn, hn, hn, hn, hn, wpad, yc, hn, w_br.astype(BF16), w_out.astype(BF16), g.reshape(1, d),
      b.reshape(1, d))


def _cast_kernel(x_ref, o_ref):
    o_ref[...] = x_ref[0].astype(o_ref.dtype)


def layer_weights_bf16(w, layer, rows):
    _, e, r, c = w.shape
    return pl.pallas_call(
        _cast_kernel,
        grid=(e, r // rows),
        in_specs=[pl.BlockSpec((1, 1, rows, c), lambda i, j: (layer, i, j, 0))],
        out_specs=pl.BlockSpec((1, rows, c), lambda i, j: (i, j, 0)),
        out_shape=jax.ShapeDtypeStruct((e, r, c), BF16),
        compiler_params=_cparams(2),
        name="cast_bf16",
    )(w)


def _mlp_kernel(te_ref, tv_ref, x_ref, wg_ref, wu_ref, wd_ref, o_ref, acc_ref):
    i = pl.program_id(0)
    f = pl.program_id(1)

    @pl.when(tv_ref[i] > 0)
    def _():
        x = x_ref[...]
        gte = jnp.dot(x, wg_ref[0], preferred_element_type=F32)
        up = jnp.dot(x, wu_ref[0], preferred_element_type=F32)
        h = (jax.nn.silu(gte) * up).astype(BF16)
        part = jnp.dot(h, wd_ref[0], preferred_element_type=F32)

        @pl.when(f == 0)
        def _():
            acc_ref[...] = part

        @pl.when(f > 0)
        def _():
            acc_ref[...] = acc_ref[...] + part

        @pl.when(f == pl.num_programs(1) - 1)
        def _():
            o_ref[...] = acc_ref[...].astype(o_ref.dtype)

    @pl.when((tv_ref[i] == 0) & (f == 0))
    def _():
        o_ref[...] = jnp.zeros_like(o_ref)


def grouped_swiglu(xs, w_gate, w_up, w_down, tile_expert, tile_valid, tm, tf):
    n, d = xs.shape
    ff = w_gate.shape[2]
    grid_spec = pltpu.PrefetchScalarGridSpec(
        num_scalar_prefetch=2,
        grid=(n // tm, ff // tf),
        in_specs=[
            pl.BlockSpec((tm, d), lambda i, f, te, tv: (i, 0)),
            pl.BlockSpec((1, d, tf), lambda i, f, te, tv: (te[i], 0, f)),
            pl.BlockSpec((1, d, tf), lambda i, f, te, tv: (te[i], 0, f)),
            pl.BlockSpec((1, tf, d), lambda i, f, te, tv: (te[i], f, 0)),
        ],
        out_specs=pl.BlockSpec((tm, d), lambda i, f, te, tv: (i, 0)),
        scratch_shapes=[pltpu.VMEM((tm, d), F32)],
    )
    return pl.pallas_call(
        _mlp_kernel,
        grid_spec=grid_spec,
        out_shape=jax.ShapeDtypeStruct((n, d), BF16),
        compiler_params=_cparams(2),
        name="grouped_swiglu",
    )(tile_expert, tile_valid, xs, w_gate, w_up, w_down)


def _add_ln_kernel(*refs):
    x_ref, sc_ref, *adds, g_ref, b_ref, o_ref = refs
    z = ALPHA * x_ref[...]
    for k, a in enumerate(adds):
        z = z + sc_ref[:, k:k + 1] * a[...].astype(F32)
    o_ref[...] = _layer_norm(z, g_ref[...], b_ref[...])


def add_layer_norm(x, adds, scales, g, b, tm):
    m, d = x.shape
    row = lambda i: (i, 0)
    return pl.pallas_call(
        _add_ln_kernel,
        grid=(m // tm,),
        in_specs=([pl.BlockSpec((tm, d), row), pl.BlockSpec((tm, len(adds)), row)]
                  + [pl.BlockSpec((tm, d), row)] * len(adds) + [pl.BlockSpec((1, d), lambda i: (0, 0))] * 2),
        out_specs=pl.BlockSpec((tm, d), row),
        out_shape=jax.ShapeDtypeStruct((m, d), F32),
        compiler_params=_cparams(1),
        name="add_layer_norm",
    )(x, scales, *adds, g.reshape(1, d), b.reshape(1, d))


def _router_kernel(x_ref, wh_ref, wl_ref, o_ref):
    x = x_ref[...]
    xh = x.astype(BF16)
    xl = (x - xh.astype(F32)).astype(BF16)
    wh = wh_ref[...]
    o_ref[...] = (jnp.dot(xh, wh, preferred_element_type=F32) + jnp.dot(xl, wh, preferred_element_type=F32)
                  + jnp.dot(xh, wl_ref[...], preferred_element_type=F32))


def router_logits(x, w_router, tm):
    m, d = x.shape
    wp = jnp.zeros((d, 128), F32).at[:, :N_EXPERTS].set(w_router)
    wh = wp.astype(BF16)
    wl = (wp - wh.astype(F32)).astype(BF16)
    return pl.pallas_call(
        _router_kernel,
        grid=(m // tm,),
        in_specs=[pl.BlockSpec((tm, d), lambda i: (i, 0)), pl.BlockSpec((d, 128), lambda i: (0, 0)),
                  pl.BlockSpec((d, 128), lambda i: (0, 0))],
        out_specs=pl.BlockSpec((tm, 128), lambda i: (i, 0)),
        out_shape=jax.ShapeDtypeStruct((m, 128), F32),
        compiler_params=_cparams(1),
        name="router",
    )(x, wh, wl)


def _dense_ffn_kernel(xb_ref, x_ref, wg_ref, wu_ref, wd_ref, g_ref, b_ref, o_ref, acc_ref):
    f = pl.program_id(1)
    xb = xb_ref[...]
    gte = jnp.dot(xb, wg_ref[0], preferred_element_type=F32)
    up = jnp.dot(xb, wu_ref[0], preferred_element_type=F32)
    h = (jax.nn.silu(gte) * up).astype(BF16)
    part = jnp.dot(h, wd_ref[0], preferred_element_type=F32)

    @pl.when(f == 0)
    def _():
        acc_ref[...] = part

    @pl.when(f > 0)
    def _():
        acc_ref[...] = acc_ref[...] + part

    @pl.when(f == pl.num_programs(1) - 1)
    def _():
        o_ref[...] = _layer_norm(ALPHA * x_ref[...] + acc_ref[...], g_ref[...], b_ref[...])


def dense_ffn(x, xb, w_gate, w_up, w_down, g, b):
    m, d = x.shape
    ff = w_gate.shape[2]
    tm = min(512, m)
    tf = 1792
    row = lambda i, f: (i, 0)
    return pl.pallas_call(
        _dense_ffn_kernel,
        grid=(m // tm, ff // tf),
        in_specs=[
            pl.BlockSpec((tm, d), row),
            pl.BlockSpec((tm, d), row),
            pl.BlockSpec((1, d, tf), lambda i, f: (0, 0, f)),
            pl.BlockSpec((1, d, tf), lambda i, f: (0, 0, f)),
            pl.BlockSpec((1, tf, d), lambda i, f: (0, f, 0)),
            pl.BlockSpec((1, d), lambda i, f: (0, 0)),
            pl.BlockSpec((1, d), lambda i, f: (0, 0)),
        ],
        out_specs=pl.BlockSpec((tm, d), row),
        out_shape=jax.ShapeDtypeStruct((m, d), F32),
        scratch_shapes=[pltpu.VMEM((tm, d), F32)],
        compiler_params=_cparams(2),
        name="dense_ffn",
    )(xb, x, w_gate, w_up, w_down, g.reshape(1, d), b.reshape(1, d))


def moe_ffn(x, xb, w_router, w_gate, w_up, w_down, g, b):
    m = x.shape[0]
    tm = 512
    logits = router_logits(x, w_router, min(1024, m))[:, :N_EXPERTS]
    top_v, top_i = lax.top_k(logits, TOP_K)
    probs = jax.nn.softmax(top_v, axis=-1)
    e_flat = top_i.reshape(-1)
    onehot = (e_flat[:, None] == jnp.arange(N_EXPERTS)[None, :]).astype(jnp.int32)
    csum = jnp.cumsum(onehot, axis=0)
    counts = csum[-1]
    rank = jnp.take_along_axis(csum, e_flat[:, None], axis=1)[:, 0] - 1
    padded = ((counts + tm - 1) // tm) * tm
    ends = jnp.cumsum(padded)
    starts = ends - padded
    pos = starts[e_flat] + rank
    n_slots = TOP_K * m + N_EXPERTS * tm
    n_tiles = n_slots // tm
    tok = jnp.arange(TOP_K * m, dtype=jnp.int32) // TOP_K
    src = jnp.zeros((n_slots,), jnp.int32).at[pos].set(tok)
    tile_start = jnp.arange(n_tiles, dtype=jnp.int32) * tm
    tile_expert = jnp.minimum(jnp.searchsorted(ends, tile_start, side="right"), N_EXPERTS - 1).astype(jnp.int32)
    tile_valid = (tile_start < ends[-1]).astype(jnp.int32)
    xs = jnp.take(xb, src, axis=0)
    ys = grouped_swiglu(xs, w_gate, w_up, w_down,
                        tile_expert, tile_valid, tm, 1792)
    pos2 = pos.reshape(m, TOP_K)
    y0 = jnp.take(ys, pos2[:, 0], axis=0)
    y1 = jnp.take(ys, pos2[:, 1], axis=0)
    return add_layer_norm(x, [y0, y1], probs, g, b, min(1024, m))


def _pack_in_weights(w):
    d = w.shape[0]
    o = 0
    q = w[:, o:o + 512]; o += 512
    kv = w[:, o:o + 768].reshape(d, 3, 2, NSA_KV_GROUPS, HEAD_DIM); o += 768
    ng = w[:, o:o + 24]; o += 24
    rest = w[:, o:o + 7 * 512]; o += 7 * 512
    mg = w[:, o:]
    kpack = jnp.concatenate([kv[:, 1, 0], kv[:, 2, 0]], axis=-1).reshape(d, 2 * NSA_KV_GROUPS * HEAD_DIM)
    kc = kv[:, 0, 0].reshape(d, NSA_KV_GROUPS * HEAD_DIM)
    vc = kv[:, 0, 1].reshape(d, NSA_KV_GROUPS * HEAD_DIM)
    w_nat = jnp.concatenate([mg, rest, kpack, kc, vc], axis=1).astype(BF16)
    vs = kv[:, 1, 1].reshape(d, NSA_KV_GROUPS * HEAD_DIM)
    vw = kv[:, 2, 1].reshape(d, NSA_KV_GROUPS * HEAD_DIM)
    ngp = jnp.pad(ng.reshape(d, NSA_KV_GROUPS, NSA_GROUP * 3), ((0, 0), (0, 0), (0, 4))).reshape(d, 32)
    w_t = jnp.concatenate([q, vs, vw, ngp], axis=1).T.astype(BF16)
    return w_nat, w_t


def token_mixer_ln(x, w_in, cmp_pos, cmp_w1, cmp_b1, cmp_w2, cmp_b2, conv_w, gn_g, gn_b, w_br, w_out, ln_g, ln_b):
    b, t, d = x.shape
    m = b * t
    w_nat, w_t = _pack_in_weights(w_in)
    x2 = x.reshape(m, d)
    hn = mm_nat(x2, w_nat, min(1024, m), 1792, BF16).reshape(b, t, NAT_COLS)
    proj_t = mm_nt(w_t, x, min(1024, t))
    nch = t // CMP_STRIDE
    raw = hn[:, :, NAT_KC:NAT_KC + 2 * NSA_KV_GROUPS * HEAD_DIM].reshape(b, t, 2, NSA_KV_GROUPS, HEAD_DIM)
    chunks = raw.transpose(2, 0, 3, 1, 4).reshape(2, b, NSA_KV_GROUPS, nch, CMP_STRIDE * HEAD_DIM)
    cn, ct = nsa_compress_kv(chunks, cmp_pos, cmp_w1, cmp_b1, cmp_w2, cmp_b2)
    kc = cn[0].astype(BF16)
    vct = ct[1].astype(BF16)
    vt4 = proj_t[:, TR_VS:TR_NG, :].astype(BF16).reshape(b, 4, HEAD_DIM, t)
    ones_pad = jnp.zeros((b, 4, VT_ROWS - HEAD_DIM, t), BF16).at[:, :, 0].set(1.0)
    vt_pack = jnp.concatenate([vt4, ones_pad], axis=2).reshape(b, 4 * VT_ROWS, t)
    y_a = nsa_attention(proj_t, kc, vct, hn, vt_pack)
    y_c = retention_branch(hn, gn_g, gn_b, min(1024, t))
    return merge_branches(x2, y_a.reshape(m, -1), y_c.reshape(m, -1), hn.reshape(m, NAT_COLS), conv_w,
                          w_br, w_out, ln_g, ln_b, min(512, t), t)


def kernel(x, w_in, cmp_pos, cmp_w1, cmp_b1, cmp_w2, cmp_b2, conv_w, ret_gn_g, ret_gn_b, w_br, w_out,
           ln1_g, ln1_b, ln2_g, ln2_b, ffn_gate, ffn_up, ffn_down, moe_router, moe_gate, moe_up, moe_down):
    b, t, d = x.shape
    for l in range(DEPTH):
        x2, x2b = token_mixer_ln(x, w_in[l], cmp_pos[l], cmp_w1[l], cmp_b1[l], cmp_w2[l], cmp_b2[l], conv_w[l],
                                 ret_gn_g[l], ret_gn_b[l], w_br[l], w_out[l], ln1_g[l], ln1_b[l])
        if l % 2 == 0:
            wg, wu, wd = (layer_weights_bf16(w[:, None], l // 2, 256) for w in (ffn_gate, ffn_up, ffn_down))
            x2 = dense_ffn(x2, x2b, wg, wu, wd, ln2_g[l], ln2_b[l])
        else:
            wg, wu, wd = (layer_weights_bf16(w, l // 2, 256) for w in (moe_gate, moe_up, moe_down))
            x2 = moe_ffn(x2, x2b, moe_router[l // 2], wg, wu, wd, ln2_g[l], ln2_b[l])
        x = x2.reshape(b, t, d)
    return x
```

```python
import functools
import math

import jax
import jax.numpy as jnp
from jax import lax
from jax.experimental import pallas as pl
from jax.experimental.pallas import tpu as pltpu

F32 = jnp.float32
BF16 = jnp.bfloat16

D_MODEL = 1024
DEPTH = 4
BRANCH_WIDTH = 512
HEAD_DIM = 64
NSA_HEADS = 8
NSA_KV_GROUPS = 2
NSA_GROUP = 4
CMP_LEN = 32
CMP_STRIDE = 16
SLC_BLOCK = 64
SLC_TOPN = 16
WINDOW = 512
CMP_HIDDEN = 256
Q_BLOCK = 128
CONV_K = 3
CONV_HALO = 16
RET_HEADS = 4
RET_HEAD_DIM = 128
RET_CHUNK = 128
ROPE_BASE = 10000.0
D_FF = 3584
N_EXPERTS = 8
TOP_K = 2
LN_EPS = 1e-5
ALPHA = (2.0 * DEPTH) ** 0.25

NAT_MG = 0
NAT_CB = 3072
NAT_CC = 3584
NAT_CH = 4096
NAT_RQ = 4608
NAT_RK = 5120
NAT_RV = 5632
NAT_RG = 6144
NAT_KP = 6656
NAT_KC = 6912
NAT_VC = 7040
NAT_COLS = 7168
TR_Q = 0
TR_VS = 512
TR_VW = 640
TR_NG = 768
TR_ROWS = 800

NEG = -1e30
SEL_TK = 1024
LOG2E = 1.4426950408889634
NSA_QSUB = 2
CMP_CLASS_ROWS = 256
PG_PAD = 8
PICKED = -3e38
VT_ROWS = 80
CAST_STEPS = 16
VMEM_LIMIT = 48 * 1024 * 1024


def _cparams(n_axes, vmem=VMEM_LIMIT):
    return pltpu.CompilerParams(dimension_semantics=("arbitrary",) * n_axes, vmem_limit_bytes=vmem)


def _proj_kernel(x_ref, w_ref, wt_ref, o_ref, ot_ref, xb_ref):
    @pl.when(pl.program_id(1) == 0)
    def _():
        xb = x_ref[...].astype(BF16)
        xb_ref[...] = xb
        ot_ref[0] = lax.dot_general(wt_ref[...], xb, (((1,), (1,)), ((), ())), preferred_element_type=F32)

    o_ref[...] = jnp.dot(xb_ref[...], w_ref[...], preferred_element_type=F32).astype(o_ref.dtype)


def input_projection(x, w_nat, w_t, tm, tn):
    b, t, k = x.shape
    m = b * t
    n = w_nat.shape[1]
    r = w_t.shape[0]
    tps = t // tm
    return pl.pallas_call(
        _proj_kernel,
        grid=(m // tm, n // tn),
        in_specs=[pl.BlockSpec((tm, k), lambda i, j: (i, 0)), pl.BlockSpec((k, tn), lambda i, j: (0, j)),
                  pl.BlockSpec((r, k), lambda i, j: (0, 0))],
        out_specs=[pl.BlockSpec((tm, tn), lambda i, j: (i, j)),
                   pl.BlockSpec((1, r, tm), lambda i, j: (i // tps, 0, i % tps))],
        out_shape=[jax.ShapeDtypeStruct((m, n), BF16), jax.ShapeDtypeStruct((b, r, t), F32)],
        scratch_shapes=[pltpu.VMEM((tm, k), BF16)],
        compiler_params=_cparams(2),
        name="input_projection",
    )(x.reshape(m, k), w_nat, w_t)


def _cmp_kernel(c_ref, pos_ref, w1_ref, b1_ref, w2_ref, w2t_ref, b2_ref, b2t_ref, on_ref, ot_ref):
    half = CMP_STRIDE * HEAD_DIM
    c = c_ref[0, 0, 0].astype(BF16)
    w1 = w1_ref[0]
    a = jnp.dot(c, w1[:half], preferred_element_type=F32)
    bm = jnp.dot(c, w1[half:], preferred_element_type=F32)
    nch = a.shape[0]
    bm = pltpu.roll(bm, nch - 1, 0)
    posb = jnp.dot(pos_ref[0].astype(BF16), w1, preferred_element_type=F32)[0:1]
    hid = jax.nn.gelu(a + bm + posb + b1_ref[0])
    hb = hid.astype(BF16)
    on_ref[0, 0, 0] = jnp.dot(hb, w2_ref[0], preferred_element_type=F32) + b2_ref[0]
    ot_ref[0, 0, 0] = lax.dot_general(w2t_ref[0], hb, (((1,), (1,)), ((), ())),
                                      preferred_element_type=F32) + b2t_ref[0]


def nsa_compress_kv(chunks, pos, w1, b1, w2, b2):
    _, b, g, nch, cw = chunks.shape
    pos_flat = jnp.broadcast_to(pos.reshape(2, 1, CMP_LEN * HEAD_DIM), (2, 8, CMP_LEN * HEAD_DIM))
    w1b = w1.astype(BF16)
    w2b = w2.astype(BF16)
    w2t = jnp.swapaxes(w2, 1, 2).astype(BF16)
    b1r = b1.reshape(2, 1, CMP_HIDDEN)
    b2r = b2.reshape(2, 1, HEAD_DIM)
    b2t = b2.reshape(2, HEAD_DIM, 1)
    sel = lambda kv, i, j: (kv, 0, 0)
    return pl.pallas_call(
        _cmp_kernel,
        grid=(2, b, g),
        in_specs=[
            pl.BlockSpec((1, 1, 1, nch, cw), lambda kv, i, j: (kv, i, j, 0, 0)),
            pl.BlockSpec((1, 8, CMP_LEN * HEAD_DIM), sel),
            pl.BlockSpec((1, CMP_LEN * HEAD_DIM, CMP_HIDDEN), sel),
            pl.BlockSpec((1, 1, CMP_HIDDEN), sel),
            pl.BlockSpec((1, CMP_HIDDEN, HEAD_DIM), sel),
            pl.BlockSpec((1, HEAD_DIM, CMP_HIDDEN), sel),
            pl.BlockSpec((1, 1, HEAD_DIM), sel),
            pl.BlockSpec((1, HEAD_DIM, 1), sel),
        ],
        out_specs=[
            pl.BlockSpec((1, 1, 1, nch, HEAD_DIM), lambda kv, i, j: (kv, i, j, 0, 0)),
            pl.BlockSpec((1, 1, 1, HEAD_DIM, nch), lambda kv, i, j: (kv, i, j, 0, 0)),
        ],
        out_shape=[
            jax.ShapeDtypeStruct((2, b, g, nch, HEAD_DIM), F32),
            jax.ShapeDtypeStruct((2, b, g, HEAD_DIM, nch), F32),
        ],
        compiler_params=_cparams(3),
        name="nsa_compress",
    )(chunks, pos_flat, w1b, b1r, w2b, w2t, b2r, b2t)


def _accumulate(s, mt, vt, m, acc):
    m_new = jnp.maximum(m, mt)
    alpha = jnp.exp2(m - m_new)
    p = jnp.exp2(s - m_new).astype(BF16)
    return m_new, alpha * acc + jnp.dot(vt, p, preferred_element_type=F32)


def _nsa_kernel(qt_ref, ng_ref, kc_ref, vct_ref, kp_ref, vst_ref, vwt_ref, e_ref, tri_ref, wpat_ref,
                cpat_ref, o_ref, bias_ref, w_ref, sbuf_ref, oc_ref, pg_ref):
    nq = Q_BLOCK
    nsub = qt_ref.shape[2] // nq
    ncol = NSA_GROUP * nq
    n_sel = bias_ref.shape[1]
    ncp = kc_ref.shape[2]
    kw = 2 * HEAD_DIM
    bpt = SEL_TK // SLC_BLOCK
    n_tiles_total = kp_ref.shape[1] // SEL_TK
    qb_first = pl.program_id(2) * nsub
    wlen = WINDOW + nq

    def prepare(u):
        qb = qb_first + u
        q0 = pl.multiple_of(qb * nq, nq)
        qt = qt_ref[0, :, u * nq:(u + 1) * nq] * (HEAD_DIM ** -0.5 * LOG2E)
        qs = jnp.concatenate([qt[r * HEAD_DIM:(r + 1) * HEAD_DIM, :] for r in range(NSA_GROUP)], axis=1)
        zq = jnp.zeros_like(qs)
        q_c = qs.astype(BF16)
        q_slc = jnp.concatenate([qs, zq], axis=0).astype(BF16)
        q_win = jnp.concatenate([zq, qs], axis=0).astype(BF16)
        tq = q0 + (lax.broadcasted_iota(jnp.int32, (1, ncol), 1) & (nq - 1))

        coff = pl.multiple_of(ncp - qb * (nq // CMP_STRIDE), nq // CMP_STRIDE)
        tq1 = q0 + lax.broadcasted_iota(jnp.int32, (1, nq), 1)
        cur = jnp.right_shift(tq1, 6).astype(F32)
        first_diag = (2 * qb).astype(F32)
        pg_ref[u, 0:PG_PAD, :] = jnp.zeros((PG_PAD, nq), F32)
        crow = min(CMP_CLASS_ROWS, ncp)
        cls = qb // (crow // (nq // CMP_STRIDE))
        for c in range(ncp // crow):
            nc_c = crow * (c + 1)
            ns_c = nc_c * CMP_STRIDE // SLC_BLOCK

            @pl.when(cls == c)
            def _(nc_c=nc_c, ns_c=ns_c):
                s_c = (jnp.dot(kc_ref[0, 0, 0:nc_c, :], q_c, preferred_element_type=F32)
                       + jnp.concatenate([cpat_ref[pl.ds(coff, nc_c), :]] * NSA_GROUP, axis=1))
                m_c = jnp.max(s_c, axis=0, keepdims=True)
                e_c = jnp.exp2(s_c - m_c)
                l_c = jnp.sum(e_c, axis=0, keepdims=True)
                p_c = e_c * jnp.where(tq >= CMP_LEN - 1, 1.0 / l_c, 0.0)
                oc_ref[u] = jnp.dot(vct_ref[0, 0, :, 0:nc_c], p_c.astype(BF16), preferred_element_type=F32)

                p_grp = p_c[:, 0:nq]
                for r in range(1, NSA_GROUP):
                    p_grp = p_grp + p_c[:, r * nq:(r + 1) * nq]
                pg_ref[u, PG_PAD:PG_PAD + nc_c, :] = p_grp
                ratio = SLC_BLOCK // CMP_STRIDE
                taps = [pg_ref[u, pl.ds(PG_PAD - 1 + k, ns_c, stride=ratio), :] for k in range(ratio + 1)]
                score = taps[0] + 2.0 * (taps[1] + taps[2] + taps[3]) + taps[4]
                j_io = lax.broadcasted_iota(jnp.int32, (ns_c, nq), 0).astype(F32)
                score = jnp.where(j_io <= cur, score, -1.0)
                score = jnp.where(j_io == 0.0, PICKED, score)
                score = jnp.where(j_io == cur, PICKED, score)
                score = jnp.where(j_io == cur - 1.0, PICKED, score)
                for _ in range(SLC_TOPN - 3):
                    mx = jnp.max(score, axis=0, keepdims=True)
                    first = jnp.min(jnp.where(score == mx, j_io, float(ns_c)), axis=0, keepdims=True)
                    score = jnp.where(j_io == first, PICKED, score)
                bias = jnp.where(j_io < first_diag, jnp.where(score == PICKED, 0.0, NEG), NEG)
                bias_ref[u, 0:ns_c, :] = bias.astype(BF16)
                if ns_c < n_sel:
                    bias_ref[u, ns_c:, :] = jnp.full((n_sel - ns_c, nq), NEG, BF16)

        o_c = oc_ref[u]

        wstart = pl.multiple_of(jnp.maximum(q0 - WINDOW, 0), nq)
        woff = pl.multiple_of(jnp.maximum(WINDOW - q0, 0), nq)
        s_w = jnp.dot(kp_ref[0, pl.ds(wstart, wlen), :], q_win, preferred_element_type=F32)
        s_w = s_w + jnp.concatenate([wpat_ref[pl.ds(woff, wlen), :]] * NSA_GROUP, axis=1)
        p_w = jnp.exp2(s_w - jnp.max(s_w, axis=0, keepdims=True)).astype(BF16)
        acc_w = jnp.dot(vwt_ref[0, :, pl.ds(wstart, wlen)], p_w, preferred_element_type=F32)
        o_w = acc_w[0:HEAD_DIM] / acc_w[HEAD_DIM:HEAD_DIM + 1]

        for slot in range(2):
            w_ref[slot, u, 0:kw, :] = q_slc
            w_ref[slot, u, kw:, :] = jnp.zeros((w_ref.shape[2] - kw, ncol), BF16)
        return q0, q_slc, o_c, o_w

    prepared = [prepare(u) for u in range(nsub)]

    def tile_start(j):
        jc = jnp.minimum(j, n_tiles_total - 1)
        return jc, pl.multiple_of(jc * SEL_TK, SEL_TK)

    def produce(j, slot):
        jc, k0 = tile_start(j)
        lhs = jnp.concatenate([kp_ref[0, pl.ds(k0, SEL_TK), :], e_ref[...]], axis=1)
        mts = []
        for u in range(nsub):
            b16 = bias_ref[u, pl.ds(pl.multiple_of(jc * bpt, bpt), bpt), :]
            w_ref[slot, u, kw:kw + bpt, :] = jnp.concatenate([b16] * NSA_GROUP, axis=1)
            s = jnp.dot(lhs, w_ref[slot, u], preferred_element_type=F32)
            sbuf_ref[slot, u] = s
            mts.append(jnp.max(s, axis=0, keepdims=True))
        return tuple(mts)

    def consume(j, slot, mts, state):
        _, k0 = tile_start(j)
        vt = vst_ref[0, :, pl.ds(k0, SEL_TK)]
        return tuple(_accumulate(sbuf_ref[slot, u], mts[u], vt, *state[u]) for u in range(nsub))

    def pair(i, carry, last):
        mt_a, state = carry
        mt_b = produce(2 * i + 1, 1)
        state = consume(2 * i, 0, mt_a, state)
        if last:
            return consume(2 * i + 1, 1, mt_b, state)
        mt_a = produce(2 * i + 2, 0)
        state = consume(2 * i + 1, 1, mt_b, state)
        return mt_a, state

    n_main = ((qb_first + nsub - 1) * nq + (SEL_TK - 1)) // SEL_TK
    n_pairs = jnp.maximum((n_main + 1) // 2, 1)
    state = tuple((jnp.full((1, ncol), NEG, F32), jnp.zeros((vst_ref.shape[1], ncol), F32)) for _ in range(nsub))
    carry = lax.fori_loop(0, n_pairs - 1, lambda i, c: pair(i, c, False), (produce(0, 0), state))
    state = pair(n_pairs - 1, carry, True)

    for u in range(nsub):
        q0, q_slc, o_c, o_w = prepared[u]
        s_d = jnp.dot(kp_ref[0, pl.ds(q0, nq), :], q_slc, preferred_element_type=F32) + tri_ref[...]
        _, acc_s = _accumulate(s_d, jnp.max(s_d, axis=0, keepdims=True), vst_ref[0, :, pl.ds(q0, nq)], *state[u])
        o_s = acc_s[0:HEAD_DIM] / acc_s[HEAD_DIM:HEAD_DIM + 1]
        gts = jax.nn.sigmoid(ng_ref[0, :, u * nq:(u + 1) * nq])
        rows = []
        for r in range(NSA_GROUP):
            sl = slice(r * nq, (r + 1) * nq)
            rows.append(gts[3 * r:3 * r + 1, :] * o_c[:, sl] + gts[3 * r + 1:3 * r + 2, :] * o_s[:, sl]
                        + gts[3 * r + 2:3 * r + 3, :] * o_w[:, sl])
        y_t = jnp.concatenate(rows, axis=0)
        o_ref[0, u * nq:(u + 1) * nq, :] = y_t.T.astype(o_ref.dtype)


def _nsa_constants(ncp):
    nq = Q_BLOCK
    ql = jnp.arange(nq)[None, :]
    r = jnp.arange(SEL_TK)[:, None]
    e = (r // SLC_BLOCK == jnp.arange(2 * HEAD_DIM)[None, :]).astype(BF16)
    rd = jnp.arange(nq)[:, None]
    tri = jnp.tile(jnp.where(rd <= ql, 0.0, NEG).astype(F32), (1, NSA_GROUP))
    rw = jnp.arange(2 * WINDOW + nq)[:, None]
    wpat = jnp.where((rw > ql) & (rw <= ql + WINDOW), 0.0, NEG).astype(F32)
    rc = jnp.arange(2 * ncp)[:, None] - ncp
    cpat = jnp.where(CMP_STRIDE * rc + (CMP_LEN - 1) <= ql, 0.0, NEG).astype(F32)
    return e, tri, wpat, cpat


def nsa_attention(proj_t, kc, vct, hn, vt_pack):
    b, _, t = proj_t.shape
    ncp = kc.shape[2]
    n_sel = t // SLC_BLOCK
    gq = NSA_GROUP * HEAD_DIM
    qstep = NSA_QSUB * Q_BLOCK
    e, tri, wpat, cpat = _nsa_constants(ncp)
    const2 = lambda i, g, q: (0, 0)
    return pl.pallas_call(
        _nsa_kernel,
        grid=(b, NSA_KV_GROUPS, t // qstep),
        in_specs=[
            pl.BlockSpec((1, gq, qstep), lambda i, g, q: (i, g, q)),
            pl.BlockSpec((1, 16, qstep), lambda i, g, q: (i, TR_NG // 16 + g, q)),
            pl.BlockSpec((1, 1, ncp, HEAD_DIM), lambda i, g, q: (i, g, 0, 0)),
            pl.BlockSpec((1, 1, HEAD_DIM, ncp), lambda i, g, q: (i, g, 0, 0)),
            pl.BlockSpec((1, t, 2 * HEAD_DIM), lambda i, g, q: (i, 0, NAT_KP // (2 * HEAD_DIM) + g)),
            pl.BlockSpec((1, VT_ROWS, t), lambda i, g, q: (i, g, 0)),
            pl.BlockSpec((1, VT_ROWS, t), lambda i, g, q: (i, NSA_KV_GROUPS + g, 0)),
            pl.BlockSpec(e.shape, const2),
            pl.BlockSpec(tri.shape, const2),
            pl.BlockSpec(wpat.shape, const2),
            pl.BlockSpec(cpat.shape, const2),
        ],
        out_specs=pl.BlockSpec((1, qstep, gq), lambda i, g, q: (i, q, g)),
        out_shape=jax.ShapeDtypeStruct((b, t, NSA_HEADS * HEAD_DIM), BF16),
        scratch_shapes=[pltpu.VMEM((NSA_QSUB, n_sel, Q_BLOCK), BF16),
                        pltpu.VMEM((2, NSA_QSUB, 4 * HEAD_DIM, NSA_GROUP * Q_BLOCK), BF16),
                        pltpu.VMEM((2, NSA_QSUB, SEL_TK, NSA_GROUP * Q_BLOCK), F32),
                        pltpu.VMEM((NSA_QSUB, HEAD_DIM, NSA_GROUP * Q_BLOCK), F32),
                        pltpu.VMEM((NSA_QSUB, PG_PAD + ncp, Q_BLOCK), F32)],
        compiler_params=_cparams(3),
        name="nsa_attention",
    )(proj_t, proj_t, kc, vct, hn, vt_pack, vt_pack, e, tri, wpat, cpat)


def _ret_kernel(rq_ref, rk_ref, rv_ref, rg_ref, cos_ref, sin_ref, dec_ref, qd_ref, kd_ref, cd_ref,
                gg_ref, gb_ref, o_ref, st_ref):
    c = RET_CHUNK
    hd = RET_HEAD_DIM

    @pl.when(pl.program_id(1) == 0)
    def _():
        st_ref[...] = jnp.zeros_like(st_ref)

    n_chunks = rq_ref.shape[1] // c

    def chunk(ci, _):
        r0 = pl.multiple_of(ci * c, c)
        cos = cos_ref[pl.ds(r0, c), :]
        sin = sin_ref[pl.ds(r0, c), :]
        for h in range(RET_HEADS):
            hs = slice(h * hd, (h + 1) * hd)
            q = rq_ref[0, pl.ds(r0, c), hs].astype(F32)
            k = rk_ref[0, pl.ds(r0, c), hs].astype(F32)
            v = rv_ref[0, pl.ds(r0, c), hs]
            q = q * cos + pltpu.roll(q, hd // 2, 1) * sin
            k = (k * cos + pltpu.roll(k, hd // 2, 1) * sin) * (hd ** -0.5)
            qb = q.astype(BF16)
            kb = k.astype(BF16)
            vb = v.astype(BF16)
            inner = lax.dot_general(qb, kb, (((1,), (1,)), ((), ())), preferred_element_type=F32) * dec_ref[h]
            st = st_ref[h]
            o = (jnp.dot(inner.astype(BF16), vb, preferred_element_type=F32)
                 + jnp.dot(qb, st.astype(BF16), preferred_element_type=F32) * qd_ref[h])
            kdt = (k * kd_ref[h]).T.astype(BF16)
            st_ref[h] = st * cd_ref[h] + jnp.dot(kdt, vb, preferred_element_type=F32)
            mu = jnp.mean(o, axis=-1, keepdims=True)
            d = o - mu
            var = jnp.mean(d * d, axis=-1, keepdims=True)
            y = d * lax.rsqrt(var + LN_EPS) * gg_ref[:, hs] + gb_ref[:, hs]
            o_ref[0, pl.ds(r0, c), hs] = (jax.nn.silu(rg_ref[0, pl.ds(r0, c), hs].astype(F32)) * y).astype(o_ref.dtype)
        return 0

    lax.fori_loop(0, n_chunks, chunk, 0)


def retention_branch(hn, gn_g, gn_b, tr):
    b, t, _ = hn.shape
    w = BRANCH_WIDTH
    hd = RET_HEAD_DIM
    c = RET_CHUNK
    inv = ROPE_BASE ** (-jnp.arange(0, hd, 2, dtype=F32) / hd)
    ang = jnp.arange(t, dtype=F32)[:, None] * inv[None, :]
    cos2 = jnp.concatenate([jnp.cos(ang), jnp.cos(ang)], axis=1)
    sin2 = jnp.concatenate([-jnp.sin(ang), jnp.sin(ang)], axis=1)
    log_g = jnp.log(1.0 - 2.0 ** (-5.0 - jnp.arange(RET_HEADS, dtype=F32)))
    idx = jnp.arange(c, dtype=F32)
    diff = idx[:, None] - idx[None, :]
    decay_in = jnp.where(diff >= 0, jnp.exp(log_g[:, None, None] * jnp.maximum(diff, 0.0)), 0.0)
    ones = jnp.ones((RET_HEADS, c, hd), F32)
    q_dec = jnp.exp(log_g[:, None] * (idx + 1.0))[:, :, None] * ones
    k_dec = jnp.exp(log_g[:, None] * (c - 1.0 - idx))[:, :, None] * ones
    c_dec = jnp.exp(log_g * c)[:, None, None] * ones
    col = lambda cc: (lambda i, j: (i, j, cc // w))
    full3 = lambda i, j: (0, 0, 0)
    return pl.pallas_call(
        _ret_kernel,
        grid=(b, t // tr),
        in_specs=[
            pl.BlockSpec((1, tr, w), col(NAT_RQ)),
            pl.BlockSpec((1, tr, w), col(NAT_RK)),
            pl.BlockSpec((1, tr, w), col(NAT_RV)),
            pl.BlockSpec((1, tr, w), col(NAT_RG)),
            pl.BlockSpec((tr, hd), lambda i, j: (j, 0)),
            pl.BlockSpec((tr, hd), lambda i, j: (j, 0)),
            pl.BlockSpec((RET_HEADS, c, c), full3),
            pl.BlockSpec((RET_HEADS, c, hd), full3),
            pl.BlockSpec((RET_HEADS, c, hd), full3),
            pl.BlockSpec((RET_HEADS, c, hd), full3),
            pl.BlockSpec((1, w), lambda i, j: (0, 0)),
            pl.BlockSpec((1, w), lambda i, j: (0, 0)),
        ],
        out_specs=pl.BlockSpec((1, tr, w), lambda i, j: (i, j, 0)),
        out_shape=jax.ShapeDtypeStruct((b, t, w), BF16),
        scratch_shapes=[pltpu.VMEM((RET_HEADS, hd, hd), F32)],
        compiler_params=_cparams(2),
        name="retention",
    )(hn, hn, hn, hn, cos2, sin2, decay_in, q_dec, k_dec, c_dec, gn_g.reshape(1, w), gn_b.reshape(1, w))


def _layer_norm(z, g, b):
    mu = jnp.mean(z, axis=-1, keepdims=True)
    d = z - mu
    var = jnp.mean(d * d, axis=-1, keepdims=True)
    return d * lax.rsqrt(var + LN_EPS) * g + b


def _merge_kernel(tiles_per_seq, x_ref, ya_ref, cb_ref, cc_ref, ch_ref, pc_ref, ph_ref, cw_ref, yc_ref, mg_ref,
                  wbr_ref, wout_ref, g_ref, b_ref, *rest):
    o_ref, ob_ref = rest[-3:-1] if len(rest) == 5 else rest
    first = (pl.program_id(0) % tiles_per_seq) == 0
    u = cc_ref[...].astype(F32) * ch_ref[...].astype(F32)
    prev = pc_ref[...].astype(F32) * ph_ref[...].astype(F32)
    prev = jnp.where(first, 0.0, prev)
    row = lax.broadcasted_iota(jnp.int32, u.shape, 0)
    p1 = prev[CONV_HALO - 1:CONV_HALO, :]
    p2 = prev[CONV_HALO - 2:CONV_HALO - 1, :]
    u1 = jnp.where(row == 0, p1, pltpu.roll(u, 1, 0))
    u2 = jnp.where(row == 0, p2, jnp.where(row == 1, p1, pltpu.roll(u, 2, 0)))
    cw = cw_ref[...]
    y_b = cb_ref[...].astype(F32) * (cw[0:1, :] * u2 + cw[1:2, :] * u1 + cw[2:3, :] * u)

    mix = None
    for c, y in enumerate((ya_ref[...], y_b.astype(BF16), yc_ref[...])):
        proj = jnp.dot(y, wbr_ref[c], preferred_element_type=F32)
        gate = jax.nn.sigmoid(mg_ref[:, c * D_MODEL:(c + 1) * D_MODEL].astype(F32))
        mix = gate * proj if mix is None else mix + gate * proj
    m = jnp.dot(mix.astype(BF16), wout_ref[...], preferred_element_type=F32)
    out = _layer_norm(ALPHA * x_ref[...] + m, g_ref[...], b_ref[...])
    o_ref[...] = out
    outb = out.astype(BF16)
    ob_ref[...] = outb
    if len(rest) == 5:
        wh_ref, wl_ref, _, _, lg_ref = rest
        lo = (out - outb.astype(F32)).astype(BF16)
        wh = wh_ref[...]
        lg_ref[...] = (jnp.dot(outb, wh, preferred_element_type=F32) + jnp.dot(lo, wh, preferred_element_type=F32)
                       + jnp.dot(outb, wl_ref[...], preferred_element_type=F32))


def merge_branches(x, ya, yc, hn, conv_w, w_br, w_out, g, b, tm, t, w_router=None):
    m = x.shape[0]
    d = D_MODEL
    w = BRANCH_WIDTH
    row = lambda i: (i, 0)
    col = lambda c: (lambda i: (i, c // w))
    hal = lambda c: (lambda i: (jnp.maximum(i * (tm // CONV_HALO) - 1, 0), c // w))
    wpad = jnp.zeros((8, w), F32).at[:CONV_K].set(conv_w)
    extra_in, extra_specs, extra_out, extra_shape = [], [], [], []
    if w_router is not None:
        wp = jnp.zeros((d, 128), F32).at[:, :N_EXPERTS].set(w_router)
        wh = wp.astype(BF16)
        extra_in = [wh, (wp - wh.astype(F32)).astype(BF16)]
        extra_specs = [pl.BlockSpec((d, 128), lambda i: (0, 0))] * 2
        extra_out = [pl.BlockSpec((tm, 128), row)]
        extra_shape = [jax.ShapeDtypeStruct((m, 128), F32)]
    return pl.pallas_call(
        functools.partial(_merge_kernel, t // tm),
        grid=(m // tm,),
        in_specs=[
            pl.BlockSpec((tm, d), row),
            pl.BlockSpec((tm, w), row),
            pl.BlockSpec((tm, w), col(NAT_CB)),
            pl.BlockSpec((tm, w), col(NAT_CC)),
            pl.BlockSpec((tm, w), col(NAT_CH)),
            pl.BlockSpec((CONV_HALO, w), hal(NAT_CC)),
            pl.BlockSpec((CONV_HALO, w), hal(NAT_CH)),
            pl.BlockSpec((8, w), lambda i: (0, 0)),
            pl.BlockSpec((tm, w), row),
            pl.BlockSpec((tm, 3 * d), row),
            pl.BlockSpec((3, w, d), lambda i: (0, 0, 0)),
            pl.BlockSpec((d, d), lambda i: (0, 0)),
            pl.BlockSpec((1, d), lambda i: (0, 0)),
            pl.BlockSpec((1, d), lambda i: (0, 0)),
        ] + extra_specs,
        out_specs=[pl.BlockSpec((tm, d), row), pl.BlockSpec((tm, d), row)] + extra_out,
        out_shape=[jax.ShapeDtypeStruct((m, d), F32), jax.ShapeDtypeStruct((m, d), BF16)] + extra_shape,
        compiler_params=_cparams(1),
        name="merge",
    )(x, ya, hn, hn, hn, hn, hn, wpad, yc, hn, w_br.astype(BF16), w_out.astype(BF16), g.reshape(1, d),
      b.reshape(1, d), *extra_in)


def _cast3_kernel(g_ref, u_ref, d_ref, og_ref, ou_ref, od_ref):
    og_ref[...] = g_ref[0].astype(og_ref.dtype)
    ou_ref[...] = u_ref[0].astype(ou_ref.dtype)
    od_ref[...] = d_ref[0].astype(od_ref.dtype)


def ffn_weights_bf16(w_gate, w_up, w_down, layer):
    _, e, d, ff = w_gate.shape
    rg, rd = d // CAST_STEPS, ff // CAST_STEPS
    up_spec = pl.BlockSpec((1, 1, rg, ff), lambda i, j: (layer, i, j, 0))
    return pl.pallas_call(
        _cast3_kernel,
        grid=(e, CAST_STEPS),
        in_specs=[up_spec, up_spec, pl.BlockSpec((1, 1, rd, d), lambda i, j: (layer, i, j, 0))],
        out_specs=[pl.BlockSpec((1, rg, ff), lambda i, j: (i, j, 0)), pl.BlockSpec((1, rg, ff), lambda i, j: (i, j, 0)),
                   pl.BlockSpec((1, rd, d), lambda i, j: (i, j, 0))],
        out_shape=[jax.ShapeDtypeStruct((e, d, ff), BF16), jax.ShapeDtypeStruct((e, d, ff), BF16),
                   jax.ShapeDtypeStruct((e, ff, d), BF16)],
        compiler_params=_cparams(2),
        name="cast_bf16",
    )(w_gate, w_up, w_down)


def _mlp_kernel(te_ref, tv_ref, x_ref, wg_ref, wu_ref, wd_ref, o_ref, acc_ref):
    i = pl.program_id(0)
    f = pl.program_id(1)

    @pl.when(tv_ref[i] > 0)
    def _():
        x = x_ref[...]
        gte = jnp.dot(x, wg_ref[0], preferred_element_type=F32)
        up = jnp.dot(x, wu_ref[0], preferred_element_type=F32)
        h = (jax.nn.silu(gte) * up).astype(BF16)
        part = jnp.dot(h, wd_ref[0], preferred_element_type=F32)

        @pl.when(f == 0)
        def _():
            acc_ref[...] = part

        @pl.when(f > 0)
        def _():
            acc_ref[...] = acc_ref[...] + part

        @pl.when(f == pl.num_programs(1) - 1)
        def _():
            o_ref[...] = acc_ref[...].astype(o_ref.dtype)

    @pl.when((tv_ref[i] == 0) & (f == 0))
    def _():
        o_ref[...] = jnp.zeros_like(o_ref)


def grouped_swiglu(xs, w_gate, w_up, w_down, tile_expert, tile_valid, tm, tf):
    n, d = xs.shape
    ff = w_gate.shape[2]
    grid_spec = pltpu.PrefetchScalarGridSpec(
        num_scalar_prefetch=2,
        grid=(n // tm, ff // tf),
        in_specs=[
            pl.BlockSpec((tm, d), lambda i, f, te, tv: (i, 0)),
            pl.BlockSpec((1, d, tf), lambda i, f, te, tv: (te[i], 0, f)),
            pl.BlockSpec((1, d, tf), lambda i, f, te, tv: (te[i], 0, f)),
            pl.BlockSpec((1, tf, d), lambda i, f, te, tv: (te[i], f, 0)),
        ],
        out_specs=pl.BlockSpec((tm, d), lambda i, f, te, tv: (i, 0)),
        scratch_shapes=[pltpu.VMEM((tm, d), F32)],
    )
    return pl.pallas_call(
        _mlp_kernel,
        grid_spec=grid_spec,
        out_shape=jax.ShapeDtypeStruct((n, d), BF16),
        compiler_params=_cparams(2),
        name="grouped_swiglu",
    )(tile_expert, tile_valid, xs, w_gate, w_up, w_down)


def _add_ln_kernel(*refs):
    x_ref, sc_ref, *adds, g_ref, b_ref, o_ref = refs
    z = ALPHA * x_ref[...]
    for k, a in enumerate(adds):
        z = z + sc_ref[:, k:k + 1] * a[...].astype(F32)
    o_ref[...] = _layer_norm(z, g_ref[...], b_ref[...])


def add_layer_norm(x, adds, scales, g, b, tm):
    m, d = x.shape
    row = lambda i: (i, 0)
    return pl.pallas_call(
        _add_ln_kernel,
        grid=(m // tm,),
        in_specs=([pl.BlockSpec((tm, d), row), pl.BlockSpec((tm, len(adds)), row)]
                  + [pl.BlockSpec((tm, d), row)] * len(adds) + [pl.BlockSpec((1, d), lambda i: (0, 0))] * 2),
        out_specs=pl.BlockSpec((tm, d), row),
        out_shape=jax.ShapeDtypeStruct((m, d), F32),
        compiler_params=_cparams(1),
        name="add_layer_norm",
    )(x, scales, *adds, g.reshape(1, d), b.reshape(1, d))


def _dense_ffn_kernel(xb_ref, x_ref, wg_ref, wu_ref, wd_ref, g_ref, b_ref, o_ref, acc_ref):
    f = pl.program_id(1)
    xb = xb_ref[...]
    gte = jnp.dot(xb, wg_ref[0], preferred_element_type=F32)
    up = jnp.dot(xb, wu_ref[0], preferred_element_type=F32)
    h = (jax.nn.silu(gte) * up).astype(BF16)
    part = jnp.dot(h, wd_ref[0], preferred_element_type=F32)

    @pl.when(f == 0)
    def _():
        acc_ref[...] = part

    @pl.when(f > 0)
    def _():
        acc_ref[...] = acc_ref[...] + part

    @pl.when(f == pl.num_programs(1) - 1)
    def _():
        o_ref[...] = _layer_norm(ALPHA * x_ref[...] + acc_ref[...], g_ref[...], b_ref[...])


def dense_ffn(x, xb, w_gate, w_up, w_down, g, b):
    m, d = x.shape
    ff = w_gate.shape[2]
    tm = min(512, m)
    tf = 1792
    row = lambda i, f: (i, 0)
    return pl.pallas_call(
        _dense_ffn_kernel,
        grid=(m // tm, ff // tf),
        in_specs=[
            pl.BlockSpec((tm, d), row),
            pl.BlockSpec((tm, d), row),
            pl.BlockSpec((1, d, tf), lambda i, f: (0, 0, f)),
            pl.BlockSpec((1, d, tf), lambda i, f: (0, 0, f)),
            pl.BlockSpec((1, tf, d), lambda i, f: (0, f, 0)),
            pl.BlockSpec((1, d), lambda i, f: (0, 0)),
            pl.BlockSpec((1, d), lambda i, f: (0, 0)),
        ],
        out_specs=pl.BlockSpec((tm, d), row),
        out_shape=jax.ShapeDtypeStruct((m, d), F32),
        scratch_shapes=[pltpu.VMEM((tm, d), F32)],
        compiler_params=_cparams(2),
        name="dense_ffn",
    )(xb, x, w_gate, w_up, w_down, g.reshape(1, d), b.reshape(1, d))


def moe_ffn(x, xb, logits, w_gate, w_up, w_down, g, b):
    m = x.shape[0]
    tm = 512
    logits = logits[:, :N_EXPERTS]
    top_v, top_i = lax.top_k(logits, TOP_K)
    probs = jax.nn.softmax(top_v, axis=-1)
    e_flat = top_i.reshape(-1)
    onehot = (e_flat[:, None] == jnp.arange(N_EXPERTS)[None, :]).astype(jnp.int32)
    csum = jnp.cumsum(onehot, axis=0)
    counts = csum[-1]
    rank = jnp.take_along_axis(csum, e_flat[:, None], axis=1)[:, 0] - 1
    padded = ((counts + tm - 1) // tm) * tm
    ends = jnp.cumsum(padded)
    starts = ends - padded
    pos = starts[e_flat] + rank
    n_slots = TOP_K * m + N_EXPERTS * tm
    n_tiles = n_slots // tm
    tok = jnp.arange(TOP_K * m, dtype=jnp.int32) // TOP_K
    src = jnp.zeros((n_slots,), jnp.int32).at[pos].set(tok)
    tile_start = jnp.arange(n_tiles, dtype=jnp.int32) * tm
    tile_expert = jnp.minimum(jnp.searchsorted(ends, tile_start, side="right"), N_EXPERTS - 1).astype(jnp.int32)
    tile_valid = (tile_start < ends[-1]).astype(jnp.int32)
    xs = jnp.take(xb, src, axis=0)
    ys = grouped_swiglu(xs, w_gate, w_up, w_down,
                        tile_expert, tile_valid, tm, 1792)
    pos2 = pos.reshape(m, TOP_K)
    y0 = jnp.take(ys, pos2[:, 0], axis=0)
    y1 = jnp.take(ys, pos2[:, 1], axis=0)
    return add_layer_norm(x, [y0, y1], probs, g, b, min(1024, m))


def _pack_in_weights(w):
    d = w.shape[0]
    o = 0
    q = w[:, o:o + 512]; o += 512
    kv = w[:, o:o + 768].reshape(d, 3, 2, NSA_KV_GROUPS, HEAD_DIM); o += 768
    ng = w[:, o:o + 24]; o += 24
    rest = w[:, o:o + 7 * 512]; o += 7 * 512
    mg = w[:, o:]
    kpack = jnp.concatenate([kv[:, 1, 0], kv[:, 2, 0]], axis=-1).reshape(d, 2 * NSA_KV_GROUPS * HEAD_DIM)
    kc = kv[:, 0, 0].reshape(d, NSA_KV_GROUPS * HEAD_DIM)
    vc = kv[:, 0, 1].reshape(d, NSA_KV_GROUPS * HEAD_DIM)
    w_nat = jnp.concatenate([mg, rest, kpack, kc, vc], axis=1).astype(BF16)
    vs = kv[:, 1, 1].reshape(d, NSA_KV_GROUPS * HEAD_DIM)
    vw = kv[:, 2, 1].reshape(d, NSA_KV_GROUPS * HEAD_DIM)
    ngp = jnp.pad(ng.reshape(d, NSA_KV_GROUPS, NSA_GROUP * 3), ((0, 0), (0, 0), (0, 4))).reshape(d, 32)
    w_t = jnp.concatenate([q, vs, vw, ngp], axis=1).T.astype(BF16)
    return w_nat, w_t


def token_mixer_ln(x, w_in, cmp_pos, cmp_w1, cmp_b1, cmp_w2, cmp_b2, conv_w, gn_g, gn_b, w_br, w_out, ln_g, ln_b,
                   w_router=None):
    b, t, d = x.shape
    m = b * t
    w_nat, w_t = _pack_in_weights(w_in)
    x2 = x.reshape(m, d)
    hn, proj_t = input_projection(x, w_nat, w_t, min(1024, t), 1792)
    hn = hn.reshape(b, t, NAT_COLS)
    nch = t // CMP_STRIDE
    raw = hn[:, :, NAT_KC:NAT_KC + 2 * NSA_KV_GROUPS * HEAD_DIM].reshape(b, t, 2, NSA_KV_GROUPS, HEAD_DIM)
    chunks = raw.transpose(2, 0, 3, 1, 4).reshape(2, b, NSA_KV_GROUPS, nch, CMP_STRIDE * HEAD_DIM)
    cn, ct = nsa_compress_kv(chunks, cmp_pos, cmp_w1, cmp_b1, cmp_w2, cmp_b2)
    kc = cn[0].astype(BF16)
    vct = ct[1].astype(BF16)
    vt4 = proj_t[:, TR_VS:TR_NG, :].astype(BF16).reshape(b, 4, HEAD_DIM, t)
    ones_pad = jnp.zeros((b, 4, VT_ROWS - HEAD_DIM, t), BF16).at[:, :, 0].set(1.0)
    vt_pack = jnp.concatenate([vt4, ones_pad], axis=2).reshape(b, 4 * VT_ROWS, t)
    y_a = nsa_attention(proj_t, kc, vct, hn, vt_pack)
    y_c = retention_branch(hn, gn_g, gn_b, min(1024, t))
    return merge_branches(x2, y_a.reshape(m, -1), y_c.reshape(m, -1), hn.reshape(m, NAT_COLS), conv_w,
                          w_br, w_out, ln_g, ln_b, min(512, t), t, w_router)


def kernel(x, w_in, cmp_pos, cmp_w1, cmp_b1, cmp_w2, cmp_b2, conv_w, ret_gn_g, ret_gn_b, w_br, w_out,
           ln1_g, ln1_b, ln2_g, ln2_b, ffn_gate, ffn_up, ffn_down, moe_router, moe_gate, moe_up, moe_down):
    b, t, d = x.shape
    for l in range(DEPTH):
        mixed = token_mixer_ln(x, w_in[l], cmp_pos[l], cmp_w1[l], cmp_b1[l], cmp_w2[l], cmp_b2[l], conv_w[l],
                               ret_gn_g[l], ret_gn_b[l], w_br[l], w_out[l], ln1_g[l], ln1_b[l],
                               None if l % 2 == 0 else moe_router[l // 2])
        x2, x2b = mixed[0], mixed[1]
        if l % 2 == 0:
            wg, wu, wd = ffn_weights_bf16(ffn_gate[:, None], ffn_up[:, None], ffn_down[:, None], l // 2)
            x2 = dense_ffn(x2, x2b, wg, wu, wd, ln2_g[l], ln2_b[l])
        else:
            wg, wu, wd = ffn_weights_bf16(moe_gate, moe_up, moe_down, l // 2)
            x2 = moe_ffn(x2, x2b, mixed[2], wg, wu, wd, ln2_g[l], ln2_b[l])
        x = x2.reshape(b, t, d)
    return x
```

```python
import functools
import math

import jax
import jax.numpy as jnp
from jax import lax
from jax.experimental import pallas as pl
from jax.experimental.pallas import tpu as pltpu

F32 = jnp.float32
BF16 = jnp.bfloat16

D_MODEL = 1024
DEPTH = 4
BRANCH_WIDTH = 512
HEAD_DIM = 64
NSA_HEADS = 8
NSA_KV_GROUPS = 2
NSA_GROUP = 4
CMP_LEN = 32
CMP_STRIDE = 16
SLC_BLOCK = 64
SLC_TOPN = 16
WINDOW = 512
CMP_HIDDEN = 256
Q_BLOCK = 128
CONV_K = 3
CONV_HALO = 16
RET_HEADS = 4
RET_HEAD_DIM = 128
RET_CHUNK = 256
ROPE_BASE = 10000.0
D_FF = 3584
N_EXPERTS = 8
TOP_K = 2
LN_EPS = 1e-5
ALPHA = (2.0 * DEPTH) ** 0.25

NAT_MG = 0
NAT_CB = 3072
NAT_CC = 3584
NAT_CH = 4096
NAT_RQ = 4608
NAT_RK = 5120
NAT_RV = 5632
NAT_RG = 6144
NAT_KP = 6656
NAT_KC = 6912
NAT_VC = 7040
NAT_COLS = 7168
TR_Q = 0
TR_VS = 512
TR_VW = 640
TR_NG = 768
TR_ROWS = 800

NEG = -1e30
SEL_TK = 1024
LOG2E = 1.4426950408889634
NSA_QSUB = 2
CMP_CLASS_ROWS = 256
PG_PAD = 8
PICKED = -3e38
VT_ROWS = 80
CAST_STEPS = 16
VMEM_LIMIT = 48 * 1024 * 1024


def _cparams(n_axes, vmem=VMEM_LIMIT):
    return pltpu.CompilerParams(dimension_semantics=("arbitrary",) * n_axes, vmem_limit_bytes=vmem)


def _proj_kernel(x_ref, w_ref, wt_ref, o_ref, ot_ref, xb_ref):
    @pl.when(pl.program_id(1) == 0)
    def _():
        xb = x_ref[...].astype(BF16)
        xb_ref[...] = xb
        ot_ref[0] = lax.dot_general(wt_ref[...], xb, (((1,), (1,)), ((), ())), preferred_element_type=F32)

    o_ref[...] = jnp.dot(xb_ref[...], w_ref[...], preferred_element_type=F32).astype(o_ref.dtype)


def input_projection(x, w_nat, w_t, tm, tn):
    b, t, k = x.shape
    m = b * t
    n = w_nat.shape[1]
    r = w_t.shape[0]
    tps = t // tm
    return pl.pallas_call(
        _proj_kernel,
        grid=(m // tm, n // tn),
        in_specs=[pl.BlockSpec((tm, k), lambda i, j: (i, 0)), pl.BlockSpec((k, tn), lambda i, j: (0, j)),
                  pl.BlockSpec((r, k), lambda i, j: (0, 0))],
        out_specs=[pl.BlockSpec((tm, tn), lambda i, j: (i, j)),
                   pl.BlockSpec((1, r, tm), lambda i, j: (i // tps, 0, i % tps))],
        out_shape=[jax.ShapeDtypeStruct((m, n), BF16), jax.ShapeDtypeStruct((b, r, t), F32)],
        scratch_shapes=[pltpu.VMEM((tm, k), BF16)],
        compiler_params=_cparams(2),
        name="input_projection",
    )(x.reshape(m, k), w_nat, w_t)


def _cmp_kernel(c_ref, pos_ref, w1_ref, b1_ref, w2_ref, w2t_ref, b2_ref, b2t_ref, on_ref, ot_ref):
    half = CMP_STRIDE * HEAD_DIM
    c = c_ref[0, 0, 0].astype(BF16)
    w1 = w1_ref[0]
    a = jnp.dot(c, w1[:half], preferred_element_type=F32)
    bm = jnp.dot(c, w1[half:], preferred_element_type=F32)
    nch = a.shape[0]
    bm = pltpu.roll(bm, nch - 1, 0)
    posb = jnp.dot(pos_ref[0].astype(BF16), w1, preferred_element_type=F32)[0:1]
    hid = jax.nn.gelu(a + bm + posb + b1_ref[0])
    hb = hid.astype(BF16)
    on_ref[0, 0, 0] = jnp.dot(hb, w2_ref[0], preferred_element_type=F32) + b2_ref[0]
    ot_ref[0, 0, 0] = lax.dot_general(w2t_ref[0], hb, (((1,), (1,)), ((), ())),
                                      preferred_element_type=F32) + b2t_ref[0]


def nsa_compress_kv(chunks, pos, w1, b1, w2, b2):
    _, b, g, nch, cw = chunks.shape
    pos_flat = jnp.broadcast_to(pos.reshape(2, 1, CMP_LEN * HEAD_DIM), (2, 8, CMP_LEN * HEAD_DIM))
    w1b = w1.astype(BF16)
    w2b = w2.astype(BF16)
    w2t = jnp.swapaxes(w2, 1, 2).astype(BF16)
    b1r = b1.reshape(2, 1, CMP_HIDDEN)
    b2r = b2.reshape(2, 1, HEAD_DIM)
    b2t = b2.reshape(2, HEAD_DIM, 1)
    sel = lambda kv, i, j: (kv, 0, 0)
    return pl.pallas_call(
        _cmp_kernel,
        grid=(2, b, g),
        in_specs=[
            pl.BlockSpec((1, 1, 1, nch, cw), lambda kv, i, j: (kv, i, j, 0, 0)),
            pl.BlockSpec((1, 8, CMP_LEN * HEAD_DIM), sel),
            pl.BlockSpec((1, CMP_LEN * HEAD_DIM, CMP_HIDDEN), sel),
            pl.BlockSpec((1, 1, CMP_HIDDEN), sel),
            pl.BlockSpec((1, CMP_HIDDEN, HEAD_DIM), sel),
            pl.BlockSpec((1, HEAD_DIM, CMP_HIDDEN), sel),
            pl.BlockSpec((1, 1, HEAD_DIM), sel),
            pl.BlockSpec((1, HEAD_DIM, 1), sel),
        ],
        out_specs=[
            pl.BlockSpec((1, 1, 1, nch, HEAD_DIM), lambda kv, i, j: (kv, i, j, 0, 0)),
            pl.BlockSpec((1, 1, 1, HEAD_DIM, nch), lambda kv, i, j: (kv, i, j, 0, 0)),
        ],
        out_shape=[
            jax.ShapeDtypeStruct((2, b, g, nch, HEAD_DIM), F32),
            jax.ShapeDtypeStruct((2, b, g, HEAD_DIM, nch), F32),
        ],
        compiler_params=_cparams(3),
        name="nsa_compress",
    )(chunks, pos_flat, w1b, b1r, w2b, w2t, b2r, b2t)


def _accumulate(s, mt, vt, m, acc):
    m_new = jnp.maximum(m, mt)
    alpha = jnp.exp2(m - m_new)
    p = jnp.exp2(s - m_new).astype(BF16)
    return m_new, alpha * acc + jnp.dot(vt, p, preferred_element_type=F32)


def _nsa_kernel(qt_ref, ng_ref, kc_ref, vct_ref, kp_ref, vst_ref, vwt_ref, e_ref, tri_ref, wpat_ref,
                cpat_ref, o_ref, bias_ref, w_ref, sbuf_ref, oc_ref, pg_ref):
    nq = Q_BLOCK
    nsub = qt_ref.shape[2] // nq
    ncol = NSA_GROUP * nq
    n_sel = bias_ref.shape[1]
    ncp = kc_ref.shape[2]
    kw = 2 * HEAD_DIM
    bpt = SEL_TK // SLC_BLOCK
    n_tiles_total = kp_ref.shape[1] // SEL_TK
    qb_first = pl.program_id(2) * nsub
    wlen = WINDOW + nq

    def prepare(u):
        qb = qb_first + u
        q0 = pl.multiple_of(qb * nq, nq)
        qt = qt_ref[0, :, u * nq:(u + 1) * nq] * (HEAD_DIM ** -0.5 * LOG2E)
        qs = jnp.concatenate([qt[r * HEAD_DIM:(r + 1) * HEAD_DIM, :] for r in range(NSA_GROUP)], axis=1)
        zq = jnp.zeros_like(qs)
        q_c = qs.astype(BF16)
        q_slc = jnp.concatenate([qs, zq], axis=0).astype(BF16)
        q_win = jnp.concatenate([zq, qs], axis=0).astype(BF16)
        tq = q0 + (lax.broadcasted_iota(jnp.int32, (1, ncol), 1) & (nq - 1))

        coff = pl.multiple_of(ncp - qb * (nq // CMP_STRIDE), nq // CMP_STRIDE)
        tq1 = q0 + lax.broadcasted_iota(jnp.int32, (1, nq), 1)
        cur = jnp.right_shift(tq1, 6).astype(F32)
        first_diag = (2 * qb).astype(F32)
        pg_ref[u, 0:PG_PAD, :] = jnp.zeros((PG_PAD, nq), F32)
        crow = min(CMP_CLASS_ROWS, ncp)
        cls = qb // (crow // (nq // CMP_STRIDE))
        for c in range(ncp // crow):
            nc_c = crow * (c + 1)
            ns_c = nc_c * CMP_STRIDE // SLC_BLOCK

            @pl.when(cls == c)
            def _(nc_c=nc_c, ns_c=ns_c):
                s_c = (jnp.dot(kc_ref[0, 0, 0:nc_c, :], q_c, preferred_element_type=F32)
                       + jnp.concatenate([cpat_ref[pl.ds(coff, nc_c), :]] * NSA_GROUP, axis=1))
                m_c = jnp.max(s_c, axis=0, keepdims=True)
                e_c = jnp.exp2(s_c - m_c)
                l_c = jnp.sum(e_c, axis=0, keepdims=True)
                p_c = e_c * jnp.where(tq >= CMP_LEN - 1, 1.0 / l_c, 0.0)
                oc_ref[u] = jnp.dot(vct_ref[0, 0, :, 0:nc_c], p_c.astype(BF16), preferred_element_type=F32)

                p_grp = p_c[:, 0:nq]
                for r in range(1, NSA_GROUP):
                    p_grp = p_grp + p_c[:, r * nq:(r + 1) * nq]
                pg_ref[u, PG_PAD:PG_PAD + nc_c, :] = p_grp
                ratio = SLC_BLOCK // CMP_STRIDE
                taps = [pg_ref[u, pl.ds(PG_PAD - 1 + k, ns_c, stride=ratio), :] for k in range(ratio + 1)]
                score = taps[0] + 2.0 * (taps[1] + taps[2] + taps[3]) + taps[4]
                j_io = lax.broadcasted_iota(jnp.int32, (ns_c, nq), 0).astype(F32)
                score = jnp.where(j_io <= cur, score, -1.0)
                score = jnp.where(j_io == 0.0, PICKED, score)
                score = jnp.where(j_io == cur, PICKED, score)
                score = jnp.where(j_io == cur - 1.0, PICKED, score)
                for _ in range(SLC_TOPN - 3):
                    mx = jnp.max(score, axis=0, keepdims=True)
                    first = jnp.min(jnp.where(score == mx, j_io, float(ns_c)), axis=0, keepdims=True)
                    score = jnp.where(j_io == first, PICKED, score)
                bias = jnp.where(j_io < first_diag, jnp.where(score == PICKED, 0.0, NEG), NEG)
                bias_ref[u, 0:ns_c, :] = bias.astype(BF16)
                if ns_c < n_sel:
                    bias_ref[u, ns_c:, :] = jnp.full((n_sel - ns_c, nq), NEG, BF16)

        o_c = oc_ref[u]

        wstart = pl.multiple_of(jnp.maximum(q0 - WINDOW, 0), nq)
        woff = pl.multiple_of(jnp.maximum(WINDOW - q0, 0), nq)
        s_w = jnp.dot(kp_ref[0, pl.ds(wstart, wlen), :], q_win, preferred_element_type=F32)
        s_w = s_w + jnp.concatenate([wpat_ref[pl.ds(woff, wlen), :]] * NSA_GROUP, axis=1)
        p_w = jnp.exp2(s_w - jnp.max(s_w, axis=0, keepdims=True)).astype(BF16)
        acc_w = jnp.dot(vwt_ref[0, :, pl.ds(wstart, wlen)], p_w, preferred_element_type=F32)
        o_w = acc_w[0:HEAD_DIM] / acc_w[HEAD_DIM:HEAD_DIM + 1]

        for slot in range(2):
            w_ref[slot, u, 0:kw, :] = q_slc
            w_ref[slot, u, kw:, :] = jnp.zeros((w_ref.shape[2] - kw, ncol), BF16)
        return q0, q_slc, o_c, o_w

    prepared = [prepare(u) for u in range(nsub)]

    def tile_start(j):
        jc = jnp.minimum(j, n_tiles_total - 1)
        return jc, pl.multiple_of(jc * SEL_TK, SEL_TK)

    def produce(j, slot):
        jc, k0 = tile_start(j)
        lhs = jnp.concatenate([kp_ref[0, pl.ds(k0, SEL_TK), :], e_ref[...]], axis=1)
        mts = []
        for u in range(nsub):
            b16 = bias_ref[u, pl.ds(pl.multiple_of(jc * bpt, bpt), bpt), :]
            w_ref[slot, u, kw:kw + bpt, :] = jnp.concatenate([b16] * NSA_GROUP, axis=1)
            s = jnp.dot(lhs, w_ref[slot, u], preferred_element_type=F32)
            sbuf_ref[slot, u] = s
            mts.append(jnp.max(s, axis=0, keepdims=True))
        return tuple(mts)

    def consume(j, slot, mts, state):
        _, k0 = tile_start(j)
        vt = vst_ref[0, :, pl.ds(k0, SEL_TK)]
        return tuple(_accumulate(sbuf_ref[slot, u], mts[u], vt, *state[u]) for u in range(nsub))

    def pair(i, carry, last):
        mt_a, state = carry
        mt_b = produce(2 * i + 1, 1)
        state = consume(2 * i, 0, mt_a, state)
        if last:
            return consume(2 * i + 1, 1, mt_b, state)
        mt_a = produce(2 * i + 2, 0)
        state = consume(2 * i + 1, 1, mt_b, state)
        return mt_a, state

    n_main = ((qb_first + nsub - 1) * nq + (SEL_TK - 1)) // SEL_TK
    n_pairs = jnp.maximum((n_main + 1) // 2, 1)
    state = tuple((jnp.full((1, ncol), NEG, F32), jnp.zeros((vst_ref.shape[1], ncol), F32)) for _ in range(nsub))
    carry = lax.fori_loop(0, n_pairs - 1, lambda i, c: pair(i, c, False), (produce(0, 0), state))
    state = pair(n_pairs - 1, carry, True)

    for u in range(nsub):
        q0, q_slc, o_c, o_w = prepared[u]
        s_d = jnp.dot(kp_ref[0, pl.ds(q0, nq), :], q_slc, preferred_element_type=F32) + tri_ref[...]
        _, acc_s = _accumulate(s_d, jnp.max(s_d, axis=0, keepdims=True), vst_ref[0, :, pl.ds(q0, nq)], *state[u])
        o_s = acc_s[0:HEAD_DIM] / acc_s[HEAD_DIM:HEAD_DIM + 1]
        gts = jax.nn.sigmoid(ng_ref[0, :, u * nq:(u + 1) * nq])
        rows = []
        for r in range(NSA_GROUP):
            sl = slice(r * nq, (r + 1) * nq)
            rows.append(gts[3 * r:3 * r + 1, :] * o_c[:, sl] + gts[3 * r + 1:3 * r + 2, :] * o_s[:, sl]
                        + gts[3 * r + 2:3 * r + 3, :] * o_w[:, sl])
        y_t = jnp.concatenate(rows, axis=0)
        o_ref[0, u * nq:(u + 1) * nq, :] = y_t.T.astype(o_ref.dtype)


def _nsa_constants(ncp):
    nq = Q_BLOCK
    ql = jnp.arange(nq)[None, :]
    r = jnp.arange(SEL_TK)[:, None]
    e = (r // SLC_BLOCK == jnp.arange(2 * HEAD_DIM)[None, :]).astype(BF16)
    rd = jnp.arange(nq)[:, None]
    tri = jnp.tile(jnp.where(rd <= ql, 0.0, NEG).astype(F32), (1, NSA_GROUP))
    rw = jnp.arange(2 * WINDOW + nq)[:, None]
    wpat = jnp.where((rw > ql) & (rw <= ql + WINDOW), 0.0, NEG).astype(F32)
    rc = jnp.arange(2 * ncp)[:, None] - ncp
    cpat = jnp.where(CMP_STRIDE * rc + (CMP_LEN - 1) <= ql, 0.0, NEG).astype(F32)
    return e, tri, wpat, cpat


def nsa_attention(proj_t, kc, vct, hn, vt_pack):
    b, _, t = proj_t.shape
    ncp = kc.shape[2]
    n_sel = t // SLC_BLOCK
    gq = NSA_GROUP * HEAD_DIM
    qstep = NSA_QSUB * Q_BLOCK
    e, tri, wpat, cpat = _nsa_constants(ncp)
    const2 = lambda i, g, q: (0, 0)
    return pl.pallas_call(
        _nsa_kernel,
        grid=(b, NSA_KV_GROUPS, t // qstep),
        in_specs=[
            pl.BlockSpec((1, gq, qstep), lambda i, g, q: (i, g, q)),
            pl.BlockSpec((1, 16, qstep), lambda i, g, q: (i, TR_NG // 16 + g, q)),
            pl.BlockSpec((1, 1, ncp, HEAD_DIM), lambda i, g, q: (i, g, 0, 0)),
            pl.BlockSpec((1, 1, HEAD_DIM, ncp), lambda i, g, q: (i, g, 0, 0)),
            pl.BlockSpec((1, t, 2 * HEAD_DIM), lambda i, g, q: (i, 0, NAT_KP // (2 * HEAD_DIM) + g)),
            pl.BlockSpec((1, VT_ROWS, t), lambda i, g, q: (i, g, 0)),
            pl.BlockSpec((1, VT_ROWS, t), lambda i, g, q: (i, NSA_KV_GROUPS + g, 0)),
            pl.BlockSpec(e.shape, const2),
            pl.BlockSpec(tri.shape, const2),
            pl.BlockSpec(wpat.shape, const2),
            pl.BlockSpec(cpat.shape, const2),
        ],
        out_specs=pl.BlockSpec((1, qstep, gq), lambda i, g, q: (i, q, g)),
        out_shape=jax.ShapeDtypeStruct((b, t, NSA_HEADS * HEAD_DIM), BF16),
        scratch_shapes=[pltpu.VMEM((NSA_QSUB, n_sel, Q_BLOCK), BF16),
                        pltpu.VMEM((2, NSA_QSUB, 4 * HEAD_DIM, NSA_GROUP * Q_BLOCK), BF16),
                        pltpu.VMEM((2, NSA_QSUB, SEL_TK, NSA_GROUP * Q_BLOCK), F32),
                        pltpu.VMEM((NSA_QSUB, HEAD_DIM, NSA_GROUP * Q_BLOCK), F32),
                        pltpu.VMEM((NSA_QSUB, PG_PAD + ncp, Q_BLOCK), F32)],
        compiler_params=_cparams(3),
        name="nsa_attention",
    )(proj_t, proj_t, kc, vct, hn, vt_pack, vt_pack, e, tri, wpat, cpat)


def _ret_kernel(rq_ref, rk_ref, rv_ref, rg_ref, cos_ref, sin_ref, dec_ref, qd_ref, kd_ref, cd_ref,
                gg_ref, gb_ref, o_ref, st_ref):
    c = RET_CHUNK
    hd = RET_HEAD_DIM

    @pl.when(pl.program_id(1) == 0)
    def _():
        st_ref[...] = jnp.zeros_like(st_ref)

    n_chunks = rq_ref.shape[1] // c

    def chunk(ci, _):
        r0 = pl.multiple_of(ci * c, c)
        cos = cos_ref[pl.ds(r0, c), :]
        sin = sin_ref[pl.ds(r0, c), :]
        for h in range(RET_HEADS):
            hs = slice(h * hd, (h + 1) * hd)
            q = rq_ref[0, pl.ds(r0, c), hs].astype(F32)
            k = rk_ref[0, pl.ds(r0, c), hs].astype(F32)
            v = rv_ref[0, pl.ds(r0, c), hs]
            q = q * cos + pltpu.roll(q, hd // 2, 1) * sin
            k = (k * cos + pltpu.roll(k, hd // 2, 1) * sin) * (hd ** -0.5)
            qb = q.astype(BF16)
            kb = k.astype(BF16)
            vb = v.astype(BF16)
            inner = lax.dot_general(qb, kb, (((1,), (1,)), ((), ())), preferred_element_type=F32) * dec_ref[h]
            st = st_ref[h]
            o = (jnp.dot(inner.astype(BF16), vb, preferred_element_type=F32)
                 + jnp.dot(qb, st.astype(BF16), preferred_element_type=F32) * qd_ref[h])
            kdt = (k * kd_ref[h]).T.astype(BF16)
            st_ref[h] = st * cd_ref[h] + jnp.dot(kdt, vb, preferred_element_type=F32)
            mu = jnp.mean(o, axis=-1, keepdims=True)
            d = o - mu
            var = jnp.mean(d * d, axis=-1, keepdims=True)
            y = d * lax.rsqrt(var + LN_EPS) * gg_ref[:, hs] + gb_ref[:, hs]
            o_ref[0, pl.ds(r0, c), hs] = (jax.nn.silu(rg_ref[0, pl.ds(r0, c), hs].astype(F32)) * y).astype(o_ref.dtype)
        return 0

    lax.fori_loop(0, n_chunks, chunk, 0)


def retention_branch(hn, gn_g, gn_b, tr):
    b, t, _ = hn.shape
    w = BRANCH_WIDTH
    hd = RET_HEAD_DIM
    c = RET_CHUNK
    inv = ROPE_BASE ** (-jnp.arange(0, hd, 2, dtype=F32) / hd)
    ang = jnp.arange(t, dtype=F32)[:, None] * inv[None, :]
    cos2 = jnp.concatenate([jnp.cos(ang), jnp.cos(ang)], axis=1)
    sin2 = jnp.concatenate([-jnp.sin(ang), jnp.sin(ang)], axis=1)
    log_g = jnp.log(1.0 - 2.0 ** (-5.0 - jnp.arange(RET_HEADS, dtype=F32)))
    idx = jnp.arange(c, dtype=F32)
    diff = idx[:, None] - idx[None, :]
    decay_in = jnp.where(diff >= 0, jnp.exp(log_g[:, None, None] * jnp.maximum(diff, 0.0)), 0.0)
    ones = jnp.ones((RET_HEADS, c, hd), F32)
    q_dec = jnp.exp(log_g[:, None] * (idx + 1.0))[:, :, None] * ones
    k_dec = jnp.exp(log_g[:, None] * (c - 1.0 - idx))[:, :, None] * ones
    c_dec = jnp.exp(log_g * c)[:, None, None] * jnp.ones((RET_HEADS, hd, hd), F32)
    col = lambda cc: (lambda i, j: (i, j, cc // w))
    full3 = lambda i, j: (0, 0, 0)
    return pl.pallas_call(
        _ret_kernel,
        grid=(b, t // tr),
        in_specs=[
            pl.BlockSpec((1, tr, w), col(NAT_RQ)),
            pl.BlockSpec((1, tr, w), col(NAT_RK)),
            pl.BlockSpec((1, tr, w), col(NAT_RV)),
            pl.BlockSpec((1, tr, w), col(NAT_RG)),
            pl.BlockSpec((tr, hd), lambda i, j: (j, 0)),
            pl.BlockSpec((tr, hd), lambda i, j: (j, 0)),
            pl.BlockSpec((RET_HEADS, c, c), full3),
            pl.BlockSpec((RET_HEADS, c, hd), full3),
            pl.BlockSpec((RET_HEADS, c, hd), full3),
            pl.BlockSpec((RET_HEADS, hd, hd), full3),
            pl.BlockSpec((1, w), lambda i, j: (0, 0)),
            pl.BlockSpec((1, w), lambda i, j: (0, 0)),
        ],
        out_specs=pl.BlockSpec((1, tr, w), lambda i, j: (i, j, 0)),
        out_shape=jax.ShapeDtypeStruct((b, t, w), BF16),
        scratch_shapes=[pltpu.VMEM((RET_HEADS, hd, hd), F32)],
        compiler_params=_cparams(2),
        name="retention",
    )(hn, hn, hn, hn, cos2, sin2, decay_in, q_dec, k_dec, c_dec, gn_g.reshape(1, w), gn_b.reshape(1, w))


def _layer_norm(z, g, b):
    mu = jnp.mean(z, axis=-1, keepdims=True)
    d = z - mu
    var = jnp.mean(d * d, axis=-1, keepdims=True)
    return d * lax.rsqrt(var + LN_EPS) * g + b


def _merge_kernel(tiles_per_seq, x_ref, ya_ref, cb_ref, cc_ref, ch_ref, pc_ref, ph_ref, cw_ref, yc_ref, mg_ref,
                  wbr_ref, wout_ref, g_ref, b_ref, o_ref, ob_ref):
    first = (pl.program_id(0) % tiles_per_seq) == 0
    u = cc_ref[...].astype(F32) * ch_ref[...].astype(F32)
    prev = pc_ref[...].astype(F32) * ph_ref[...].astype(F32)
    prev = jnp.where(first, 0.0, prev)
    row = lax.broadcasted_iota(jnp.int32, u.shape, 0)
    p1 = prev[CONV_HALO - 1:CONV_HALO, :]
    p2 = prev[CONV_HALO - 2:CONV_HALO - 1, :]
    u1 = jnp.where(row == 0, p1, pltpu.roll(u, 1, 0))
    u2 = jnp.where(row == 0, p2, jnp.where(row == 1, p1, pltpu.roll(u, 2, 0)))
    cw = cw_ref[...]
    y_b = cb_ref[...].astype(F32) * (cw[0:1, :] * u2 + cw[1:2, :] * u1 + cw[2:3, :] * u)

    mix = None
    for c, y in enumerate((ya_ref[...], y_b.astype(BF16), yc_ref[...])):
        proj = jnp.dot(y, wbr_ref[c], preferred_element_type=F32)
        gate = jax.nn.sigmoid(mg_ref[:, c * D_MODEL:(c + 1) * D_MODEL].astype(F32))
        mix = gate * proj if mix is None else mix + gate * proj
    m = jnp.dot(mix.astype(BF16), wout_ref[...], preferred_element_type=F32)
    out = _layer_norm(ALPHA * x_ref[...] + m, g_ref[...], b_ref[...])
    o_ref[...] = out
    ob_ref[...] = out.astype(BF16)


def merge_branches(x, ya, yc, hn, conv_w, w_br, w_out, g, b, tm, t):
    m = x.shape[0]
    d = D_MODEL
    w = BRANCH_WIDTH
    row = lambda i: (i, 0)
    col = lambda c: (lambda i: (i, c // w))
    hal = lambda c: (lambda i: (jnp.maximum(i * (tm // CONV_HALO) - 1, 0), c // w))
    wpad = jnp.zeros((8, w), F32).at[:CONV_K].set(conv_w)
    return pl.pallas_call(
        functools.partial(_merge_kernel, t // tm),
        grid=(m // tm,),
        in_specs=[
            pl.BlockSpec((tm, d), row),
            pl.BlockSpec((tm, w), row),
            pl.BlockSpec((tm, w), col(NAT_CB)),
            pl.BlockSpec((tm, w), col(NAT_CC)),
            pl.BlockSpec((tm, w), col(NAT_CH)),
            pl.BlockSpec((CONV_HALO, w), hal(NAT_CC)),
            pl.BlockSpec((CONV_HALO, w), hal(NAT_CH)),
            pl.BlockSpec((8, w), lambda i: (0, 0)),
            pl.BlockSpec((tm, w), row),
            pl.BlockSpec((tm, 3 * d), row),
            pl.BlockSpec((3, w, d), lambda i: (0, 0, 0)),
            pl.BlockSpec((d, d), lambda i: (0, 0)),
            pl.BlockSpec((1, d), lambda i: (0, 0)),
            pl.BlockSpec((1, d), lambda i: (0, 0)),
        ],
        out_specs=[pl.BlockSpec((tm, d), row), pl.BlockSpec((tm, d), row)],
        out_shape=[jax.ShapeDtypeStruct((m, d), F32), jax.ShapeDtypeStruct((m, d), BF16)],
        compiler_params=_cparams(1),
        name="merge",
    )(x, ya, hn, hn, hn, hn, hn, wpad, yc, hn, w_br.astype(BF16), w_out.astype(BF16), g.reshape(1, d),
      b.reshape(1, d))


def _cast3_kernel(g_ref, u_ref, d_ref, og_ref, ou_ref, od_ref):
    og_ref[...] = g_ref[0].astype(og_ref.dtype)
    ou_ref[...] = u_ref[0].astype(ou_ref.dtype)
    od_ref[...] = d_ref[0].astype(od_ref.dtype)


def ffn_weights_bf16(w_gate, w_up, w_down, layer):
    _, e, d, ff = w_gate.shape
    rg, rd = d // CAST_STEPS, ff // CAST_STEPS
    up_spec = pl.BlockSpec((1, 1, rg, ff), lambda i, j: (layer, i, j, 0))
    return pl.pallas_call(
        _cast3_kernel,
        grid=(e, CAST_STEPS),
        in_specs=[up_spec, up_spec, pl.BlockSpec((1, 1, rd, d), lambda i, j: (layer, i, j, 0))],
        out_specs=[pl.BlockSpec((1, rg, ff), lambda i, j: (i, j, 0)), pl.BlockSpec((1, rg, ff), lambda i, j: (i, j, 0)),
                   pl.BlockSpec((1, rd, d), lambda i, j: (i, j, 0))],
        out_shape=[jax.ShapeDtypeStruct((e, d, ff), BF16), jax.ShapeDtypeStruct((e, d, ff), BF16),
                   jax.ShapeDtypeStruct((e, ff, d), BF16)],
        compiler_params=_cparams(2),
        name="cast_bf16",
    )(w_gate, w_up, w_down)


def _mlp_kernel(te_ref, tv_ref, x_ref, wg_ref, wu_ref, wd_ref, o_ref, acc_ref):
    i = pl.program_id(0)
    f = pl.program_id(1)

    @pl.when(tv_ref[i] > 0)
    def _():
        x = x_ref[...]
        gte = jnp.dot(x, wg_ref[0], preferred_element_type=F32)
        up = jnp.dot(x, wu_ref[0], preferred_element_type=F32)
        h = (jax.nn.silu(gte) * up).astype(BF16)
        part = jnp.dot(h, wd_ref[0], preferred_element_type=F32)

        @pl.when(f == 0)
        def _():
            acc_ref[...] = part

        @pl.when(f > 0)
        def _():
            acc_ref[...] = acc_ref[...] + part

        @pl.when(f == pl.num_programs(1) - 1)
        def _():
            o_ref[...] = acc_ref[...].astype(o_ref.dtype)

    @pl.when((tv_ref[i] == 0) & (f == 0))
    def _():
        o_ref[...] = jnp.zeros_like(o_ref)


def grouped_swiglu(xs, w_gate, w_up, w_down, tile_expert, tile_valid, tm, tf):
    n, d = xs.shape
    ff = w_gate.shape[2]
    grid_spec = pltpu.PrefetchScalarGridSpec(
        num_scalar_prefetch=2,
        grid=(n // tm, ff // tf),
        in_specs=[
            pl.BlockSpec((tm, d), lambda i, f, te, tv: (i, 0)),
            pl.BlockSpec((1, d, tf), lambda i, f, te, tv: (te[i], 0, f)),
            pl.BlockSpec((1, d, tf), lambda i, f, te, tv: (te[i], 0, f)),
            pl.BlockSpec((1, tf, d), lambda i, f, te, tv: (te[i], f, 0)),
        ],
        out_specs=pl.BlockSpec((tm, d), lambda i, f, te, tv: (i, 0)),
        scratch_shapes=[pltpu.VMEM((tm, d), F32)],
    )
    return pl.pallas_call(
        _mlp_kernel,
        grid_spec=grid_spec,
        out_shape=jax.ShapeDtypeStruct((n, d), BF16),
        compiler_params=_cparams(2),
        name="grouped_swiglu",
    )(tile_expert, tile_valid, xs, w_gate, w_up, w_down)


def _add_ln_kernel(*refs):
    x_ref, sc_ref, *adds, g_ref, b_ref, o_ref = refs
    z = ALPHA * x_ref[...]
    for k, a in enumerate(adds):
        z = z + sc_ref[:, k:k + 1] * a[...].astype(F32)
    o_ref[...] = _layer_norm(z, g_ref[...], b_ref[...])


def add_layer_norm(x, adds, scales, g, b, tm):
    m, d = x.shape
    row = lambda i: (i, 0)
    return pl.pallas_call(
        _add_ln_kernel,
        grid=(m // tm,),
        in_specs=([pl.BlockSpec((tm, d), row), pl.BlockSpec((tm, len(adds)), row)]
                  + [pl.BlockSpec((tm, d), row)] * len(adds) + [pl.BlockSpec((1, d), lambda i: (0, 0))] * 2),
        out_specs=pl.BlockSpec((tm, d), row),
        out_shape=jax.ShapeDtypeStruct((m, d), F32),
        compiler_params=_cparams(1),
        name="add_layer_norm",
    )(x, scales, *adds, g.reshape(1, d), b.reshape(1, d))


def _router_kernel(x_ref, wh_ref, wl_ref, o_ref):
    x = x_ref[...]
    xh = x.astype(BF16)
    xl = (x - xh.astype(F32)).astype(BF16)
    wh = wh_ref[...]
    o_ref[...] = (jnp.dot(xh, wh, preferred_element_type=F32) + jnp.dot(xl, wh, preferred_element_type=F32)
                  + jnp.dot(xh, wl_ref[...], preferred_element_type=F32))


def router_logits(x, w_router, tm):
    m, d = x.shape
    wp = jnp.zeros((d, 128), F32).at[:, :N_EXPERTS].set(w_router)
    wh = wp.astype(BF16)
    wl = (wp - wh.astype(F32)).astype(BF16)
    return pl.pallas_call(
        _router_kernel,
        grid=(m // tm,),
        in_specs=[pl.BlockSpec((tm, d), lambda i: (i, 0)), pl.BlockSpec((d, 128), lambda i: (0, 0)),
                  pl.BlockSpec((d, 128), lambda i: (0, 0))],
        out_specs=pl.BlockSpec((tm, 128), lambda i: (i, 0)),
        out_shape=jax.ShapeDtypeStruct((m, 128), F32),
        compiler_params=_cparams(1),
        name="router",
    )(x, wh, wl)


def _dense_ffn_kernel(xb_ref, x_ref, wg_ref, wu_ref, wd_ref, g_ref, b_ref, o_ref, acc_ref):
    f = pl.program_id(1)
    xb = xb_ref[...]
    gte = jnp.dot(xb, wg_ref[0], preferred_element_type=F32)
    up = jnp.dot(xb, wu_ref[0], preferred_element_type=F32)
    h = (jax.nn.silu(gte) * up).astype(BF16)
    part = jnp.dot(h, wd_ref[0], preferred_element_type=F32)

    @pl.when(f == 0)
    def _():
        acc_ref[...] = part

    @pl.when(f > 0)
    def _():
        acc_ref[...] = acc_ref[...] + part

    @pl.when(f == pl.num_programs(1) - 1)
    def _():
        o_ref[...] = _layer_norm(ALPHA * x_ref[...] + acc_ref[...], g_ref[...], b_ref[...])


def dense_ffn(x, xb, w_gate, w_up, w_down, g, b):
    m, d = x.shape
    ff = w_gate.shape[2]
    tm = min(512, m)
    tf = 1792
    row = lambda i, f: (i, 0)
    return pl.pallas_call(
        _dense_ffn_kernel,
        grid=(m // tm, ff // tf),
        in_specs=[
            pl.BlockSpec((tm, d), row),
            pl.BlockSpec((tm, d), row),
            pl.BlockSpec((1, d, tf), lambda i, f: (0, 0, f)),
            pl.BlockSpec((1, d, tf), lambda i, f: (0, 0, f)),
            pl.BlockSpec((1, tf, d), lambda i, f: (0, f, 0)),
            pl.BlockSpec((1, d), lambda i, f: (0, 0)),
            pl.BlockSpec((1, d), lambda i, f: (0, 0)),
        ],
        out_specs=pl.BlockSpec((tm, d), row),
        out_shape=jax.ShapeDtypeStruct((m, d), F32),
        scratch_shapes=[pltpu.VMEM((tm, d), F32)],
        compiler_params=_cparams(2),
        name="dense_ffn",
    )(xb, x, w_gate, w_up, w_down, g.reshape(1, d), b.reshape(1, d))


def moe_ffn(x, xb, w_router, w_gate, w_up, w_down, g, b):
    m = x.shape[0]
    tm = 512
    logits = router_logits(x, w_router, min(1024, m))[:, :N_EXPERTS]
    top_v, top_i = lax.top_k(logits, TOP_K)
    probs = jax.nn.softmax(top_v, axis=-1)
    e_flat = top_i.reshape(-1)
    onehot = (e_flat[:, None] == jnp.arange(N_EXPERTS)[None, :]).astype(jnp.int32)
    csum = jnp.cumsum(onehot, axis=0)
    counts = csum[-1]
    rank = jnp.take_along_axis(csum, e_flat[:, None], axis=1)[:, 0] - 1
    padded = ((counts + tm - 1) // tm) * tm
    ends = jnp.cumsum(padded)
    starts = ends - padded
    pos = starts[e_flat] + rank
    n_slots = TOP_K * m + N_EXPERTS * tm
    n_tiles = n_slots // tm
    tok = jnp.arange(TOP_K * m, dtype=jnp.int32) // TOP_K
    src = jnp.zeros((n_slots,), jnp.int32).at[pos].set(tok)
    tile_start = jnp.arange(n_tiles, dtype=jnp.int32) * tm
    tile_expert = jnp.minimum(jnp.searchsorted(ends, tile_start, side="right"), N_EXPERTS - 1).astype(jnp.int32)
    tile_valid = (tile_start < ends[-1]).astype(jnp.int32)
    xs = jnp.take(xb, src, axis=0)
    ys = grouped_swiglu(xs, w_gate, w_up, w_down,
                        tile_expert, tile_valid, tm, 1792)
    pos2 = pos.reshape(m, TOP_K)
    y0 = jnp.take(ys, pos2[:, 0], axis=0)
    y1 = jnp.take(ys, pos2[:, 1], axis=0)
    return add_layer_norm(x, [y0, y1], probs, g, b, min(1024, m))


def _pack_in_weights(w):
    d = w.shape[0]
    o = 0
    q = w[:, o:o + 512]; o += 512
    kv = w[:, o:o + 768].reshape(d, 3, 2, NSA_KV_GROUPS, HEAD_DIM); o += 768
    ng = w[:, o:o + 24]; o += 24
    rest = w[:, o:o + 7 * 512]; o += 7 * 512
    mg = w[:, o:]
    kpack = jnp.concatenate([kv[:, 1, 0], kv[:, 2, 0]], axis=-1).reshape(d, 2 * NSA_KV_GROUPS * HEAD_DIM)
    kc = kv[:, 0, 0].reshape(d, NSA_KV_GROUPS * HEAD_DIM)
    vc = kv[:, 0, 1].reshape(d, NSA_KV_GROUPS * HEAD_DIM)
    w_nat = jnp.concatenate([mg, rest, kpack, kc, vc], axis=1).astype(BF16)
    vs = kv[:, 1, 1].reshape(d, NSA_KV_GROUPS * HEAD_DIM)
    vw = kv[:, 2, 1].reshape(d, NSA_KV_GROUPS * HEAD_DIM)
    ngp = jnp.pad(ng.reshape(d, NSA_KV_GROUPS, NSA_GROUP * 3), ((0, 0), (0, 0), (0, 4))).reshape(d, 32)
    w_t = jnp.concatenate([q, vs, vw, ngp], axis=1).T.astype(BF16)
    return w_nat, w_t


def token_mixer_ln(x, w_in, cmp_pos, cmp_w1, cmp_b1, cmp_w2, cmp_b2, conv_w, gn_g, gn_b, w_br, w_out, ln_g, ln_b):
    b, t, d = x.shape
    m = b * t
    w_nat, w_t = _pack_in_weights(w_in)
    x2 = x.reshape(m, d)
    hn, proj_t = input_projection(x, w_nat, w_t, min(1024, t), 1792)
    hn = hn.reshape(b, t, NAT_COLS)
    nch = t // CMP_STRIDE
    raw = hn[:, :, NAT_KC:NAT_KC + 2 * NSA_KV_GROUPS * HEAD_DIM].reshape(b, t, 2, NSA_KV_GROUPS, HEAD_DIM)
    chunks = raw.transpose(2, 0, 3, 1, 4).reshape(2, b, NSA_KV_GROUPS, nch, CMP_STRIDE * HEAD_DIM)
    cn, ct = nsa_compress_kv(chunks, cmp_pos, cmp_w1, cmp_b1, cmp_w2, cmp_b2)
    kc = cn[0].astype(BF16)
    vct = ct[1].astype(BF16)
    vt4 = proj_t[:, TR_VS:TR_NG, :].astype(BF16).reshape(b, 4, HEAD_DIM, t)
    ones_pad = jnp.zeros((b, 4, VT_ROWS - HEAD_DIM, t), BF16).at[:, :, 0].set(1.0)
    vt_pack = jnp.concatenate([vt4, ones_pad], axis=2).reshape(b, 4 * VT_ROWS, t)
    y_a = nsa_attention(proj_t, kc, vct, hn, vt_pack)
    y_c = retention_branch(hn, gn_g, gn_b, min(1024, t))
    return merge_branches(x2, y_a.reshape(m, -1), y_c.reshape(m, -1), hn.reshape(m, NAT_COLS), conv_w,
                          w_br, w_out, ln_g, ln_b, min(512, t), t)


def kernel(x, w_in, cmp_pos, cmp_w1, cmp_b1, cmp_w2, cmp_b2, conv_w, ret_gn_g, ret_gn_b, w_br, w_out,
           ln1_g, ln1_b, ln2_g, ln2_b, ffn_gate, ffn_up, ffn_down, moe_router, moe_gate, moe_up, moe_down):
    b, t, d = x.shape
    for l in range(DEPTH):
        x2, x2b = token_mixer_ln(x, w_in[l], cmp_pos[l], cmp_w1[l], cmp_b1[l], cmp_w2[l], cmp_b2[l], conv_w[l],
                                 ret_gn_g[l], ret_gn_b[l], w_br[l], w_out[l], ln1_g[l], ln1_b[l])
        if l % 2 == 0:
            wg, wu, wd = ffn_weights_bf16(ffn_gate[:, None], ffn_up[:, None], ffn_down[:, None], l // 2)
            x2 = dense_ffn(x2, x2b, wg, wu, wd, ln2_g[l], ln2_b[l])
        else:
            wg, wu, wd = ffn_weights_bf16(moe_gate, moe_up, moe_down, l // 2)
            x2 = moe_ffn(x2, x2b, moe_router[l // 2], wg, wu, wd, ln2_g[l], ln2_b[l])
        x = x2.reshape(b, t, d)
    return x
```

```python
import functools
import math

import jax
import jax.numpy as jnp
from jax import lax
from jax.experimental import pallas as pl
from jax.experimental.pallas import tpu as pltpu

F32 = jnp.float32
BF16 = jnp.bfloat16

D_MODEL = 1024
DEPTH = 4
BRANCH_WIDTH = 512
HEAD_DIM = 64
NSA_HEADS = 8
NSA_KV_GROUPS = 2
NSA_GROUP = 4
CMP_LEN = 32
CMP_STRIDE = 16
SLC_BLOCK = 64
SLC_TOPN = 16
WINDOW = 512
CMP_HIDDEN = 256
Q_BLOCK = 128
CONV_K = 3
CONV_HALO = 16
RET_HEADS = 4
RET_HEAD_DIM = 128
RET_CHUNK = 256
ROPE_BASE = 10000.0
D_FF = 3584
N_EXPERTS = 8
TOP_K = 2
LN_EPS = 1e-5
ALPHA = (2.0 * DEPTH) ** 0.25

NAT_MG = 0
NAT_CB = 3072
NAT_CC = 3584
NAT_CH = 4096
NAT_RQ = 4608
NAT_RK = 5120
NAT_RV = 5632
NAT_RG = 6144
NAT_KP = 6656
NAT_KC = 6912
NAT_VC = 7040
NAT_COLS = 7168
TR_Q = 0
TR_VS = 512
TR_VW = 640
TR_NG = 768
TR_ROWS = 800

NEG = -1e30
SEL_TK = 1024
LOG2E = 1.4426950408889634
NSA_QSUB = 2
CMP_CLASS_ROWS = 256
PG_PAD = 8
PICKED = -3e38
VT_ROWS = 80
FFN_TM = 1024
FFN_TF = 896
CAST_STEPS = 16
VMEM_LIMIT = 48 * 1024 * 1024
BIG_VMEM_LIMIT = 56 * 1024 * 1024


def _cparams(n_axes, vmem=VMEM_LIMIT):
    return pltpu.CompilerParams(dimension_semantics=("arbitrary",) * n_axes, vmem_limit_bytes=vmem)


def _proj_kernel(x_ref, w_ref, wt_ref, o_ref, ot_ref):
    @pl.when(pl.program_id(1) == 0)
    def _():
        ot_ref[0] = lax.dot_general(wt_ref[...], x_ref[...], (((1,), (1,)), ((), ())), preferred_element_type=F32)

    o_ref[...] = jnp.dot(x_ref[...], w_ref[...], preferred_element_type=F32).astype(o_ref.dtype)


def input_projection(x, w_nat, w_t, tm, tn):
    b, t, k = x.shape
    m = b * t
    n = w_nat.shape[1]
    r = w_t.shape[0]
    tps = t // tm
    return pl.pallas_call(
        _proj_kernel,
        grid=(m // tm, n // tn),
        in_specs=[pl.BlockSpec((tm, k), lambda i, j: (i, 0)), pl.BlockSpec((k, tn), lambda i, j: (0, j)),
                  pl.BlockSpec((r, k), lambda i, j: (0, 0))],
        out_specs=[pl.BlockSpec((tm, tn), lambda i, j: (i, j)),
                   pl.BlockSpec((1, r, tm), lambda i, j: (i // tps, 0, i % tps))],
        out_shape=[jax.ShapeDtypeStruct((m, n), BF16), jax.ShapeDtypeStruct((b, r, t), F32)],
        compiler_params=_cparams(2, BIG_VMEM_LIMIT),
        name="input_projection",
    )(x.reshape(m, k), w_nat, w_t)


def _cmp_kernel(c_ref, pos_ref, w1_ref, b1_ref, w2_ref, w2t_ref, b2_ref, b2t_ref, on_ref, ot_ref):
    half = CMP_STRIDE * HEAD_DIM
    c = c_ref[0, 0, 0].astype(BF16)
    w1 = w1_ref[0]
    a = jnp.dot(c, w1[:half], preferred_element_type=F32)
    bm = jnp.dot(c, w1[half:], preferred_element_type=F32)
    nch = a.shape[0]
    bm = pltpu.roll(bm, nch - 1, 0)
    posb = jnp.dot(pos_ref[0].astype(BF16), w1, preferred_element_type=F32)[0:1]
    hid = jax.nn.gelu(a + bm + posb + b1_ref[0])
    hb = hid.astype(BF16)
    on_ref[0, 0, 0] = jnp.dot(hb, w2_ref[0], preferred_element_type=F32) + b2_ref[0]
    ot_ref[0, 0, 0] = lax.dot_general(w2t_ref[0], hb, (((1,), (1,)), ((), ())),
                                      preferred_element_type=F32) + b2t_ref[0]


def nsa_compress_kv(chunks, pos, w1, b1, w2, b2):
    _, b, g, nch, cw = chunks.shape
    pos_flat = jnp.broadcast_to(pos.reshape(2, 1, CMP_LEN * HEAD_DIM), (2, 8, CMP_LEN * HEAD_DIM))
    w1b = w1.astype(BF16)
    w2b = w2.astype(BF16)
    w2t = jnp.swapaxes(w2, 1, 2).astype(BF16)
    b1r = b1.reshape(2, 1, CMP_HIDDEN)
    b2r = b2.reshape(2, 1, HEAD_DIM)
    b2t = b2.reshape(2, HEAD_DIM, 1)
    sel = lambda kv, i, j: (kv, 0, 0)
    return pl.pallas_call(
        _cmp_kernel,
        grid=(2, b, g),
        in_specs=[
            pl.BlockSpec((1, 1, 1, nch, cw), lambda kv, i, j: (kv, i, j, 0, 0)),
            pl.BlockSpec((1, 8, CMP_LEN * HEAD_DIM), sel),
            pl.BlockSpec((1, CMP_LEN * HEAD_DIM, CMP_HIDDEN), sel),
            pl.BlockSpec((1, 1, CMP_HIDDEN), sel),
            pl.BlockSpec((1, CMP_HIDDEN, HEAD_DIM), sel),
            pl.BlockSpec((1, HEAD_DIM, CMP_HIDDEN), sel),
            pl.BlockSpec((1, 1, HEAD_DIM), sel),
            pl.BlockSpec((1, HEAD_DIM, 1), sel),
        ],
        out_specs=[
            pl.BlockSpec((1, 1, 1, nch, HEAD_DIM), lambda kv, i, j: (kv, i, j, 0, 0)),
            pl.BlockSpec((1, 1, 1, HEAD_DIM, nch), lambda kv, i, j: (kv, i, j, 0, 0)),
        ],
        out_shape=[
            jax.ShapeDtypeStruct((2, b, g, nch, HEAD_DIM), F32),
            jax.ShapeDtypeStruct((2, b, g, HEAD_DIM, nch), F32),
        ],
        compiler_params=_cparams(3),
        name="nsa_compress",
    )(chunks, pos_flat, w1b, b1r, w2b, w2t, b2r, b2t)


def _accumulate(s, mt, vt, m, acc):
    m_new = jnp.maximum(m, mt)
    alpha = jnp.exp2(m - m_new)
    p = jnp.exp2(s - m_new).astype(BF16)
    return m_new, alpha * acc + jnp.dot(vt, p, preferred_element_type=F32)


def _nsa_kernel(qt_ref, ng_ref, kc_ref, vct_ref, kp_ref, vst_ref, vwt_ref, e_ref, tri_ref, wpat_ref,
                cpat_ref, o_ref, bias_ref, w_ref, sbuf_ref, oc_ref, pg_ref):
    nq = Q_BLOCK
    nsub = qt_ref.shape[2] // nq
    ncol = NSA_GROUP * nq
    n_sel = bias_ref.shape[1]
    ncp = kc_ref.shape[2]
    kw = 2 * HEAD_DIM
    bpt = SEL_TK // SLC_BLOCK
    n_tiles_total = kp_ref.shape[1] // SEL_TK
    qb_first = pl.program_id(2) * nsub
    wlen = WINDOW + nq

    def prepare(u):
        qb = qb_first + u
        q0 = pl.multiple_of(qb * nq, nq)
        qt = qt_ref[0, :, u * nq:(u + 1) * nq] * (HEAD_DIM ** -0.5 * LOG2E)
        qs = jnp.concatenate([qt[r * HEAD_DIM:(r + 1) * HEAD_DIM, :] for r in range(NSA_GROUP)], axis=1)
        zq = jnp.zeros_like(qs)
        q_c = qs.astype(BF16)
        q_slc = jnp.concatenate([qs, zq], axis=0).astype(BF16)
        q_win = jnp.concatenate([zq, qs], axis=0).astype(BF16)
        tq = q0 + (lax.broadcasted_iota(jnp.int32, (1, ncol), 1) & (nq - 1))

        coff = pl.multiple_of(ncp - qb * (nq // CMP_STRIDE), nq // CMP_STRIDE)
        tq1 = q0 + lax.broadcasted_iota(jnp.int32, (1, nq), 1)
        cur = jnp.right_shift(tq1, 6).astype(F32)
        first_diag = (2 * qb).astype(F32)
        pg_ref[u, 0:PG_PAD, :] = jnp.zeros((PG_PAD, nq), F32)
        crow = min(CMP_CLASS_ROWS, ncp)
        cls = qb // (crow // (nq // CMP_STRIDE))
        for c in range(ncp // crow):
            nc_c = crow * (c + 1)
            ns_c = nc_c * CMP_STRIDE // SLC_BLOCK

            @pl.when(cls == c)
            def _(nc_c=nc_c, ns_c=ns_c):
                s_c = (jnp.dot(kc_ref[0, 0, 0:nc_c, :], q_c, preferred_element_type=F32)
                       + jnp.concatenate([cpat_ref[pl.ds(coff, nc_c), :]] * NSA_GROUP, axis=1))
                m_c = jnp.max(s_c, axis=0, keepdims=True)
                e_c = jnp.exp2(s_c - m_c)
                l_c = jnp.sum(e_c, axis=0, keepdims=True)
                p_c = e_c * jnp.where(tq >= CMP_LEN - 1, 1.0 / l_c, 0.0)
                oc_ref[u] = jnp.dot(vct_ref[0, 0, :, 0:nc_c], p_c.astype(BF16), preferred_element_type=F32)

                p_grp = p_c[:, 0:nq]
                for r in range(1, NSA_GROUP):
                    p_grp = p_grp + p_c[:, r * nq:(r + 1) * nq]
                pg_ref[u, PG_PAD:PG_PAD + nc_c, :] = p_grp
                ratio = SLC_BLOCK // CMP_STRIDE
                taps = [pg_ref[u, pl.ds(PG_PAD - 1 + k, ns_c, stride=ratio), :] for k in range(ratio + 1)]
                score = taps[0] + 2.0 * (taps[1] + taps[2] + taps[3]) + taps[4]
                j_io = lax.broadcasted_iota(jnp.int32, (ns_c, nq), 0).astype(F32)
                score = jnp.where(j_io <= cur, score, -1.0)
                score = jnp.where(j_io == 0.0, PICKED, score)
                score = jnp.where(j_io == cur, PICKED, score)
                score = jnp.where(j_io == cur - 1.0, PICKED, score)
                for _ in range(SLC_TOPN - 3):
                    mx = jnp.max(score, axis=0, keepdims=True)
                    first = jnp.min(jnp.where(score == mx, j_io, float(ns_c)), axis=0, keepdims=True)
                    score = jnp.where(j_io == first, PICKED, score)
                bias = jnp.where(j_io < first_diag, jnp.where(score == PICKED, 0.0, NEG), NEG)
                bias_ref[u, 0:ns_c, :] = bias.astype(BF16)
                if ns_c < n_sel:
                    bias_ref[u, ns_c:, :] = jnp.full((n_sel - ns_c, nq), NEG, BF16)

        o_c = oc_ref[u]

        wstart = pl.multiple_of(jnp.maximum(q0 - WINDOW, 0), nq)
        woff = pl.multiple_of(jnp.maximum(WINDOW - q0, 0), nq)
        s_w = jnp.dot(kp_ref[0, pl.ds(wstart, wlen), :], q_win, preferred_element_type=F32)
        s_w = s_w + jnp.concatenate([wpat_ref[pl.ds(woff, wlen), :]] * NSA_GROUP, axis=1)
        p_w = jnp.exp2(s_w - jnp.max(s_w, axis=0, keepdims=True)).astype(BF16)
        acc_w = jnp.dot(vwt_ref[0, :, pl.ds(wstart, wlen)], p_w, preferred_element_type=F32)
        o_w = acc_w[0:HEAD_DIM] / acc_w[HEAD_DIM:HEAD_DIM + 1]

        for slot in range(2):
            w_ref[slot, u, 0:kw, :] = q_slc
            w_ref[slot, u, kw:, :] = jnp.zeros((w_ref.shape[2] - kw, ncol), BF16)
        return q0, q_slc, o_c, o_w

    prepared = [prepare(u) for u in range(nsub)]

    def tile_start(j):
        jc = jnp.minimum(j, n_tiles_total - 1)
        return jc, pl.multiple_of(jc * SEL_TK, SEL_TK)

    def produce(j, slot):
        jc, k0 = tile_start(j)
        lhs = jnp.concatenate([kp_ref[0, pl.ds(k0, SEL_TK), :], e_ref[...]], axis=1)
        mts = []
        for u in range(nsub):
            b16 = bias_ref[u, pl.ds(pl.multiple_of(jc * bpt, bpt), bpt), :]
            w_ref[slot, u, kw:kw + bpt, :] = jnp.concatenate([b16] * NSA_GROUP, axis=1)
            s = jnp.dot(lhs, w_ref[slot, u], preferred_element_type=F32)
            sbuf_ref[slot, u] = s
            mts.append(jnp.max(s, axis=0, keepdims=True))
        return tuple(mts)

    def consume(j, slot, mts, state):
        _, k0 = tile_start(j)
        vt = vst_ref[0, :, pl.ds(k0, SEL_TK)]
        return tuple(_accumulate(sbuf_ref[slot, u], mts[u], vt, *state[u]) for u in range(nsub))

    def pair(i, carry, last):
        mt_a, state = carry
        mt_b = produce(2 * i + 1, 1)
        state = consume(2 * i, 0, mt_a, state)
        if last:
            return consume(2 * i + 1, 1, mt_b, state)
        mt_a = produce(2 * i + 2, 0)
        state = consume(2 * i + 1, 1, mt_b, state)
        return mt_a, state

    n_main = ((qb_first + nsub - 1) * nq + (SEL_TK - 1)) // SEL_TK
    n_pairs = jnp.maximum((n_main + 1) // 2, 1)
    state = tuple((jnp.full((1, ncol), NEG, F32), jnp.zeros((vst_ref.shape[1], ncol), F32)) for _ in range(nsub))
    carry = lax.fori_loop(0, n_pairs - 1, lambda i, c: pair(i, c, False), (produce(0, 0), state))
    state = pair(n_pairs - 1, carry, True)

    for u in range(nsub):
        q0, q_slc, o_c, o_w = prepared[u]
        s_d = jnp.dot(kp_ref[0, pl.ds(q0, nq), :], q_slc, preferred_element_type=F32) + tri_ref[...]
        _, acc_s = _accumulate(s_d, jnp.max(s_d, axis=0, keepdims=True), vst_ref[0, :, pl.ds(q0, nq)], *state[u])
        o_s = acc_s[0:HEAD_DIM] / acc_s[HEAD_DIM:HEAD_DIM + 1]
        gts = jax.nn.sigmoid(ng_ref[0, :, u * nq:(u + 1) * nq])
        rows = []
        for r in range(NSA_GROUP):
            sl = slice(r * nq, (r + 1) * nq)
            rows.append(gts[3 * r:3 * r + 1, :] * o_c[:, sl] + gts[3 * r + 1:3 * r + 2, :] * o_s[:, sl]
                        + gts[3 * r + 2:3 * r + 3, :] * o_w[:, sl])
        y_t = jnp.concatenate(rows, axis=0)
        o_ref[0, u * nq:(u + 1) * nq, :] = y_t.T.astype(o_ref.dtype)


def _nsa_constants(ncp):
    nq = Q_BLOCK
    ql = jnp.arange(nq)[None, :]
    r = jnp.arange(SEL_TK)[:, None]
    e = (r // SLC_BLOCK == jnp.arange(2 * HEAD_DIM)[None, :]).astype(BF16)
    rd = jnp.arange(nq)[:, None]
    tri = jnp.tile(jnp.where(rd <= ql, 0.0, NEG).astype(F32), (1, NSA_GROUP))
    rw = jnp.arange(2 * WINDOW + nq)[:, None]
    wpat = jnp.where((rw > ql) & (rw <= ql + WINDOW), 0.0, NEG).astype(F32)
    rc = jnp.arange(2 * ncp)[:, None] - ncp
    cpat = jnp.where(CMP_STRIDE * rc + (CMP_LEN - 1) <= ql, 0.0, NEG).astype(F32)
    return e, tri, wpat, cpat


def nsa_attention(proj_t, kc, vct, hn, vt_pack):
    b, _, t = proj_t.shape
    ncp = kc.shape[2]
    n_sel = t // SLC_BLOCK
    gq = NSA_GROUP * HEAD_DIM
    qstep = NSA_QSUB * Q_BLOCK
    e, tri, wpat, cpat = _nsa_constants(ncp)
    const2 = lambda i, g, q: (0, 0)
    return pl.pallas_call(
        _nsa_kernel,
        grid=(b, NSA_KV_GROUPS, t // qstep),
        in_specs=[
            pl.BlockSpec((1, gq, qstep), lambda i, g, q: (i, g, q)),
            pl.BlockSpec((1, 16, qstep), lambda i, g, q: (i, TR_NG // 16 + g, q)),
            pl.BlockSpec((1, 1, ncp, HEAD_DIM), lambda i, g, q: (i, g, 0, 0)),
            pl.BlockSpec((1, 1, HEAD_DIM, ncp), lambda i, g, q: (i, g, 0, 0)),
            pl.BlockSpec((1, t, 2 * HEAD_DIM), lambda i, g, q: (i, 0, NAT_KP // (2 * HEAD_DIM) + g)),
            pl.BlockSpec((1, VT_ROWS, t), lambda i, g, q: (i, g, 0)),
            pl.BlockSpec((1, VT_ROWS, t), lambda i, g, q: (i, NSA_KV_GROUPS + g, 0)),
            pl.BlockSpec(e.shape, const2),
            pl.BlockSpec(tri.shape, const2),
            pl.BlockSpec(wpat.shape, const2),
            pl.BlockSpec(cpat.shape, const2),
        ],
        out_specs=pl.BlockSpec((1, qstep, gq), lambda i, g, q: (i, q, g)),
        out_shape=jax.ShapeDtypeStruct((b, t, NSA_HEADS * HEAD_DIM), BF16),
        scratch_shapes=[pltpu.VMEM((NSA_QSUB, n_sel, Q_BLOCK), BF16),
                        pltpu.VMEM((2, NSA_QSUB, 4 * HEAD_DIM, NSA_GROUP * Q_BLOCK), BF16),
                        pltpu.VMEM((2, NSA_QSUB, SEL_TK, NSA_GROUP * Q_BLOCK), F32),
                        pltpu.VMEM((NSA_QSUB, HEAD_DIM, NSA_GROUP * Q_BLOCK), F32),
                        pltpu.VMEM((NSA_QSUB, PG_PAD + ncp, Q_BLOCK), F32)],
        compiler_params=_cparams(3),
        name="nsa_attention",
    )(proj_t, proj_t, kc, vct, hn, vt_pack, vt_pack, e, tri, wpat, cpat)


def _ret_kernel(rq_ref, rk_ref, rv_ref, rg_ref, cos_ref, sin_ref, dec_ref, qd_ref, kd_ref, cd_ref,
                gg_ref, gb_ref, o_ref, st_ref):
    c = RET_CHUNK
    hd = RET_HEAD_DIM

    @pl.when(pl.program_id(1) == 0)
    def _():
        st_ref[...] = jnp.zeros_like(st_ref)

    n_chunks = rq_ref.shape[1] // c

    def chunk(ci, _):
        r0 = pl.multiple_of(ci * c, c)
        cos = cos_ref[pl.ds(r0, c), :]
        sin = sin_ref[pl.ds(r0, c), :]
        for h in range(RET_HEADS):
            hs = slice(h * hd, (h + 1) * hd)
            q = rq_ref[0, pl.ds(r0, c), hs].astype(F32)
            k = rk_ref[0, pl.ds(r0, c), hs].astype(F32)
            v = rv_ref[0, pl.ds(r0, c), hs]
            q = q * cos + pltpu.roll(q, hd // 2, 1) * sin
            k = (k * cos + pltpu.roll(k, hd // 2, 1) * sin) * (hd ** -0.5)
            qb = q.astype(BF16)
            kb = k.astype(BF16)
            vb = v.astype(BF16)
            inner = lax.dot_general(qb, kb, (((1,), (1,)), ((), ())), preferred_element_type=F32) * dec_ref[h]
            st = st_ref[h]
            o = (jnp.dot(inner.astype(BF16), vb, preferred_element_type=F32)
                 + jnp.dot(qb, st.astype(BF16), preferred_element_type=F32) * qd_ref[h])
            kdt = (k * kd_ref[h]).T.astype(BF16)
            st_ref[h] = st * cd_ref[h] + jnp.dot(kdt, vb, preferred_element_type=F32)
            mu = jnp.mean(o, axis=-1, keepdims=True)
            d = o - mu
            var = jnp.mean(d * d, axis=-1, keepdims=True)
            y = d * lax.rsqrt(var + LN_EPS) * gg_ref[:, hs] + gb_ref[:, hs]
            o_ref[0, pl.ds(r0, c), hs] = (jax.nn.silu(rg_ref[0, pl.ds(r0, c), hs].astype(F32)) * y).astype(o_ref.dtype)
        return 0

    lax.fori_loop(0, n_chunks, chunk, 0)


def retention_branch(hn, gn_g, gn_b, tr):
    b, t, _ = hn.shape
    w = BRANCH_WIDTH
    hd = RET_HEAD_DIM
    c = RET_CHUNK
    inv = ROPE_BASE ** (-jnp.arange(0, hd, 2, dtype=F32) / hd)
    ang = jnp.arange(t, dtype=F32)[:, None] * inv[None, :]
    cos2 = jnp.concatenate([jnp.cos(ang), jnp.cos(ang)], axis=1)
    sin2 = jnp.concatenate([-jnp.sin(ang), jnp.sin(ang)], axis=1)
    log_g = jnp.log(1.0 - 2.0 ** (-5.0 - jnp.arange(RET_HEADS, dtype=F32)))
    idx = jnp.arange(c, dtype=F32)
    diff = idx[:, None] - idx[None, :]
    decay_in = jnp.where(diff >= 0, jnp.exp(log_g[:, None, None] * jnp.maximum(diff, 0.0)), 0.0)
    ones = jnp.ones((RET_HEADS, c, hd), F32)
    q_dec = jnp.exp(log_g[:, None] * (idx + 1.0))[:, :, None] * ones
    k_dec = jnp.exp(log_g[:, None] * (c - 1.0 - idx))[:, :, None] * ones
    c_dec = jnp.exp(log_g * c)[:, None, None] * jnp.ones((RET_HEADS, hd, hd), F32)
    col = lambda cc: (lambda i, j: (i, j, cc // w))
    full3 = lambda i, j: (0, 0, 0)
    return pl.pallas_call(
        _ret_kernel,
        grid=(b, t // tr),
        in_specs=[
            pl.BlockSpec((1, tr, w), col(NAT_RQ)),
            pl.BlockSpec((1, tr, w), col(NAT_RK)),
            pl.BlockSpec((1, tr, w), col(NAT_RV)),
            pl.BlockSpec((1, tr, w), col(NAT_RG)),
            pl.BlockSpec((tr, hd), lambda i, j: (j, 0)),
            pl.BlockSpec((tr, hd), lambda i, j: (j, 0)),
            pl.BlockSpec((RET_HEADS, c, c), full3),
            pl.BlockSpec((RET_HEADS, c, hd), full3),
            pl.BlockSpec((RET_HEADS, c, hd), full3),
            pl.BlockSpec((RET_HEADS, hd, hd), full3),
            pl.BlockSpec((1, w), lambda i, j: (0, 0)),
            pl.BlockSpec((1, w), lambda i, j: (0, 0)),
        ],
        out_specs=pl.BlockSpec((1, tr, w), lambda i, j: (i, j, 0)),
        out_shape=jax.ShapeDtypeStruct((b, t, w), BF16),
        scratch_shapes=[pltpu.VMEM((RET_HEADS, hd, hd), F32)],
        compiler_params=_cparams(2),
        name="retention",
    )(hn, hn, hn, hn, cos2, sin2, decay_in, q_dec, k_dec, c_dec, gn_g.reshape(1, w), gn_b.reshape(1, w))


def _layer_norm(z, g, b):
    mu = jnp.mean(z, axis=-1, keepdims=True)
    d = z - mu
    var = jnp.mean(d * d, axis=-1, keepdims=True)
    return d * lax.rsqrt(var + LN_EPS) * g + b


def _merge_kernel(tiles_per_seq, x_ref, ya_ref, cb_ref, cc_ref, ch_ref, pc_ref, ph_ref, cw_ref, yc_ref, mg_ref,
                  wbr_ref, wout_ref, g_ref, b_ref, o_ref, ob_ref):
    first = (pl.program_id(0) % tiles_per_seq) == 0
    u = cc_ref[...].astype(F32) * ch_ref[...].astype(F32)
    prev = pc_ref[...].astype(F32) * ph_ref[...].astype(F32)
    prev = jnp.where(first, 0.0, prev)
    row = lax.broadcasted_iota(jnp.int32, u.shape, 0)
    p1 = prev[CONV_HALO - 1:CONV_HALO, :]
    p2 = prev[CONV_HALO - 2:CONV_HALO - 1, :]
    u1 = jnp.where(row == 0, p1, pltpu.roll(u, 1, 0))
    u2 = jnp.where(row == 0, p2, jnp.where(row == 1, p1, pltpu.roll(u, 2, 0)))
    cw = cw_ref[...]
    y_b = cb_ref[...].astype(F32) * (cw[0:1, :] * u2 + cw[1:2, :] * u1 + cw[2:3, :] * u)

    mix = None
    for c, y in enumerate((ya_ref[...], y_b.astype(BF16), yc_ref[...])):
        proj = jnp.dot(y, wbr_ref[c], preferred_element_type=F32)
        gate = jax.nn.sigmoid(mg_ref[:, c * D_MODEL:(c + 1) * D_MODEL].astype(F32))
        mix = gate * proj if mix is None else mix + gate * proj
    m = jnp.dot(mix.astype(BF16), wout_ref[...], preferred_element_type=F32)
    out = _layer_norm(ALPHA * x_ref[...] + m, g_ref[...], b_ref[...])
    o_ref[...] = out
    ob_ref[...] = out.astype(BF16)


def merge_branches(x, ya, yc, hn, conv_w, w_br, w_out, g, b, tm, t):
    m = x.shape[0]
    d = D_MODEL
    w = BRANCH_WIDTH
    row = lambda i: (i, 0)
    col = lambda c: (lambda i: (i, c // w))
    hal = lambda c: (lambda i: (jnp.maximum(i * (tm // CONV_HALO) - 1, 0), c // w))
    wpad = jnp.zeros((8, w), F32).at[:CONV_K].set(conv_w)
    return pl.pallas_call(
        functools.partial(_merge_kernel, t // tm),
        grid=(m // tm,),
        in_specs=[
            pl.BlockSpec((tm, d), row),
            pl.BlockSpec((tm, w), row),
            pl.BlockSpec((tm, w), col(NAT_CB)),
            pl.BlockSpec((tm, w), col(NAT_CC)),
            pl.BlockSpec((tm, w), col(NAT_CH)),
            pl.BlockSpec((CONV_HALO, w), hal(NAT_CC)),
            pl.BlockSpec((CONV_HALO, w), hal(NAT_CH)),
            pl.BlockSpec((8, w), lambda i: (0, 0)),
            pl.BlockSpec((tm, w), row),
            pl.BlockSpec((tm, 3 * d), row),
            pl.BlockSpec((3, w, d), lambda i: (0, 0, 0)),
            pl.BlockSpec((d, d), lambda i: (0, 0)),
            pl.BlockSpec((1, d), lambda i: (0, 0)),
            pl.BlockSpec((1, d), lambda i: (0, 0)),
        ],
        out_specs=[pl.BlockSpec((tm, d), row), pl.BlockSpec((tm, d), row)],
        out_shape=[jax.ShapeDtypeStruct((m, d), F32), jax.ShapeDtypeStruct((m, d), BF16)],
        compiler_params=_cparams(1),
        name="merge",
    )(x, ya, hn, hn, hn, hn, hn, wpad, yc, hn, w_br.astype(BF16), w_out.astype(BF16), g.reshape(1, d),
      b.reshape(1, d))


def _cast3_kernel(g_ref, u_ref, d_ref, og_ref, ou_ref, od_ref):
    og_ref[...] = g_ref[0].astype(og_ref.dtype)
    ou_ref[...] = u_ref[0].astype(ou_ref.dtype)
    od_ref[...] = d_ref[0].astype(od_ref.dtype)


def ffn_weights_bf16(w_gate, w_up, w_down, layer):
    _, e, d, ff = w_gate.shape
    rg, rd = d // CAST_STEPS, ff // CAST_STEPS
    up_spec = pl.BlockSpec((1, 1, rg, ff), lambda i, j: (layer, i, j, 0))
    return pl.pallas_call(
        _cast3_kernel,
        grid=(e, CAST_STEPS),
        in_specs=[up_spec, up_spec, pl.BlockSpec((1, 1, rd, d), lambda i, j: (layer, i, j, 0))],
        out_specs=[pl.BlockSpec((1, rg, ff), lambda i, j: (i, j, 0)), pl.BlockSpec((1, rg, ff), lambda i, j: (i, j, 0)),
                   pl.BlockSpec((1, rd, d), lambda i, j: (i, j, 0))],
        out_shape=[jax.ShapeDtypeStruct((e, d, ff), BF16), jax.ShapeDtypeStruct((e, d, ff), BF16),
                   jax.ShapeDtypeStruct((e, ff, d), BF16)],
        compiler_params=_cparams(2),
        name="cast_bf16",
    )(w_gate, w_up, w_down)


def _mlp_kernel(te_ref, tv_ref, x_ref, wg_ref, wu_ref, wd_ref, o_ref, acc_ref):
    i = pl.program_id(0)
    f = pl.program_id(1)

    @pl.when(tv_ref[i] > 0)
    def _():
        x = x_ref[...]
        gte = jnp.dot(x, wg_ref[0], preferred_element_type=F32)
        up = jnp.dot(x, wu_ref[0], preferred_element_type=F32)
        h = (jax.nn.silu(gte) * up).astype(BF16)
        part = jnp.dot(h, wd_ref[0], preferred_element_type=F32)

        @pl.when(f == 0)
        def _():
            acc_ref[...] = part

        @pl.when(f > 0)
        def _():
            acc_ref[...] = acc_ref[...] + part

        @pl.when(f == pl.num_programs(1) - 1)
        def _():
            o_ref[...] = acc_ref[...].astype(o_ref.dtype)

    @pl.when((tv_ref[i] == 0) & (f == 0))
    def _():
        o_ref[...] = jnp.zeros_like(o_ref)


def grouped_swiglu(xs, w_gate, w_up, w_down, tile_expert, tile_valid, tm, tf):
    n, d = xs.shape
    ff = w_gate.shape[2]
    grid_spec = pltpu.PrefetchScalarGridSpec(
        num_scalar_prefetch=2,
        grid=(n // tm, ff // tf),
        in_specs=[
            pl.BlockSpec((tm, d), lambda i, f, te, tv: (i, 0)),
            pl.BlockSpec((1, d, tf), lambda i, f, te, tv: (te[i], 0, f)),
            pl.BlockSpec((1, d, tf), lambda i, f, te, tv: (te[i], 0, f)),
            pl.BlockSpec((1, tf, d), lambda i, f, te, tv: (te[i], f, 0)),
        ],
        out_specs=pl.BlockSpec((tm, d), lambda i, f, te, tv: (i, 0)),
        scratch_shapes=[pltpu.VMEM((tm, d), F32)],
    )
    return pl.pallas_call(
        _mlp_kernel,
        grid_spec=grid_spec,
        out_shape=jax.ShapeDtypeStruct((n, d), BF16),
        compiler_params=_cparams(2, BIG_VMEM_LIMIT),
        name="grouped_swiglu",
    )(tile_expert, tile_valid, xs, w_gate, w_up, w_down)


def _add_ln_kernel(*refs):
    x_ref, sc_ref, *adds, g_ref, b_ref, o_ref, ob_ref = refs
    z = ALPHA * x_ref[...]
    for k, a in enumerate(adds):
        z = z + sc_ref[:, k:k + 1] * a[...].astype(F32)
    out = _layer_norm(z, g_ref[...], b_ref[...])
    o_ref[...] = out
    ob_ref[...] = out.astype(BF16)


def add_layer_norm(x, adds, scales, g, b, tm):
    m, d = x.shape
    row = lambda i: (i, 0)
    return pl.pallas_call(
        _add_ln_kernel,
        grid=(m // tm,),
        in_specs=([pl.BlockSpec((tm, d), row), pl.BlockSpec((tm, len(adds)), row)]
                  + [pl.BlockSpec((tm, d), row)] * len(adds) + [pl.BlockSpec((1, d), lambda i: (0, 0))] * 2),
        out_specs=[pl.BlockSpec((tm, d), row)] * 2,
        out_shape=[jax.ShapeDtypeStruct((m, d), F32), jax.ShapeDtypeStruct((m, d), BF16)],
        compiler_params=_cparams(1),
        name="add_layer_norm",
    )(x, scales, *adds, g.reshape(1, d), b.reshape(1, d))


def _router_kernel(x_ref, wh_ref, wl_ref, o_ref):
    x = x_ref[...]
    xh = x.astype(BF16)
    xl = (x - xh.astype(F32)).astype(BF16)
    wh = wh_ref[...]
    o_ref[...] = (jnp.dot(xh, wh, preferred_element_type=F32) + jnp.dot(xl, wh, preferred_element_type=F32)
                  + jnp.dot(xh, wl_ref[...], preferred_element_type=F32))


def router_logits(x, w_router, tm):
    m, d = x.shape
    wp = jnp.zeros((d, 128), F32).at[:, :N_EXPERTS].set(w_router)
    wh = wp.astype(BF16)
    wl = (wp - wh.astype(F32)).astype(BF16)
    return pl.pallas_call(
        _router_kernel,
        grid=(m // tm,),
        in_specs=[pl.BlockSpec((tm, d), lambda i: (i, 0)), pl.BlockSpec((d, 128), lambda i: (0, 0)),
                  pl.BlockSpec((d, 128), lambda i: (0, 0))],
        out_specs=pl.BlockSpec((tm, 128), lambda i: (i, 0)),
        out_shape=jax.ShapeDtypeStruct((m, 128), F32),
        compiler_params=_cparams(1),
        name="router",
    )(x, wh, wl)


def _dense_ffn_kernel(xb_ref, x_ref, wg_ref, wu_ref, wd_ref, g_ref, b_ref, o_ref, ob_ref, acc_ref):
    f = pl.program_id(1)
    xb = xb_ref[...]
    gte = jnp.dot(xb, wg_ref[0], preferred_element_type=F32)
    up = jnp.dot(xb, wu_ref[0], preferred_element_type=F32)
    h = (jax.nn.silu(gte) * up).astype(BF16)
    part = jnp.dot(h, wd_ref[0], preferred_element_type=F32)

    @pl.when(f == 0)
    def _():
        acc_ref[...] = part

    @pl.when(f > 0)
    def _():
        acc_ref[...] = acc_ref[...] + part

    @pl.when(f == pl.num_programs(1) - 1)
    def _():
        out = _layer_norm(ALPHA * x_ref[...] + acc_ref[...], g_ref[...], b_ref[...])
        o_ref[...] = out
        ob_ref[...] = out.astype(BF16)


def dense_ffn(x, xb, w_gate, w_up, w_down, g, b):
    m, d = x.shape
    ff = w_gate.shape[2]
    tm = min(FFN_TM, m)
    tf = FFN_TF
    row = lambda i, f: (i, 0)
    return pl.pallas_call(
        _dense_ffn_kernel,
        grid=(m // tm, ff // tf),
        in_specs=[
            pl.BlockSpec((tm, d), row),
            pl.BlockSpec((tm, d), row),
            pl.BlockSpec((1, d, tf), lambda i, f: (0, 0, f)),
            pl.BlockSpec((1, d, tf), lambda i, f: (0, 0, f)),
            pl.BlockSpec((1, tf, d), lambda i, f: (0, f, 0)),
            pl.BlockSpec((1, d), lambda i, f: (0, 0)),
            pl.BlockSpec((1, d), lambda i, f: (0, 0)),
        ],
        out_specs=[pl.BlockSpec((tm, d), row)] * 2,
        out_shape=[jax.ShapeDtypeStruct((m, d), F32), jax.ShapeDtypeStruct((m, d), BF16)],
        scratch_shapes=[pltpu.VMEM((tm, d), F32)],
        compiler_params=_cparams(2, BIG_VMEM_LIMIT),
        name="dense_ffn",
    )(xb, x, w_gate, w_up, w_down, g.reshape(1, d), b.reshape(1, d))


def moe_ffn(x, xb, w_router, w_gate, w_up, w_down, g, b):
    m = x.shape[0]
    tm = FFN_TM
    logits = router_logits(x, w_router, min(1024, m))[:, :N_EXPERTS]
    top_v, top_i = lax.top_k(logits, TOP_K)
    probs = jax.nn.softmax(top_v, axis=-1)
    e_flat = top_i.reshape(-1)
    onehot = (e_flat[:, None] == jnp.arange(N_EXPERTS)[None, :]).astype(jnp.int32)
    csum = jnp.cumsum(onehot, axis=0)
    counts = csum[-1]
    rank = jnp.take_along_axis(csum, e_flat[:, None], axis=1)[:, 0] - 1
    padded = ((counts + tm - 1) // tm) * tm
    ends = jnp.cumsum(padded)
    starts = ends - padded
    pos = starts[e_flat] + rank
    n_slots = TOP_K * m + N_EXPERTS * tm
    n_tiles = n_slots // tm
    tok = jnp.arange(TOP_K * m, dtype=jnp.int32) // TOP_K
    src = jnp.zeros((n_slots,), jnp.int32).at[pos].set(tok)
    tile_start = jnp.arange(n_tiles, dtype=jnp.int32) * tm
    tile_expert = jnp.minimum(jnp.searchsorted(ends, tile_start, side="right"), N_EXPERTS - 1).astype(jnp.int32)
    tile_valid = (tile_start < ends[-1]).astype(jnp.int32)
    xs = jnp.take(xb, src, axis=0)
    ys = grouped_swiglu(xs, w_gate, w_up, w_down,
                        tile_expert, tile_valid, tm, FFN_TF)
    pos2 = pos.reshape(m, TOP_K)
    y0 = jnp.take(ys, pos2[:, 0], axis=0)
    y1 = jnp.take(ys, pos2[:, 1], axis=0)
    return add_layer_norm(x, [y0, y1], probs, g, b, min(1024, m))


def _pack_in_weights(w):
    d = w.shape[0]
    o = 0
    q = w[:, o:o + 512]; o += 512
    kv = w[:, o:o + 768].reshape(d, 3, 2, NSA_KV_GROUPS, HEAD_DIM); o += 768
    ng = w[:, o:o + 24]; o += 24
    rest = w[:, o:o + 7 * 512]; o += 7 * 512
    mg = w[:, o:]
    kpack = jnp.concatenate([kv[:, 1, 0], kv[:, 2, 0]], axis=-1).reshape(d, 2 * NSA_KV_GROUPS * HEAD_DIM)
    kc = kv[:, 0, 0].reshape(d, NSA_KV_GROUPS * HEAD_DIM)
    vc = kv[:, 0, 1].reshape(d, NSA_KV_GROUPS * HEAD_DIM)
    w_nat = jnp.concatenate([mg, rest, kpack, kc, vc], axis=1).astype(BF16)
    vs = kv[:, 1, 1].reshape(d, NSA_KV_GROUPS * HEAD_DIM)
    vw = kv[:, 2, 1].reshape(d, NSA_KV_GROUPS * HEAD_DIM)
    ngp = jnp.pad(ng.reshape(d, NSA_KV_GROUPS, NSA_GROUP * 3), ((0, 0), (0, 0), (0, 4))).reshape(d, 32)
    w_t = jnp.concatenate([q, vs, vw, ngp], axis=1).T.astype(BF16)
    return w_nat, w_t


def token_mixer_ln(x, xb, w_in, cmp_pos, cmp_w1, cmp_b1, cmp_w2, cmp_b2, conv_w, gn_g, gn_b, w_br, w_out, ln_g, ln_b):
    b, t, d = x.shape
    m = b * t
    w_nat, w_t = _pack_in_weights(w_in)
    x2 = x.reshape(m, d)
    hn, proj_t = input_projection(xb, w_nat, w_t, min(2048, t), 1024)
    hn = hn.reshape(b, t, NAT_COLS)
    nch = t // CMP_STRIDE
    raw = hn[:, :, NAT_KC:NAT_KC + 2 * NSA_KV_GROUPS * HEAD_DIM].reshape(b, t, 2, NSA_KV_GROUPS, HEAD_DIM)
    chunks = raw.transpose(2, 0, 3, 1, 4).reshape(2, b, NSA_KV_GROUPS, nch, CMP_STRIDE * HEAD_DIM)
    cn, ct = nsa_compress_kv(chunks, cmp_pos, cmp_w1, cmp_b1, cmp_w2, cmp_b2)
    kc = cn[0].astype(BF16)
    vct = ct[1].astype(BF16)
    vt4 = proj_t[:, TR_VS:TR_NG, :].astype(BF16).reshape(b, 4, HEAD_DIM, t)
    ones_pad = jnp.zeros((b, 4, VT_ROWS - HEAD_DIM, t), BF16).at[:, :, 0].set(1.0)
    vt_pack = jnp.concatenate([vt4, ones_pad], axis=2).reshape(b, 4 * VT_ROWS, t)
    y_a = nsa_attention(proj_t, kc, vct, hn, vt_pack)
    y_c = retention_branch(hn, gn_g, gn_b, min(1024, t))
    return merge_branches(x2, y_a.reshape(m, -1), y_c.reshape(m, -1), hn.reshape(m, NAT_COLS), conv_w,
                          w_br, w_out, ln_g, ln_b, min(512, t), t)


def kernel(x, w_in, cmp_pos, cmp_w1, cmp_b1, cmp_w2, cmp_b2, conv_w, ret_gn_g, ret_gn_b, w_br, w_out,
           ln1_g, ln1_b, ln2_g, ln2_b, ffn_gate, ffn_up, ffn_down, moe_router, moe_gate, moe_up, moe_down):
    b, t, d = x.shape
    xb = x.astype(BF16)
    for l in range(DEPTH):
        x2, x2b = token_mixer_ln(x, xb, w_in[l], cmp_pos[l], cmp_w1[l], cmp_b1[l], cmp_w2[l], cmp_b2[l], conv_w[l],
                                 ret_gn_g[l], ret_gn_b[l], w_br[l], w_out[l], ln1_g[l], ln1_b[l])
        if l % 2 == 0:
            wg, wu, wd = ffn_weights_bf16(ffn_gate[:, None], ffn_up[:, None], ffn_down[:, None], l // 2)
            x2, x2b = dense_ffn(x2, x2b, wg, wu, wd, ln2_g[l], ln2_b[l])
        else:
            wg, wu, wd = ffn_weights_bf16(moe_gate, moe_up, moe_down, l // 2)
            x2, x2b = moe_ffn(x2, x2b, moe_router[l // 2], wg, wu, wd, ln2_g[l], ln2_b[l])
        x, xb = x2.reshape(b, t, d), x2b.reshape(b, t, d)
    return x
```

```python
import functools
import math

import jax
import jax.numpy as jnp
from jax import lax
from jax.experimental import pallas as pl
from jax.experimental.pallas import tpu as pltpu

F32 = jnp.float32
BF16 = jnp.bfloat16

D_MODEL = 1024
DEPTH = 4
BRANCH_WIDTH = 512
HEAD_DIM = 64
NSA_HEADS = 8
NSA_KV_GROUPS = 2
NSA_GROUP = 4
CMP_LEN = 32
CMP_STRIDE = 16
SLC_BLOCK = 64
SLC_TOPN = 16
WINDOW = 512
CMP_HIDDEN = 256
Q_BLOCK = 128
CONV_K = 3
CONV_HALO = 16
RET_HEADS = 4
RET_HEAD_DIM = 128
RET_CHUNK = 256
ROPE_BASE = 10000.0
D_FF = 3584
N_EXPERTS = 8
TOP_K = 2
LN_EPS = 1e-5
ALPHA = (2.0 * DEPTH) ** 0.25

NAT_MG = 0
NAT_CB = 3072
NAT_CC = 3584
NAT_CH = 4096
NAT_RQ = 4608
NAT_RK = 5120
NAT_RV = 5632
NAT_RG = 6144
NAT_KP = 6656
NAT_KC = 6912
NAT_VC = 7040
NAT_COLS = 7168
TR_Q = 0
TR_VS = 512
TR_VW = 640
TR_NG = 768
TR_ROWS = 800

NEG = -1e30
SEL_TK = 1024
LOG2E = 1.4426950408889634
NSA_QSUB = 2
CMP_CLASS_ROWS = 256
PG_PAD = 8
PICKED = -3e38
VT_ROWS = 80
FFN_TM = 512
FFN_TF = 1792
CAST_STEPS = 16
VMEM_LIMIT = 48 * 1024 * 1024
BIG_VMEM_LIMIT = 56 * 1024 * 1024


def _cparams(n_axes, vmem=VMEM_LIMIT):
    return pltpu.CompilerParams(dimension_semantics=("arbitrary",) * n_axes, vmem_limit_bytes=vmem)


def _proj_kernel(x_ref, w_ref, wt_ref, o_ref, ot_ref):
    @pl.when(pl.program_id(1) == 0)
    def _():
        ot_ref[0] = lax.dot_general(wt_ref[...], x_ref[...], (((1,), (1,)), ((), ())), preferred_element_type=F32)

    o_ref[...] = jnp.dot(x_ref[...], w_ref[...], preferred_element_type=F32).astype(o_ref.dtype)


def input_projection(x, w_nat, w_t, tm, tn):
    b, t, k = x.shape
    m = b * t
    n = w_nat.shape[1]
    r = w_t.shape[0]
    tps = t // tm
    return pl.pallas_call(
        _proj_kernel,
        grid=(m // tm, n // tn),
        in_specs=[pl.BlockSpec((tm, k), lambda i, j: (i, 0)), pl.BlockSpec((k, tn), lambda i, j: (0, j)),
                  pl.BlockSpec((r, k), lambda i, j: (0, 0))],
        out_specs=[pl.BlockSpec((tm, tn), lambda i, j: (i, j)),
                   pl.BlockSpec((1, r, tm), lambda i, j: (i // tps, 0, i % tps))],
        out_shape=[jax.ShapeDtypeStruct((m, n), BF16), jax.ShapeDtypeStruct((b, r, t), F32)],
        compiler_params=_cparams(2, BIG_VMEM_LIMIT),
        name="input_projection",
    )(x.reshape(m, k), w_nat, w_t)


def _cmp_kernel(c_ref, pos_ref, w1_ref, b1_ref, w2_ref, w2t_ref, b2_ref, b2t_ref, on_ref, ot_ref):
    half = CMP_STRIDE * HEAD_DIM
    c = c_ref[0, 0, 0].astype(BF16)
    w1 = w1_ref[0]
    a = jnp.dot(c, w1[:half], preferred_element_type=F32)
    bm = jnp.dot(c, w1[half:], preferred_element_type=F32)
    nch = a.shape[0]
    bm = pltpu.roll(bm, nch - 1, 0)
    posb = jnp.dot(pos_ref[0].astype(BF16), w1, preferred_element_type=F32)[0:1]
    hid = jax.nn.gelu(a + bm + posb + b1_ref[0])
    hb = hid.astype(BF16)
    on_ref[0, 0, 0] = jnp.dot(hb, w2_ref[0], preferred_element_type=F32) + b2_ref[0]
    ot_ref[0, 0, 0] = lax.dot_general(w2t_ref[0], hb, (((1,), (1,)), ((), ())),
                                      preferred_element_type=F32) + b2t_ref[0]


def nsa_compress_kv(chunks, pos, w1, b1, w2, b2):
    _, b, g, nch, cw = chunks.shape
    pos_flat = jnp.broadcast_to(pos.reshape(2, 1, CMP_LEN * HEAD_DIM), (2, 8, CMP_LEN * HEAD_DIM))
    w1b = w1.astype(BF16)
    w2b = w2.astype(BF16)
    w2t = jnp.swapaxes(w2, 1, 2).astype(BF16)
    b1r = b1.reshape(2, 1, CMP_HIDDEN)
    b2r = b2.reshape(2, 1, HEAD_DIM)
    b2t = b2.reshape(2, HEAD_DIM, 1)
    sel = lambda kv, i, j: (kv, 0, 0)
    return pl.pallas_call(
        _cmp_kernel,
        grid=(2, b, g),
        in_specs=[
            pl.BlockSpec((1, 1, 1, nch, cw), lambda kv, i, j: (kv, i, j, 0, 0)),
            pl.BlockSpec((1, 8, CMP_LEN * HEAD_DIM), sel),
            pl.BlockSpec((1, CMP_LEN * HEAD_DIM, CMP_HIDDEN), sel),
            pl.BlockSpec((1, 1, CMP_HIDDEN), sel),
            pl.BlockSpec((1, CMP_HIDDEN, HEAD_DIM), sel),
            pl.BlockSpec((1, HEAD_DIM, CMP_HIDDEN), sel),
            pl.BlockSpec((1, 1, HEAD_DIM), sel),
            pl.BlockSpec((1, HEAD_DIM, 1), sel),
        ],
        out_specs=[
            pl.BlockSpec((1, 1, 1, nch, HEAD_DIM), lambda kv, i, j: (kv, i, j, 0, 0)),
            pl.BlockSpec((1, 1, 1, HEAD_DIM, nch), lambda kv, i, j: (kv, i, j, 0, 0)),
        ],
        out_shape=[
            jax.ShapeDtypeStruct((2, b, g, nch, HEAD_DIM), F32),
            jax.ShapeDtypeStruct((2, b, g, HEAD_DIM, nch), F32),
        ],
        compiler_params=_cparams(3),
        name="nsa_compress",
    )(chunks, pos_flat, w1b, b1r, w2b, w2t, b2r, b2t)


def _accumulate(s, mt, vt, m, acc):
    m_new = jnp.maximum(m, mt)
    alpha = jnp.exp2(m - m_new)
    p = jnp.exp2(s - m_new).astype(BF16)
    return m_new, alpha * acc + jnp.dot(vt, p, preferred_element_type=F32)


def _nsa_kernel(qt_ref, ng_ref, kc_ref, vct_ref, kp_ref, vst_ref, vwt_ref, e_ref, tri_ref, wpat_ref,
                cpat_ref, o_ref, bias_ref, w_ref, sbuf_ref, oc_ref, pg_ref):
    nq = Q_BLOCK
    nsub = qt_ref.shape[2] // nq
    ncol = NSA_GROUP * nq
    n_sel = bias_ref.shape[1]
    ncp = kc_ref.shape[2]
    kw = 2 * HEAD_DIM
    bpt = SEL_TK // SLC_BLOCK
    n_tiles_total = kp_ref.shape[1] // SEL_TK
    qb_first = pl.program_id(2) * nsub
    wlen = WINDOW + nq

    def prepare(u):
        qb = qb_first + u
        q0 = pl.multiple_of(qb * nq, nq)
        qt = qt_ref[0, :, u * nq:(u + 1) * nq] * (HEAD_DIM ** -0.5 * LOG2E)
        qs = jnp.concatenate([qt[r * HEAD_DIM:(r + 1) * HEAD_DIM, :] for r in range(NSA_GROUP)], axis=1)
        zq = jnp.zeros_like(qs)
        q_c = qs.astype(BF16)
        q_slc = jnp.concatenate([qs, zq], axis=0).astype(BF16)
        q_win = jnp.concatenate([zq, qs], axis=0).astype(BF16)
        tq = q0 + (lax.broadcasted_iota(jnp.int32, (1, ncol), 1) & (nq - 1))

        coff = pl.multiple_of(ncp - qb * (nq // CMP_STRIDE), nq // CMP_STRIDE)
        tq1 = q0 + lax.broadcasted_iota(jnp.int32, (1, nq), 1)
        cur = jnp.right_shift(tq1, 6).astype(F32)
        first_diag = (2 * qb).astype(F32)
        pg_ref[u, 0:PG_PAD, :] = jnp.zeros((PG_PAD, nq), F32)
        crow = min(CMP_CLASS_ROWS, ncp)
        cls = qb // (crow // (nq // CMP_STRIDE))
        for c in range(ncp // crow):
            nc_c = crow * (c + 1)
            ns_c = nc_c * CMP_STRIDE // SLC_BLOCK

            @pl.when(cls == c)
            def _(nc_c=nc_c, ns_c=ns_c):
                s_c = (jnp.dot(kc_ref[0, 0, 0:nc_c, :], q_c, preferred_element_type=F32)
                       + jnp.concatenate([cpat_ref[pl.ds(coff, nc_c), :]] * NSA_GROUP, axis=1))
                m_c = jnp.max(s_c, axis=0, keepdims=True)
                e_c = jnp.exp2(s_c - m_c)
                l_c = jnp.sum(e_c, axis=0, keepdims=True)
                p_c = e_c * jnp.where(tq >= CMP_LEN - 1, 1.0 / l_c, 0.0)
                oc_ref[u] = jnp.dot(vct_ref[0, 0, :, 0:nc_c], p_c.astype(BF16), preferred_element_type=F32)

                p_grp = p_c[:, 0:nq]
                for r in range(1, NSA_GROUP):
                    p_grp = p_grp + p_c[:, r * nq:(r + 1) * nq]
                pg_ref[u, PG_PAD:PG_PAD + nc_c, :] = p_grp
                ratio = SLC_BLOCK // CMP_STRIDE
                taps = [pg_ref[u, pl.ds(PG_PAD - 1 + k, ns_c, stride=ratio), :] for k in range(ratio + 1)]
                score = taps[0] + 2.0 * (taps[1] + taps[2] + taps[3]) + taps[4]
                j_io = lax.broadcasted_iota(jnp.int32, (ns_c, nq), 0).astype(F32)
                score = jnp.where(j_io <= cur, score, -1.0)
                score = jnp.where(j_io == 0.0, PICKED, score)
                score = jnp.where(j_io == cur, PICKED, score)
                score = jnp.where(j_io == cur - 1.0, PICKED, score)
                for _ in range(SLC_TOPN - 3):
                    mx = jnp.max(score, axis=0, keepdims=True)
                    first = jnp.min(jnp.where(score == mx, j_io, float(ns_c)), axis=0, keepdims=True)
                    score = jnp.where(j_io == first, PICKED, score)
                bias = jnp.where(j_io < first_diag, jnp.where(score == PICKED, 0.0, NEG), NEG)
                bias_ref[u, 0:ns_c, :] = bias.astype(BF16)
                if ns_c < n_sel:
                    bias_ref[u, ns_c:, :] = jnp.full((n_sel - ns_c, nq), NEG, BF16)

        o_c = oc_ref[u]

        wstart = pl.multiple_of(jnp.maximum(q0 - WINDOW, 0), nq)
        woff = pl.multiple_of(jnp.maximum(WINDOW - q0, 0), nq)
        s_w = jnp.dot(kp_ref[0, pl.ds(wstart, wlen), :], q_win, preferred_element_type=F32)
        s_w = s_w + jnp.concatenate([wpat_ref[pl.ds(woff, wlen), :]] * NSA_GROUP, axis=1)
        p_w = jnp.exp2(s_w - jnp.max(s_w, axis=0, keepdims=True)).astype(BF16)
        acc_w = jnp.dot(vwt_ref[0, :, pl.ds(wstart, wlen)], p_w, preferred_element_type=F32)
        o_w = acc_w[0:HEAD_DIM] / acc_w[HEAD_DIM:HEAD_DIM + 1]

        for slot in range(2):
            w_ref[slot, u, 0:kw, :] = q_slc
            w_ref[slot, u, kw:, :] = jnp.zeros((w_ref.shape[2] - kw, ncol), BF16)
        return q0, q_slc, o_c, o_w

    prepared = [prepare(u) for u in range(nsub)]

    def tile_start(j):
        jc = jnp.minimum(j, n_tiles_total - 1)
        return jc, pl.multiple_of(jc * SEL_TK, SEL_TK)

    def produce(j, slot):
        jc, k0 = tile_start(j)
        lhs = jnp.concatenate([kp_ref[0, pl.ds(k0, SEL_TK), :], e_ref[...]], axis=1)
        mts = []
        for u in range(nsub):
            b16 = bias_ref[u, pl.ds(pl.multiple_of(jc * bpt, bpt), bpt), :]
            w_ref[slot, u, kw:kw + bpt, :] = jnp.concatenate([b16] * NSA_GROUP, axis=1)
            s = jnp.dot(lhs, w_ref[slot, u], preferred_element_type=F32)
            sbuf_ref[slot, u] = s
            mts.append(jnp.max(s, axis=0, keepdims=True))
        return tuple(mts)

    def consume(j, slot, mts, state):
        _, k0 = tile_start(j)
        vt = vst_ref[0, :, pl.ds(k0, SEL_TK)]
        return tuple(_accumulate(sbuf_ref[slot, u], mts[u], vt, *state[u]) for u in range(nsub))

    def pair(i, carry, last):
        mt_a, state = carry
        mt_b = produce(2 * i + 1, 1)
        state = consume(2 * i, 0, mt_a, state)
        if last:
            return consume(2 * i + 1, 1, mt_b, state)
        mt_a = produce(2 * i + 2, 0)
        state = consume(2 * i + 1, 1, mt_b, state)
        return mt_a, state

    n_main = ((qb_first + nsub - 1) * nq + (SEL_TK - 1)) // SEL_TK
    n_pairs = jnp.maximum((n_main + 1) // 2, 1)
    state = tuple((jnp.full((1, ncol), NEG, F32), jnp.zeros((vst_ref.shape[1], ncol), F32)) for _ in range(nsub))
    carry = lax.fori_loop(0, n_pairs - 1, lambda i, c: pair(i, c, False), (produce(0, 0), state))
    state = pair(n_pairs - 1, carry, True)

    for u in range(nsub):
        q0, q_slc, o_c, o_w = prepared[u]
        s_d = jnp.dot(kp_ref[0, pl.ds(q0, nq), :], q_slc, preferred_element_type=F32) + tri_ref[...]
        _, acc_s = _accumulate(s_d, jnp.max(s_d, axis=0, keepdims=True), vst_ref[0, :, pl.ds(q0, nq)], *state[u])
        o_s = acc_s[0:HEAD_DIM] / acc_s[HEAD_DIM:HEAD_DIM + 1]
        gts = jax.nn.sigmoid(ng_ref[0, :, u * nq:(u + 1) * nq])
        rows = []
        for r in range(NSA_GROUP):
            sl = slice(r * nq, (r + 1) * nq)
            rows.append(gts[3 * r:3 * r + 1, :] * o_c[:, sl] + gts[3 * r + 1:3 * r + 2, :] * o_s[:, sl]
                        + gts[3 * r + 2:3 * r + 3, :] * o_w[:, sl])
        y_t = jnp.concatenate(rows, axis=0)
        o_ref[0, u * nq:(u + 1) * nq, :] = y_t.T.astype(o_ref.dtype)


def _nsa_constants(ncp):
    nq = Q_BLOCK
    ql = jnp.arange(nq)[None, :]
    r = jnp.arange(SEL_TK)[:, None]
    e = (r // SLC_BLOCK == jnp.arange(2 * HEAD_DIM)[None, :]).astype(BF16)
    rd = jnp.arange(nq)[:, None]
    tri = jnp.tile(jnp.where(rd <= ql, 0.0, NEG).astype(F32), (1, NSA_GROUP))
    rw = jnp.arange(2 * WINDOW + nq)[:, None]
    wpat = jnp.where((rw > ql) & (rw <= ql + WINDOW), 0.0, NEG).astype(F32)
    rc = jnp.arange(2 * ncp)[:, None] - ncp
    cpat = jnp.where(CMP_STRIDE * rc + (CMP_LEN - 1) <= ql, 0.0, NEG).astype(F32)
    return e, tri, wpat, cpat


def nsa_attention(proj_t, kc, vct, hn, vt_pack):
    b, _, t = proj_t.shape
    ncp = kc.shape[2]
    n_sel = t // SLC_BLOCK
    gq = NSA_GROUP * HEAD_DIM
    qstep = NSA_QSUB * Q_BLOCK
    e, tri, wpat, cpat = _nsa_constants(ncp)
    const2 = lambda i, g, q: (0, 0)
    return pl.pallas_call(
        _nsa_kernel,
        grid=(b, NSA_KV_GROUPS, t // qstep),
        in_specs=[
            pl.BlockSpec((1, gq, qstep), lambda i, g, q: (i, g, q)),
            pl.BlockSpec((1, 16, qstep), lambda i, g, q: (i, TR_NG // 16 + g, q)),
            pl.BlockSpec((1, 1, ncp, HEAD_DIM), lambda i, g, q: (i, g, 0, 0)),
            pl.BlockSpec((1, 1, HEAD_DIM, ncp), lambda i, g, q: (i, g, 0, 0)),
            pl.BlockSpec((1, t, 2 * HEAD_DIM), lambda i, g, q: (i, 0, NAT_KP // (2 * HEAD_DIM) + g)),
            pl.BlockSpec((1, VT_ROWS, t), lambda i, g, q: (i, g, 0)),
            pl.BlockSpec((1, VT_ROWS, t), lambda i, g, q: (i, NSA_KV_GROUPS + g, 0)),
            pl.BlockSpec(e.shape, const2),
            pl.BlockSpec(tri.shape, const2),
            pl.BlockSpec(wpat.shape, const2),
            pl.BlockSpec(cpat.shape, const2),
        ],
        out_specs=pl.BlockSpec((1, qstep, gq), lambda i, g, q: (i, q, g)),
        out_shape=jax.ShapeDtypeStruct((b, t, NSA_HEADS * HEAD_DIM), BF16),
        scratch_shapes=[pltpu.VMEM((NSA_QSUB, n_sel, Q_BLOCK), BF16),
                        pltpu.VMEM((2, NSA_QSUB, 4 * HEAD_DIM, NSA_GROUP * Q_BLOCK), BF16),
                        pltpu.VMEM((2, NSA_QSUB, SEL_TK, NSA_GROUP * Q_BLOCK), F32),
                        pltpu.VMEM((NSA_QSUB, HEAD_DIM, NSA_GROUP * Q_BLOCK), F32),
                        pltpu.VMEM((NSA_QSUB, PG_PAD + ncp, Q_BLOCK), F32)],
        compiler_params=_cparams(3),
        name="nsa_attention",
    )(proj_t, proj_t, kc, vct, hn, vt_pack, vt_pack, e, tri, wpat, cpat)


def _ret_kernel(rq_ref, rk_ref, rv_ref, rg_ref, cos_ref, sin_ref, dec_ref, qd_ref, kd_ref, cd_ref,
                gg_ref, gb_ref, o_ref, st_ref):
    c = RET_CHUNK
    hd = RET_HEAD_DIM

    @pl.when(pl.program_id(1) == 0)
    def _():
        st_ref[...] = jnp.zeros_like(st_ref)

    n_chunks = rq_ref.shape[1] // c

    def chunk(ci, _):
        r0 = pl.multiple_of(ci * c, c)
        cos = cos_ref[pl.ds(r0, c), :]
        sin = sin_ref[pl.ds(r0, c), :]
        for h in range(RET_HEADS):
            hs = slice(h * hd, (h + 1) * hd)
            q = rq_ref[0, pl.ds(r0, c), hs].astype(F32)
            k = rk_ref[0, pl.ds(r0, c), hs].astype(F32)
            v = rv_ref[0, pl.ds(r0, c), hs]
            q = q * cos + pltpu.roll(q, hd // 2, 1) * sin
            k = (k * cos + pltpu.roll(k, hd // 2, 1) * sin) * (hd ** -0.5)
            qb = q.astype(BF16)
            kb = k.astype(BF16)
            vb = v.astype(BF16)
            inner = lax.dot_general(qb, kb, (((1,), (1,)), ((), ())), preferred_element_type=F32) * dec_ref[h]
            st = st_ref[h]
            o = (jnp.dot(inner.astype(BF16), vb, preferred_element_type=F32)
                 + jnp.dot(qb, st.astype(BF16), preferred_element_type=F32) * qd_ref[h])
            kdt = (k * kd_ref[h]).T.astype(BF16)
            st_ref[h] = st * cd_ref[h] + jnp.dot(kdt, vb, preferred_element_type=F32)
            mu = jnp.mean(o, axis=-1, keepdims=True)
            d = o - mu
            var = jnp.mean(d * d, axis=-1, keepdims=True)
            y = d * lax.rsqrt(var + LN_EPS) * gg_ref[:, hs] + gb_ref[:, hs]
            o_ref[0, pl.ds(r0, c), hs] = (jax.nn.silu(rg_ref[0, pl.ds(r0, c), hs].astype(F32)) * y).astype(o_ref.dtype)
        return 0

    lax.fori_loop(0, n_chunks, chunk, 0)


def retention_branch(hn, gn_g, gn_b, tr):
    b, t, _ = hn.shape
    w = BRANCH_WIDTH
    hd = RET_HEAD_DIM
    c = RET_CHUNK
    inv = ROPE_BASE ** (-jnp.arange(0, hd, 2, dtype=F32) / hd)
    ang = jnp.arange(t, dtype=F32)[:, None] * inv[None, :]
    cos2 = jnp.concatenate([jnp.cos(ang), jnp.cos(ang)], axis=1)
    sin2 = jnp.concatenate([-jnp.sin(ang), jnp.sin(ang)], axis=1)
    log_g = jnp.log(1.0 - 2.0 ** (-5.0 - jnp.arange(RET_HEADS, dtype=F32)))
    idx = jnp.arange(c, dtype=F32)
    diff = idx[:, None] - idx[None, :]
    decay_in = jnp.where(diff >= 0, jnp.exp(log_g[:, None, None] * jnp.maximum(diff, 0.0)), 0.0)
    ones = jnp.ones((RET_HEADS, c, hd), F32)
    q_dec = jnp.exp(log_g[:, None] * (idx + 1.0))[:, :, None] * ones
    k_dec = jnp.exp(log_g[:, None] * (c - 1.0 - idx))[:, :, None] * ones
    c_dec = jnp.exp(log_g * c)[:, None, None] * jnp.ones((RET_HEADS, hd, hd), F32)
    col = lambda cc: (lambda i, j: (i, j, cc // w))
    full3 = lambda i, j: (0, 0, 0)
    return pl.pallas_call(
        _ret_kernel,
        grid=(b, t // tr),
        in_specs=[
            pl.BlockSpec((1, tr, w), col(NAT_RQ)),
            pl.BlockSpec((1, tr, w), col(NAT_RK)),
            pl.BlockSpec((1, tr, w), col(NAT_RV)),
            pl.BlockSpec((1, tr, w), col(NAT_RG)),
            pl.BlockSpec((tr, hd), lambda i, j: (j, 0)),
            pl.BlockSpec((tr, hd), lambda i, j: (j, 0)),
            pl.BlockSpec((RET_HEADS, c, c), full3),
            pl.BlockSpec((RET_HEADS, c, hd), full3),
            pl.BlockSpec((RET_HEADS, c, hd), full3),
            pl.BlockSpec((RET_HEADS, hd, hd), full3),
            pl.BlockSpec((1, w), lambda i, j: (0, 0)),
            pl.BlockSpec((1, w), lambda i, j: (0, 0)),
        ],
        out_specs=pl.BlockSpec((1, tr, w), lambda i, j: (i, j, 0)),
        out_shape=jax.ShapeDtypeStruct((b, t, w), BF16),
        scratch_shapes=[pltpu.VMEM((RET_HEADS, hd, hd), F32)],
        compiler_params=_cparams(2),
        name="retention",
    )(hn, hn, hn, hn, cos2, sin2, decay_in, q_dec, k_dec, c_dec, gn_g.reshape(1, w), gn_b.reshape(1, w))


def _layer_norm(z, g, b):
    mu = jnp.mean(z, axis=-1, keepdims=True)
    d = z - mu
    var = jnp.mean(d * d, axis=-1, keepdims=True)
    return d * lax.rsqrt(var + LN_EPS) * g + b


def _merge_kernel(tiles_per_seq, x_ref, ya_ref, cb_ref, cc_ref, ch_ref, pc_ref, ph_ref, cw_ref, yc_ref, mg_ref,
                  wbr_ref, wout_ref, g_ref, b_ref, o_ref, ob_ref):
    first = (pl.program_id(0) % tiles_per_seq) == 0
    u = cc_ref[...].astype(F32) * ch_ref[...].astype(F32)
    prev = pc_ref[...].astype(F32) * ph_ref[...].astype(F32)
    prev = jnp.where(first, 0.0, prev)
    row = lax.broadcasted_iota(jnp.int32, u.shape, 0)
    p1 = prev[CONV_HALO - 1:CONV_HALO, :]
    p2 = prev[CONV_HALO - 2:CONV_HALO - 1, :]
    u1 = jnp.where(row == 0, p1, pltpu.roll(u, 1, 0))
    u2 = jnp.where(row == 0, p2, jnp.where(row == 1, p1, pltpu.roll(u, 2, 0)))
    cw = cw_ref[...]
    y_b = cb_ref[...].astype(F32) * (cw[0:1, :] * u2 + cw[1:2, :] * u1 + cw[2:3, :] * u)

    mix = None
    for c, y in enumerate((ya_ref[...], y_b.astype(BF16), yc_ref[...])):
        proj = jnp.dot(y, wbr_ref[c], preferred_element_type=F32)
        gate = jax.nn.sigmoid(mg_ref[:, c * D_MODEL:(c + 1) * D_MODEL].astype(F32))
        mix = gate * proj if mix is None else mix + gate * proj
    m = jnp.dot(mix.astype(BF16), wout_ref[...], preferred_element_type=F32)
    out = _layer_norm(ALPHA * x_ref[...] + m, g_ref[...], b_ref[...])
    o_ref[...] = out
    ob_ref[...] = out.astype(BF16)


def merge_branches(x, ya, yc, hn, conv_w, w_br, w_out, g, b, tm, t):
    m = x.shape[0]
    d = D_MODEL
    w = BRANCH_WIDTH
    row = lambda i: (i, 0)
    col = lambda c: (lambda i: (i, c // w))
    hal = lambda c: (lambda i: (jnp.maximum(i * (tm // CONV_HALO) - 1, 0), c // w))
    wpad = jnp.zeros((8, w), F32).at[:CONV_K].set(conv_w)
    return pl.pallas_call(
        functools.partial(_merge_kernel, t // tm),
        grid=(m // tm,),
        in_specs=[
            pl.BlockSpec((tm, d), row),
            pl.BlockSpec((tm, w), row),
            pl.BlockSpec((tm, w), col(NAT_CB)),
            pl.BlockSpec((tm, w), col(NAT_CC)),
            pl.BlockSpec((tm, w), col(NAT_CH)),
            pl.BlockSpec((CONV_HALO, w), hal(NAT_CC)),
            pl.BlockSpec((CONV_HALO, w), hal(NAT_CH)),
            pl.BlockSpec((8, w), lambda i: (0, 0)),
            pl.BlockSpec((tm, w), row),
            pl.BlockSpec((tm, 3 * d), row),
            pl.BlockSpec((3, w, d), lambda i: (0, 0, 0)),
            pl.BlockSpec((d, d), lambda i: (0, 0)),
            pl.BlockSpec((1, d), lambda i: (0, 0)),
            pl.BlockSpec((1, d), lambda i: (0, 0)),
        ],
        out_specs=[pl.BlockSpec((tm, d), row), pl.BlockSpec((tm, d), row)],
        out_shape=[jax.ShapeDtypeStruct((m, d), F32), jax.ShapeDtypeStruct((m, d), BF16)],
        compiler_params=_cparams(1),
        name="merge",
    )(x, ya, hn, hn, hn, hn, hn, wpad, yc, hn, w_br.astype(BF16), w_out.astype(BF16), g.reshape(1, d),
      b.reshape(1, d))


def _cast3_kernel(g_ref, u_ref, d_ref, og_ref, ou_ref, od_ref):
    og_ref[...] = g_ref[0].astype(og_ref.dtype)
    ou_ref[...] = u_ref[0].astype(ou_ref.dtype)
    od_ref[...] = d_ref[0].astype(od_ref.dtype)


def ffn_weights_bf16(w_gate, w_up, w_down, layer):
    _, e, d, ff = w_gate.shape
    rg, rd = d // CAST_STEPS, ff // CAST_STEPS
    up_spec = pl.BlockSpec((1, 1, rg, ff), lambda i, j: (layer, i, j, 0))
    return pl.pallas_call(
        _cast3_kernel,
        grid=(e, CAST_STEPS),
        in_specs=[up_spec, up_spec, pl.BlockSpec((1, 1, rd, d), lambda i, j: (layer, i, j, 0))],
        out_specs=[pl.BlockSpec((1, rg, ff), lambda i, j: (i, j, 0)), pl.BlockSpec((1, rg, ff), lambda i, j: (i, j, 0)),
                   pl.BlockSpec((1, rd, d), lambda i, j: (i, j, 0))],
        out_shape=[jax.ShapeDtypeStruct((e, d, ff), BF16), jax.ShapeDtypeStruct((e, d, ff), BF16),
                   jax.ShapeDtypeStruct((e, ff, d), BF16)],
        compiler_params=_cparams(2),
        name="cast_bf16",
    )(w_gate, w_up, w_down)


def _mlp_kernel(te_ref, tv_ref, x_ref, wg_ref, wu_ref, wd_ref, o_ref, acc_ref):
    i = pl.program_id(0)
    f = pl.program_id(1)

    @pl.when(tv_ref[i] > 0)
    def _():
        x = x_ref[...]
        gte = jnp.dot(x, wg_ref[0], preferred_element_type=F32)
        up = jnp.dot(x, wu_ref[0], preferred_element_type=F32)
        h = (jax.nn.silu(gte) * up).astype(BF16)
        part = jnp.dot(h, wd_ref[0], preferred_element_type=F32)

        @pl.when(f == 0)
        def _():
            acc_ref[...] = part

        @pl.when(f > 0)
        def _():
            acc_ref[...] = acc_ref[...] + part

        @pl.when(f == pl.num_programs(1) - 1)
        def _():
            o_ref[...] = acc_ref[...].astype(o_ref.dtype)

    @pl.when((tv_ref[i] == 0) & (f == 0))
    def _():
        o_ref[...] = jnp.zeros_like(o_ref)


def grouped_swiglu(xs, w_gate, w_up, w_down, tile_expert, tile_valid, tm, tf):
    n, d = xs.shape
    ff = w_gate.shape[2]
    grid_spec = pltpu.PrefetchScalarGridSpec(
        num_scalar_prefetch=2,
        grid=(n // tm, ff // tf),
        in_specs=[
            pl.BlockSpec((tm, d), lambda i, f, te, tv: (i, 0)),
            pl.BlockSpec((1, d, tf), lambda i, f, te, tv: (te[i], 0, f)),
            pl.BlockSpec((1, d, tf), lambda i, f, te, tv: (te[i], 0, f)),
            pl.BlockSpec((1, tf, d), lambda i, f, te, tv: (te[i], f, 0)),
        ],
        out_specs=pl.BlockSpec((tm, d), lambda i, f, te, tv: (i, 0)),
        scratch_shapes=[pltpu.VMEM((tm, d), F32)],
    )
    return pl.pallas_call(
        _mlp_kernel,
        grid_spec=grid_spec,
        out_shape=jax.ShapeDtypeStruct((n, d), BF16),
        compiler_params=_cparams(2, BIG_VMEM_LIMIT),
        name="grouped_swiglu",
    )(tile_expert, tile_valid, xs, w_gate, w_up, w_down)


def _add_ln_kernel(*refs):
    x_ref, sc_ref, *adds, g_ref, b_ref, o_ref, ob_ref = refs
    z = ALPHA * x_ref[...]
    for k, a in enumerate(adds):
        z = z + sc_ref[:, k:k + 1] * a[...].astype(F32)
    out = _layer_norm(z, g_ref[...], b_ref[...])
    o_ref[...] = out
    ob_ref[...] = out.astype(BF16)


def add_layer_norm(x, adds, scales, g, b, tm):
    m, d = x.shape
    row = lambda i: (i, 0)
    return pl.pallas_call(
        _add_ln_kernel,
        grid=(m // tm,),
        in_specs=([pl.BlockSpec((tm, d), row), pl.BlockSpec((tm, len(adds)), row)]
                  + [pl.BlockSpec((tm, d), row)] * len(adds) + [pl.BlockSpec((1, d), lambda i: (0, 0))] * 2),
        out_specs=[pl.BlockSpec((tm, d), row)] * 2,
        out_shape=[jax.ShapeDtypeStruct((m, d), F32), jax.ShapeDtypeStruct((m, d), BF16)],
        compiler_params=_cparams(1),
        name="add_layer_norm",
    )(x, scales, *adds, g.reshape(1, d), b.reshape(1, d))


def _router_kernel(x_ref, wh_ref, wl_ref, o_ref):
    x = x_ref[...]
    xh = x.astype(BF16)
    xl = (x - xh.astype(F32)).astype(BF16)
    wh = wh_ref[...]
    o_ref[...] = (jnp.dot(xh, wh, preferred_element_type=F32) + jnp.dot(xl, wh, preferred_element_type=F32)
                  + jnp.dot(xh, wl_ref[...], preferred_element_type=F32))


def router_logits(x, w_router, tm):
    m, d = x.shape
    wp = jnp.zeros((d, 128), F32).at[:, :N_EXPERTS].set(w_router)
    wh = wp.astype(BF16)
    wl = (wp - wh.astype(F32)).astype(BF16)
    return pl.pallas_call(
        _router_kernel,
        grid=(m // tm,),
        in_specs=[pl.BlockSpec((tm, d), lambda i: (i, 0)), pl.BlockSpec((d, 128), lambda i: (0, 0)),
                  pl.BlockSpec((d, 128), lambda i: (0, 0))],
        out_specs=pl.BlockSpec((tm, 128), lambda i: (i, 0)),
        out_shape=jax.ShapeDtypeStruct((m, 128), F32),
        compiler_params=_cparams(1),
        name="router",
    )(x, wh, wl)


def _dense_ffn_kernel(xb_ref, x_ref, wg_ref, wu_ref, wd_ref, g_ref, b_ref, o_ref, ob_ref, acc_ref):
    f = pl.program_id(1)
    xb = xb_ref[...]
    gte = jnp.dot(xb, wg_ref[0], preferred_element_type=F32)
    up = jnp.dot(xb, wu_ref[0], preferred_element_type=F32)
    h = (jax.nn.silu(gte) * up).astype(BF16)
    part = jnp.dot(h, wd_ref[0], preferred_element_type=F32)

    @pl.when(f == 0)
    def _():
        acc_ref[...] = part

    @pl.when(f > 0)
    def _():
        acc_ref[...] = acc_ref[...] + part

    @pl.when(f == pl.num_programs(1) - 1)
    def _():
        out = _layer_norm(ALPHA * x_ref[...] + acc_ref[...], g_ref[...], b_ref[...])
        o_ref[...] = out
        ob_ref[...] = out.astype(BF16)


def dense_ffn(x, xb, w_gate, w_up, w_down, g, b):
    m, d = x.shape
    ff = w_gate.shape[2]
    tm = min(FFN_TM, m)
    tf = FFN_TF
    row = lambda i, f: (i, 0)
    return pl.pallas_call(
        _dense_ffn_kernel,
        grid=(m // tm, ff // tf),
        in_specs=[
            pl.BlockSpec((tm, d), row),
            pl.BlockSpec((tm, d), row),
            pl.BlockSpec((1, d, tf), lambda i, f: (0, 0, f)),
            pl.BlockSpec((1, d, tf), lambda i, f: (0, 0, f)),
            pl.BlockSpec((1, tf, d), lambda i, f: (0, f, 0)),
            pl.BlockSpec((1, d), lambda i, f: (0, 0)),
            pl.BlockSpec((1, d), lambda i, f: (0, 0)),
        ],
        out_specs=[pl.BlockSpec((tm, d), row)] * 2,
        out_shape=[jax.ShapeDtypeStruct((m, d), F32), jax.ShapeDtypeStruct((m, d), BF16)],
        scratch_shapes=[pltpu.VMEM((tm, d), F32)],
        compiler_params=_cparams(2, BIG_VMEM_LIMIT),
        name="dense_ffn",
    )(xb, x, w_gate, w_up, w_down, g.reshape(1, d), b.reshape(1, d))


def moe_ffn(x, xb, w_router, w_gate, w_up, w_down, g, b):
    m = x.shape[0]
    tm = FFN_TM
    logits = router_logits(x, w_router, min(1024, m))[:, :N_EXPERTS]
    top_v, top_i = lax.top_k(logits, TOP_K)
    probs = jax.nn.softmax(top_v, axis=-1)
    e_flat = top_i.reshape(-1)
    onehot = (e_flat[:, None] == jnp.arange(N_EXPERTS)[None, :]).astype(jnp.int32)
    csum = jnp.cumsum(onehot, axis=0)
    counts = csum[-1]
    rank = jnp.take_along_axis(csum, e_flat[:, None], axis=1)[:, 0] - 1
    padded = ((counts + tm - 1) // tm) * tm
    ends = jnp.cumsum(padded)
    starts = ends - padded
    pos = starts[e_flat] + rank
    n_slots = TOP_K * m + N_EXPERTS * tm
    n_tiles = n_slots // tm
    tok = jnp.arange(TOP_K * m, dtype=jnp.int32) // TOP_K
    src = jnp.zeros((n_slots,), jnp.int32).at[pos].set(tok)
    tile_start = jnp.arange(n_tiles, dtype=jnp.int32) * tm
    tile_expert = jnp.minimum(jnp.searchsorted(ends, tile_start, side="right"), N_EXPERTS - 1).astype(jnp.int32)
    tile_valid = (tile_start < ends[-1]).astype(jnp.int32)
    xs = jnp.take(xb, src, axis=0)
    ys = grouped_swiglu(xs, w_gate, w_up, w_down,
                        tile_expert, tile_valid, tm, FFN_TF)
    pos2 = pos.reshape(m, TOP_K)
    y0 = jnp.take(ys, pos2[:, 0], axis=0)
    y1 = jnp.take(ys, pos2[:, 1], axis=0)
    return add_layer_norm(x, [y0, y1], probs, g, b, min(1024, m))


def _pack_in_weights(w):
    d = w.shape[0]
    o = 0
    q = w[:, o:o + 512]; o += 512
    kv = w[:, o:o + 768].reshape(d, 3, 2, NSA_KV_GROUPS, HEAD_DIM); o += 768
    ng = w[:, o:o + 24]; o += 24
    rest = w[:, o:o + 7 * 512]; o += 7 * 512
    mg = w[:, o:]
    kpack = jnp.concatenate([kv[:, 1, 0], kv[:, 2, 0]], axis=-1).reshape(d, 2 * NSA_KV_GROUPS * HEAD_DIM)
    kc = kv[:, 0, 0].reshape(d, NSA_KV_GROUPS * HEAD_DIM)
    vc = kv[:, 0, 1].reshape(d, NSA_KV_GROUPS * HEAD_DIM)
    w_nat = jnp.concatenate([mg, rest, kpack, kc, vc], axis=1).astype(BF16)
    vs = kv[:, 1, 1].reshape(d, NSA_KV_GROUPS * HEAD_DIM)
    vw = kv[:, 2, 1].reshape(d, NSA_KV_GROUPS * HEAD_DIM)
    ngp = jnp.pad(ng.reshape(d, NSA_KV_GROUPS, NSA_GROUP * 3), ((0, 0), (0, 0), (0, 4))).reshape(d, 32)
    w_t = jnp.concatenate([q, vs, vw, ngp], axis=1).T.astype(BF16)
    return w_nat, w_t


def token_mixer_ln(x, xb, w_in, cmp_pos, cmp_w1, cmp_b1, cmp_w2, cmp_b2, conv_w, gn_g, gn_b, w_br, w_out, ln_g, ln_b):
    b, t, d = x.shape
    m = b * t
    w_nat, w_t = _pack_in_weights(w_in)
    x2 = x.reshape(m, d)
    hn, proj_t = input_projection(xb, w_nat, w_t, min(2048, t), 1024)
    hn = hn.reshape(b, t, NAT_COLS)
    nch = t // CMP_STRIDE
    raw = hn[:, :, NAT_KC:NAT_KC + 2 * NSA_KV_GROUPS * HEAD_DIM].reshape(b, t, 2, NSA_KV_GROUPS, HEAD_DIM)
    chunks = raw.transpose(2, 0, 3, 1, 4).reshape(2, b, NSA_KV_GROUPS, nch, CMP_STRIDE * HEAD_DIM)
    cn, ct = nsa_compress_kv(chunks, cmp_pos, cmp_w1, cmp_b1, cmp_w2, cmp_b2)
    kc = cn[0].astype(BF16)
    vct = ct[1].astype(BF16)
    vt4 = proj_t[:, TR_VS:TR_NG, :].astype(BF16).reshape(b, 4, HEAD_DIM, t)
    ones_pad = jnp.zeros((b, 4, VT_ROWS - HEAD_DIM, t), BF16).at[:, :, 0].set(1.0)
    vt_pack = jnp.concatenate([vt4, ones_pad], axis=2).reshape(b, 4 * VT_ROWS, t)
    y_a = nsa_attention(proj_t, kc, vct, hn, vt_pack)
    y_c = retention_branch(hn, gn_g, gn_b, min(1024, t))
    return merge_branches(x2, y_a.reshape(m, -1), y_c.reshape(m, -1), hn.reshape(m, NAT_COLS), conv_w,
                          w_br, w_out, ln_g, ln_b, min(512, t), t)


def kernel(x, w_in, cmp_pos, cmp_w1, cmp_b1, cmp_w2, cmp_b2, conv_w, ret_gn_g, ret_gn_b, w_br, w_out,
           ln1_g, ln1_b, ln2_g, ln2_b, ffn_gate, ffn_up, ffn_down, moe_router, moe_gate, moe_up, moe_down):
    b, t, d = x.shape
    xb = x.astype(BF16)
    for l in range(DEPTH):
        x2, x2b = token_mixer_ln(x, xb, w_in[l], cmp_pos[l], cmp_w1[l], cmp_b1[l], cmp_w2[l], cmp_b2[l], conv_w[l],
                                 ret_gn_g[l], ret_gn_b[l], w_br[l], w_out[l], ln1_g[l], ln1_b[l])
        if l % 2 == 0:
            wg, wu, wd = ffn_weights_bf16(ffn_gate[:, None], ffn_up[:, None], ffn_down[:, None], l // 2)
            x2, x2b = dense_ffn(x2, x2b, wg, wu, wd, ln2_g[l], ln2_b[l])
        else:
            wg, wu, wd = ffn_weights_bf16(moe_gate, moe_up, moe_down, l // 2)
            x2, x2b = moe_ffn(x2, x2b, moe_router[l // 2], wg, wu, wd, ln2_g[l], ln2_b[l])
        x, xb = x2.reshape(b, t, d), x2b.reshape(b, t, d)
    return x
```

```python
import functools
import math

import jax
import jax.numpy as jnp
from jax import lax
from jax.experimental import pallas as pl
from jax.experimental.pallas import tpu as pltpu

F32 = jnp.float32
BF16 = jnp.bfloat16

D_MODEL = 1024
DEPTH = 4
BRANCH_WIDTH = 512
HEAD_DIM = 64
NSA_HEADS = 8
NSA_KV_GROUPS = 2
NSA_GROUP = 4
CMP_LEN = 32
CMP_STRIDE = 16
SLC_BLOCK = 64
SLC_TOPN = 16
WINDOW = 512
CMP_HIDDEN = 256
Q_BLOCK = 128
CONV_K = 3
CONV_HALO = 16
RET_HEADS = 4
RET_HEAD_DIM = 128
RET_CHUNK = 256
ROPE_BASE = 10000.0
D_FF = 3584
N_EXPERTS = 8
TOP_K = 2
LN_EPS = 1e-5
ALPHA = (2.0 * DEPTH) ** 0.25

NAT_MG = 0
NAT_CB = 3072
NAT_CC = 3584
NAT_CH = 4096
NAT_RQ = 4608
NAT_RK = 5120
NAT_RV = 5632
NAT_RG = 6144
NAT_KP = 6656
NAT_KC = 6912
NAT_VC = 7040
NAT_COLS = 7168
TR_Q = 0
TR_VS = 512
TR_VW = 640
TR_NG = 768
TR_ROWS = 800

NEG = -1e30
SEL_TK = 1024
LOG2E = 1.4426950408889634
NSA_QSUB = 2
CMP_CLASS_ROWS = 128
PG_PAD = 8
PICKED = -3e38
VT_ROWS = 80
FFN_TM = 512
FFN_TF = 1792
CAST_STEPS = 16
VMEM_LIMIT = 48 * 1024 * 1024
BIG_VMEM_LIMIT = 56 * 1024 * 1024


def _cparams(n_axes, vmem=VMEM_LIMIT):
    return pltpu.CompilerParams(dimension_semantics=("arbitrary",) * n_axes, vmem_limit_bytes=vmem)


def _proj_kernel(x_ref, w_ref, wt_ref, o_ref, ot_ref):
    @pl.when(pl.program_id(1) == 0)
    def _():
        ot_ref[0] = lax.dot_general(wt_ref[...], x_ref[...], (((1,), (1,)), ((), ())), preferred_element_type=F32)

    o_ref[...] = jnp.dot(x_ref[...], w_ref[...], preferred_element_type=F32).astype(o_ref.dtype)


def input_projection(x, w_nat, w_t, tm, tn):
    b, t, k = x.shape
    m = b * t
    n = w_nat.shape[1]
    r = w_t.shape[0]
    tps = t // tm
    return pl.pallas_call(
        _proj_kernel,
        grid=(m // tm, n // tn),
        in_specs=[pl.BlockSpec((tm, k), lambda i, j: (i, 0)), pl.BlockSpec((k, tn), lambda i, j: (0, j)),
                  pl.BlockSpec((r, k), lambda i, j: (0, 0))],
        out_specs=[pl.BlockSpec((tm, tn), lambda i, j: (i, j)),
                   pl.BlockSpec((1, r, tm), lambda i, j: (i // tps, 0, i % tps))],
        out_shape=[jax.ShapeDtypeStruct((m, n), BF16), jax.ShapeDtypeStruct((b, r, t), F32)],
        compiler_params=_cparams(2, BIG_VMEM_LIMIT),
        name="input_projection",
    )(x.reshape(m, k), w_nat, w_t)


def _cmp_kernel(c_ref, pos_ref, w1_ref, b1_ref, w2_ref, w2t_ref, b2_ref, b2t_ref, on_ref, ot_ref):
    half = CMP_STRIDE * HEAD_DIM
    c = c_ref[0, 0, 0].astype(BF16)
    w1 = w1_ref[0]
    a = jnp.dot(c, w1[:half], preferred_element_type=F32)
    bm = jnp.dot(c, w1[half:], preferred_element_type=F32)
    nch = a.shape[0]
    bm = pltpu.roll(bm, nch - 1, 0)
    posb = jnp.dot(pos_ref[0].astype(BF16), w1, preferred_element_type=F32)[0:1]
    hid = jax.nn.gelu(a + bm + posb + b1_ref[0])
    hb = hid.astype(BF16)
    on_ref[0, 0, 0] = jnp.dot(hb, w2_ref[0], preferred_element_type=F32) + b2_ref[0]
    ot_ref[0, 0, 0] = lax.dot_general(w2t_ref[0], hb, (((1,), (1,)), ((), ())),
                                      preferred_element_type=F32) + b2t_ref[0]


def nsa_compress_kv(chunks, pos, w1, b1, w2, b2):
    _, b, g, nch, cw = chunks.shape
    pos_flat = jnp.broadcast_to(pos.reshape(2, 1, CMP_LEN * HEAD_DIM), (2, 8, CMP_LEN * HEAD_DIM))
    w1b = w1.astype(BF16)
    w2b = w2.astype(BF16)
    w2t = jnp.swapaxes(w2, 1, 2).astype(BF16)
    b1r = b1.reshape(2, 1, CMP_HIDDEN)
    b2r = b2.reshape(2, 1, HEAD_DIM)
    b2t = b2.reshape(2, HEAD_DIM, 1)
    sel = lambda kv, i, j: (kv, 0, 0)
    return pl.pallas_call(
        _cmp_kernel,
        grid=(2, b, g),
        in_specs=[
            pl.BlockSpec((1, 1, 1, nch, cw), lambda kv, i, j: (kv, i, j, 0, 0)),
            pl.BlockSpec((1, 8, CMP_LEN * HEAD_DIM), sel),
            pl.BlockSpec((1, CMP_LEN * HEAD_DIM, CMP_HIDDEN), sel),
            pl.BlockSpec((1, 1, CMP_HIDDEN), sel),
            pl.BlockSpec((1, CMP_HIDDEN, HEAD_DIM), sel),
            pl.BlockSpec((1, HEAD_DIM, CMP_HIDDEN), sel),
            pl.BlockSpec((1, 1, HEAD_DIM), sel),
            pl.BlockSpec((1, HEAD_DIM, 1), sel),
        ],
        out_specs=[
            pl.BlockSpec((1, 1, 1, nch, HEAD_DIM), lambda kv, i, j: (kv, i, j, 0, 0)),
            pl.BlockSpec((1, 1, 1, HEAD_DIM, nch), lambda kv, i, j: (kv, i, j, 0, 0)),
        ],
        out_shape=[
            jax.ShapeDtypeStruct((2, b, g, nch, HEAD_DIM), F32),
            jax.ShapeDtypeStruct((2, b, g, HEAD_DIM, nch), F32),
        ],
        compiler_params=_cparams(3),
        name="nsa_compress",
    )(chunks, pos_flat, w1b, b1r, w2b, w2t, b2r, b2t)


def _accumulate(s, mt, vt, m, acc):
    m_new = jnp.maximum(m, mt)
    alpha = jnp.exp2(m - m_new)
    p = jnp.exp2(s - m_new).astype(BF16)
    return m_new, alpha * acc + jnp.dot(vt, p, preferred_element_type=F32)


def _nsa_kernel(qt_ref, ng_ref, kc_ref, vct_ref, kp_ref, vst_ref, vwt_ref, e_ref, tri_ref, wpat_ref,
                cpat_ref, o_ref, bias_ref, w_ref, sbuf_ref, oc_ref, pg_ref):
    nq = Q_BLOCK
    nsub = qt_ref.shape[2] // nq
    ncol = NSA_GROUP * nq
    n_sel = bias_ref.shape[1]
    ncp = kc_ref.shape[2]
    kw = 2 * HEAD_DIM
    bpt = SEL_TK // SLC_BLOCK
    n_tiles_total = kp_ref.shape[1] // SEL_TK
    qb_first = pl.program_id(2) * nsub
    wlen = WINDOW + nq

    def prepare(u):
        qb = qb_first + u
        q0 = pl.multiple_of(qb * nq, nq)
        qt = qt_ref[0, :, u * nq:(u + 1) * nq] * (HEAD_DIM ** -0.5 * LOG2E)
        qs = jnp.concatenate([qt[r * HEAD_DIM:(r + 1) * HEAD_DIM, :] for r in range(NSA_GROUP)], axis=1)
        zq = jnp.zeros_like(qs)
        q_c = qs.astype(BF16)
        q_slc = jnp.concatenate([qs, zq], axis=0).astype(BF16)
        q_win = jnp.concatenate([zq, qs], axis=0).astype(BF16)
        tq = q0 + (lax.broadcasted_iota(jnp.int32, (1, ncol), 1) & (nq - 1))

        coff = pl.multiple_of(ncp - qb * (nq // CMP_STRIDE), nq // CMP_STRIDE)
        tq1 = q0 + lax.broadcasted_iota(jnp.int32, (1, nq), 1)
        cur = jnp.right_shift(tq1, 6).astype(F32)
        first_diag = (2 * qb).astype(F32)
        pg_ref[u, 0:PG_PAD, :] = jnp.zeros((PG_PAD, nq), F32)
        crow = min(CMP_CLASS_ROWS, ncp)
        cls = qb // (crow // (nq // CMP_STRIDE))
        for c in range(ncp // crow):
            nc_c = crow * (c + 1)
            ns_c = nc_c * CMP_STRIDE // SLC_BLOCK

            @pl.when(cls == c)
            def _(nc_c=nc_c, ns_c=ns_c):
                s_c = (jnp.dot(kc_ref[0, 0, 0:nc_c, :], q_c, preferred_element_type=F32)
                       + jnp.concatenate([cpat_ref[pl.ds(coff, nc_c), :]] * NSA_GROUP, axis=1))
                m_c = jnp.max(s_c, axis=0, keepdims=True)
                e_c = jnp.exp2(s_c - m_c)
                l_c = jnp.sum(e_c, axis=0, keepdims=True)
                p_c = e_c * jnp.where(tq >= CMP_LEN - 1, 1.0 / l_c, 0.0)
                oc_ref[u] = jnp.dot(vct_ref[0, 0, :, 0:nc_c], p_c.astype(BF16), preferred_element_type=F32)

                p_grp = p_c[:, 0:nq]
                for r in range(1, NSA_GROUP):
                    p_grp = p_grp + p_c[:, r * nq:(r + 1) * nq]
                pg_ref[u, PG_PAD:PG_PAD + nc_c, :] = p_grp
                ratio = SLC_BLOCK // CMP_STRIDE
                taps = [pg_ref[u, pl.ds(PG_PAD - 1 + k, ns_c, stride=ratio), :] for k in range(ratio + 1)]
                score = taps[0] + 2.0 * (taps[1] + taps[2] + taps[3]) + taps[4]
                j_io = lax.broadcasted_iota(jnp.int32, (ns_c, nq), 0).astype(F32)
                score = jnp.where(j_io <= cur, score, -1.0)
                score = jnp.where(j_io == 0.0, PICKED, score)
                score = jnp.where(j_io == cur, PICKED, score)
                score = jnp.where(j_io == cur - 1.0, PICKED, score)
                for _ in range(SLC_TOPN - 3):
                    mx = jnp.max(score, axis=0, keepdims=True)
                    first = jnp.min(jnp.where(score == mx, j_io, float(ns_c)), axis=0, keepdims=True)
                    score = jnp.where(j_io == first, PICKED, score)
                bias = jnp.where(j_io < first_diag, jnp.where(score == PICKED, 0.0, NEG), NEG)
                bias_ref[u, 0:ns_c, :] = bias.astype(BF16)
                if ns_c < n_sel:
                    bias_ref[u, ns_c:, :] = jnp.full((n_sel - ns_c, nq), NEG, BF16)

        o_c = oc_ref[u]

        wstart = pl.multiple_of(jnp.maximum(q0 - WINDOW, 0), nq)
        woff = pl.multiple_of(jnp.maximum(WINDOW - q0, 0), nq)
        s_w = jnp.dot(kp_ref[0, pl.ds(wstart, wlen), :], q_win, preferred_element_type=F32)
        s_w = s_w + jnp.concatenate([wpat_ref[pl.ds(woff, wlen), :]] * NSA_GROUP, axis=1)
        p_w = jnp.exp2(s_w - jnp.max(s_w, axis=0, keepdims=True)).astype(BF16)
        acc_w = jnp.dot(vwt_ref[0, :, pl.ds(wstart, wlen)], p_w, preferred_element_type=F32)
        o_w = acc_w[0:HEAD_DIM] / acc_w[HEAD_DIM:HEAD_DIM + 1]

        for slot in range(2):
            w_ref[slot, u, 0:kw, :] = q_slc
            w_ref[slot, u, kw:, :] = jnp.zeros((w_ref.shape[2] - kw, ncol), BF16)
        return q0, q_slc, o_c, o_w

    prepared = [prepare(u) for u in range(nsub)]

    def tile_start(j):
        jc = jnp.minimum(j, n_tiles_total - 1)
        return jc, pl.multiple_of(jc * SEL_TK, SEL_TK)

    def produce(j, slot):
        jc, k0 = tile_start(j)
        lhs = jnp.concatenate([kp_ref[0, pl.ds(k0, SEL_TK), :], e_ref[...]], axis=1)
        mts = []
        for u in range(nsub):
            b16 = bias_ref[u, pl.ds(pl.multiple_of(jc * bpt, bpt), bpt), :]
            w_ref[slot, u, kw:kw + bpt, :] = jnp.concatenate([b16] * NSA_GROUP, axis=1)
            s = jnp.dot(lhs, w_ref[slot, u], preferred_element_type=F32)
            sbuf_ref[slot, u] = s
            mts.append(jnp.max(s, axis=0, keepdims=True))
        return tuple(mts)

    def consume(j, slot, mts, state):
        _, k0 = tile_start(j)
        vt = vst_ref[0, :, pl.ds(k0, SEL_TK)]
        return tuple(_accumulate(sbuf_ref[slot, u], mts[u], vt, *state[u]) for u in range(nsub))

    def pair(i, carry, last):
        mt_a, state = carry
        mt_b = produce(2 * i + 1, 1)
        state = consume(2 * i, 0, mt_a, state)
        if last:
            return consume(2 * i + 1, 1, mt_b, state)
        mt_a = produce(2 * i + 2, 0)
        state = consume(2 * i + 1, 1, mt_b, state)
        return mt_a, state

    n_main = ((qb_first + nsub - 1) * nq + (SEL_TK - 1)) // SEL_TK
    n_pairs = jnp.maximum((n_main + 1) // 2, 1)
    state = tuple((jnp.full((1, ncol), NEG, F32), jnp.zeros((vst_ref.shape[1], ncol), F32)) for _ in range(nsub))
    carry = lax.fori_loop(0, n_pairs - 1, lambda i, c: pair(i, c, False), (produce(0, 0), state))
    state = pair(n_pairs - 1, carry, True)

    for u in range(nsub):
        q0, q_slc, o_c, o_w = prepared[u]
        s_d = jnp.dot(kp_ref[0, pl.ds(q0, nq), :], q_slc, preferred_element_type=F32) + tri_ref[...]
        _, acc_s = _accumulate(s_d, jnp.max(s_d, axis=0, keepdims=True), vst_ref[0, :, pl.ds(q0, nq)], *state[u])
        o_s = acc_s[0:HEAD_DIM] / acc_s[HEAD_DIM:HEAD_DIM + 1]
        gts = jax.nn.sigmoid(ng_ref[0, :, u * nq:(u + 1) * nq])
        rows = []
        for r in range(NSA_GROUP):
            sl = slice(r * nq, (r + 1) * nq)
            rows.append(gts[3 * r:3 * r + 1, :] * o_c[:, sl] + gts[3 * r + 1:3 * r + 2, :] * o_s[:, sl]
                        + gts[3 * r + 2:3 * r + 3, :] * o_w[:, sl])
        y_t = jnp.concatenate(rows, axis=0)
        o_ref[0, u * nq:(u + 1) * nq, :] = y_t.T.astype(o_ref.dtype)


def _nsa_constants(ncp):
    nq = Q_BLOCK
    ql = jnp.arange(nq)[None, :]
    r = jnp.arange(SEL_TK)[:, None]
    e = (r // SLC_BLOCK == jnp.arange(2 * HEAD_DIM)[None, :]).astype(BF16)
    rd = jnp.arange(nq)[:, None]
    tri = jnp.tile(jnp.where(rd <= ql, 0.0, NEG).astype(F32), (1, NSA_GROUP))
    rw = jnp.arange(2 * WINDOW + nq)[:, None]
    wpat = jnp.where((rw > ql) & (rw <= ql + WINDOW), 0.0, NEG).astype(F32)
    rc = jnp.arange(2 * ncp)[:, None] - ncp
    cpat = jnp.where(CMP_STRIDE * rc + (CMP_LEN - 1) <= ql, 0.0, NEG).astype(F32)
    return e, tri, wpat, cpat


def nsa_attention(proj_t, kc, vct, hn, vt_pack):
    b, _, t = proj_t.shape
    ncp = kc.shape[2]
    n_sel = t // SLC_BLOCK
    gq = NSA_GROUP * HEAD_DIM
    qstep = NSA_QSUB * Q_BLOCK
    e, tri, wpat, cpat = _nsa_constants(ncp)
    const2 = lambda i, g, q: (0, 0)
    return pl.pallas_call(
        _nsa_kernel,
        grid=(b, NSA_KV_GROUPS, t // qstep),
        in_specs=[
            pl.BlockSpec((1, gq, qstep), lambda i, g, q: (i, g, q)),
            pl.BlockSpec((1, 16, qstep), lambda i, g, q: (i, TR_NG // 16 + g, q)),
            pl.BlockSpec((1, 1, ncp, HEAD_DIM), lambda i, g, q: (i, g, 0, 0)),
            pl.BlockSpec((1, 1, HEAD_DIM, ncp), lambda i, g, q: (i, g, 0, 0)),
            pl.BlockSpec((1, t, 2 * HEAD_DIM), lambda i, g, q: (i, 0, NAT_KP // (2 * HEAD_DIM) + g)),
            pl.BlockSpec((1, VT_ROWS, t), lambda i, g, q: (i, g, 0)),
            pl.BlockSpec((1, VT_ROWS, t), lambda i, g, q: (i, NSA_KV_GROUPS + g, 0)),
            pl.BlockSpec(e.shape, const2),
            pl.BlockSpec(tri.shape, const2),
            pl.BlockSpec(wpat.shape, const2),
            pl.BlockSpec(cpat.shape, const2),
        ],
        out_specs=pl.BlockSpec((1, qstep, gq), lambda i, g, q: (i, q, g)),
        out_shape=jax.ShapeDtypeStruct((b, t, NSA_HEADS * HEAD_DIM), BF16),
        scratch_shapes=[pltpu.VMEM((NSA_QSUB, n_sel, Q_BLOCK), BF16),
                        pltpu.VMEM((2, NSA_QSUB, 4 * HEAD_DIM, NSA_GROUP * Q_BLOCK), BF16),
                        pltpu.VMEM((2, NSA_QSUB, SEL_TK, NSA_GROUP * Q_BLOCK), F32),
                        pltpu.VMEM((NSA_QSUB, HEAD_DIM, NSA_GROUP * Q_BLOCK), F32),
                        pltpu.VMEM((NSA_QSUB, PG_PAD + ncp, Q_BLOCK), F32)],
        compiler_params=_cparams(3),
        name="nsa_attention",
    )(proj_t, proj_t, kc, vct, hn, vt_pack, vt_pack, e, tri, wpat, cpat)


def _ret_kernel(rq_ref, rk_ref, rv_ref, rg_ref, cos_ref, sin_ref, dec_ref, qd_ref, kd_ref, cd_ref,
                gg_ref, gb_ref, o_ref, st_ref):
    c = RET_CHUNK
    hd = RET_HEAD_DIM

    @pl.when(pl.program_id(1) == 0)
    def _():
        st_ref[...] = jnp.zeros_like(st_ref)

    n_chunks = rq_ref.shape[1] // c

    def chunk(ci, _):
        r0 = pl.multiple_of(ci * c, c)
        cos = cos_ref[pl.ds(r0, c), :]
        sin = sin_ref[pl.ds(r0, c), :]
        for h in range(RET_HEADS):
            hs = slice(h * hd, (h + 1) * hd)
            q = rq_ref[0, pl.ds(r0, c), hs].astype(F32)
            k = rk_ref[0, pl.ds(r0, c), hs].astype(F32)
            v = rv_ref[0, pl.ds(r0, c), hs]
            q = q * cos + pltpu.roll(q, hd // 2, 1) * sin
            k = (k * cos + pltpu.roll(k, hd // 2, 1) * sin) * (hd ** -0.5)
            qb = q.astype(BF16)
            kb = k.astype(BF16)
            vb = v.astype(BF16)
            inner = lax.dot_general(qb, kb, (((1,), (1,)), ((), ())), preferred_element_type=F32) * dec_ref[h]
            st = st_ref[h]
            o = (jnp.dot(inner.astype(BF16), vb, preferred_element_type=F32)
                 + jnp.dot(qb, st.astype(BF16), preferred_element_type=F32) * qd_ref[h])
            kdt = (k * kd_ref[h]).T.astype(BF16)
            st_ref[h] = st * cd_ref[h] + jnp.dot(kdt, vb, preferred_element_type=F32)
            mu = jnp.mean(o, axis=-1, keepdims=True)
            d = o - mu
            var = jnp.mean(d * d, axis=-1, keepdims=True)
            y = d * lax.rsqrt(var + LN_EPS) * gg_ref[:, hs] + gb_ref[:, hs]
            o_ref[0, pl.ds(r0, c), hs] = (jax.nn.silu(rg_ref[0, pl.ds(r0, c), hs].astype(F32)) * y).astype(o_ref.dtype)
        return 0

    lax.fori_loop(0, n_chunks, chunk, 0)


def retention_branch(hn, gn_g, gn_b, tr):
    b, t, _ = hn.shape
    w = BRANCH_WIDTH
    hd = RET_HEAD_DIM
    c = RET_CHUNK
    inv = ROPE_BASE ** (-jnp.arange(0, hd, 2, dtype=F32) / hd)
    ang = jnp.arange(t, dtype=F32)[:, None] * inv[None, :]
    cos2 = jnp.concatenate([jnp.cos(ang), jnp.cos(ang)], axis=1)
    sin2 = jnp.concatenate([-jnp.sin(ang), jnp.sin(ang)], axis=1)
    log_g = jnp.log(1.0 - 2.0 ** (-5.0 - jnp.arange(RET_HEADS, dtype=F32)))
    idx = jnp.arange(c, dtype=F32)
    diff = idx[:, None] - idx[None, :]
    decay_in = jnp.where(diff >= 0, jnp.exp(log_g[:, None, None] * jnp.maximum(diff, 0.0)), 0.0)
    ones = jnp.ones((RET_HEADS, c, hd), F32)
    q_dec = jnp.exp(log_g[:, None] * (idx + 1.0))[:, :, None] * ones
    k_dec = jnp.exp(log_g[:, None] * (c - 1.0 - idx))[:, :, None] * ones
    c_dec = jnp.exp(log_g * c)[:, None, None] * jnp.ones((RET_HEADS, hd, hd), F32)
    col = lambda cc: (lambda i, j: (i, j, cc // w))
    full3 = lambda i, j: (0, 0, 0)
    return pl.pallas_call(
        _ret_kernel,
        grid=(b, t // tr),
        in_specs=[
            pl.BlockSpec((1, tr, w), col(NAT_RQ)),
            pl.BlockSpec((1, tr, w), col(NAT_RK)),
            pl.BlockSpec((1, tr, w), col(NAT_RV)),
            pl.BlockSpec((1, tr, w), col(NAT_RG)),
            pl.BlockSpec((tr, hd), lambda i, j: (j, 0)),
            pl.BlockSpec((tr, hd), lambda i, j: (j, 0)),
            pl.BlockSpec((RET_HEADS, c, c), full3),
            pl.BlockSpec((RET_HEADS, c, hd), full3),
            pl.BlockSpec((RET_HEADS, c, hd), full3),
            pl.BlockSpec((RET_HEADS, hd, hd), full3),
            pl.BlockSpec((1, w), lambda i, j: (0, 0)),
            pl.BlockSpec((1, w), lambda i, j: (0, 0)),
        ],
        out_specs=pl.BlockSpec((1, tr, w), lambda i, j: (i, j, 0)),
        out_shape=jax.ShapeDtypeStruct((b, t, w), BF16),
        scratch_shapes=[pltpu.VMEM((RET_HEADS, hd, hd), F32)],
        compiler_params=_cparams(2),
        name="retention",
    )(hn, hn, hn, hn, cos2, sin2, decay_in, q_dec, k_dec, c_dec, gn_g.reshape(1, w), gn_b.reshape(1, w))


def _layer_norm(z, g, b):
    mu = jnp.mean(z, axis=-1, keepdims=True)
    d = z - mu
    var = jnp.mean(d * d, axis=-1, keepdims=True)
    return d * lax.rsqrt(var + LN_EPS) * g + b


def _merge_kernel(tiles_per_seq, x_ref, ya_ref, cb_ref, cc_ref, ch_ref, pc_ref, ph_ref, cw_ref, yc_ref, mg_ref,
                  wbr_ref, wout_ref, g_ref, b_ref, o_ref, ob_ref):
    first = (pl.program_id(0) % tiles_per_seq) == 0
    u = cc_ref[...].astype(F32) * ch_ref[...].astype(F32)
    prev = pc_ref[...].astype(F32) * ph_ref[...].astype(F32)
    prev = jnp.where(first, 0.0, prev)
    row = lax.broadcasted_iota(jnp.int32, u.shape, 0)
    p1 = prev[CONV_HALO - 1:CONV_HALO, :]
    p2 = prev[CONV_HALO - 2:CONV_HALO - 1, :]
    u1 = jnp.where(row == 0, p1, pltpu.roll(u, 1, 0))
    u2 = jnp.where(row == 0, p2, jnp.where(row == 1, p1, pltpu.roll(u, 2, 0)))
    cw = cw_ref[...]
    y_b = cb_ref[...].astype(F32) * (cw[0:1, :] * u2 + cw[1:2, :] * u1 + cw[2:3, :] * u)

    mix = None
    for c, y in enumerate((ya_ref[...], y_b.astype(BF16), yc_ref[...])):
        proj = jnp.dot(y, wbr_ref[c], preferred_element_type=F32)
        gate = jax.nn.sigmoid(mg_ref[:, c * D_MODEL:(c + 1) * D_MODEL].astype(F32))
        mix = gate * proj if mix is None else mix + gate * proj
    m = jnp.dot(mix.astype(BF16), wout_ref[...], preferred_element_type=F32)
    out = _layer_norm(ALPHA * x_ref[...] + m, g_ref[...], b_ref[...])
    o_ref[...] = out
    ob_ref[...] = out.astype(BF16)


def merge_branches(x, ya, yc, hn, conv_w, w_br, w_out, g, b, tm, t):
    m = x.shape[0]
    d = D_MODEL
    w = BRANCH_WIDTH
    row = lambda i: (i, 0)
    col = lambda c: (lambda i: (i, c // w))
    hal = lambda c: (lambda i: (jnp.maximum(i * (tm // CONV_HALO) - 1, 0), c // w))
    wpad = jnp.zeros((8, w), F32).at[:CONV_K].set(conv_w)
    return pl.pallas_call(
        functools.partial(_merge_kernel, t // tm),
        grid=(m // tm,),
        in_specs=[
            pl.BlockSpec((tm, d), row),
            pl.BlockSpec((tm, w), row),
            pl.BlockSpec((tm, w), col(NAT_CB)),
            pl.BlockSpec((tm, w), col(NAT_CC)),
            pl.BlockSpec((tm, w), col(NAT_CH)),
            pl.BlockSpec((CONV_HALO, w), hal(NAT_CC)),
            pl.BlockSpec((CONV_HALO, w), hal(NAT_CH)),
            pl.BlockSpec((8, w), lambda i: (0, 0)),
            pl.BlockSpec((tm, w), row),
            pl.BlockSpec((tm, 3 * d), row),
            pl.BlockSpec((3, w, d), lambda i: (0, 0, 0)),
            pl.BlockSpec((d, d), lambda i: (0, 0)),
            pl.BlockSpec((1, d), lambda i: (0, 0)),
            pl.BlockSpec((1, d), lambda i: (0, 0)),
        ],
        out_specs=[pl.BlockSpec((tm, d), row), pl.BlockSpec((tm, d), row)],
        out_shape=[jax.ShapeDtypeStruct((m, d), F32), jax.ShapeDtypeStruct((m, d), BF16)],
        compiler_params=_cparams(1),
        name="merge",
    )(x, ya, hn, hn, hn, hn, hn, wpad, yc, hn, w_br.astype(BF16), w_out.astype(BF16), g.reshape(1, d),
      b.reshape(1, d))


def _cast3_kernel(g_ref, u_ref, d_ref, og_ref, ou_ref, od_ref):
    og_ref[...] = g_ref[0].astype(og_ref.dtype)
    ou_ref[...] = u_ref[0].astype(ou_ref.dtype)
    od_ref[...] = d_ref[0].astype(od_ref.dtype)


def ffn_weights_bf16(w_gate, w_up, w_down, layer):
    _, e, d, ff = w_gate.shape
    rg, rd = d // CAST_STEPS, ff // CAST_STEPS
    up_spec = pl.BlockSpec((1, 1, rg, ff), lambda i, j: (layer, i, j, 0))
    return pl.pallas_call(
        _cast3_kernel,
        grid=(e, CAST_STEPS),
        in_specs=[up_spec, up_spec, pl.BlockSpec((1, 1, rd, d), lambda i, j: (layer, i, j, 0))],
        out_specs=[pl.BlockSpec((1, rg, ff), lambda i, j: (i, j, 0)), pl.BlockSpec((1, rg, ff), lambda i, j: (i, j, 0)),
                   pl.BlockSpec((1, rd, d), lambda i, j: (i, j, 0))],
        out_shape=[jax.ShapeDtypeStruct((e, d, ff), BF16), jax.ShapeDtypeStruct((e, d, ff), BF16),
                   jax.ShapeDtypeStruct((e, ff, d), BF16)],
        compiler_params=_cparams(2),
        name="cast_bf16",
    )(w_gate, w_up, w_down)


def _swiglu_chunks(x, wg_ref, wu_ref, wd_ref, tf):
    acc = None
    for c in range(wg_ref.shape[2] // tf):
        sl = slice(c * tf, (c + 1) * tf)
        gte = jnp.dot(x, wg_ref[0, :, sl], preferred_element_type=F32)
        up = jnp.dot(x, wu_ref[0, :, sl], preferred_element_type=F32)
        h = (jax.nn.silu(gte) * up).astype(BF16)
        part = jnp.dot(h, wd_ref[0, sl, :], preferred_element_type=F32)
        acc = part if acc is None else acc + part
    return acc


def _mlp_kernel(tf, te_ref, tv_ref, x_ref, wg_ref, wu_ref, wd_ref, o_ref):
    i = pl.program_id(0)

    @pl.when(tv_ref[i] > 0)
    def _():
        o_ref[...] = _swiglu_chunks(x_ref[...], wg_ref, wu_ref, wd_ref, tf).astype(o_ref.dtype)

    @pl.when(tv_ref[i] == 0)
    def _():
        o_ref[...] = jnp.zeros_like(o_ref)


def _resident(block_shape, index_map):
    return pl.BlockSpec(block_shape, index_map, pipeline_mode=pl.Buffered(1))


def grouped_swiglu(xs, w_gate, w_up, w_down, tile_expert, tile_valid, tm, tf):
    n, d = xs.shape
    ff = w_gate.shape[2]
    grid_spec = pltpu.PrefetchScalarGridSpec(
        num_scalar_prefetch=2,
        grid=(n // tm,),
        in_specs=[
            pl.BlockSpec((tm, d), lambda i, te, tv: (i, 0)),
            _resident((1, d, ff), lambda i, te, tv: (te[i], 0, 0)),
            _resident((1, d, ff), lambda i, te, tv: (te[i], 0, 0)),
            _resident((1, ff, d), lambda i, te, tv: (te[i], 0, 0)),
        ],
        out_specs=pl.BlockSpec((tm, d), lambda i, te, tv: (i, 0)),
    )
    return pl.pallas_call(
        functools.partial(_mlp_kernel, tf),
        grid_spec=grid_spec,
        out_shape=jax.ShapeDtypeStruct((n, d), BF16),
        compiler_params=_cparams(1, BIG_VMEM_LIMIT),
        name="grouped_swiglu",
    )(tile_expert, tile_valid, xs, w_gate, w_up, w_down)


def _add_ln_kernel(*refs):
    x_ref, sc_ref, *adds, g_ref, b_ref, o_ref, ob_ref = refs
    z = ALPHA * x_ref[...]
    for k, a in enumerate(adds):
        z = z + sc_ref[:, k:k + 1] * a[...].astype(F32)
    out = _layer_norm(z, g_ref[...], b_ref[...])
    o_ref[...] = out
    ob_ref[...] = out.astype(BF16)


def add_layer_norm(x, adds, scales, g, b, tm):
    m, d = x.shape
    row = lambda i: (i, 0)
    return pl.pallas_call(
        _add_ln_kernel,
        grid=(m // tm,),
        in_specs=([pl.BlockSpec((tm, d), row), pl.BlockSpec((tm, len(adds)), row)]
                  + [pl.BlockSpec((tm, d), row)] * len(adds) + [pl.BlockSpec((1, d), lambda i: (0, 0))] * 2),
        out_specs=[pl.BlockSpec((tm, d), row)] * 2,
        out_shape=[jax.ShapeDtypeStruct((m, d), F32), jax.ShapeDtypeStruct((m, d), BF16)],
        compiler_params=_cparams(1),
        name="add_layer_norm",
    )(x, scales, *adds, g.reshape(1, d), b.reshape(1, d))


def _router_kernel(x_ref, wh_ref, wl_ref, o_ref):
    x = x_ref[...]
    xh = x.astype(BF16)
    xl = (x - xh.astype(F32)).astype(BF16)
    wh = wh_ref[...]
    o_ref[...] = (jnp.dot(xh, wh, preferred_element_type=F32) + jnp.dot(xl, wh, preferred_element_type=F32)
                  + jnp.dot(xh, wl_ref[...], preferred_element_type=F32))


def router_logits(x, w_router, tm):
    m, d = x.shape
    wp = jnp.zeros((d, 128), F32).at[:, :N_EXPERTS].set(w_router)
    wh = wp.astype(BF16)
    wl = (wp - wh.astype(F32)).astype(BF16)
    return pl.pallas_call(
        _router_kernel,
        grid=(m // tm,),
        in_specs=[pl.BlockSpec((tm, d), lambda i: (i, 0)), pl.BlockSpec((d, 128), lambda i: (0, 0)),
                  pl.BlockSpec((d, 128), lambda i: (0, 0))],
        out_specs=pl.BlockSpec((tm, 128), lambda i: (i, 0)),
        out_shape=jax.ShapeDtypeStruct((m, 128), F32),
        compiler_params=_cparams(1),
        name="router",
    )(x, wh, wl)


def _dense_ffn_kernel(tf, xb_ref, x_ref, wg_ref, wu_ref, wd_ref, g_ref, b_ref, o_ref, ob_ref):
    acc = _swiglu_chunks(xb_ref[...], wg_ref, wu_ref, wd_ref, tf)
    out = _layer_norm(ALPHA * x_ref[...] + acc, g_ref[...], b_ref[...])
    o_ref[...] = out
    ob_ref[...] = out.astype(BF16)


def dense_ffn(x, xb, w_gate, w_up, w_down, g, b):
    m, d = x.shape
    ff = w_gate.shape[2]
    tm = min(FFN_TM, m)
    row = lambda i: (i, 0)
    return pl.pallas_call(
        functools.partial(_dense_ffn_kernel, FFN_TF),
        grid=(m // tm,),
        in_specs=[
            pl.BlockSpec((tm, d), row),
            pl.BlockSpec((tm, d), row),
            _resident((1, d, ff), lambda i: (0, 0, 0)),
            _resident((1, d, ff), lambda i: (0, 0, 0)),
            _resident((1, ff, d), lambda i: (0, 0, 0)),
            pl.BlockSpec((1, d), lambda i: (0, 0)),
            pl.BlockSpec((1, d), lambda i: (0, 0)),
        ],
        out_specs=[pl.BlockSpec((tm, d), row)] * 2,
        out_shape=[jax.ShapeDtypeStruct((m, d), F32), jax.ShapeDtypeStruct((m, d), BF16)],
        compiler_params=_cparams(1, BIG_VMEM_LIMIT),
        name="dense_ffn",
    )(xb, x, w_gate, w_up, w_down, g.reshape(1, d), b.reshape(1, d))


def moe_ffn(x, xb, w_router, w_gate, w_up, w_down, g, b):
    m = x.shape[0]
    tm = FFN_TM
    logits = router_logits(x, w_router, min(1024, m))[:, :N_EXPERTS]
    top_v, top_i = lax.top_k(logits, TOP_K)
    probs = jax.nn.softmax(top_v, axis=-1)
    e_flat = top_i.reshape(-1)
    onehot = (e_flat[:, None] == jnp.arange(N_EXPERTS)[None, :]).astype(jnp.int32)
    csum = jnp.cumsum(onehot, axis=0)
    counts = csum[-1]
    rank = jnp.take_along_axis(csum, e_flat[:, None], axis=1)[:, 0] - 1
    padded = ((counts + tm - 1) // tm) * tm
    ends = jnp.cumsum(padded)
    starts = ends - padded
    pos = starts[e_flat] + rank
    n_slots = TOP_K * m + N_EXPERTS * tm
    n_tiles = n_slots // tm
    tok = jnp.arange(TOP_K * m, dtype=jnp.int32) // TOP_K
    src = jnp.zeros((n_slots,), jnp.int32).at[pos].set(tok)
    tile_start = jnp.arange(n_tiles, dtype=jnp.int32) * tm
    tile_expert = jnp.minimum(jnp.searchsorted(ends, tile_start, side="right"), N_EXPERTS - 1).astype(jnp.int32)
    tile_valid = (tile_start < ends[-1]).astype(jnp.int32)
    xs = jnp.take(xb, src, axis=0)
    ys = grouped_swiglu(xs, w_gate, w_up, w_down,
                        tile_expert, tile_valid, tm, FFN_TF)
    pos2 = pos.reshape(m, TOP_K)
    y0 = jnp.take(ys, pos2[:, 0], axis=0)
    y1 = jnp.take(ys, pos2[:, 1], axis=0)
    return add_layer_norm(x, [y0, y1], probs, g, b, min(1024, m))


def _pack_in_weights(w):
    d = w.shape[0]
    o = 0
    q = w[:, o:o + 512]; o += 512
    kv = w[:, o:o + 768].reshape(d, 3, 2, NSA_KV_GROUPS, HEAD_DIM); o += 768
    ng = w[:, o:o + 24]; o += 24
    rest = w[:, o:o + 7 * 512]; o += 7 * 512
    mg = w[:, o:]
    kpack = jnp.concatenate([kv[:, 1, 0], kv[:, 2, 0]], axis=-1).reshape(d, 2 * NSA_KV_GROUPS * HEAD_DIM)
    kc = kv[:, 0, 0].reshape(d, NSA_KV_GROUPS * HEAD_DIM)
    vc = kv[:, 0, 1].reshape(d, NSA_KV_GROUPS * HEAD_DIM)
    w_nat = jnp.concatenate([mg, rest, kpack, kc, vc], axis=1).astype(BF16)
    vs = kv[:, 1, 1].reshape(d, NSA_KV_GROUPS * HEAD_DIM)
    vw = kv[:, 2, 1].reshape(d, NSA_KV_GROUPS * HEAD_DIM)
    ngp = jnp.pad(ng.reshape(d, NSA_KV_GROUPS, NSA_GROUP * 3), ((0, 0), (0, 0), (0, 4))).reshape(d, 32)
    w_t = jnp.concatenate([q, vs, vw, ngp], axis=1).T.astype(BF16)
    return w_nat, w_t


def token_mixer_ln(x, xb, w_in, cmp_pos, cmp_w1, cmp_b1, cmp_w2, cmp_b2, conv_w, gn_g, gn_b, w_br, w_out, ln_g, ln_b):
    b, t, d = x.shape
    m = b * t
    w_nat, w_t = _pack_in_weights(w_in)
    x2 = x.reshape(m, d)
    hn, proj_t = input_projection(xb, w_nat, w_t, min(2048, t), 1024)
    hn = hn.reshape(b, t, NAT_COLS)
    nch = t // CMP_STRIDE
    raw = hn[:, :, NAT_KC:NAT_KC + 2 * NSA_KV_GROUPS * HEAD_DIM].reshape(b, t, 2, NSA_KV_GROUPS, HEAD_DIM)
    chunks = raw.transpose(2, 0, 3, 1, 4).reshape(2, b, NSA_KV_GROUPS, nch, CMP_STRIDE * HEAD_DIM)
    cn, ct = nsa_compress_kv(chunks, cmp_pos, cmp_w1, cmp_b1, cmp_w2, cmp_b2)
    kc = cn[0].astype(BF16)
    vct = ct[1].astype(BF16)
    vt4 = proj_t[:, TR_VS:TR_NG, :].astype(BF16).reshape(b, 4, HEAD_DIM, t)
    ones_pad = jnp.zeros((b, 4, VT_ROWS - HEAD_DIM, t), BF16).at[:, :, 0].set(1.0)
    vt_pack = jnp.concatenate([vt4, ones_pad], axis=2).reshape(b, 4 * VT_ROWS, t)
    y_a = nsa_attention(proj_t, kc, vct, hn, vt_pack)
    y_c = retention_branch(hn, gn_g, gn_b, min(1024, t))
    return merge_branches(x2, y_a.reshape(m, -1), y_c.reshape(m, -1), hn.reshape(m, NAT_COLS), conv_w,
                          w_br, w_out, ln_g, ln_b, min(512, t), t)


def kernel(x, w_in, cmp_pos, cmp_w1, cmp_b1, cmp_w2, cmp_b2, conv_w, ret_gn_g, ret_gn_b, w_br, w_out,
           ln1_g, ln1_b, ln2_g, ln2_b, ffn_gate, ffn_up, ffn_down, moe_router, moe_gate, moe_up, moe_down):
    b, t, d = x.shape
    xb = x.astype(BF16)
    for l in range(DEPTH):
        x2, x2b = token_mixer_ln(x, xb, w_in[l], cmp_pos[l], cmp_w1[l], cmp_b1[l], cmp_w2[l], cmp_b2[l], conv_w[l],
                                 ret_gn_g[l], ret_gn_b[l], w_br[l], w_out[l], ln1_g[l], ln1_b[l])
        if l % 2 == 0:
            wg, wu, wd = ffn_weights_bf16(ffn_gate[:, None], ffn_up[:, None], ffn_down[:, None], l // 2)
            x2, x2b = dense_ffn(x2, x2b, wg, wu, wd, ln2_g[l], ln2_b[l])
        else:
            wg, wu, wd = ffn_weights_bf16(moe_gate, moe_up, moe_down, l // 2)
            x2, x2b = moe_ffn(x2, x2b, moe_router[l // 2], wg, wu, wd, ln2_g[l], ln2_b[l])
        x, xb = x2.reshape(b, t, d), x2b.reshape(b, t, d)
    return x
```

```python
import functools
import math

import jax
import jax.numpy as jnp
from jax import lax
from jax.experimental import pallas as pl
from jax.experimental.pallas import tpu as pltpu

F32 = jnp.float32
BF16 = jnp.bfloat16

D_MODEL = 1024
DEPTH = 4
BRANCH_WIDTH = 512
HEAD_DIM = 64
NSA_HEADS = 8
NSA_KV_GROUPS = 2
NSA_GROUP = 4
CMP_LEN = 32
CMP_STRIDE = 16
SLC_BLOCK = 64
SLC_TOPN = 16
WINDOW = 512
CMP_HIDDEN = 256
Q_BLOCK = 128
CONV_K = 3
CONV_HALO = 16
RET_HEADS = 4
RET_HEAD_DIM = 128
RET_CHUNK = 256
ROPE_BASE = 10000.0
D_FF = 3584
N_EXPERTS = 8
TOP_K = 2
LN_EPS = 1e-5
ALPHA = (2.0 * DEPTH) ** 0.25

NAT_MG = 0
NAT_CB = 3072
NAT_CC = 3584
NAT_CH = 4096
NAT_RQ = 4608
NAT_RK = 5120
NAT_RV = 5632
NAT_RG = 6144
NAT_KP = 6656
NAT_KC = 6912
NAT_VC = 7040
NAT_COLS = 7168
TR_Q = 0
TR_VS = 512
TR_VW = 640
TR_NG = 768
TR_ROWS = 800

NEG = -1e30
SEL_TK = 1024
LOG2E = 1.4426950408889634
NSA_QSUB = 2
CMP_CLASS_ROWS = 128
PG_PAD = 8
PICKED = -3e38
VT_ROWS = 80
FFN_TM = 512
FFN_TF = 1792
CAST_STEPS = 16
VMEM_LIMIT = 48 * 1024 * 1024
BIG_VMEM_LIMIT = 56 * 1024 * 1024


def _cparams(n_axes, vmem=VMEM_LIMIT):
    return pltpu.CompilerParams(dimension_semantics=("arbitrary",) * n_axes, vmem_limit_bytes=vmem)


def _proj_kernel(x_ref, w_ref, wt_ref, o_ref, ot_ref):
    @pl.when(pl.program_id(1) == 0)
    def _():
        ot_ref[0] = lax.dot_general(wt_ref[...], x_ref[...], (((1,), (1,)), ((), ())), preferred_element_type=F32)

    o_ref[...] = jnp.dot(x_ref[...], w_ref[...], preferred_element_type=F32).astype(o_ref.dtype)


def input_projection(x, w_nat, w_t, tm, tn):
    b, t, k = x.shape
    m = b * t
    n = w_nat.shape[1]
    r = w_t.shape[0]
    tps = t // tm
    return pl.pallas_call(
        _proj_kernel,
        grid=(m // tm, n // tn),
        in_specs=[pl.BlockSpec((tm, k), lambda i, j: (i, 0)), pl.BlockSpec((k, tn), lambda i, j: (0, j)),
                  pl.BlockSpec((r, k), lambda i, j: (0, 0))],
        out_specs=[pl.BlockSpec((tm, tn), lambda i, j: (i, j)),
                   pl.BlockSpec((1, r, tm), lambda i, j: (i // tps, 0, i % tps))],
        out_shape=[jax.ShapeDtypeStruct((m, n), BF16), jax.ShapeDtypeStruct((b, r, t), F32)],
        compiler_params=_cparams(2, BIG_VMEM_LIMIT),
        name="input_projection",
    )(x.reshape(m, k), w_nat, w_t)


def _cmp_kernel(c_ref, pos_ref, w1_ref, b1_ref, w2_ref, w2t_ref, b2_ref, b2t_ref, on_ref, ot_ref):
    half = CMP_STRIDE * HEAD_DIM
    c = c_ref[0, 0, 0].astype(BF16)
    w1 = w1_ref[0]
    a = jnp.dot(c, w1[:half], preferred_element_type=F32)
    bm = jnp.dot(c, w1[half:], preferred_element_type=F32)
    nch = a.shape[0]
    bm = pltpu.roll(bm, nch - 1, 0)
    posb = jnp.dot(pos_ref[0].astype(BF16), w1, preferred_element_type=F32)[0:1]
    hid = jax.nn.gelu(a + bm + posb + b1_ref[0])
    hb = hid.astype(BF16)
    on_ref[0, 0, 0] = jnp.dot(hb, w2_ref[0], preferred_element_type=F32) + b2_ref[0]
    ot_ref[0, 0, 0] = lax.dot_general(w2t_ref[0], hb, (((1,), (1,)), ((), ())),
                                      preferred_element_type=F32) + b2t_ref[0]


def nsa_compress_kv(chunks, pos, w1, b1, w2, b2):
    _, b, g, nch, cw = chunks.shape
    pos_flat = jnp.broadcast_to(pos.reshape(2, 1, CMP_LEN * HEAD_DIM), (2, 8, CMP_LEN * HEAD_DIM))
    w1b = w1.astype(BF16)
    w2b = w2.astype(BF16)
    w2t = jnp.swapaxes(w2, 1, 2).astype(BF16)
    b1r = b1.reshape(2, 1, CMP_HIDDEN)
    b2r = b2.reshape(2, 1, HEAD_DIM)
    b2t = b2.reshape(2, HEAD_DIM, 1)
    sel = lambda kv, i, j: (kv, 0, 0)
    return pl.pallas_call(
        _cmp_kernel,
        grid=(2, b, g),
        in_specs=[
            pl.BlockSpec((1, 1, 1, nch, cw), lambda kv, i, j: (kv, i, j, 0, 0)),
            pl.BlockSpec((1, 8, CMP_LEN * HEAD_DIM), sel),
            pl.BlockSpec((1, CMP_LEN * HEAD_DIM, CMP_HIDDEN), sel),
            pl.BlockSpec((1, 1, CMP_HIDDEN), sel),
            pl.BlockSpec((1, CMP_HIDDEN, HEAD_DIM), sel),
            pl.BlockSpec((1, HEAD_DIM, CMP_HIDDEN), sel),
            pl.BlockSpec((1, 1, HEAD_DIM), sel),
            pl.BlockSpec((1, HEAD_DIM, 1), sel),
        ],
        out_specs=[
            pl.BlockSpec((1, 1, 1, nch, HEAD_DIM), lambda kv, i, j: (kv, i, j, 0, 0)),
            pl.BlockSpec((1, 1, 1, HEAD_DIM, nch), lambda kv, i, j: (kv, i, j, 0, 0)),
        ],
        out_shape=[
            jax.ShapeDtypeStruct((2, b, g, nch, HEAD_DIM), F32),
            jax.ShapeDtypeStruct((2, b, g, HEAD_DIM, nch), F32),
        ],
        compiler_params=_cparams(3),
        name="nsa_compress",
    )(chunks, pos_flat, w1b, b1r, w2b, w2t, b2r, b2t)


def _accumulate(s, mt, vt, m, acc):
    m_new = jnp.maximum(m, mt)
    alpha = jnp.exp2(m - m_new)
    p = jnp.exp2(s - m_new).astype(BF16)
    return m_new, alpha * acc + jnp.dot(vt, p, preferred_element_type=F32)


def _nsa_kernel(qt_ref, ng_ref, kc_ref, vct_ref, kp_ref, vst_ref, vwt_ref, e_ref, tri_ref, wpat_ref,
                cpat_ref, o_ref, bias_ref, w_ref, sbuf_ref, oc_ref, pg_ref, ms_ref, as_ref):
    nq = Q_BLOCK
    nsub = qt_ref.shape[2] // nq
    ncol = NSA_GROUP * nq
    n_sel = bias_ref.shape[1]
    ncp = kc_ref.shape[2]
    kw = 2 * HEAD_DIM
    bpt = SEL_TK // SLC_BLOCK
    n_tiles_total = kp_ref.shape[1] // SEL_TK
    qb_first = pl.program_id(2) * nsub
    wlen = WINDOW + nq

    def prepare(u):
        qb = qb_first + u
        q0 = pl.multiple_of(qb * nq, nq)
        qt = qt_ref[0, :, u * nq:(u + 1) * nq] * (HEAD_DIM ** -0.5 * LOG2E)
        qs = jnp.concatenate([qt[r * HEAD_DIM:(r + 1) * HEAD_DIM, :] for r in range(NSA_GROUP)], axis=1)
        zq = jnp.zeros_like(qs)
        q_c = qs.astype(BF16)
        q_slc = jnp.concatenate([qs, zq], axis=0).astype(BF16)
        q_win = jnp.concatenate([zq, qs], axis=0).astype(BF16)
        tq = q0 + (lax.broadcasted_iota(jnp.int32, (1, ncol), 1) & (nq - 1))

        coff = pl.multiple_of(ncp - qb * (nq // CMP_STRIDE), nq // CMP_STRIDE)
        tq1 = q0 + lax.broadcasted_iota(jnp.int32, (1, nq), 1)
        cur = jnp.right_shift(tq1, 6).astype(F32)
        first_diag = (2 * qb).astype(F32)
        pg_ref[u, 0:PG_PAD, :] = jnp.zeros((PG_PAD, nq), F32)
        crow = min(CMP_CLASS_ROWS, ncp)
        cls = qb // (crow // (nq // CMP_STRIDE))
        for c in range(ncp // crow):
            nc_c = crow * (c + 1)
            ns_c = nc_c * CMP_STRIDE // SLC_BLOCK

            @pl.when(cls == c)
            def _(nc_c=nc_c, ns_c=ns_c):
                s_c = (jnp.dot(kc_ref[0, 0, 0:nc_c, :], q_c, preferred_element_type=F32)
                       + jnp.concatenate([cpat_ref[pl.ds(coff, nc_c), :]] * NSA_GROUP, axis=1))
                m_c = jnp.max(s_c, axis=0, keepdims=True)
                e_c = jnp.exp2(s_c - m_c)
                l_c = jnp.sum(e_c, axis=0, keepdims=True)
                p_c = e_c * jnp.where(tq >= CMP_LEN - 1, 1.0 / l_c, 0.0)
                oc_ref[u] = jnp.dot(vct_ref[0, 0, :, 0:nc_c], p_c.astype(BF16), preferred_element_type=F32)

                p_grp = p_c[:, 0:nq]
                for r in range(1, NSA_GROUP):
                    p_grp = p_grp + p_c[:, r * nq:(r + 1) * nq]
                pg_ref[u, PG_PAD:PG_PAD + nc_c, :] = p_grp
                ratio = SLC_BLOCK // CMP_STRIDE
                taps = [pg_ref[u, pl.ds(PG_PAD - 1 + k, ns_c, stride=ratio), :] for k in range(ratio + 1)]
                score = taps[0] + 2.0 * (taps[1] + taps[2] + taps[3]) + taps[4]
                j_io = lax.broadcasted_iota(jnp.int32, (ns_c, nq), 0).astype(F32)
                score = jnp.where(j_io <= cur, score, -1.0)
                score = jnp.where(j_io == 0.0, PICKED, score)
                score = jnp.where(j_io == cur, PICKED, score)
                score = jnp.where(j_io == cur - 1.0, PICKED, score)
                for _ in range(SLC_TOPN - 3):
                    mx = jnp.max(score, axis=0, keepdims=True)
                    first = jnp.min(jnp.where(score == mx, j_io, float(ns_c)), axis=0, keepdims=True)
                    score = jnp.where(j_io == first, PICKED, score)
                bias = jnp.where(j_io < first_diag, jnp.where(score == PICKED, 0.0, NEG), NEG)
                bias_ref[u, 0:ns_c, :] = bias.astype(BF16)
                if ns_c < n_sel:
                    bias_ref[u, ns_c:, :] = jnp.full((n_sel - ns_c, nq), NEG, BF16)

        o_c = oc_ref[u]

        wstart = pl.multiple_of(jnp.maximum(q0 - WINDOW, 0), nq)
        woff = pl.multiple_of(jnp.maximum(WINDOW - q0, 0), nq)
        s_w = jnp.dot(kp_ref[0, pl.ds(wstart, wlen), :], q_win, preferred_element_type=F32)
        s_w = s_w + jnp.concatenate([wpat_ref[pl.ds(woff, wlen), :]] * NSA_GROUP, axis=1)
        p_w = jnp.exp2(s_w - jnp.max(s_w, axis=0, keepdims=True)).astype(BF16)
        acc_w = jnp.dot(vwt_ref[0, :, pl.ds(wstart, wlen)], p_w, preferred_element_type=F32)
        o_w = acc_w[0:HEAD_DIM] / acc_w[HEAD_DIM:HEAD_DIM + 1]

        for slot in range(2):
            w_ref[slot, u, 0:kw, :] = q_slc
            w_ref[slot, u, kw:, :] = jnp.zeros((w_ref.shape[2] - kw, ncol), BF16)
        return q0, q_slc, o_c, o_w

    prepared = [prepare(u) for u in range(nsub)]

    def tile_start(j):
        jc = jnp.minimum(j, n_tiles_total - 1)
        return jc, pl.multiple_of(jc * SEL_TK, SEL_TK)

    def produce(j, slot):
        jc, k0 = tile_start(j)
        lhs = jnp.concatenate([kp_ref[0, pl.ds(k0, SEL_TK), :], e_ref[...]], axis=1)
        mts = []
        for u in range(nsub):
            b16 = bias_ref[u, pl.ds(pl.multiple_of(jc * bpt, bpt), bpt), :]
            w_ref[slot, u, kw:kw + bpt, :] = jnp.concatenate([b16] * NSA_GROUP, axis=1)
            s = jnp.dot(lhs, w_ref[slot, u], preferred_element_type=F32)
            sbuf_ref[slot, u] = s
            mts.append(jnp.max(s, axis=0, keepdims=True))
        return tuple(mts)

    def consume(j, slot, mts, state):
        _, k0 = tile_start(j)
        vt = vst_ref[0, :, pl.ds(k0, SEL_TK)]
        return tuple(_accumulate(sbuf_ref[slot, u], mts[u], vt, *state[u]) for u in range(nsub))

    def pair(i, carry):
        mt_a, state = carry
        mt_b = produce(2 * i + 1, 1)
        state = consume(2 * i, 0, mt_a, state)
        mt_a = produce(2 * i + 2, 0)
        state = consume(2 * i + 1, 1, mt_b, state)
        return mt_a, state

    def park(state):
        for u in range(nsub):
            ms_ref[u] = state[u][0]
            as_ref[u] = state[u][1]

    n_main = jnp.maximum(((qb_first + nsub - 1) * nq + (SEL_TK - 1)) // SEL_TK, 1)
    n_pairs = (n_main + 1) // 2
    state = tuple((jnp.full((1, ncol), NEG, F32), jnp.zeros((vst_ref.shape[1], ncol), F32)) for _ in range(nsub))
    mt_a, state = lax.fori_loop(0, n_pairs - 1, pair, (produce(0, 0), state))
    tail = 2 * (n_pairs - 1)
    odd = (n_main & 1) == 1

    @pl.when(odd)
    def _():
        park(consume(tail, 0, mt_a, state))

    @pl.when(jnp.logical_not(odd))
    def _():
        mt_b = produce(tail + 1, 1)
        park(consume(tail + 1, 1, mt_b, consume(tail, 0, mt_a, state)))

    state = tuple((ms_ref[u], as_ref[u]) for u in range(nsub))

    for u in range(nsub):
        q0, q_slc, o_c, o_w = prepared[u]
        s_d = jnp.dot(kp_ref[0, pl.ds(q0, nq), :], q_slc, preferred_element_type=F32) + tri_ref[...]
        _, acc_s = _accumulate(s_d, jnp.max(s_d, axis=0, keepdims=True), vst_ref[0, :, pl.ds(q0, nq)], *state[u])
        o_s = acc_s[0:HEAD_DIM] / acc_s[HEAD_DIM:HEAD_DIM + 1]
        gts = jax.nn.sigmoid(ng_ref[0, :, u * nq:(u + 1) * nq])
        rows = []
        for r in range(NSA_GROUP):
            sl = slice(r * nq, (r + 1) * nq)
            rows.append(gts[3 * r:3 * r + 1, :] * o_c[:, sl] + gts[3 * r + 1:3 * r + 2, :] * o_s[:, sl]
                        + gts[3 * r + 2:3 * r + 3, :] * o_w[:, sl])
        y_t = jnp.concatenate(rows, axis=0)
        o_ref[0, u * nq:(u + 1) * nq, :] = y_t.T.astype(o_ref.dtype)


def _nsa_constants(ncp):
    nq = Q_BLOCK
    ql = jnp.arange(nq)[None, :]
    r = jnp.arange(SEL_TK)[:, None]
    e = (r // SLC_BLOCK == jnp.arange(2 * HEAD_DIM)[None, :]).astype(BF16)
    rd = jnp.arange(nq)[:, None]
    tri = jnp.tile(jnp.where(rd <= ql, 0.0, NEG).astype(F32), (1, NSA_GROUP))
    rw = jnp.arange(2 * WINDOW + nq)[:, None]
    wpat = jnp.where((rw > ql) & (rw <= ql + WINDOW), 0.0, NEG).astype(F32)
    rc = jnp.arange(2 * ncp)[:, None] - ncp
    cpat = jnp.where(CMP_STRIDE * rc + (CMP_LEN - 1) <= ql, 0.0, NEG).astype(F32)
    return e, tri, wpat, cpat


def nsa_attention(proj_t, kc, vct, hn, vt_pack):
    b, _, t = proj_t.shape
    ncp = kc.shape[2]
    n_sel = t // SLC_BLOCK
    gq = NSA_GROUP * HEAD_DIM
    qstep = NSA_QSUB * Q_BLOCK
    e, tri, wpat, cpat = _nsa_constants(ncp)
    const2 = lambda i, g, q: (0, 0)
    return pl.pallas_call(
        _nsa_kernel,
        grid=(b, NSA_KV_GROUPS, t // qstep),
        in_specs=[
            pl.BlockSpec((1, gq, qstep), lambda i, g, q: (i, g, q)),
            pl.BlockSpec((1, 16, qstep), lambda i, g, q: (i, TR_NG // 16 + g, q)),
            pl.BlockSpec((1, 1, ncp, HEAD_DIM), lambda i, g, q: (i, g, 0, 0)),
            pl.BlockSpec((1, 1, HEAD_DIM, ncp), lambda i, g, q: (i, g, 0, 0)),
            pl.BlockSpec((1, t, 2 * HEAD_DIM), lambda i, g, q: (i, 0, NAT_KP // (2 * HEAD_DIM) + g)),
            pl.BlockSpec((1, VT_ROWS, t), lambda i, g, q: (i, g, 0)),
            pl.BlockSpec((1, VT_ROWS, t), lambda i, g, q: (i, NSA_KV_GROUPS + g, 0)),
            pl.BlockSpec(e.shape, const2),
            pl.BlockSpec(tri.shape, const2),
            pl.BlockSpec(wpat.shape, const2),
            pl.BlockSpec(cpat.shape, const2),
        ],
        out_specs=pl.BlockSpec((1, qstep, gq), lambda i, g, q: (i, q, g)),
        out_shape=jax.ShapeDtypeStruct((b, t, NSA_HEADS * HEAD_DIM), BF16),
        scratch_shapes=[pltpu.VMEM((NSA_QSUB, n_sel, Q_BLOCK), BF16),
                        pltpu.VMEM((2, NSA_QSUB, 4 * HEAD_DIM, NSA_GROUP * Q_BLOCK), BF16),
                        pltpu.VMEM((2, NSA_QSUB, SEL_TK, NSA_GROUP * Q_BLOCK), F32),
                        pltpu.VMEM((NSA_QSUB, HEAD_DIM, NSA_GROUP * Q_BLOCK), F32),
                        pltpu.VMEM((NSA_QSUB, PG_PAD + ncp, Q_BLOCK), F32),
                        pltpu.VMEM((NSA_QSUB, 1, NSA_GROUP * Q_BLOCK), F32),
                        pltpu.VMEM((NSA_QSUB, VT_ROWS, NSA_GROUP * Q_BLOCK), F32)],
        compiler_params=_cparams(3),
        name="nsa_attention",
    )(proj_t, proj_t, kc, vct, hn, vt_pack, vt_pack, e, tri, wpat, cpat)


def _ret_kernel(rq_ref, rk_ref, rv_ref, rg_ref, cos_ref, sin_ref, dec_ref, qd_ref, kd_ref, cd_ref,
                gg_ref, gb_ref, o_ref, st_ref):
    c = RET_CHUNK
    hd = RET_HEAD_DIM

    @pl.when(pl.program_id(1) == 0)
    def _():
        st_ref[...] = jnp.zeros_like(st_ref)

    n_chunks = rq_ref.shape[1] // c

    def chunk(ci, _):
        r0 = pl.multiple_of(ci * c, c)
        cos = cos_ref[pl.ds(r0, c), :]
        sin = sin_ref[pl.ds(r0, c), :]
        for h in range(RET_HEADS):
            hs = slice(h * hd, (h + 1) * hd)
            q = rq_ref[0, pl.ds(r0, c), hs].astype(F32)
            k = rk_ref[0, pl.ds(r0, c), hs].astype(F32)
            v = rv_ref[0, pl.ds(r0, c), hs]
            q = q * cos + pltpu.roll(q, hd // 2, 1) * sin
            k = (k * cos + pltpu.roll(k, hd // 2, 1) * sin) * (hd ** -0.5)
            qb = q.astype(BF16)
            kb = k.astype(BF16)
            vb = v.astype(BF16)
            inner = lax.dot_general(qb, kb, (((1,), (1,)), ((), ())), preferred_element_type=F32) * dec_ref[h]
            st = st_ref[h]
            o = (jnp.dot(inner.astype(BF16), vb, preferred_element_type=F32)
                 + jnp.dot(qb, st.astype(BF16), preferred_element_type=F32) * qd_ref[h])
            kdt = (k * kd_ref[h]).T.astype(BF16)
            st_ref[h] = st * cd_ref[h] + jnp.dot(kdt, vb, preferred_element_type=F32)
            mu = jnp.mean(o, axis=-1, keepdims=True)
            d = o - mu
            var = jnp.mean(d * d, axis=-1, keepdims=True)
            y = d * lax.rsqrt(var + LN_EPS) * gg_ref[:, hs] + gb_ref[:, hs]
            o_ref[0, pl.ds(r0, c), hs] = (jax.nn.silu(rg_ref[0, pl.ds(r0, c), hs].astype(F32)) * y).astype(o_ref.dtype)
        return 0

    lax.fori_loop(0, n_chunks, chunk, 0)


def retention_branch(hn, gn_g, gn_b, tr):
    b, t, _ = hn.shape
    w = BRANCH_WIDTH
    hd = RET_HEAD_DIM
    c = RET_CHUNK
    inv = ROPE_BASE ** (-jnp.arange(0, hd, 2, dtype=F32) / hd)
    ang = jnp.arange(t, dtype=F32)[:, None] * inv[None, :]
    cos2 = jnp.concatenate([jnp.cos(ang), jnp.cos(ang)], axis=1)
    sin2 = jnp.concatenate([-jnp.sin(ang), jnp.sin(ang)], axis=1)
    log_g = jnp.log(1.0 - 2.0 ** (-5.0 - jnp.arange(RET_HEADS, dtype=F32)))
    idx = jnp.arange(c, dtype=F32)
    diff = idx[:, None] - idx[None, :]
    decay_in = jnp.where(diff >= 0, jnp.exp(log_g[:, None, None] * jnp.maximum(diff, 0.0)), 0.0)
    ones = jnp.ones((RET_HEADS, c, hd), F32)
    q_dec = jnp.exp(log_g[:, None] * (idx + 1.0))[:, :, None] * ones
    k_dec = jnp.exp(log_g[:, None] * (c - 1.0 - idx))[:, :, None] * ones
    c_dec = jnp.exp(log_g * c)[:, None, None] * jnp.ones((RET_HEADS, hd, hd), F32)
    col = lambda cc: (lambda i, j: (i, j, cc // w))
    full3 = lambda i, j: (0, 0, 0)
    return pl.pallas_call(
        _ret_kernel,
        grid=(b, t // tr),
        in_specs=[
            pl.BlockSpec((1, tr, w), col(NAT_RQ)),
            pl.BlockSpec((1, tr, w), col(NAT_RK)),
            pl.BlockSpec((1, tr, w), col(NAT_RV)),
            pl.BlockSpec((1, tr, w), col(NAT_RG)),
            pl.BlockSpec((tr, hd), lambda i, j: (j, 0)),
            pl.BlockSpec((tr, hd), lambda i, j: (j, 0)),
            pl.BlockSpec((RET_HEADS, c, c), full3),
            pl.BlockSpec((RET_HEADS, c, hd), full3),
            pl.BlockSpec((RET_HEADS, c, hd), full3),
            pl.BlockSpec((RET_HEADS, hd, hd), full3),
            pl.BlockSpec((1, w), lambda i, j: (0, 0)),
            pl.BlockSpec((1, w), lambda i, j: (0, 0)),
        ],
        out_specs=pl.BlockSpec((1, tr, w), lambda i, j: (i, j, 0)),
        out_shape=jax.ShapeDtypeStruct((b, t, w), BF16),
        scratch_shapes=[pltpu.VMEM((RET_HEADS, hd, hd), F32)],
        compiler_params=_cparams(2),
        name="retention",
    )(hn, hn, hn, hn, cos2, sin2, decay_in, q_dec, k_dec, c_dec, gn_g.reshape(1, w), gn_b.reshape(1, w))


def _layer_norm(z, g, b):
    mu = jnp.mean(z, axis=-1, keepdims=True)
    d = z - mu
    var = jnp.mean(d * d, axis=-1, keepdims=True)
    return d * lax.rsqrt(var + LN_EPS) * g + b


def _merge_kernel(tiles_per_seq, x_ref, ya_ref, cb_ref, cc_ref, ch_ref, pc_ref, ph_ref, cw_ref, yc_ref, mg_ref,
                  wbr_ref, wout_ref, g_ref, b_ref, o_ref, ob_ref):
    first = (pl.program_id(0) % tiles_per_seq) == 0
    u = cc_ref[...].astype(F32) * ch_ref[...].astype(F32)
    prev = pc_ref[...].astype(F32) * ph_ref[...].astype(F32)
    prev = jnp.where(first, 0.0, prev)
    row = lax.broadcasted_iota(jnp.int32, u.shape, 0)
    p1 = prev[CONV_HALO - 1:CONV_HALO, :]
    p2 = prev[CONV_HALO - 2:CONV_HALO - 1, :]
    u1 = jnp.where(row == 0, p1, pltpu.roll(u, 1, 0))
    u2 = jnp.where(row == 0, p2, jnp.where(row == 1, p1, pltpu.roll(u, 2, 0)))
    cw = cw_ref[...]
    y_b = cb_ref[...].astype(F32) * (cw[0:1, :] * u2 + cw[1:2, :] * u1 + cw[2:3, :] * u)

    mix = None
    for c, y in enumerate((ya_ref[...], y_b.astype(BF16), yc_ref[...])):
        proj = jnp.dot(y, wbr_ref[c], preferred_element_type=F32)
        gate = jax.nn.sigmoid(mg_ref[:, c * D_MODEL:(c + 1) * D_MODEL].astype(F32))
        mix = gate * proj if mix is None else mix + gate * proj
    m = jnp.dot(mix.astype(BF16), wout_ref[...], preferred_element_type=F32)
    out = _layer_norm(ALPHA * x_ref[...] + m, g_ref[...], b_ref[...])
    o_ref[...] = out
    ob_ref[...] = out.astype(BF16)


def merge_branches(x, ya, yc, hn, conv_w, w_br, w_out, g, b, tm, t):
    m = x.shape[0]
    d = D_MODEL
    w = BRANCH_WIDTH
    row = lambda i: (i, 0)
    col = lambda c: (lambda i: (i, c // w))
    hal = lambda c: (lambda i: (jnp.maximum(i * (tm // CONV_HALO) - 1, 0), c // w))
    wpad = jnp.zeros((8, w), F32).at[:CONV_K].set(conv_w)
    return pl.pallas_call(
        functools.partial(_merge_kernel, t // tm),
        grid=(m // tm,),
        in_specs=[
            pl.BlockSpec((tm, d), row),
            pl.BlockSpec((tm, w), row),
            pl.BlockSpec((tm, w), col(NAT_CB)),
            pl.BlockSpec((tm, w), col(NAT_CC)),
            pl.BlockSpec((tm, w), col(NAT_CH)),
            pl.BlockSpec((CONV_HALO, w), hal(NAT_CC)),
            pl.BlockSpec((CONV_HALO, w), hal(NAT_CH)),
            pl.BlockSpec((8, w), lambda i: (0, 0)),
            pl.BlockSpec((tm, w), row),
            pl.BlockSpec((tm, 3 * d), row),
            pl.BlockSpec((3, w, d), lambda i: (0, 0, 0)),
            pl.BlockSpec((d, d), lambda i: (0, 0)),
            pl.BlockSpec((1, d), lambda i: (0, 0)),
            pl.BlockSpec((1, d), lambda i: (0, 0)),
        ],
        out_specs=[pl.BlockSpec((tm, d), row), pl.BlockSpec((tm, d), row)],
        out_shape=[jax.ShapeDtypeStruct((m, d), F32), jax.ShapeDtypeStruct((m, d), BF16)],
        compiler_params=_cparams(1),
        name="merge",
    )(x, ya, hn, hn, hn, hn, hn, wpad, yc, hn, w_br.astype(BF16), w_out.astype(BF16), g.reshape(1, d),
      b.reshape(1, d))


def _cast3_kernel(g_ref, u_ref, d_ref, og_ref, ou_ref, od_ref):
    og_ref[...] = g_ref[0].astype(og_ref.dtype)
    ou_ref[...] = u_ref[0].astype(ou_ref.dtype)
    od_ref[...] = d_ref[0].astype(od_ref.dtype)


def ffn_weights_bf16(w_gate, w_up, w_down, layer):
    _, e, d, ff = w_gate.shape
    rg, rd = d // CAST_STEPS, ff // CAST_STEPS
    up_spec = pl.BlockSpec((1, 1, rg, ff), lambda i, j: (layer, i, j, 0))
    return pl.pallas_call(
        _cast3_kernel,
        grid=(e, CAST_STEPS),
        in_specs=[up_spec, up_spec, pl.BlockSpec((1, 1, rd, d), lambda i, j: (layer, i, j, 0))],
        out_specs=[pl.BlockSpec((1, rg, ff), lambda i, j: (i, j, 0)), pl.BlockSpec((1, rg, ff), lambda i, j: (i, j, 0)),
                   pl.BlockSpec((1, rd, d), lambda i, j: (i, j, 0))],
        out_shape=[jax.ShapeDtypeStruct((e, d, ff), BF16), jax.ShapeDtypeStruct((e, d, ff), BF16),
                   jax.ShapeDtypeStruct((e, ff, d), BF16)],
        compiler_params=_cparams(2),
        name="cast_bf16",
    )(w_gate, w_up, w_down)


def _swiglu_chunks(x, wg_ref, wu_ref, wd_ref, tf):
    acc = None
    for c in range(wg_ref.shape[2] // tf):
        sl = slice(c * tf, (c + 1) * tf)
        gte = jnp.dot(x, wg_ref[0, :, sl], preferred_element_type=F32)
        up = jnp.dot(x, wu_ref[0, :, sl], preferred_element_type=F32)
        h = (jax.nn.silu(gte) * up).astype(BF16)
        part = jnp.dot(h, wd_ref[0, sl, :], preferred_element_type=F32)
        acc = part if acc is None else acc + part
    return acc


def _mlp_kernel(tf, te_ref, tv_ref, x_ref, wg_ref, wu_ref, wd_ref, o_ref):
    i = pl.program_id(0)

    @pl.when(tv_ref[i] > 0)
    def _():
        o_ref[...] = _swiglu_chunks(x_ref[...], wg_ref, wu_ref, wd_ref, tf).astype(o_ref.dtype)

    @pl.when(tv_ref[i] == 0)
    def _():
        o_ref[...] = jnp.zeros_like(o_ref)


def _resident(block_shape, index_map):
    return pl.BlockSpec(block_shape, index_map, pipeline_mode=pl.Buffered(1))


def grouped_swiglu(xs, w_gate, w_up, w_down, tile_expert, tile_valid, tm, tf):
    n, d = xs.shape
    ff = w_gate.shape[2]
    grid_spec = pltpu.PrefetchScalarGridSpec(
        num_scalar_prefetch=2,
        grid=(n // tm,),
        in_specs=[
            pl.BlockSpec((tm, d), lambda i, te, tv: (i, 0)),
            _resident((1, d, ff), lambda i, te, tv: (te[i], 0, 0)),
            _resident((1, d, ff), lambda i, te, tv: (te[i], 0, 0)),
            _resident((1, ff, d), lambda i, te, tv: (te[i], 0, 0)),
        ],
        out_specs=pl.BlockSpec((tm, d), lambda i, te, tv: (i, 0)),
    )
    return pl.pallas_call(
        functools.partial(_mlp_kernel, tf),
        grid_spec=grid_spec,
        out_shape=jax.ShapeDtypeStruct((n, d), BF16),
        compiler_params=_cparams(1, BIG_VMEM_LIMIT),
        name="grouped_swiglu",
    )(tile_expert, tile_valid, xs, w_gate, w_up, w_down)


def _add_ln_kernel(*refs):
    x_ref, sc_ref, *adds, g_ref, b_ref, o_ref, ob_ref = refs
    z = ALPHA * x_ref[...]
    for k, a in enumerate(adds):
        z = z + sc_ref[:, k:k + 1] * a[...].astype(F32)
    out = _layer_norm(z, g_ref[...], b_ref[...])
    o_ref[...] = out
    ob_ref[...] = out.astype(BF16)


def add_layer_norm(x, adds, scales, g, b, tm):
    m, d = x.shape
    row = lambda i: (i, 0)
    return pl.pallas_call(
        _add_ln_kernel,
        grid=(m // tm,),
        in_specs=([pl.BlockSpec((tm, d), row), pl.BlockSpec((tm, len(adds)), row)]
                  + [pl.BlockSpec((tm, d), row)] * len(adds) + [pl.BlockSpec((1, d), lambda i: (0, 0))] * 2),
        out_specs=[pl.BlockSpec((tm, d), row)] * 2,
        out_shape=[jax.ShapeDtypeStruct((m, d), F32), jax.ShapeDtypeStruct((m, d), BF16)],
        compiler_params=_cparams(1),
        name="add_layer_norm",
    )(x, scales, *adds, g.reshape(1, d), b.reshape(1, d))


def _router_kernel(x_ref, wh_ref, wl_ref, o_ref):
    x = x_ref[...]
    xh = x.astype(BF16)
    xl = (x - xh.astype(F32)).astype(BF16)
    wh = wh_ref[...]
    o_ref[...] = (jnp.dot(xh, wh, preferred_element_type=F32) + jnp.dot(xl, wh, preferred_element_type=F32)
                  + jnp.dot(xh, wl_ref[...], preferred_element_type=F32))


def router_logits(x, w_router, tm):
    m, d = x.shape
    wp = jnp.zeros((d, 128), F32).at[:, :N_EXPERTS].set(w_router)
    wh = wp.astype(BF16)
    wl = (wp - wh.astype(F32)).astype(BF16)
    return pl.pallas_call(
        _router_kernel,
        grid=(m // tm,),
        in_specs=[pl.BlockSpec((tm, d), lambda i: (i, 0)), pl.BlockSpec((d, 128), lambda i: (0, 0)),
                  pl.BlockSpec((d, 128), lambda i: (0, 0))],
        out_specs=pl.BlockSpec((tm, 128), lambda i: (i, 0)),
        out_shape=jax.ShapeDtypeStruct((m, 128), F32),
        compiler_params=_cparams(1),
        name="router",
    )(x, wh, wl)


def _dense_ffn_kernel(tf, xb_ref, x_ref, wg_ref, wu_ref, wd_ref, g_ref, b_ref, o_ref, ob_ref):
    acc = _swiglu_chunks(xb_ref[...], wg_ref, wu_ref, wd_ref, tf)
    out = _layer_norm(ALPHA * x_ref[...] + acc, g_ref[...], b_ref[...])
    o_ref[...] = out
    ob_ref[...] = out.astype(BF16)


def dense_ffn(x, xb, w_gate, w_up, w_down, g, b):
    m, d = x.shape
    ff = w_gate.shape[2]
    tm = min(FFN_TM, m)
    row = lambda i: (i, 0)
    return pl.pallas_call(
        functools.partial(_dense_ffn_kernel, FFN_TF),
        grid=(m // tm,),
        in_specs=[
            pl.BlockSpec((tm, d), row),
            pl.BlockSpec((tm, d), row),
            _resident((1, d, ff), lambda i: (0, 0, 0)),
            _resident((1, d, ff), lambda i: (0, 0, 0)),
            _resident((1, ff, d), lambda i: (0, 0, 0)),
            pl.BlockSpec((1, d), lambda i: (0, 0)),
            pl.BlockSpec((1, d), lambda i: (0, 0)),
        ],
        out_specs=[pl.BlockSpec((tm, d), row)] * 2,
        out_shape=[jax.ShapeDtypeStruct((m, d), F32), jax.ShapeDtypeStruct((m, d), BF16)],
        compiler_params=_cparams(1, BIG_VMEM_LIMIT),
        name="dense_ffn",
    )(xb, x, w_gate, w_up, w_down, g.reshape(1, d), b.reshape(1, d))


def moe_ffn(x, xb, w_router, w_gate, w_up, w_down, g, b):
    m = x.shape[0]
    tm = FFN_TM
    logits = router_logits(x, w_router, min(1024, m))[:, :N_EXPERTS]
    top_v, top_i = lax.top_k(logits, TOP_K)
    probs = jax.nn.softmax(top_v, axis=-1)
    e_flat = top_i.reshape(-1)
    onehot = (e_flat[:, None] == jnp.arange(N_EXPERTS)[None, :]).astype(jnp.int32)
    csum = jnp.cumsum(onehot, axis=0)
    counts = csum[-1]
    rank = jnp.take_along_axis(csum, e_flat[:, None], axis=1)[:, 0] - 1
    padded = ((counts + tm - 1) // tm) * tm
    ends = jnp.cumsum(padded)
    starts = ends - padded
    pos = starts[e_flat] + rank
    n_slots = TOP_K * m + N_EXPERTS * tm
    n_tiles = n_slots // tm
    tok = jnp.arange(TOP_K * m, dtype=jnp.int32) // TOP_K
    src = jnp.zeros((n_slots,), jnp.int32).at[pos].set(tok)
    tile_start = jnp.arange(n_tiles, dtype=jnp.int32) * tm
    tile_expert = jnp.minimum(jnp.searchsorted(ends, tile_start, side="right"), N_EXPERTS - 1).astype(jnp.int32)
    tile_valid = (tile_start < ends[-1]).astype(jnp.int32)
    xs = jnp.take(xb, src, axis=0)
    ys = grouped_swiglu(xs, w_gate, w_up, w_down,
                        tile_expert, tile_valid, tm, FFN_TF)
    pos2 = pos.reshape(m, TOP_K)
    y0 = jnp.take(ys, pos2[:, 0], axis=0)
    y1 = jnp.take(ys, pos2[:, 1], axis=0)
    return add_layer_norm(x, [y0, y1], probs, g, b, min(1024, m))


def _pack_in_weights(w):
    d = w.shape[0]
    o = 0
    q = w[:, o:o + 512]; o += 512
    kv = w[:, o:o + 768].reshape(d, 3, 2, NSA_KV_GROUPS, HEAD_DIM); o += 768
    ng = w[:, o:o + 24]; o += 24
    rest = w[:, o:o + 7 * 512]; o += 7 * 512
    mg = w[:, o:]
    kpack = jnp.concatenate([kv[:, 1, 0], kv[:, 2, 0]], axis=-1).reshape(d, 2 * NSA_KV_GROUPS * HEAD_DIM)
    kc = kv[:, 0, 0].reshape(d, NSA_KV_GROUPS * HEAD_DIM)
    vc = kv[:, 0, 1].reshape(d, NSA_KV_GROUPS * HEAD_DIM)
    w_nat = jnp.concatenate([mg, rest, kpack, kc, vc], axis=1).astype(BF16)
    vs = kv[:, 1, 1].reshape(d, NSA_KV_GROUPS * HEAD_DIM)
    vw = kv[:, 2, 1].reshape(d, NSA_KV_GROUPS * HEAD_DIM)
    ngp = jnp.pad(ng.reshape(d, NSA_KV_GROUPS, NSA_GROUP * 3), ((0, 0), (0, 0), (0, 4))).reshape(d, 32)
    w_t = jnp.concatenate([q, vs, vw, ngp], axis=1).T.astype(BF16)
    return w_nat, w_t


def token_mixer_ln(x, xb, w_in, cmp_pos, cmp_w1, cmp_b1, cmp_w2, cmp_b2, conv_w, gn_g, gn_b, w_br, w_out, ln_g, ln_b):
    b, t, d = x.shape
    m = b * t
    w_nat, w_t = _pack_in_weights(w_in)
    x2 = x.reshape(m, d)
    hn, proj_t = input_projection(xb, w_nat, w_t, min(2048, t), 1024)
    hn = hn.reshape(b, t, NAT_COLS)
    nch = t // CMP_STRIDE
    raw = hn[:, :, NAT_KC:NAT_KC + 2 * NSA_KV_GROUPS * HEAD_DIM].reshape(b, t, 2, NSA_KV_GROUPS, HEAD_DIM)
    chunks = raw.transpose(2, 0, 3, 1, 4).reshape(2, b, NSA_KV_GROUPS, nch, CMP_STRIDE * HEAD_DIM)
    cn, ct = nsa_compress_kv(chunks, cmp_pos, cmp_w1, cmp_b1, cmp_w2, cmp_b2)
    kc = cn[0].astype(BF16)
    vct = ct[1].astype(BF16)
    vt4 = proj_t[:, TR_VS:TR_NG, :].astype(BF16).reshape(b, 4, HEAD_DIM, t)
    ones_pad = jnp.zeros((b, 4, VT_ROWS - HEAD_DIM, t), BF16).at[:, :, 0].set(1.0)
    vt_pack = jnp.concatenate([vt4, ones_pad], axis=2).reshape(b, 4 * VT_ROWS, t)
    y_a = nsa_attention(proj_t, kc, vct, hn, vt_pack)
    y_c = retention_branch(hn, gn_g, gn_b, min(1024, t))
    return merge_branches(x2, y_a.reshape(m, -1), y_c.reshape(m, -1), hn.reshape(m, NAT_COLS), conv_w,
                          w_br, w_out, ln_g, ln_b, min(512, t), t)


def kernel(x, w_in, cmp_pos, cmp_w1, cmp_b1, cmp_w2, cmp_b2, conv_w, ret_gn_g, ret_gn_b, w_br, w_out,
           ln1_g, ln1_b, ln2_g, ln2_b, ffn_gate, ffn_up, ffn_down, moe_router, moe_gate, moe_up, moe_down):
    b, t, d = x.shape
    xb = x.astype(BF16)
    for l in range(DEPTH):
        x2, x2b = token_mixer_ln(x, xb, w_in[l], cmp_pos[l], cmp_w1[l], cmp_b1[l], cmp_w2[l], cmp_b2[l], conv_w[l],
                                 ret_gn_g[l], ret_gn_b[l], w_br[l], w_out[l], ln1_g[l], ln1_b[l])
        if l % 2 == 0:
            wg, wu, wd = ffn_weights_bf16(ffn_gate[:, None], ffn_up[:, None], ffn_down[:, None], l // 2)
            x2, x2b = dense_ffn(x2, x2b, wg, wu, wd, ln2_g[l], ln2_b[l])
        else:
            wg, wu, wd = ffn_weights_bf16(moe_gate, moe_up, moe_down, l // 2)
            x2, x2b = moe_ffn(x2, x2b, moe_router[l // 2], wg, wu, wd, ln2_g[l], ln2_b[l])
        x, xb = x2.reshape(b, t, d), x2b.reshape(b, t, d)
    return x
```

```python
import functools

import jax
import jax.numpy as jnp
from jax import lax
from jax.experimental import pallas as pl
from jax.experimental.pallas import tpu as pltpu

F32 = jnp.float32
BF16 = jnp.bfloat16

D_MODEL = 1024
DEPTH = 4
BRANCH_WIDTH = 512
HEAD_DIM = 64
NSA_HEADS = 8
NSA_KV_GROUPS = 2
NSA_GROUP = 4
CMP_LEN = 32
CMP_STRIDE = 16
SLC_BLOCK = 64
SLC_TOPN = 16
WINDOW = 512
CMP_HIDDEN = 256
Q_BLOCK = 128
CONV_K = 3
CONV_HALO = 16
RET_HEADS = 4
RET_HEAD_DIM = 128
RET_CHUNK = 256
ROPE_BASE = 10000.0
D_FF = 3584
N_EXPERTS = 8
TOP_K = 2
LN_EPS = 1e-5
ALPHA = (2.0 * DEPTH) ** 0.25

NAT_MG = 0
NAT_CB = 3072
NAT_CC = 3584
NAT_CH = 4096
NAT_RQ = 4608
NAT_RK = 5120
NAT_RV = 5632
NAT_RG = 6144
NAT_KP = 6656
NAT_KC = 6912
NAT_VC = 7040
NAT_COLS = 7168
TR_Q = 0
TR_VS = 512
TR_VW = 640
TR_NG = 768
TR_ROWS = 800

NEG = -1e30
SEL_TK = 1024
LOG2E = 1.4426950408889634
NSA_QSUB = 2
CMP_CLASS_ROWS = 128
PG_PAD = 8
PICKED = -3e38
VT_ROWS = 80
ROW_TILE = 1024
MERGE_TM = 512
PROJ_TM = 2048
PROJ_TN = 1024
FFN_TM = 512
FFN_TF = 1792
CAST_STEPS = 16
VMEM_LIMIT = 48 * 1024 * 1024
BIG_VMEM_LIMIT = 56 * 1024 * 1024


def _cparams(n_axes, vmem=VMEM_LIMIT):
    return pltpu.CompilerParams(dimension_semantics=("arbitrary",) * n_axes, vmem_limit_bytes=vmem)


def _proj_kernel(x_ref, w_ref, wt_ref, o_ref, ot_ref):
    @pl.when(pl.program_id(1) == 0)
    def _():
        ot_ref[0] = lax.dot_general(wt_ref[...], x_ref[...], (((1,), (1,)), ((), ())), preferred_element_type=F32)

    o_ref[...] = jnp.dot(x_ref[...], w_ref[...], preferred_element_type=F32).astype(o_ref.dtype)


def input_projection(x, w_nat, w_t, tm, tn):
    b, t, k = x.shape
    m = b * t
    n = w_nat.shape[1]
    r = w_t.shape[0]
    tps = t // tm
    return pl.pallas_call(
        _proj_kernel,
        grid=(m // tm, n // tn),
        in_specs=[pl.BlockSpec((tm, k), lambda i, j: (i, 0)), pl.BlockSpec((k, tn), lambda i, j: (0, j)),
                  pl.BlockSpec((r, k), lambda i, j: (0, 0))],
        out_specs=[pl.BlockSpec((tm, tn), lambda i, j: (i, j)),
                   pl.BlockSpec((1, r, tm), lambda i, j: (i // tps, 0, i % tps))],
        out_shape=[jax.ShapeDtypeStruct((m, n), BF16), jax.ShapeDtypeStruct((b, r, t), F32)],
        compiler_params=_cparams(2, BIG_VMEM_LIMIT),
        name="input_projection",
    )(x.reshape(m, k), w_nat, w_t)


def _cmp_kernel(c_ref, pos_ref, w1_ref, b1_ref, w2_ref, w2t_ref, b2_ref, b2t_ref, on_ref, ot_ref):
    half = CMP_STRIDE * HEAD_DIM
    c = c_ref[0, 0, 0].astype(BF16)
    w1 = w1_ref[0]
    a = jnp.dot(c, w1[:half], preferred_element_type=F32)
    bm = jnp.dot(c, w1[half:], preferred_element_type=F32)
    nch = a.shape[0]
    bm = pltpu.roll(bm, nch - 1, 0)
    posb = jnp.dot(pos_ref[0].astype(BF16), w1, preferred_element_type=F32)[0:1]
    hid = jax.nn.gelu(a + bm + posb + b1_ref[0])
    hb = hid.astype(BF16)
    on_ref[0, 0, 0] = jnp.dot(hb, w2_ref[0], preferred_element_type=F32) + b2_ref[0]
    ot_ref[0, 0, 0] = lax.dot_general(w2t_ref[0], hb, (((1,), (1,)), ((), ())),
                                      preferred_element_type=F32) + b2t_ref[0]


def nsa_compress_kv(chunks, pos, w1, b1, w2, b2):
    _, b, g, nch, cw = chunks.shape
    pos_flat = jnp.broadcast_to(pos.reshape(2, 1, CMP_LEN * HEAD_DIM), (2, 8, CMP_LEN * HEAD_DIM))
    w1b = w1.astype(BF16)
    w2b = w2.astype(BF16)
    w2t = jnp.swapaxes(w2, 1, 2).astype(BF16)
    b1r = b1.reshape(2, 1, CMP_HIDDEN)
    b2r = b2.reshape(2, 1, HEAD_DIM)
    b2t = b2.reshape(2, HEAD_DIM, 1)
    sel = lambda kv, i, j: (kv, 0, 0)
    return pl.pallas_call(
        _cmp_kernel,
        grid=(2, b, g),
        in_specs=[
            pl.BlockSpec((1, 1, 1, nch, cw), lambda kv, i, j: (kv, i, j, 0, 0)),
            pl.BlockSpec((1, 8, CMP_LEN * HEAD_DIM), sel),
            pl.BlockSpec((1, CMP_LEN * HEAD_DIM, CMP_HIDDEN), sel),
            pl.BlockSpec((1, 1, CMP_HIDDEN), sel),
            pl.BlockSpec((1, CMP_HIDDEN, HEAD_DIM), sel),
            pl.BlockSpec((1, HEAD_DIM, CMP_HIDDEN), sel),
            pl.BlockSpec((1, 1, HEAD_DIM), sel),
            pl.BlockSpec((1, HEAD_DIM, 1), sel),
        ],
        out_specs=[
            pl.BlockSpec((1, 1, 1, nch, HEAD_DIM), lambda kv, i, j: (kv, i, j, 0, 0)),
            pl.BlockSpec((1, 1, 1, HEAD_DIM, nch), lambda kv, i, j: (kv, i, j, 0, 0)),
        ],
        out_shape=[
            jax.ShapeDtypeStruct((2, b, g, nch, HEAD_DIM), F32),
            jax.ShapeDtypeStruct((2, b, g, HEAD_DIM, nch), F32),
        ],
        compiler_params=_cparams(3),
        name="nsa_compress",
    )(chunks, pos_flat, w1b, b1r, w2b, w2t, b2r, b2t)


def _accumulate(s, mt, vt, m, acc):
    m_new = jnp.maximum(m, mt)
    alpha = jnp.exp2(m - m_new)
    p = jnp.exp2(s - m_new).astype(BF16)
    return m_new, alpha * acc + jnp.dot(vt, p, preferred_element_type=F32)


def _nsa_kernel(qt_ref, ng_ref, kc_ref, vct_ref, kp_ref, vst_ref, vwt_ref, e_ref, tri_ref, wpat_ref,
                cpat_ref, o_ref, bias_ref, w_ref, sbuf_ref, oc_ref, pg_ref, ms_ref, as_ref):
    nq = Q_BLOCK
    nsub = qt_ref.shape[2] // nq
    ncol = NSA_GROUP * nq
    n_sel = bias_ref.shape[1]
    ncp = kc_ref.shape[2]
    kw = 2 * HEAD_DIM
    bpt = SEL_TK // SLC_BLOCK
    n_tiles_total = kp_ref.shape[1] // SEL_TK
    qb_first = pl.program_id(2) * nsub
    wlen = WINDOW + nq

    def query_setup(u):
        qb = qb_first + u
        q0 = pl.multiple_of(qb * nq, nq)
        qt = qt_ref[0, :, u * nq:(u + 1) * nq] * (HEAD_DIM ** -0.5 * LOG2E)
        qs = jnp.concatenate([qt[r * HEAD_DIM:(r + 1) * HEAD_DIM, :] for r in range(NSA_GROUP)], axis=1)
        zq = jnp.zeros_like(qs)
        tq1 = q0 + lax.broadcasted_iota(jnp.int32, (1, nq), 1)
        return dict(
            q0=q0, q_c=qs.astype(BF16),
            q_slc=jnp.concatenate([qs, zq], axis=0).astype(BF16),
            q_win=jnp.concatenate([zq, qs], axis=0).astype(BF16),
            tq=q0 + (lax.broadcasted_iota(jnp.int32, (1, ncol), 1) & (nq - 1)),
            coff=pl.multiple_of(ncp - qb * (nq // CMP_STRIDE), nq // CMP_STRIDE),
            cur=jnp.right_shift(tq1, 6).astype(F32),
            first_diag=(2 * qb).astype(F32))

    qs_ = [query_setup(u) for u in range(nsub)]

    crow = min(CMP_CLASS_ROWS, ncp)
    qb_per_class = crow // (nq // CMP_STRIDE)
    assert qb_per_class % nsub == 0
    cls = qb_first // qb_per_class
    ratio = SLC_BLOCK // CMP_STRIDE
    for c in range(ncp // crow):
        nc_c = crow * (c + 1)
        ns_c = nc_c * CMP_STRIDE // SLC_BLOCK

        @pl.when(cls == c)
        def _(nc_c=nc_c, ns_c=ns_c):
            j_io = lax.broadcasted_iota(jnp.int32, (ns_c, nq), 0).astype(F32)
            scores = []
            for u, q in enumerate(qs_):
                pg_ref[u, 0:PG_PAD, :] = jnp.zeros((PG_PAD, nq), F32)
                s_c = (jnp.dot(kc_ref[0, 0, 0:nc_c, :], q["q_c"], preferred_element_type=F32)
                       + jnp.concatenate([cpat_ref[pl.ds(q["coff"], nc_c), :]] * NSA_GROUP, axis=1))
                m_c = jnp.max(s_c, axis=0, keepdims=True)
                e_c = jnp.exp2(s_c - m_c)
                l_c = jnp.sum(e_c, axis=0, keepdims=True)
                p_c = e_c * jnp.where(q["tq"] >= CMP_LEN - 1, 1.0 / l_c, 0.0)
                oc_ref[u] = jnp.dot(vct_ref[0, 0, :, 0:nc_c], p_c.astype(BF16), preferred_element_type=F32)

                p_grp = p_c[:, 0:nq]
                for r in range(1, NSA_GROUP):
                    p_grp = p_grp + p_c[:, r * nq:(r + 1) * nq]
                pg_ref[u, PG_PAD:PG_PAD + nc_c, :] = p_grp
                taps = [pg_ref[u, pl.ds(PG_PAD - 1 + k, ns_c, stride=ratio), :] for k in range(ratio + 1)]
                score = taps[0] + 2.0 * (taps[1] + taps[2] + taps[3]) + taps[4]
                score = jnp.where(j_io <= q["cur"], score, -1.0)
                score = jnp.where(j_io == 0.0, PICKED, score)
                score = jnp.where(j_io == q["cur"], PICKED, score)
                score = jnp.where(j_io == q["cur"] - 1.0, PICKED, score)
                scores.append(score)
            for _ in range(SLC_TOPN - 3):
                for u in range(nsub):
                    mx = jnp.max(scores[u], axis=0, keepdims=True)
                    first = jnp.min(jnp.where(scores[u] == mx, j_io, float(ns_c)), axis=0, keepdims=True)
                    scores[u] = jnp.where(j_io == first, PICKED, scores[u])
            for u, q in enumerate(qs_):
                bias = jnp.where(j_io < q["first_diag"], jnp.where(scores[u] == PICKED, 0.0, NEG), NEG)
                bias_ref[u, 0:ns_c, :] = bias.astype(BF16)
                if ns_c < n_sel:
                    bias_ref[u, ns_c:, :] = jnp.full((n_sel - ns_c, nq), NEG, BF16)

    def prepare(u):
        q0, q_slc, q_win = qs_[u]["q0"], qs_[u]["q_slc"], qs_[u]["q_win"]
        o_c = oc_ref[u]

        wstart = pl.multiple_of(jnp.maximum(q0 - WINDOW, 0), nq)
        woff = pl.multiple_of(jnp.maximum(WINDOW - q0, 0), nq)
        s_w = jnp.dot(kp_ref[0, pl.ds(wstart, wlen), :], q_win, preferred_element_type=F32)
        s_w = s_w + jnp.concatenate([wpat_ref[pl.ds(woff, wlen), :]] * NSA_GROUP, axis=1)
        p_w = jnp.exp2(s_w - jnp.max(s_w, axis=0, keepdims=True)).astype(BF16)
        acc_w = jnp.dot(vwt_ref[0, :, pl.ds(wstart, wlen)], p_w, preferred_element_type=F32)
        o_w = acc_w[0:HEAD_DIM] / acc_w[HEAD_DIM:HEAD_DIM + 1]

        for slot in range(2):
            w_ref[slot, u, 0:kw, :] = q_slc
            w_ref[slot, u, kw:, :] = jnp.zeros((w_ref.shape[2] - kw, ncol), BF16)
        return q0, q_slc, o_c, o_w

    prepared = [prepare(u) for u in range(nsub)]

    def tile_start(j):
        jc = jnp.minimum(j, n_tiles_total - 1)
        return jc, pl.multiple_of(jc * SEL_TK, SEL_TK)

    def produce(j, slot):
        jc, k0 = tile_start(j)
        lhs = jnp.concatenate([kp_ref[0, pl.ds(k0, SEL_TK), :], e_ref[...]], axis=1)
        mts = []
        for u in range(nsub):
            b16 = bias_ref[u, pl.ds(pl.multiple_of(jc * bpt, bpt), bpt), :]
            w_ref[slot, u, kw:kw + bpt, :] = jnp.concatenate([b16] * NSA_GROUP, axis=1)
            s = jnp.dot(lhs, w_ref[slot, u], preferred_element_type=F32)
            sbuf_ref[slot, u] = s
            mts.append(jnp.max(s, axis=0, keepdims=True))
        return tuple(mts)

    def consume(j, slot, mts, state):
        _, k0 = tile_start(j)
        vt = vst_ref[0, :, pl.ds(k0, SEL_TK)]
        return tuple(_accumulate(sbuf_ref[slot, u], mts[u], vt, *state[u]) for u in range(nsub))

    def pair(i, carry):
        mt_a, state = carry
        mt_b = produce(2 * i + 1, 1)
        state = consume(2 * i, 0, mt_a, state)
        mt_a = produce(2 * i + 2, 0)
        state = consume(2 * i + 1, 1, mt_b, state)
        return mt_a, state

    def park(state):
        for u in range(nsub):
            ms_ref[u] = state[u][0]
            as_ref[u] = state[u][1]

    n_main = jnp.maximum(((qb_first + nsub - 1) * nq + (SEL_TK - 1)) // SEL_TK, 1)
    n_pairs = (n_main + 1) // 2
    state = tuple((jnp.full((1, ncol), NEG, F32), jnp.zeros((vst_ref.shape[1], ncol), F32)) for _ in range(nsub))
    mt_a, state = lax.fori_loop(0, n_pairs - 1, pair, (produce(0, 0), state))
    tail = 2 * (n_pairs - 1)
    odd = (n_main & 1) == 1

    @pl.when(odd)
    def _():
        park(consume(tail, 0, mt_a, state))

    @pl.when(jnp.logical_not(odd))
    def _():
        mt_b = produce(tail + 1, 1)
        park(consume(tail + 1, 1, mt_b, consume(tail, 0, mt_a, state)))

    state = tuple((ms_ref[u], as_ref[u]) for u in range(nsub))

    for u in range(nsub):
        q0, q_slc, o_c, o_w = prepared[u]
        s_d = jnp.dot(kp_ref[0, pl.ds(q0, nq), :], q_slc, preferred_element_type=F32) + tri_ref[...]
        _, acc_s = _accumulate(s_d, jnp.max(s_d, axis=0, keepdims=True), vst_ref[0, :, pl.ds(q0, nq)], *state[u])
        o_s = acc_s[0:HEAD_DIM] / acc_s[HEAD_DIM:HEAD_DIM + 1]
        gts = jax.nn.sigmoid(ng_ref[0, :, u * nq:(u + 1) * nq])
        rows = []
        for r in range(NSA_GROUP):
            sl = slice(r * nq, (r + 1) * nq)
            rows.append(gts[3 * r:3 * r + 1, :] * o_c[:, sl] + gts[3 * r + 1:3 * r + 2, :] * o_s[:, sl]
                        + gts[3 * r + 2:3 * r + 3, :] * o_w[:, sl])
        y_t = jnp.concatenate(rows, axis=0)
        o_ref[0, u * nq:(u + 1) * nq, :] = y_t.T.astype(o_ref.dtype)


def _nsa_constants(ncp):
    nq = Q_BLOCK
    ql = jnp.arange(nq)[None, :]
    r = jnp.arange(SEL_TK)[:, None]
    e = (r // SLC_BLOCK == jnp.arange(2 * HEAD_DIM)[None, :]).astype(BF16)
    rd = jnp.arange(nq)[:, None]
    tri = jnp.tile(jnp.where(rd <= ql, 0.0, NEG).astype(F32), (1, NSA_GROUP))
    rw = jnp.arange(2 * WINDOW + nq)[:, None]
    wpat = jnp.where((rw > ql) & (rw <= ql + WINDOW), 0.0, NEG).astype(F32)
    rc = jnp.arange(2 * ncp)[:, None] - ncp
    cpat = jnp.where(CMP_STRIDE * rc + (CMP_LEN - 1) <= ql, 0.0, NEG).astype(F32)
    return e, tri, wpat, cpat


def nsa_attention(proj_t, kc, vct, hn, vt_pack):
    b, _, t = proj_t.shape
    ncp = kc.shape[2]
    n_sel = t // SLC_BLOCK
    gq = NSA_GROUP * HEAD_DIM
    qstep = NSA_QSUB * Q_BLOCK
    e, tri, wpat, cpat = _nsa_constants(ncp)
    const2 = lambda i, g, q: (0, 0)
    return pl.pallas_call(
        _nsa_kernel,
        grid=(b, NSA_KV_GROUPS, t // qstep),
        in_specs=[
            pl.BlockSpec((1, gq, qstep), lambda i, g, q: (i, g, q)),
            pl.BlockSpec((1, 16, qstep), lambda i, g, q: (i, TR_NG // 16 + g, q)),
            pl.BlockSpec((1, 1, ncp, HEAD_DIM), lambda i, g, q: (i, g, 0, 0)),
            pl.BlockSpec((1, 1, HEAD_DIM, ncp), lambda i, g, q: (i, g, 0, 0)),
            pl.BlockSpec((1, t, 2 * HEAD_DIM), lambda i, g, q: (i, 0, NAT_KP // (2 * HEAD_DIM) + g)),
            pl.BlockSpec((1, VT_ROWS, t), lambda i, g, q: (i, g, 0)),
            pl.BlockSpec((1, VT_ROWS, t), lambda i, g, q: (i, NSA_KV_GROUPS + g, 0)),
            pl.BlockSpec(e.shape, const2),
            pl.BlockSpec(tri.shape, const2),
            pl.BlockSpec(wpat.shape, const2),
            pl.BlockSpec(cpat.shape, const2),
        ],
        out_specs=pl.BlockSpec((1, qstep, gq), lambda i, g, q: (i, q, g)),
        out_shape=jax.ShapeDtypeStruct((b, t, NSA_HEADS * HEAD_DIM), BF16),
        scratch_shapes=[pltpu.VMEM((NSA_QSUB, n_sel, Q_BLOCK), BF16),
                        pltpu.VMEM((2, NSA_QSUB, 4 * HEAD_DIM, NSA_GROUP * Q_BLOCK), BF16),
                        pltpu.VMEM((2, NSA_QSUB, SEL_TK, NSA_GROUP * Q_BLOCK), F32),
                        pltpu.VMEM((NSA_QSUB, HEAD_DIM, NSA_GROUP * Q_BLOCK), F32),
                        pltpu.VMEM((NSA_QSUB, PG_PAD + ncp, Q_BLOCK), F32),
                        pltpu.VMEM((NSA_QSUB, 1, NSA_GROUP * Q_BLOCK), F32),
                        pltpu.VMEM((NSA_QSUB, VT_ROWS, NSA_GROUP * Q_BLOCK), F32)],
        compiler_params=_cparams(3),
        name="nsa_attention",
    )(proj_t, proj_t, kc, vct, hn, vt_pack, vt_pack, e, tri, wpat, cpat)


def _ret_kernel(rq_ref, rk_ref, rv_ref, rg_ref, cos_ref, sin_ref, dec_ref, qd_ref, kd_ref, cd_ref,
                gg_ref, gb_ref, o_ref, st_ref):
    c = RET_CHUNK
    hd = RET_HEAD_DIM

    @pl.when(pl.program_id(1) == 0)
    def _():
        st_ref[...] = jnp.zeros_like(st_ref)

    n_chunks = rq_ref.shape[1] // c

    def chunk(ci, _):
        r0 = pl.multiple_of(ci * c, c)
        cos = cos_ref[pl.ds(r0, c), :]
        sin = sin_ref[pl.ds(r0, c), :]
        for h in range(RET_HEADS):
            hs = slice(h * hd, (h + 1) * hd)
            q = rq_ref[0, pl.ds(r0, c), hs].astype(F32)
            k = rk_ref[0, pl.ds(r0, c), hs].astype(F32)
            v = rv_ref[0, pl.ds(r0, c), hs]
            q = q * cos + pltpu.roll(q, hd // 2, 1) * sin
            k = (k * cos + pltpu.roll(k, hd // 2, 1) * sin) * (hd ** -0.5)
            qb = q.astype(BF16)
            kb = k.astype(BF16)
            vb = v.astype(BF16)
            inner = lax.dot_general(qb, kb, (((1,), (1,)), ((), ())), preferred_element_type=F32) * dec_ref[h]
            st = st_ref[h]
            o = (jnp.dot(inner.astype(BF16), vb, preferred_element_type=F32)
                 + jnp.dot(qb, st.astype(BF16), preferred_element_type=F32) * qd_ref[h])
            kdt = (k * kd_ref[h]).T.astype(BF16)
            st_ref[h] = st * cd_ref[h] + jnp.dot(kdt, vb, preferred_element_type=F32)
            mu = jnp.mean(o, axis=-1, keepdims=True)
            d = o - mu
            var = jnp.mean(d * d, axis=-1, keepdims=True)
            y = d * lax.rsqrt(var + LN_EPS) * gg_ref[:, hs] + gb_ref[:, hs]
            o_ref[0, pl.ds(r0, c), hs] = (jax.nn.silu(rg_ref[0, pl.ds(r0, c), hs].astype(F32)) * y).astype(o_ref.dtype)
        return 0

    lax.fori_loop(0, n_chunks, chunk, 0)


def retention_branch(hn, gn_g, gn_b, tr):
    b, t, _ = hn.shape
    w = BRANCH_WIDTH
    hd = RET_HEAD_DIM
    c = RET_CHUNK
    inv = ROPE_BASE ** (-jnp.arange(0, hd, 2, dtype=F32) / hd)
    ang = jnp.arange(t, dtype=F32)[:, None] * inv[None, :]
    cos2 = jnp.concatenate([jnp.cos(ang), jnp.cos(ang)], axis=1)
    sin2 = jnp.concatenate([-jnp.sin(ang), jnp.sin(ang)], axis=1)
    log_g = jnp.log(1.0 - 2.0 ** (-5.0 - jnp.arange(RET_HEADS, dtype=F32)))
    idx = jnp.arange(c, dtype=F32)
    diff = idx[:, None] - idx[None, :]
    decay_in = jnp.where(diff >= 0, jnp.exp(log_g[:, None, None] * jnp.maximum(diff, 0.0)), 0.0)
    ones = jnp.ones((RET_HEADS, c, hd), F32)
    q_dec = jnp.exp(log_g[:, None] * (idx + 1.0))[:, :, None] * ones
    k_dec = jnp.exp(log_g[:, None] * (c - 1.0 - idx))[:, :, None] * ones
    c_dec = jnp.exp(log_g * c)[:, None, None] * jnp.ones((RET_HEADS, hd, hd), F32)
    col = lambda cc: (lambda i, j: (i, j, cc // w))
    full3 = lambda i, j: (0, 0, 0)
    return pl.pallas_call(
        _ret_kernel,
        grid=(b, t // tr),
        in_specs=[
            pl.BlockSpec((1, tr, w), col(NAT_RQ)),
            pl.BlockSpec((1, tr, w), col(NAT_RK)),
            pl.BlockSpec((1, tr, w), col(NAT_RV)),
            pl.BlockSpec((1, tr, w), col(NAT_RG)),
            pl.BlockSpec((tr, hd), lambda i, j: (j, 0)),
            pl.BlockSpec((tr, hd), lambda i, j: (j, 0)),
            pl.BlockSpec((RET_HEADS, c, c), full3),
            pl.BlockSpec((RET_HEADS, c, hd), full3),
            pl.BlockSpec((RET_HEADS, c, hd), full3),
            pl.BlockSpec((RET_HEADS, hd, hd), full3),
            pl.BlockSpec((1, w), lambda i, j: (0, 0)),
            pl.BlockSpec((1, w), lambda i, j: (0, 0)),
        ],
        out_specs=pl.BlockSpec((1, tr, w), lambda i, j: (i, j, 0)),
        out_shape=jax.ShapeDtypeStruct((b, t, w), BF16),
        scratch_shapes=[pltpu.VMEM((RET_HEADS, hd, hd), F32)],
        compiler_params=_cparams(2),
        name="retention",
    )(hn, hn, hn, hn, cos2, sin2, decay_in, q_dec, k_dec, c_dec, gn_g.reshape(1, w), gn_b.reshape(1, w))


def _layer_norm(z, g, b):
    mu = jnp.mean(z, axis=-1, keepdims=True)
    d = z - mu
    var = jnp.mean(d * d, axis=-1, keepdims=True)
    return d * lax.rsqrt(var + LN_EPS) * g + b


def _merge_kernel(tiles_per_seq, x_ref, ya_ref, cb_ref, cc_ref, ch_ref, pc_ref, ph_ref, cw_ref, yc_ref, mg_ref,
                  wbr_ref, wout_ref, g_ref, b_ref, o_ref, ob_ref):
    first = (pl.program_id(0) % tiles_per_seq) == 0
    u = cc_ref[...].astype(F32) * ch_ref[...].astype(F32)
    prev = pc_ref[...].astype(F32) * ph_ref[...].astype(F32)
    prev = jnp.where(first, 0.0, prev)
    row = lax.broadcasted_iota(jnp.int32, u.shape, 0)
    p1 = prev[CONV_HALO - 1:CONV_HALO, :]
    p2 = prev[CONV_HALO - 2:CONV_HALO - 1, :]
    u1 = jnp.where(row == 0, p1, pltpu.roll(u, 1, 0))
    u2 = jnp.where(row == 0, p2, jnp.where(row == 1, p1, pltpu.roll(u, 2, 0)))
    cw = cw_ref[...]
    y_b = cb_ref[...].astype(F32) * (cw[0:1, :] * u2 + cw[1:2, :] * u1 + cw[2:3, :] * u)

    mix = None
    for c, y in enumerate((ya_ref[...], y_b.astype(BF16), yc_ref[...])):
        proj = jnp.dot(y, wbr_ref[c], preferred_element_type=F32)
        gate = jax.nn.sigmoid(mg_ref[:, c * D_MODEL:(c + 1) * D_MODEL].astype(F32))
        mix = gate * proj if mix is None else mix + gate * proj
    m = jnp.dot(mix.astype(BF16), wout_ref[...], preferred_element_type=F32)
    out = _layer_norm(ALPHA * x_ref[...] + m, g_ref[...], b_ref[...])
    o_ref[...] = out
    ob_ref[...] = out.astype(BF16)


def merge_branches(x, ya, yc, hn, conv_w, w_br, w_out, g, b, tm, t):
    m = x.shape[0]
    d = D_MODEL
    w = BRANCH_WIDTH
    row = lambda i: (i, 0)
    col = lambda c: (lambda i: (i, c // w))
    hal = lambda c: (lambda i: (jnp.maximum(i * (tm // CONV_HALO) - 1, 0), c // w))
    wpad = jnp.zeros((8, w), F32).at[:CONV_K].set(conv_w)
    return pl.pallas_call(
        functools.partial(_merge_kernel, t // tm),
        grid=(m // tm,),
        in_specs=[
            pl.BlockSpec((tm, d), row),
            pl.BlockSpec((tm, w), row),
            pl.BlockSpec((tm, w), col(NAT_CB)),
            pl.BlockSpec((tm, w), col(NAT_CC)),
            pl.BlockSpec((tm, w), col(NAT_CH)),
            pl.BlockSpec((CONV_HALO, w), hal(NAT_CC)),
            pl.BlockSpec((CONV_HALO, w), hal(NAT_CH)),
            pl.BlockSpec((8, w), lambda i: (0, 0)),
            pl.BlockSpec((tm, w), row),
            pl.BlockSpec((tm, 3 * d), row),
            pl.BlockSpec((3, w, d), lambda i: (0, 0, 0)),
            pl.BlockSpec((d, d), lambda i: (0, 0)),
            pl.BlockSpec((1, d), lambda i: (0, 0)),
            pl.BlockSpec((1, d), lambda i: (0, 0)),
        ],
        out_specs=[pl.BlockSpec((tm, d), row), pl.BlockSpec((tm, d), row)],
        out_shape=[jax.ShapeDtypeStruct((m, d), F32), jax.ShapeDtypeStruct((m, d), BF16)],
        compiler_params=_cparams(1),
        name="merge",
    )(x, ya, hn, hn, hn, hn, hn, wpad, yc, hn, w_br.astype(BF16), w_out.astype(BF16), g.reshape(1, d),
      b.reshape(1, d))


def _cast3_kernel(g_ref, u_ref, d_ref, og_ref, ou_ref, od_ref):
    og_ref[...] = g_ref[0].astype(og_ref.dtype)
    ou_ref[...] = u_ref[0].astype(ou_ref.dtype)
    od_ref[...] = d_ref[0].astype(od_ref.dtype)


def ffn_weights_bf16(w_gate, w_up, w_down, layer):
    _, e, d, ff = w_gate.shape
    rg, rd = d // CAST_STEPS, ff // CAST_STEPS
    up_spec = pl.BlockSpec((1, 1, rg, ff), lambda i, j: (layer, i, j, 0))
    return pl.pallas_call(
        _cast3_kernel,
        grid=(e, CAST_STEPS),
        in_specs=[up_spec, up_spec, pl.BlockSpec((1, 1, rd, d), lambda i, j: (layer, i, j, 0))],
        out_specs=[pl.BlockSpec((1, rg, ff), lambda i, j: (i, j, 0)), pl.BlockSpec((1, rg, ff), lambda i, j: (i, j, 0)),
                   pl.BlockSpec((1, rd, d), lambda i, j: (i, j, 0))],
        out_shape=[jax.ShapeDtypeStruct((e, d, ff), BF16), jax.ShapeDtypeStruct((e, d, ff), BF16),
                   jax.ShapeDtypeStruct((e, ff, d), BF16)],
        compiler_params=_cparams(2),
        name="cast_bf16",
    )(w_gate, w_up, w_down)


def _swiglu_chunks(x, wg_ref, wu_ref, wd_ref, tf):
    acc = None
    for c in range(wg_ref.shape[2] // tf):
        sl = slice(c * tf, (c + 1) * tf)
        gte = jnp.dot(x, wg_ref[0, :, sl], preferred_element_type=F32)
        up = jnp.dot(x, wu_ref[0, :, sl], preferred_element_type=F32)
        h = (jax.nn.silu(gte) * up).astype(BF16)
        part = jnp.dot(h, wd_ref[0, sl, :], preferred_element_type=F32)
        acc = part if acc is None else acc + part
    return acc


def _mlp_kernel(tf, te_ref, tv_ref, x_ref, wg_ref, wu_ref, wd_ref, o_ref):
    i = pl.program_id(0)

    @pl.when(tv_ref[i] > 0)
    def _():
        o_ref[...] = _swiglu_chunks(x_ref[...], wg_ref, wu_ref, wd_ref, tf).astype(o_ref.dtype)

    @pl.when(tv_ref[i] == 0)
    def _():
        o_ref[...] = jnp.zeros_like(o_ref)


def _resident(block_shape, index_map):
    return pl.BlockSpec(block_shape, index_map, pipeline_mode=pl.Buffered(1))


def grouped_swiglu(xs, w_gate, w_up, w_down, tile_expert, tile_valid, tm, tf):
    n, d = xs.shape
    ff = w_gate.shape[2]
    grid_spec = pltpu.PrefetchScalarGridSpec(
        num_scalar_prefetch=2,
        grid=(n // tm,),
        in_specs=[
            pl.BlockSpec((tm, d), lambda i, te, tv: (i, 0)),
            _resident((1, d, ff), lambda i, te, tv: (te[i], 0, 0)),
            _resident((1, d, ff), lambda i, te, tv: (te[i], 0, 0)),
            _resident((1, ff, d), lambda i, te, tv: (te[i], 0, 0)),
        ],
        out_specs=pl.BlockSpec((tm, d), lambda i, te, tv: (i, 0)),
    )
    return pl.pallas_call(
        functools.partial(_mlp_kernel, tf),
        grid_spec=grid_spec,
        out_shape=jax.ShapeDtypeStruct((n, d), BF16),
        compiler_params=_cparams(1, BIG_VMEM_LIMIT),
        name="grouped_swiglu",
    )(tile_expert, tile_valid, xs, w_gate, w_up, w_down)


def _add_ln_kernel(*refs):
    x_ref, sc_ref, *adds, g_ref, b_ref, o_ref, ob_ref = refs
    z = ALPHA * x_ref[...]
    for k, a in enumerate(adds):
        z = z + sc_ref[:, k:k + 1] * a[...].astype(F32)
    out = _layer_norm(z, g_ref[...], b_ref[...])
    o_ref[...] = out
    ob_ref[...] = out.astype(BF16)


def add_layer_norm(x, adds, scales, g, b, tm):
    m, d = x.shape
    row = lambda i: (i, 0)
    return pl.pallas_call(
        _add_ln_kernel,
        grid=(m // tm,),
        in_specs=([pl.BlockSpec((tm, d), row), pl.BlockSpec((tm, len(adds)), row)]
                  + [pl.BlockSpec((tm, d), row)] * len(adds) + [pl.BlockSpec((1, d), lambda i: (0, 0))] * 2),
        out_specs=[pl.BlockSpec((tm, d), row)] * 2,
        out_shape=[jax.ShapeDtypeStruct((m, d), F32), jax.ShapeDtypeStruct((m, d), BF16)],
        compiler_params=_cparams(1),
        name="add_layer_norm",
    )(x, scales, *adds, g.reshape(1, d), b.reshape(1, d))


def _router_kernel(x_ref, wh_ref, wl_ref, o_ref):
    x = x_ref[...]
    xh = x.astype(BF16)
    xl = (x - xh.astype(F32)).astype(BF16)
    wh = wh_ref[...]
    o_ref[...] = (jnp.dot(xh, wh, preferred_element_type=F32) + jnp.dot(xl, wh, preferred_element_type=F32)
                  + jnp.dot(xh, wl_ref[...], preferred_element_type=F32))


def router_logits(x, w_router, tm):
    m, d = x.shape
    wp = jnp.zeros((d, 128), F32).at[:, :N_EXPERTS].set(w_router)
    wh = wp.astype(BF16)
    wl = (wp - wh.astype(F32)).astype(BF16)
    return pl.pallas_call(
        _router_kernel,
        grid=(m // tm,),
        in_specs=[pl.BlockSpec((tm, d), lambda i: (i, 0)), pl.BlockSpec((d, 128), lambda i: (0, 0)),
                  pl.BlockSpec((d, 128), lambda i: (0, 0))],
        out_specs=pl.BlockSpec((tm, 128), lambda i: (i, 0)),
        out_shape=jax.ShapeDtypeStruct((m, 128), F32),
        compiler_params=_cparams(1),
        name="router",
    )(x, wh, wl)


def _dense_ffn_kernel(tf, xb_ref, x_ref, wg_ref, wu_ref, wd_ref, g_ref, b_ref, o_ref, ob_ref):
    acc = _swiglu_chunks(xb_ref[...], wg_ref, wu_ref, wd_ref, tf)
    out = _layer_norm(ALPHA * x_ref[...] + acc, g_ref[...], b_ref[...])
    o_ref[...] = out
    ob_ref[...] = out.astype(BF16)


def dense_ffn(x, xb, w_gate, w_up, w_down, g, b):
    m, d = x.shape
    ff = w_gate.shape[2]
    tm = min(FFN_TM, m)
    row = lambda i: (i, 0)
    return pl.pallas_call(
        functools.partial(_dense_ffn_kernel, FFN_TF),
        grid=(m // tm,),
        in_specs=[
            pl.BlockSpec((tm, d), row),
            pl.BlockSpec((tm, d), row),
            _resident((1, d, ff), lambda i: (0, 0, 0)),
            _resident((1, d, ff), lambda i: (0, 0, 0)),
            _resident((1, ff, d), lambda i: (0, 0, 0)),
            pl.BlockSpec((1, d), lambda i: (0, 0)),
            pl.BlockSpec((1, d), lambda i: (0, 0)),
        ],
        out_specs=[pl.BlockSpec((tm, d), row)] * 2,
        out_shape=[jax.ShapeDtypeStruct((m, d), F32), jax.ShapeDtypeStruct((m, d), BF16)],
        compiler_params=_cparams(1, BIG_VMEM_LIMIT),
        name="dense_ffn",
    )(xb, x, w_gate, w_up, w_down, g.reshape(1, d), b.reshape(1, d))


def moe_ffn(x, xb, w_router, w_gate, w_up, w_down, g, b):
    m = x.shape[0]
    tm = FFN_TM
    logits = router_logits(x, w_router, min(ROW_TILE, m))[:, :N_EXPERTS]
    top_v, top_i = lax.top_k(logits, TOP_K)
    probs = jax.nn.softmax(top_v, axis=-1)
    e_flat = top_i.reshape(-1)
    onehot = (e_flat[:, None] == jnp.arange(N_EXPERTS)[None, :]).astype(jnp.int32)
    csum = jnp.cumsum(onehot, axis=0)
    counts = csum[-1]
    rank = jnp.take_along_axis(csum, e_flat[:, None], axis=1)[:, 0] - 1
    padded = ((counts + tm - 1) // tm) * tm
    ends = jnp.cumsum(padded)
    starts = ends - padded
    pos = starts[e_flat] + rank
    n_slots = TOP_K * m + N_EXPERTS * tm
    n_tiles = n_slots // tm
    tok = jnp.arange(TOP_K * m, dtype=jnp.int32) // TOP_K
    src = jnp.zeros((n_slots,), jnp.int32).at[pos].set(tok)
    tile_start = jnp.arange(n_tiles, dtype=jnp.int32) * tm
    tile_expert = jnp.minimum(jnp.searchsorted(ends, tile_start, side="right"), N_EXPERTS - 1).astype(jnp.int32)
    tile_valid = (tile_start < ends[-1]).astype(jnp.int32)
    xs = jnp.take(xb, src, axis=0)
    ys = grouped_swiglu(xs, w_gate, w_up, w_down,
                        tile_expert, tile_valid, tm, FFN_TF)
    pos2 = pos.reshape(m, TOP_K)
    y0 = jnp.take(ys, pos2[:, 0], axis=0)
    y1 = jnp.take(ys, pos2[:, 1], axis=0)
    return add_layer_norm(x, [y0, y1], probs, g, b, min(ROW_TILE, m))


def _pack_in_weights(w):
    d = w.shape[0]
    o = 0
    q = w[:, o:o + 512]; o += 512
    kv = w[:, o:o + 768].reshape(d, 3, 2, NSA_KV_GROUPS, HEAD_DIM); o += 768
    ng = w[:, o:o + 24]; o += 24
    rest = w[:, o:o + 7 * 512]; o += 7 * 512
    mg = w[:, o:]
    kpack = jnp.concatenate([kv[:, 1, 0], kv[:, 2, 0]], axis=-1).reshape(d, 2 * NSA_KV_GROUPS * HEAD_DIM)
    kc = kv[:, 0, 0].reshape(d, NSA_KV_GROUPS * HEAD_DIM)
    vc = kv[:, 0, 1].reshape(d, NSA_KV_GROUPS * HEAD_DIM)
    w_nat = jnp.concatenate([mg, rest, kpack, kc, vc], axis=1).astype(BF16)
    vs = kv[:, 1, 1].reshape(d, NSA_KV_GROUPS * HEAD_DIM)
    vw = kv[:, 2, 1].reshape(d, NSA_KV_GROUPS * HEAD_DIM)
    ngp = jnp.pad(ng.reshape(d, NSA_KV_GROUPS, NSA_GROUP * 3), ((0, 0), (0, 0), (0, 4))).reshape(d, 32)
    w_t = jnp.concatenate([q, vs, vw, ngp], axis=1).T.astype(BF16)
    return w_nat, w_t


def token_mixer_ln(x, xb, w_in, cmp_pos, cmp_w1, cmp_b1, cmp_w2, cmp_b2, conv_w, gn_g, gn_b, w_br, w_out, ln_g, ln_b):
    b, t, d = x.shape
    m = b * t
    w_nat, w_t = _pack_in_weights(w_in)
    x2 = x.reshape(m, d)
    hn, proj_t = input_projection(xb, w_nat, w_t, min(PROJ_TM, t), PROJ_TN)
    hn = hn.reshape(b, t, NAT_COLS)
    nch = t // CMP_STRIDE
    raw = hn[:, :, NAT_KC:NAT_KC + 2 * NSA_KV_GROUPS * HEAD_DIM].reshape(b, t, 2, NSA_KV_GROUPS, HEAD_DIM)
    chunks = raw.transpose(2, 0, 3, 1, 4).reshape(2, b, NSA_KV_GROUPS, nch, CMP_STRIDE * HEAD_DIM)
    cn, ct = nsa_compress_kv(chunks, cmp_pos, cmp_w1, cmp_b1, cmp_w2, cmp_b2)
    kc = cn[0].astype(BF16)
    vct = ct[1].astype(BF16)
    vt4 = proj_t[:, TR_VS:TR_NG, :].astype(BF16).reshape(b, 4, HEAD_DIM, t)
    ones_pad = jnp.zeros((b, 4, VT_ROWS - HEAD_DIM, t), BF16).at[:, :, 0].set(1.0)
    vt_pack = jnp.concatenate([vt4, ones_pad], axis=2).reshape(b, 4 * VT_ROWS, t)
    y_a = nsa_attention(proj_t, kc, vct, hn, vt_pack)
    y_c = retention_branch(hn, gn_g, gn_b, min(ROW_TILE, t))
    return merge_branches(x2, y_a.reshape(m, -1), y_c.reshape(m, -1), hn.reshape(m, NAT_COLS), conv_w,
                          w_br, w_out, ln_g, ln_b, min(MERGE_TM, t), t)


def kernel(x, w_in, cmp_pos, cmp_w1, cmp_b1, cmp_w2, cmp_b2, conv_w, ret_gn_g, ret_gn_b, w_br, w_out,
           ln1_g, ln1_b, ln2_g, ln2_b, ffn_gate, ffn_up, ffn_down, moe_router, moe_gate, moe_up, moe_down):
    b, t, d = x.shape
    xb = x.astype(BF16)
    for l in range(DEPTH):
        x2, x2b = token_mixer_ln(x, xb, w_in[l], cmp_pos[l], cmp_w1[l], cmp_b1[l], cmp_w2[l], cmp_b2[l], conv_w[l],
                                 ret_gn_g[l], ret_gn_b[l], w_br[l], w_out[l], ln1_g[l], ln1_b[l])
        if l % 2 == 0:
            wg, wu, wd = ffn_weights_bf16(ffn_gate[:, None], ffn_up[:, None], ffn_down[:, None], l // 2)
            x2, x2b = dense_ffn(x2, x2b, wg, wu, wd, ln2_g[l], ln2_b[l])
        else:
            wg, wu, wd = ffn_weights_bf16(moe_gate, moe_up, moe_down, l // 2)
            x2, x2b = moe_ffn(x2, x2b, moe_router[l // 2], wg, wu, wd, ln2_g[l], ln2_b[l])
        x, xb = x2.reshape(b, t, d), x2b.reshape(b, t, d)
    return x
```

```python
import functools

import jax
import jax.numpy as jnp
from jax import lax
from jax.experimental import pallas as pl
from jax.experimental.pallas import tpu as pltpu

F32 = jnp.float32
BF16 = jnp.bfloat16

D_MODEL = 1024
DEPTH = 4
BRANCH_WIDTH = 512
HEAD_DIM = 64
NSA_HEADS = 8
NSA_KV_GROUPS = 2
NSA_GROUP = 4
CMP_LEN = 32
CMP_STRIDE = 16
SLC_BLOCK = 64
SLC_TOPN = 16
WINDOW = 512
CMP_HIDDEN = 256
Q_BLOCK = 128
CONV_K = 3
CONV_HALO = 16
RET_HEADS = 4
RET_HEAD_DIM = 128
RET_CHUNK = 256
ROPE_BASE = 10000.0
D_FF = 3584
N_EXPERTS = 8
TOP_K = 2
LN_EPS = 1e-5
ALPHA = (2.0 * DEPTH) ** 0.25

NAT_MG = 0
NAT_CB = 3072
NAT_CC = 3584
NAT_CH = 4096
NAT_RQ = 4608
NAT_RK = 5120
NAT_RV = 5632
NAT_RG = 6144
NAT_KP = 6656
NAT_KC = 6912
NAT_VC = 7040
NAT_COLS = 7168
TR_Q = 0
TR_VS = 512
TR_VW = 640
TR_NG = 768
TR_ROWS = 800

NEG = -1e30
SEL_TK = 1024
LOG2E = 1.4426950408889634
NSA_QSUB = 2
CMP_CLASS_ROWS = 128
PG_PAD = 8
PICKED = -3e38
VT_ROWS = 80
RANK_WIDTH = 1024
ROW_TILE = 1024
MERGE_TM = 512
PROJ_TM = 2048
PROJ_TN = 1024
FFN_TM = 512
FFN_TF = 1792
CAST_STEPS = 4
VMEM_LIMIT = 48 * 1024 * 1024
BIG_VMEM_LIMIT = 56 * 1024 * 1024


def _cparams(n_axes, vmem=VMEM_LIMIT):
    return pltpu.CompilerParams(dimension_semantics=("arbitrary",) * n_axes, vmem_limit_bytes=vmem)


def _proj_kernel(x_ref, w_ref, wt_ref, o_ref, ot_ref):
    @pl.when(pl.program_id(1) == 0)
    def _():
        ot_ref[0] = lax.dot_general(wt_ref[...], x_ref[...], (((1,), (1,)), ((), ())), preferred_element_type=F32)

    o_ref[...] = jnp.dot(x_ref[...], w_ref[...], preferred_element_type=F32).astype(o_ref.dtype)


def input_projection(x, w_nat, w_t, tm, tn):
    b, t, k = x.shape
    m = b * t
    n = w_nat.shape[1]
    r = w_t.shape[0]
    tps = t // tm
    return pl.pallas_call(
        _proj_kernel,
        grid=(m // tm, n // tn),
        in_specs=[pl.BlockSpec((tm, k), lambda i, j: (i, 0)), pl.BlockSpec((k, tn), lambda i, j: (0, j)),
                  pl.BlockSpec((r, k), lambda i, j: (0, 0))],
        out_specs=[pl.BlockSpec((tm, tn), lambda i, j: (i, j)),
                   pl.BlockSpec((1, r, tm), lambda i, j: (i // tps, 0, i % tps))],
        out_shape=[jax.ShapeDtypeStruct((m, n), BF16), jax.ShapeDtypeStruct((b, r, t), F32)],
        compiler_params=_cparams(2, BIG_VMEM_LIMIT),
        name="input_projection",
    )(x.reshape(m, k), w_nat, w_t)


def _cmp_kernel(c_ref, pos_ref, w1_ref, b1_ref, w2_ref, w2t_ref, b2_ref, b2t_ref, on_ref, ot_ref):
    half = CMP_STRIDE * HEAD_DIM
    c = c_ref[0, 0, 0].astype(BF16)
    w1 = w1_ref[0]
    a = jnp.dot(c, w1[:half], preferred_element_type=F32)
    bm = jnp.dot(c, w1[half:], preferred_element_type=F32)
    nch = a.shape[0]
    bm = pltpu.roll(bm, nch - 1, 0)
    posb = jnp.dot(pos_ref[0].astype(BF16), w1, preferred_element_type=F32)[0:1]
    hid = jax.nn.gelu(a + bm + posb + b1_ref[0])
    hb = hid.astype(BF16)
    on_ref[0, 0, 0] = jnp.dot(hb, w2_ref[0], preferred_element_type=F32) + b2_ref[0]
    ot_ref[0, 0, 0] = lax.dot_general(w2t_ref[0], hb, (((1,), (1,)), ((), ())),
                                      preferred_element_type=F32) + b2t_ref[0]


def nsa_compress_kv(chunks, pos, w1, b1, w2, b2):
    _, b, g, nch, cw = chunks.shape
    pos_flat = jnp.broadcast_to(pos.reshape(2, 1, CMP_LEN * HEAD_DIM), (2, 8, CMP_LEN * HEAD_DIM))
    w1b = w1.astype(BF16)
    w2b = w2.astype(BF16)
    w2t = jnp.swapaxes(w2, 1, 2).astype(BF16)
    b1r = b1.reshape(2, 1, CMP_HIDDEN)
    b2r = b2.reshape(2, 1, HEAD_DIM)
    b2t = b2.reshape(2, HEAD_DIM, 1)
    sel = lambda kv, i, j: (kv, 0, 0)
    return pl.pallas_call(
        _cmp_kernel,
        grid=(2, b, g),
        in_specs=[
            pl.BlockSpec((1, 1, 1, nch, cw), lambda kv, i, j: (kv, i, j, 0, 0)),
            pl.BlockSpec((1, 8, CMP_LEN * HEAD_DIM), sel),
            pl.BlockSpec((1, CMP_LEN * HEAD_DIM, CMP_HIDDEN), sel),
            pl.BlockSpec((1, 1, CMP_HIDDEN), sel),
            pl.BlockSpec((1, CMP_HIDDEN, HEAD_DIM), sel),
            pl.BlockSpec((1, HEAD_DIM, CMP_HIDDEN), sel),
            pl.BlockSpec((1, 1, HEAD_DIM), sel),
            pl.BlockSpec((1, HEAD_DIM, 1), sel),
        ],
        out_specs=[
            pl.BlockSpec((1, 1, 1, nch, HEAD_DIM), lambda kv, i, j: (kv, i, j, 0, 0)),
            pl.BlockSpec((1, 1, 1, HEAD_DIM, nch), lambda kv, i, j: (kv, i, j, 0, 0)),
        ],
        out_shape=[
            jax.ShapeDtypeStruct((2, b, g, nch, HEAD_DIM), F32),
            jax.ShapeDtypeStruct((2, b, g, HEAD_DIM, nch), F32),
        ],
        compiler_params=_cparams(3),
        name="nsa_compress",
    )(chunks, pos_flat, w1b, b1r, w2b, w2t, b2r, b2t)


def _accumulate(s, mt, vt, m, acc):
    m_new = jnp.maximum(m, mt)
    alpha = jnp.exp2(m - m_new)
    p = jnp.exp2(s - m_new).astype(BF16)
    return m_new, alpha * acc + jnp.dot(vt, p, preferred_element_type=F32)


def _nsa_kernel(qt_ref, ng_ref, kc_ref, vct_ref, kp_ref, vst_ref, vwt_ref, e_ref, tri_ref, wpat_ref,
                cpat_ref, o_ref, bias_ref, w_ref, sbuf_ref, oc_ref, pg_ref, ms_ref, as_ref):
    nq = Q_BLOCK
    nsub = qt_ref.shape[2] // nq
    ncol = NSA_GROUP * nq
    n_sel = bias_ref.shape[1]
    ncp = kc_ref.shape[2]
    kw = 2 * HEAD_DIM
    bpt = SEL_TK // SLC_BLOCK
    n_tiles_total = kp_ref.shape[1] // SEL_TK
    qb_first = pl.program_id(2) * nsub
    wlen = WINDOW + nq

    def query_setup(u):
        qb = qb_first + u
        q0 = pl.multiple_of(qb * nq, nq)
        qt = qt_ref[0, :, u * nq:(u + 1) * nq] * (HEAD_DIM ** -0.5 * LOG2E)
        qs = jnp.concatenate([qt[r * HEAD_DIM:(r + 1) * HEAD_DIM, :] for r in range(NSA_GROUP)], axis=1)
        zq = jnp.zeros_like(qs)
        tq1 = q0 + lax.broadcasted_iota(jnp.int32, (1, nq), 1)
        return dict(
            q0=q0, q_c=qs.astype(BF16),
            q_slc=jnp.concatenate([qs, zq], axis=0).astype(BF16),
            q_win=jnp.concatenate([zq, qs], axis=0).astype(BF16),
            tq=q0 + (lax.broadcasted_iota(jnp.int32, (1, ncol), 1) & (nq - 1)),
            coff=pl.multiple_of(ncp - qb * (nq // CMP_STRIDE), nq // CMP_STRIDE),
            cur=jnp.right_shift(tq1, 6).astype(F32),
            first_diag=(2 * qb).astype(F32))

    qs_ = [query_setup(u) for u in range(nsub)]

    crow = min(CMP_CLASS_ROWS, ncp)
    qb_per_class = crow // (nq // CMP_STRIDE)
    assert qb_per_class % nsub == 0
    cls = qb_first // qb_per_class
    ratio = SLC_BLOCK // CMP_STRIDE
    for c in range(ncp // crow):
        nc_c = crow * (c + 1)
        ns_c = nc_c * CMP_STRIDE // SLC_BLOCK

        @pl.when(cls == c)
        def _(nc_c=nc_c, ns_c=ns_c):
            j_io = lax.broadcasted_iota(jnp.int32, (ns_c, nq), 0).astype(F32)
            scores = []
            for u, q in enumerate(qs_):
                pg_ref[u, 0:PG_PAD, :] = jnp.zeros((PG_PAD, nq), F32)
                s_c = (jnp.dot(kc_ref[0, 0, 0:nc_c, :], q["q_c"], preferred_element_type=F32)
                       + jnp.concatenate([cpat_ref[pl.ds(q["coff"], nc_c), :]] * NSA_GROUP, axis=1))
                m_c = jnp.max(s_c, axis=0, keepdims=True)
                e_c = jnp.exp2(s_c - m_c)
                l_c = jnp.sum(e_c, axis=0, keepdims=True)
                p_c = e_c * jnp.where(q["tq"] >= CMP_LEN - 1, 1.0 / l_c, 0.0)
                oc_ref[u] = jnp.dot(vct_ref[0, 0, :, 0:nc_c], p_c.astype(BF16), preferred_element_type=F32)

                p_grp = p_c[:, 0:nq]
                for r in range(1, NSA_GROUP):
                    p_grp = p_grp + p_c[:, r * nq:(r + 1) * nq]
                pg_ref[u, PG_PAD:PG_PAD + nc_c, :] = p_grp
                taps = [pg_ref[u, pl.ds(PG_PAD - 1 + k, ns_c, stride=ratio), :] for k in range(ratio + 1)]
                score = taps[0] + 2.0 * (taps[1] + taps[2] + taps[3]) + taps[4]
                score = jnp.where(j_io <= q["cur"], score, -1.0)
                score = jnp.where(j_io == 0.0, PICKED, score)
                score = jnp.where(j_io == q["cur"], PICKED, score)
                score = jnp.where(j_io == q["cur"] - 1.0, PICKED, score)
                scores.append(score)
            for _ in range(SLC_TOPN - 3):
                for u in range(nsub):
                    mx = jnp.max(scores[u], axis=0, keepdims=True)
                    first = jnp.min(jnp.where(scores[u] == mx, j_io, float(ns_c)), axis=0, keepdims=True)
                    scores[u] = jnp.where(j_io == first, PICKED, scores[u])
            for u, q in enumerate(qs_):
                bias = jnp.where(j_io < q["first_diag"], jnp.where(scores[u] == PICKED, 0.0, NEG), NEG)
                bias_ref[u, 0:ns_c, :] = bias.astype(BF16)
                if ns_c < n_sel:
                    bias_ref[u, ns_c:, :] = jnp.full((n_sel - ns_c, nq), NEG, BF16)

    def prepare(u):
        q0, q_slc, q_win = qs_[u]["q0"], qs_[u]["q_slc"], qs_[u]["q_win"]
        o_c = oc_ref[u]

        wstart = pl.multiple_of(jnp.maximum(q0 - WINDOW, 0), nq)
        woff = pl.multiple_of(jnp.maximum(WINDOW - q0, 0), nq)
        s_w = jnp.dot(kp_ref[0, pl.ds(wstart, wlen), :], q_win, preferred_element_type=F32)
        s_w = s_w + jnp.concatenate([wpat_ref[pl.ds(woff, wlen), :]] * NSA_GROUP, axis=1)
        p_w = jnp.exp2(s_w - jnp.max(s_w, axis=0, keepdims=True)).astype(BF16)
        acc_w = jnp.dot(vwt_ref[0, :, pl.ds(wstart, wlen)], p_w, preferred_element_type=F32)
        o_w = acc_w[0:HEAD_DIM] / acc_w[HEAD_DIM:HEAD_DIM + 1]

        for slot in range(2):
            w_ref[slot, u, 0:kw, :] = q_slc
            w_ref[slot, u, kw:, :] = jnp.zeros((w_ref.shape[2] - kw, ncol), BF16)
        return q0, q_slc, o_c, o_w

    prepared = [prepare(u) for u in range(nsub)]

    def tile_start(j):
        jc = jnp.minimum(j, n_tiles_total - 1)
        return jc, pl.multiple_of(jc * SEL_TK, SEL_TK)

    def produce(j, slot):
        jc, k0 = tile_start(j)
        lhs = jnp.concatenate([kp_ref[0, pl.ds(k0, SEL_TK), :], e_ref[...]], axis=1)
        mts = []
        for u in range(nsub):
            b16 = bias_ref[u, pl.ds(pl.multiple_of(jc * bpt, bpt), bpt), :]
            w_ref[slot, u, kw:kw + bpt, :] = jnp.concatenate([b16] * NSA_GROUP, axis=1)
            s = jnp.dot(lhs, w_ref[slot, u], preferred_element_type=F32)
            sbuf_ref[slot, u] = s
            mts.append(jnp.max(s, axis=0, keepdims=True))
        return tuple(mts)

    def consume(j, slot, mts, state):
        _, k0 = tile_start(j)
        vt = vst_ref[0, :, pl.ds(k0, SEL_TK)]
        return tuple(_accumulate(sbuf_ref[slot, u], mts[u], vt, *state[u]) for u in range(nsub))

    def pair(i, carry):
        mt_a, state = carry
        mt_b = produce(2 * i + 1, 1)
        state = consume(2 * i, 0, mt_a, state)
        mt_a = produce(2 * i + 2, 0)
        state = consume(2 * i + 1, 1, mt_b, state)
        return mt_a, state

    def park(state):
        for u in range(nsub):
            ms_ref[u] = state[u][0]
            as_ref[u] = state[u][1]

    n_main = jnp.maximum(((qb_first + nsub - 1) * nq + (SEL_TK - 1)) // SEL_TK, 1)
    n_pairs = (n_main + 1) // 2
    state = tuple((jnp.full((1, ncol), NEG, F32), jnp.zeros((vst_ref.shape[1], ncol), F32)) for _ in range(nsub))
    mt_a, state = lax.fori_loop(0, n_pairs - 1, pair, (produce(0, 0), state))
    tail = 2 * (n_pairs - 1)
    odd = (n_main & 1) == 1

    @pl.when(odd)
    def _():
        park(consume(tail, 0, mt_a, state))

    @pl.when(jnp.logical_not(odd))
    def _():
        mt_b = produce(tail + 1, 1)
        park(consume(tail + 1, 1, mt_b, consume(tail, 0, mt_a, state)))

    state = tuple((ms_ref[u], as_ref[u]) for u in range(nsub))

    for u in range(nsub):
        q0, q_slc, o_c, o_w = prepared[u]
        s_d = jnp.dot(kp_ref[0, pl.ds(q0, nq), :], q_slc, preferred_element_type=F32) + tri_ref[...]
        _, acc_s = _accumulate(s_d, jnp.max(s_d, axis=0, keepdims=True), vst_ref[0, :, pl.ds(q0, nq)], *state[u])
        o_s = acc_s[0:HEAD_DIM] / acc_s[HEAD_DIM:HEAD_DIM + 1]
        gts = jax.nn.sigmoid(ng_ref[0, :, u * nq:(u + 1) * nq])
        rows = []
        for r in range(NSA_GROUP):
            sl = slice(r * nq, (r + 1) * nq)
            rows.append(gts[3 * r:3 * r + 1, :] * o_c[:, sl] + gts[3 * r + 1:3 * r + 2, :] * o_s[:, sl]
                        + gts[3 * r + 2:3 * r + 3, :] * o_w[:, sl])
        y_t = jnp.concatenate(rows, axis=0)
        o_ref[0, u * nq:(u + 1) * nq, :] = y_t.T.astype(o_ref.dtype)


def _nsa_constants(ncp):
    nq = Q_BLOCK
    ql = jnp.arange(nq)[None, :]
    r = jnp.arange(SEL_TK)[:, None]
    e = (r // SLC_BLOCK == jnp.arange(2 * HEAD_DIM)[None, :]).astype(BF16)
    rd = jnp.arange(nq)[:, None]
    tri = jnp.tile(jnp.where(rd <= ql, 0.0, NEG).astype(F32), (1, NSA_GROUP))
    rw = jnp.arange(2 * WINDOW + nq)[:, None]
    wpat = jnp.where((rw > ql) & (rw <= ql + WINDOW), 0.0, NEG).astype(F32)
    rc = jnp.arange(2 * ncp)[:, None] - ncp
    cpat = jnp.where(CMP_STRIDE * rc + (CMP_LEN - 1) <= ql, 0.0, NEG).astype(F32)
    return e, tri, wpat, cpat


def nsa_attention(proj_t, kc, vct, hn, vt_pack):
    b, _, t = proj_t.shape
    ncp = kc.shape[2]
    n_sel = t // SLC_BLOCK
    gq = NSA_GROUP * HEAD_DIM
    qstep = NSA_QSUB * Q_BLOCK
    e, tri, wpat, cpat = _nsa_constants(ncp)
    const2 = lambda i, g, q: (0, 0)
    return pl.pallas_call(
        _nsa_kernel,
        grid=(b, NSA_KV_GROUPS, t // qstep),
        in_specs=[
            pl.BlockSpec((1, gq, qstep), lambda i, g, q: (i, g, q)),
            pl.BlockSpec((1, 16, qstep), lambda i, g, q: (i, TR_NG // 16 + g, q)),
            pl.BlockSpec((1, 1, ncp, HEAD_DIM), lambda i, g, q: (i, g, 0, 0)),
            pl.BlockSpec((1, 1, HEAD_DIM, ncp), lambda i, g, q: (i, g, 0, 0)),
            pl.BlockSpec((1, t, 2 * HEAD_DIM), lambda i, g, q: (i, 0, NAT_KP // (2 * HEAD_DIM) + g)),
            pl.BlockSpec((1, VT_ROWS, t), lambda i, g, q: (i, g, 0)),
            pl.BlockSpec((1, VT_ROWS, t), lambda i, g, q: (i, NSA_KV_GROUPS + g, 0)),
            pl.BlockSpec(e.shape, const2),
            pl.BlockSpec(tri.shape, const2),
            pl.BlockSpec(wpat.shape, const2),
            pl.BlockSpec(cpat.shape, const2),
        ],
        out_specs=pl.BlockSpec((1, qstep, gq), lambda i, g, q: (i, q, g)),
        out_shape=jax.ShapeDtypeStruct((b, t, NSA_HEADS * HEAD_DIM), BF16),
        scratch_shapes=[pltpu.VMEM((NSA_QSUB, n_sel, Q_BLOCK), BF16),
                        pltpu.VMEM((2, NSA_QSUB, 4 * HEAD_DIM, NSA_GROUP * Q_BLOCK), BF16),
                        pltpu.VMEM((2, NSA_QSUB, SEL_TK, NSA_GROUP * Q_BLOCK), F32),
                        pltpu.VMEM((NSA_QSUB, HEAD_DIM, NSA_GROUP * Q_BLOCK), F32),
                        pltpu.VMEM((NSA_QSUB, PG_PAD + ncp, Q_BLOCK), F32),
                        pltpu.VMEM((NSA_QSUB, 1, NSA_GROUP * Q_BLOCK), F32),
                        pltpu.VMEM((NSA_QSUB, VT_ROWS, NSA_GROUP * Q_BLOCK), F32)],
        compiler_params=_cparams(3),
        name="nsa_attention",
    )(proj_t, proj_t, kc, vct, hn, vt_pack, vt_pack, e, tri, wpat, cpat)


def _ret_kernel(rq_ref, rk_ref, rv_ref, rg_ref, cos_ref, sin_ref, dec_ref, qd_ref, kd_ref, cd_ref,
                gg_ref, gb_ref, o_ref, st_ref):
    c = RET_CHUNK
    hd = RET_HEAD_DIM

    @pl.when(pl.program_id(1) == 0)
    def _():
        st_ref[...] = jnp.zeros_like(st_ref)

    n_chunks = rq_ref.shape[1] // c

    def chunk(ci, _):
        r0 = pl.multiple_of(ci * c, c)
        cos = cos_ref[pl.ds(r0, c), :]
        sin = sin_ref[pl.ds(r0, c), :]
        for h in range(RET_HEADS):
            hs = slice(h * hd, (h + 1) * hd)
            q = rq_ref[0, pl.ds(r0, c), hs].astype(F32)
            k = rk_ref[0, pl.ds(r0, c), hs].astype(F32)
            v = rv_ref[0, pl.ds(r0, c), hs]
            q = q * cos + pltpu.roll(q, hd // 2, 1) * sin
            k = (k * cos + pltpu.roll(k, hd // 2, 1) * sin) * (hd ** -0.5)
            qb = q.astype(BF16)
            kb = k.astype(BF16)
            vb = v.astype(BF16)
            inner = lax.dot_general(qb, kb, (((1,), (1,)), ((), ())), preferred_element_type=F32) * dec_ref[h]
            st = st_ref[h]
            o = (jnp.dot(inner.astype(BF16), vb, preferred_element_type=F32)
                 + jnp.dot(qb, st.astype(BF16), preferred_element_type=F32) * qd_ref[h])
            kdt = (k * kd_ref[h]).T.astype(BF16)
            st_ref[h] = st * cd_ref[h] + jnp.dot(kdt, vb, preferred_element_type=F32)
            mu = jnp.mean(o, axis=-1, keepdims=True)
            d = o - mu
            var = jnp.mean(d * d, axis=-1, keepdims=True)
            y = d * lax.rsqrt(var + LN_EPS) * gg_ref[:, hs] + gb_ref[:, hs]
            o_ref[0, pl.ds(r0, c), hs] = (jax.nn.silu(rg_ref[0, pl.ds(r0, c), hs].astype(F32)) * y).astype(o_ref.dtype)
        return 0

    lax.fori_loop(0, n_chunks, chunk, 0)


def retention_branch(hn, gn_g, gn_b, tr):
    b, t, _ = hn.shape
    w = BRANCH_WIDTH
    hd = RET_HEAD_DIM
    c = RET_CHUNK
    inv = ROPE_BASE ** (-jnp.arange(0, hd, 2, dtype=F32) / hd)
    ang = jnp.arange(t, dtype=F32)[:, None] * inv[None, :]
    cos2 = jnp.concatenate([jnp.cos(ang), jnp.cos(ang)], axis=1)
    sin2 = jnp.concatenate([-jnp.sin(ang), jnp.sin(ang)], axis=1)
    log_g = jnp.log(1.0 - 2.0 ** (-5.0 - jnp.arange(RET_HEADS, dtype=F32)))
    idx = jnp.arange(c, dtype=F32)
    diff = idx[:, None] - idx[None, :]
    decay_in = jnp.where(diff >= 0, jnp.exp(log_g[:, None, None] * jnp.maximum(diff, 0.0)), 0.0)
    ones = jnp.ones((RET_HEADS, c, hd), F32)
    q_dec = jnp.exp(log_g[:, None] * (idx + 1.0))[:, :, None] * ones
    k_dec = jnp.exp(log_g[:, None] * (c - 1.0 - idx))[:, :, None] * ones
    c_dec = jnp.exp(log_g * c)[:, None, None] * jnp.ones((RET_HEADS, hd, hd), F32)
    col = lambda cc: (lambda i, j: (i, j, cc // w))
    full3 = lambda i, j: (0, 0, 0)
    return pl.pallas_call(
        _ret_kernel,
        grid=(b, t // tr),
        in_specs=[
            pl.BlockSpec((1, tr, w), col(NAT_RQ)),
            pl.BlockSpec((1, tr, w), col(NAT_RK)),
            pl.BlockSpec((1, tr, w), col(NAT_RV)),
            pl.BlockSpec((1, tr, w), col(NAT_RG)),
            pl.BlockSpec((tr, hd), lambda i, j: (j, 0)),
            pl.BlockSpec((tr, hd), lambda i, j: (j, 0)),
            pl.BlockSpec((RET_HEADS, c, c), full3),
            pl.BlockSpec((RET_HEADS, c, hd), full3),
            pl.BlockSpec((RET_HEADS, c, hd), full3),
            pl.BlockSpec((RET_HEADS, hd, hd), full3),
            pl.BlockSpec((1, w), lambda i, j: (0, 0)),
            pl.BlockSpec((1, w), lambda i, j: (0, 0)),
        ],
        out_specs=pl.BlockSpec((1, tr, w), lambda i, j: (i, j, 0)),
        out_shape=jax.ShapeDtypeStruct((b, t, w), BF16),
        scratch_shapes=[pltpu.VMEM((RET_HEADS, hd, hd), F32)],
        compiler_params=_cparams(2),
        name="retention",
    )(hn, hn, hn, hn, cos2, sin2, decay_in, q_dec, k_dec, c_dec, gn_g.reshape(1, w), gn_b.reshape(1, w))


def _layer_norm(z, g, b):
    mu = jnp.mean(z, axis=-1, keepdims=True)
    d = z - mu
    var = jnp.mean(d * d, axis=-1, keepdims=True)
    return d * lax.rsqrt(var + LN_EPS) * g + b


def _merge_kernel(tiles_per_seq, x_ref, ya_ref, cb_ref, cc_ref, ch_ref, pc_ref, ph_ref, cw_ref, yc_ref, mg_ref,
                  wbr_ref, wout_ref, g_ref, b_ref, o_ref, ob_ref):
    first = (pl.program_id(0) % tiles_per_seq) == 0
    u = cc_ref[...].astype(F32) * ch_ref[...].astype(F32)
    prev = pc_ref[...].astype(F32) * ph_ref[...].astype(F32)
    prev = jnp.where(first, 0.0, prev)
    row = lax.broadcasted_iota(jnp.int32, u.shape, 0)
    p1 = prev[CONV_HALO - 1:CONV_HALO, :]
    p2 = prev[CONV_HALO - 2:CONV_HALO - 1, :]
    u1 = jnp.where(row == 0, p1, pltpu.roll(u, 1, 0))
    u2 = jnp.where(row == 0, p2, jnp.where(row == 1, p1, pltpu.roll(u, 2, 0)))
    cw = cw_ref[...]
    y_b = cb_ref[...].astype(F32) * (cw[0:1, :] * u2 + cw[1:2, :] * u1 + cw[2:3, :] * u)

    mix = None
    for c, y in enumerate((ya_ref[...], y_b.astype(BF16), yc_ref[...])):
        proj = jnp.dot(y, wbr_ref[c], preferred_element_type=F32)
        gate = jax.nn.sigmoid(mg_ref[:, c * D_MODEL:(c + 1) * D_MODEL].astype(F32))
        mix = gate * proj if mix is None else mix + gate * proj
    m = jnp.dot(mix.astype(BF16), wout_ref[...], preferred_element_type=F32)
    out = _layer_norm(ALPHA * x_ref[...] + m, g_ref[...], b_ref[...])
    o_ref[...] = out
    ob_ref[...] = out.astype(BF16)


def merge_branches(x, ya, yc, hn, conv_w, w_br, w_out, g, b, tm, t):
    m = x.shape[0]
    d = D_MODEL
    w = BRANCH_WIDTH
    row = lambda i: (i, 0)
    col = lambda c: (lambda i: (i, c // w))
    hal = lambda c: (lambda i: (jnp.maximum(i * (tm // CONV_HALO) - 1, 0), c // w))
    wpad = jnp.zeros((8, w), F32).at[:CONV_K].set(conv_w)
    return pl.pallas_call(
        functools.partial(_merge_kernel, t // tm),
        grid=(m // tm,),
        in_specs=[
            pl.BlockSpec((tm, d), row),
            pl.BlockSpec((tm, w), row),
            pl.BlockSpec((tm, w), col(NAT_CB)),
            pl.BlockSpec((tm, w), col(NAT_CC)),
            pl.BlockSpec((tm, w), col(NAT_CH)),
            pl.BlockSpec((CONV_HALO, w), hal(NAT_CC)),
            pl.BlockSpec((CONV_HALO, w), hal(NAT_CH)),
            pl.BlockSpec((8, w), lambda i: (0, 0)),
            pl.BlockSpec((tm, w), row),
            pl.BlockSpec((tm, 3 * d), row),
            pl.BlockSpec((3, w, d), lambda i: (0, 0, 0)),
            pl.BlockSpec((d, d), lambda i: (0, 0)),
            pl.BlockSpec((1, d), lambda i: (0, 0)),
            pl.BlockSpec((1, d), lambda i: (0, 0)),
        ],
        out_specs=[pl.BlockSpec((tm, d), row), pl.BlockSpec((tm, d), row)],
        out_shape=[jax.ShapeDtypeStruct((m, d), F32), jax.ShapeDtypeStruct((m, d), BF16)],
        compiler_params=_cparams(1),
        name="merge",
    )(x, ya, hn, hn, hn, hn, hn, wpad, yc, hn, w_br.astype(BF16), w_out.astype(BF16), g.reshape(1, d),
      b.reshape(1, d))


def _cast3_kernel(g_ref, u_ref, d_ref, og_ref, ou_ref, od_ref):
    og_ref[...] = g_ref[0].astype(og_ref.dtype)
    ou_ref[...] = u_ref[0].astype(ou_ref.dtype)
    od_ref[...] = d_ref[0].astype(od_ref.dtype)


def ffn_weights_bf16(w_gate, w_up, w_down, layer):
    _, e, d, ff = w_gate.shape
    rg, rd = d // CAST_STEPS, ff // CAST_STEPS
    up_spec = pl.BlockSpec((1, 1, rg, ff), lambda i, j: (layer, i, j, 0))
    return pl.pallas_call(
        _cast3_kernel,
        grid=(e, CAST_STEPS),
        in_specs=[up_spec, up_spec, pl.BlockSpec((1, 1, rd, d), lambda i, j: (layer, i, j, 0))],
        out_specs=[pl.BlockSpec((1, rg, ff), lambda i, j: (i, j, 0)), pl.BlockSpec((1, rg, ff), lambda i, j: (i, j, 0)),
                   pl.BlockSpec((1, rd, d), lambda i, j: (i, j, 0))],
        out_shape=[jax.ShapeDtypeStruct((e, d, ff), BF16), jax.ShapeDtypeStruct((e, d, ff), BF16),
                   jax.ShapeDtypeStruct((e, ff, d), BF16)],
        compiler_params=_cparams(2),
        name="cast_bf16",
    )(w_gate, w_up, w_down)


def _swiglu_chunks(x, wg_ref, wu_ref, wd_ref, tf):
    acc = None
    for c in range(wg_ref.shape[2] // tf):
        sl = slice(c * tf, (c + 1) * tf)
        gte = jnp.dot(x, wg_ref[0, :, sl], preferred_element_type=F32)
        up = jnp.dot(x, wu_ref[0, :, sl], preferred_element_type=F32)
        h = (jax.nn.silu(gte) * up).astype(BF16)
        part = jnp.dot(h, wd_ref[0, sl, :], preferred_element_type=F32)
        acc = part if acc is None else acc + part
    return acc


def _mlp_kernel(tf, te_ref, tv_ref, x_ref, wg_ref, wu_ref, wd_ref, o_ref):
    i = pl.program_id(0)

    @pl.when(tv_ref[i] > 0)
    def _():
        o_ref[...] = _swiglu_chunks(x_ref[...], wg_ref, wu_ref, wd_ref, tf).astype(o_ref.dtype)

    @pl.when(tv_ref[i] == 0)
    def _():
        o_ref[...] = jnp.zeros_like(o_ref)


def _resident(block_shape, index_map):
    return pl.BlockSpec(block_shape, index_map, pipeline_mode=pl.Buffered(1))


def grouped_swiglu(xs, w_gate, w_up, w_down, tile_expert, tile_valid, tm, tf):
    n, d = xs.shape
    ff = w_gate.shape[2]
    grid_spec = pltpu.PrefetchScalarGridSpec(
        num_scalar_prefetch=2,
        grid=(n // tm,),
        in_specs=[
            pl.BlockSpec((tm, d), lambda i, te, tv: (i, 0)),
            _resident((1, d, ff), lambda i, te, tv: (te[i], 0, 0)),
            _resident((1, d, ff), lambda i, te, tv: (te[i], 0, 0)),
            _resident((1, ff, d), lambda i, te, tv: (te[i], 0, 0)),
        ],
        out_specs=pl.BlockSpec((tm, d), lambda i, te, tv: (i, 0)),
    )
    return pl.pallas_call(
        functools.partial(_mlp_kernel, tf),
        grid_spec=grid_spec,
        out_shape=jax.ShapeDtypeStruct((n, d), BF16),
        compiler_params=_cparams(1, BIG_VMEM_LIMIT),
        name="grouped_swiglu",
    )(tile_expert, tile_valid, xs, w_gate, w_up, w_down)


def _add_ln_kernel(*refs):
    x_ref, sc_ref, *adds, g_ref, b_ref, o_ref, ob_ref = refs
    z = ALPHA * x_ref[...]
    for k, a in enumerate(adds):
        z = z + sc_ref[:, k:k + 1] * a[...].astype(F32)
    out = _layer_norm(z, g_ref[...], b_ref[...])
    o_ref[...] = out
    ob_ref[...] = out.astype(BF16)


def add_layer_norm(x, adds, scales, g, b, tm):
    m, d = x.shape
    row = lambda i: (i, 0)
    return pl.pallas_call(
        _add_ln_kernel,
        grid=(m // tm,),
        in_specs=([pl.BlockSpec((tm, d), row), pl.BlockSpec((tm, len(adds)), row)]
                  + [pl.BlockSpec((tm, d), row)] * len(adds) + [pl.BlockSpec((1, d), lambda i: (0, 0))] * 2),
        out_specs=[pl.BlockSpec((tm, d), row)] * 2,
        out_shape=[jax.ShapeDtypeStruct((m, d), F32), jax.ShapeDtypeStruct((m, d), BF16)],
        compiler_params=_cparams(1),
        name="add_layer_norm",
    )(x, scales, *adds, g.reshape(1, d), b.reshape(1, d))


def _router_kernel(x_ref, wh_ref, wl_ref, o_ref):
    x = x_ref[...]
    xh = x.astype(BF16)
    xl = (x - xh.astype(F32)).astype(BF16)
    wh = wh_ref[...]
    o_ref[...] = (jnp.dot(xh, wh, preferred_element_type=F32) + jnp.dot(xl, wh, preferred_element_type=F32)
                  + jnp.dot(xh, wl_ref[...], preferred_element_type=F32))


def router_logits(x, w_router, tm):
    m, d = x.shape
    wp = jnp.zeros((d, 128), F32).at[:, :N_EXPERTS].set(w_router)
    wh = wp.astype(BF16)
    wl = (wp - wh.astype(F32)).astype(BF16)
    return pl.pallas_call(
        _router_kernel,
        grid=(m // tm,),
        in_specs=[pl.BlockSpec((tm, d), lambda i: (i, 0)), pl.BlockSpec((d, 128), lambda i: (0, 0)),
                  pl.BlockSpec((d, 128), lambda i: (0, 0))],
        out_specs=pl.BlockSpec((tm, 128), lambda i: (i, 0)),
        out_shape=jax.ShapeDtypeStruct((m, 128), F32),
        compiler_params=_cparams(1),
        name="router",
    )(x, wh, wl)


def _dense_ffn_kernel(tf, xb_ref, x_ref, wg_ref, wu_ref, wd_ref, g_ref, b_ref, o_ref, ob_ref):
    acc = _swiglu_chunks(xb_ref[...], wg_ref, wu_ref, wd_ref, tf)
    out = _layer_norm(ALPHA * x_ref[...] + acc, g_ref[...], b_ref[...])
    o_ref[...] = out
    ob_ref[...] = out.astype(BF16)


def dense_ffn(x, xb, w_gate, w_up, w_down, g, b):
    m, d = x.shape
    ff = w_gate.shape[2]
    tm = min(FFN_TM, m)
    row = lambda i: (i, 0)
    return pl.pallas_call(
        functools.partial(_dense_ffn_kernel, FFN_TF),
        grid=(m // tm,),
        in_specs=[
            pl.BlockSpec((tm, d), row),
            pl.BlockSpec((tm, d), row),
            _resident((1, d, ff), lambda i: (0, 0, 0)),
            _resident((1, d, ff), lambda i: (0, 0, 0)),
            _resident((1, ff, d), lambda i: (0, 0, 0)),
            pl.BlockSpec((1, d), lambda i: (0, 0)),
            pl.BlockSpec((1, d), lambda i: (0, 0)),
        ],
        out_specs=[pl.BlockSpec((tm, d), row)] * 2,
        out_shape=[jax.ShapeDtypeStruct((m, d), F32), jax.ShapeDtypeStruct((m, d), BF16)],
        compiler_params=_cparams(1, BIG_VMEM_LIMIT),
        name="dense_ffn",
    )(xb, x, w_gate, w_up, w_down, g.reshape(1, d), b.reshape(1, d))


def _rank_kernel(e_ref, u_ref, rank_ref, cnt_ref, carry_ref):
    @pl.when(pl.program_id(0) == 0)
    def _():
        carry_ref[...] = jnp.zeros_like(carry_ref)

    rows, width = e_ref.shape
    expert = lax.broadcasted_iota(jnp.int32, (N_EXPERTS, width), 0)
    carry = carry_ref[...]
    for r in range(rows):
        hot = e_ref[r:r + 1, :] == expert
        prefix = jnp.dot(hot.astype(BF16), u_ref[...], preferred_element_type=F32) + carry
        rank_ref[r:r + 1, :] = (jnp.sum(jnp.where(hot, prefix, 0.0), axis=0, keepdims=True) - 1.0).astype(jnp.int32)
        carry = prefix[:, width - 1:width]
    carry_ref[...] = carry
    cnt_ref[...] = jnp.broadcast_to(carry, cnt_ref.shape).astype(jnp.int32)


def routing_ranks(e_flat):
    n = e_flat.shape[0]
    width = RANK_WIDTH
    rows = 8
    e2 = e_flat.reshape(n // width, width)
    upper = (jnp.arange(width)[:, None] <= jnp.arange(width)[None, :]).astype(BF16)
    rank, cnt = pl.pallas_call(
        _rank_kernel,
        grid=(n // (rows * width),),
        in_specs=[pl.BlockSpec((rows, width), lambda i: (i, 0)), pl.BlockSpec((width, width), lambda i: (0, 0))],
        out_specs=[pl.BlockSpec((rows, width), lambda i: (i, 0)), pl.BlockSpec((N_EXPERTS, 128), lambda i: (0, 0))],
        out_shape=[jax.ShapeDtypeStruct((n // width, width), jnp.int32),
                   jax.ShapeDtypeStruct((N_EXPERTS, 128), jnp.int32)],
        scratch_shapes=[pltpu.VMEM((N_EXPERTS, 1), F32)],
        compiler_params=_cparams(1),
        name="routing_ranks",
    )(e2, upper)
    return rank.reshape(n), cnt[:, 0]


def moe_ffn(x, xb, w_router, w_gate, w_up, w_down, g, b):
    m = x.shape[0]
    tm = FFN_TM
    logits = router_logits(x, w_router, min(ROW_TILE, m))[:, :N_EXPERTS]
    top_v, top_i = lax.top_k(logits, TOP_K)
    probs = jax.nn.softmax(top_v, axis=-1)
    e_flat = top_i.reshape(-1)
    rank, counts = routing_ranks(e_flat)
    padded = ((counts + tm - 1) // tm) * tm
    ends = jnp.cumsum(padded)
    starts = ends - padded
    pos = starts[e_flat] + rank
    n_slots = TOP_K * m + N_EXPERTS * tm
    n_tiles = n_slots // tm
    tok = jnp.arange(TOP_K * m, dtype=jnp.int32) // TOP_K
    src = jnp.zeros((n_slots,), jnp.int32).at[pos].set(tok)
    tile_start = jnp.arange(n_tiles, dtype=jnp.int32) * tm
    tile_expert = jnp.minimum(jnp.searchsorted(ends, tile_start, side="right"), N_EXPERTS - 1).astype(jnp.int32)
    tile_valid = (tile_start < ends[-1]).astype(jnp.int32)
    xs = jnp.take(xb, src, axis=0)
    ys = grouped_swiglu(xs, w_gate, w_up, w_down,
                        tile_expert, tile_valid, tm, FFN_TF)
    pos2 = pos.reshape(m, TOP_K)
    y0 = jnp.take(ys, pos2[:, 0], axis=0)
    y1 = jnp.take(ys, pos2[:, 1], axis=0)
    return add_layer_norm(x, [y0, y1], probs, g, b, min(ROW_TILE, m))


def _pack_in_weights(w):
    d = w.shape[0]
    o = 0
    q = w[:, o:o + 512]; o += 512
    kv = w[:, o:o + 768].reshape(d, 3, 2, NSA_KV_GROUPS, HEAD_DIM); o += 768
    ng = w[:, o:o + 24]; o += 24
    rest = w[:, o:o + 7 * 512]; o += 7 * 512
    mg = w[:, o:]
    kpack = jnp.concatenate([kv[:, 1, 0], kv[:, 2, 0]], axis=-1).reshape(d, 2 * NSA_KV_GROUPS * HEAD_DIM)
    kc = kv[:, 0, 0].reshape(d, NSA_KV_GROUPS * HEAD_DIM)
    vc = kv[:, 0, 1].reshape(d, NSA_KV_GROUPS * HEAD_DIM)
    w_nat = jnp.concatenate([mg, rest, kpack, kc, vc], axis=1).astype(BF16)
    vs = kv[:, 1, 1].reshape(d, NSA_KV_GROUPS * HEAD_DIM)
    vw = kv[:, 2, 1].reshape(d, NSA_KV_GROUPS * HEAD_DIM)
    ngp = jnp.pad(ng.reshape(d, NSA_KV_GROUPS, NSA_GROUP * 3), ((0, 0), (0, 0), (0, 4))).reshape(d, 32)
    w_t = jnp.concatenate([q, vs, vw, ngp], axis=1).T.astype(BF16)
    return w_nat, w_t


def token_mixer_ln(x, xb, w_in, cmp_pos, cmp_w1, cmp_b1, cmp_w2, cmp_b2, conv_w, gn_g, gn_b, w_br, w_out, ln_g, ln_b):
    b, t, d = x.shape
    m = b * t
    w_nat, w_t = _pack_in_weights(w_in)
    x2 = x.reshape(m, d)
    hn, proj_t = input_projection(xb, w_nat, w_t, min(PROJ_TM, t), PROJ_TN)
    hn = hn.reshape(b, t, NAT_COLS)
    nch = t // CMP_STRIDE
    raw = hn[:, :, NAT_KC:NAT_KC + 2 * NSA_KV_GROUPS * HEAD_DIM].reshape(b, t, 2, NSA_KV_GROUPS, HEAD_DIM)
    chunks = raw.transpose(2, 0, 3, 1, 4).reshape(2, b, NSA_KV_GROUPS, nch, CMP_STRIDE * HEAD_DIM)
    cn, ct = nsa_compress_kv(chunks, cmp_pos, cmp_w1, cmp_b1, cmp_w2, cmp_b2)
    kc = cn[0].astype(BF16)
    vct = ct[1].astype(BF16)
    vt4 = proj_t[:, TR_VS:TR_NG, :].astype(BF16).reshape(b, 4, HEAD_DIM, t)
    ones_pad = jnp.zeros((b, 4, VT_ROWS - HEAD_DIM, t), BF16).at[:, :, 0].set(1.0)
    vt_pack = jnp.concatenate([vt4, ones_pad], axis=2).reshape(b, 4 * VT_ROWS, t)
    y_a = nsa_attention(proj_t, kc, vct, hn, vt_pack)
    y_c = retention_branch(hn, gn_g, gn_b, min(ROW_TILE, t))
    return merge_branches(x2, y_a.reshape(m, -1), y_c.reshape(m, -1), hn.reshape(m, NAT_COLS), conv_w,
                          w_br, w_out, ln_g, ln_b, min(MERGE_TM, t), t)


def kernel(x, w_in, cmp_pos, cmp_w1, cmp_b1, cmp_w2, cmp_b2, conv_w, ret_gn_g, ret_gn_b, w_br, w_out,
           ln1_g, ln1_b, ln2_g, ln2_b, ffn_gate, ffn_up, ffn_down, moe_router, moe_gate, moe_up, moe_down):
    b, t, d = x.shape
    xb = x.astype(BF16)
    for l in range(DEPTH):
        x2, x2b = token_mixer_ln(x, xb, w_in[l], cmp_pos[l], cmp_w1[l], cmp_b1[l], cmp_w2[l], cmp_b2[l], conv_w[l],
                                 ret_gn_g[l], ret_gn_b[l], w_br[l], w_out[l], ln1_g[l], ln1_b[l])
        if l % 2 == 0:
            wg, wu, wd = ffn_weights_bf16(ffn_gate[:, None], ffn_up[:, None], ffn_down[:, None], l // 2)
            x2, x2b = dense_ffn(x2, x2b, wg, wu, wd, ln2_g[l], ln2_b[l])
        else:
            wg, wu, wd = ffn_weights_bf16(moe_gate, moe_up, moe_down, l // 2)
            x2, x2b = moe_ffn(x2, x2b, moe_router[l // 2], wg, wu, wd, ln2_g[l], ln2_b[l])
        x, xb = x2.reshape(b, t, d), x2b.reshape(b, t, d)
    return x
```

```python
import functools

import jax
import jax.numpy as jnp
from jax import lax
from jax.experimental import pallas as pl
from jax.experimental.pallas import tpu as pltpu

F32 = jnp.float32
BF16 = jnp.bfloat16

D_MODEL = 1024
DEPTH = 4
BRANCH_WIDTH = 512
HEAD_DIM = 64
NSA_HEADS = 8
NSA_KV_GROUPS = 2
NSA_GROUP = 4
CMP_LEN = 32
CMP_STRIDE = 16
SLC_BLOCK = 64
SLC_TOPN = 16
WINDOW = 512
CMP_HIDDEN = 256
Q_BLOCK = 128
CONV_K = 3
CONV_HALO = 16
RET_HEADS = 4
RET_HEAD_DIM = 128
RET_CHUNK = 256
ROPE_BASE = 10000.0
D_FF = 3584
N_EXPERTS = 8
TOP_K = 2
LN_EPS = 1e-5
ALPHA = (2.0 * DEPTH) ** 0.25

NAT_MG = 0
NAT_CB = 3072
NAT_CC = 3584
NAT_CH = 4096
NAT_RQ = 4608
NAT_RK = 5120
NAT_RV = 5632
NAT_RG = 6144
NAT_KP = 6656
NAT_KC = 6912
NAT_VC = 7040
NAT_COLS = 7168
TR_Q = 0
TR_VS = 512
TR_VW = 640
TR_NG = 768
TR_ROWS = 800

NEG = -1e30
SEL_TK = 1024
LOG2E = 1.4426950408889634
NSA_QSUB = 2
CMP_CLASS_ROWS = 128
PG_PAD = 8
PICKED = -3e38
VT_ROWS = 80
RANK_WIDTH = 1024
ROW_TILE = 1024
MERGE_TM = 512
PROJ_TM = 2048
PROJ_TN = 1024
FFN_TM = 512
FFN_TF = 1792
CAST_STEPS = 4
VMEM_LIMIT = 48 * 1024 * 1024
BIG_VMEM_LIMIT = 56 * 1024 * 1024


def _cparams(n_axes, vmem=VMEM_LIMIT):
    return pltpu.CompilerParams(dimension_semantics=("arbitrary",) * n_axes, vmem_limit_bytes=vmem)


def _proj_kernel(x_ref, w_ref, wt_ref, o_ref, ot_ref):
    @pl.when(pl.program_id(1) == 0)
    def _():
        ot_ref[0] = lax.dot_general(wt_ref[...], x_ref[...], (((1,), (1,)), ((), ())), preferred_element_type=F32)

    o_ref[...] = jnp.dot(x_ref[...], w_ref[...], preferred_element_type=F32).astype(o_ref.dtype)


def input_projection(x, w_nat, w_t, tm, tn):
    b, t, k = x.shape
    m = b * t
    n = w_nat.shape[1]
    r = w_t.shape[0]
    tps = t // tm
    return pl.pallas_call(
        _proj_kernel,
        grid=(m // tm, n // tn),
        in_specs=[pl.BlockSpec((tm, k), lambda i, j: (i, 0)), pl.BlockSpec((k, tn), lambda i, j: (0, j)),
                  pl.BlockSpec((r, k), lambda i, j: (0, 0))],
        out_specs=[pl.BlockSpec((tm, tn), lambda i, j: (i, j)),
                   pl.BlockSpec((1, r, tm), lambda i, j: (i // tps, 0, i % tps))],
        out_shape=[jax.ShapeDtypeStruct((m, n), BF16), jax.ShapeDtypeStruct((b, r, t), F32)],
        compiler_params=_cparams(2, BIG_VMEM_LIMIT),
        name="input_projection",
    )(x.reshape(m, k), w_nat, w_t)


def _cmp_kernel(c_ref, pos_ref, w1_ref, b1_ref, w2_ref, w2t_ref, b2_ref, b2t_ref, on_ref, ot_ref):
    half = CMP_STRIDE * HEAD_DIM
    c = c_ref[0, 0, 0].astype(BF16)
    w1 = w1_ref[0]
    a = jnp.dot(c, w1[:half], preferred_element_type=F32)
    bm = jnp.dot(c, w1[half:], preferred_element_type=F32)
    nch = a.shape[0]
    bm = pltpu.roll(bm, nch - 1, 0)
    posb = jnp.dot(pos_ref[0].astype(BF16), w1, preferred_element_type=F32)[0:1]
    hid = jax.nn.gelu(a + bm + posb + b1_ref[0])
    hb = hid.astype(BF16)
    on_ref[0, 0, 0] = jnp.dot(hb, w2_ref[0], preferred_element_type=F32) + b2_ref[0]
    ot_ref[0, 0, 0] = lax.dot_general(w2t_ref[0], hb, (((1,), (1,)), ((), ())),
                                      preferred_element_type=F32) + b2t_ref[0]


def nsa_compress_kv(chunks, pos, w1, b1, w2, b2):
    _, b, g, nch, cw = chunks.shape
    pos_flat = jnp.broadcast_to(pos.reshape(2, 1, CMP_LEN * HEAD_DIM), (2, 8, CMP_LEN * HEAD_DIM))
    w1b = w1.astype(BF16)
    w2b = w2.astype(BF16)
    w2t = jnp.swapaxes(w2, 1, 2).astype(BF16)
    b1r = b1.reshape(2, 1, CMP_HIDDEN)
    b2r = b2.reshape(2, 1, HEAD_DIM)
    b2t = b2.reshape(2, HEAD_DIM, 1)
    sel = lambda kv, i, j: (kv, 0, 0)
    return pl.pallas_call(
        _cmp_kernel,
        grid=(2, b, g),
        in_specs=[
            pl.BlockSpec((1, 1, 1, nch, cw), lambda kv, i, j: (kv, i, j, 0, 0)),
            pl.BlockSpec((1, 8, CMP_LEN * HEAD_DIM), sel),
            pl.BlockSpec((1, CMP_LEN * HEAD_DIM, CMP_HIDDEN), sel),
            pl.BlockSpec((1, 1, CMP_HIDDEN), sel),
            pl.BlockSpec((1, CMP_HIDDEN, HEAD_DIM), sel),
            pl.BlockSpec((1, HEAD_DIM, CMP_HIDDEN), sel),
            pl.BlockSpec((1, 1, HEAD_DIM), sel),
            pl.BlockSpec((1, HEAD_DIM, 1), sel),
        ],
        out_specs=[
            pl.BlockSpec((1, 1, 1, nch, HEAD_DIM), lambda kv, i, j: (kv, i, j, 0, 0)),
            pl.BlockSpec((1, 1, 1, HEAD_DIM, nch), lambda kv, i, j: (kv, i, j, 0, 0)),
        ],
        out_shape=[
            jax.ShapeDtypeStruct((2, b, g, nch, HEAD_DIM), F32),
            jax.ShapeDtypeStruct((2, b, g, HEAD_DIM, nch), F32),
        ],
        compiler_params=_cparams(3),
        name="nsa_compress",
    )(chunks, pos_flat, w1b, b1r, w2b, w2t, b2r, b2t)


def _accumulate(s, mt, vt, m, acc):
    m_new = jnp.maximum(m, mt)
    alpha = jnp.exp2(m - m_new)
    p = jnp.exp2(s - m_new).astype(BF16)
    return m_new, alpha * acc + jnp.dot(vt, p, preferred_element_type=F32)


def _nsa_kernel(qt_ref, ng_ref, kc_ref, vct_ref, kp_ref, vst_ref, vwt_ref, e_ref, tri_ref, wpat_ref,
                cpat_ref, o_ref, bias_ref, w_ref, sbuf_ref, oc_ref, pg_ref, ms_ref, as_ref):
    nq = Q_BLOCK
    nsub = qt_ref.shape[2] // nq
    ncol = NSA_GROUP * nq
    n_sel = bias_ref.shape[1]
    ncp = kc_ref.shape[2]
    kw = 2 * HEAD_DIM
    bpt = SEL_TK // SLC_BLOCK
    n_tiles_total = kp_ref.shape[1] // SEL_TK
    qb_first = pl.program_id(2) * nsub
    wlen = WINDOW + nq

    def query_setup(u):
        qb = qb_first + u
        q0 = pl.multiple_of(qb * nq, nq)
        qt = qt_ref[0, :, u * nq:(u + 1) * nq] * (HEAD_DIM ** -0.5 * LOG2E)
        qs = jnp.concatenate([qt[r * HEAD_DIM:(r + 1) * HEAD_DIM, :] for r in range(NSA_GROUP)], axis=1)
        zq = jnp.zeros_like(qs)
        tq1 = q0 + lax.broadcasted_iota(jnp.int32, (1, nq), 1)
        return dict(
            q0=q0, q_c=qs.astype(BF16),
            q_slc=jnp.concatenate([qs, zq], axis=0).astype(BF16),
            q_win=jnp.concatenate([zq, qs], axis=0).astype(BF16),
            tq=q0 + (lax.broadcasted_iota(jnp.int32, (1, ncol), 1) & (nq - 1)),
            coff=pl.multiple_of(ncp - qb * (nq // CMP_STRIDE), nq // CMP_STRIDE),
            cur=jnp.right_shift(tq1, 6).astype(F32),
            first_diag=(2 * qb).astype(F32))

    qs_ = [query_setup(u) for u in range(nsub)]

    crow = min(CMP_CLASS_ROWS, ncp)
    qb_per_class = crow // (nq // CMP_STRIDE)
    assert qb_per_class % nsub == 0
    cls = qb_first // qb_per_class
    ratio = SLC_BLOCK // CMP_STRIDE
    for c in range(ncp // crow):
        nc_c = crow * (c + 1)
        ns_c = nc_c * CMP_STRIDE // SLC_BLOCK

        @pl.when(cls == c)
        def _(nc_c=nc_c, ns_c=ns_c):
            j_io = lax.broadcasted_iota(jnp.int32, (ns_c, nq), 0).astype(F32)
            scores = []
            for u, q in enumerate(qs_):
                pg_ref[u, 0:PG_PAD, :] = jnp.zeros((PG_PAD, nq), F32)
                s_c = (jnp.dot(kc_ref[0, 0, 0:nc_c, :], q["q_c"], preferred_element_type=F32)
                       + jnp.concatenate([cpat_ref[pl.ds(q["coff"], nc_c), :]] * NSA_GROUP, axis=1))
                m_c = jnp.max(s_c, axis=0, keepdims=True)
                e_c = jnp.exp2(s_c - m_c)
                l_c = jnp.sum(e_c, axis=0, keepdims=True)
                p_c = e_c * jnp.where(q["tq"] >= CMP_LEN - 1, 1.0 / l_c, 0.0)
                oc_ref[u] = jnp.dot(vct_ref[0, 0, :, 0:nc_c], p_c.astype(BF16), preferred_element_type=F32)

                p_grp = p_c[:, 0:nq]
                for r in range(1, NSA_GROUP):
                    p_grp = p_grp + p_c[:, r * nq:(r + 1) * nq]
                pg_ref[u, PG_PAD:PG_PAD + nc_c, :] = p_grp
                taps = [pg_ref[u, pl.ds(PG_PAD - 1 + k, ns_c, stride=ratio), :] for k in range(ratio + 1)]
                score = taps[0] + 2.0 * (taps[1] + taps[2] + taps[3]) + taps[4]
                score = jnp.where(j_io <= q["cur"], score, -1.0)
                score = jnp.where(j_io == 0.0, PICKED, score)
                score = jnp.where(j_io == q["cur"], PICKED, score)
                score = jnp.where(j_io == q["cur"] - 1.0, PICKED, score)
                scores.append(score)
            for _ in range(SLC_TOPN - 3):
                for u in range(nsub):
                    mx = jnp.max(scores[u], axis=0, keepdims=True)
                    first = jnp.min(jnp.where(scores[u] == mx, j_io, float(ns_c)), axis=0, keepdims=True)
                    scores[u] = jnp.where(j_io == first, PICKED, scores[u])
            for u, q in enumerate(qs_):
                bias = jnp.where(j_io < q["first_diag"], jnp.where(scores[u] == PICKED, 0.0, NEG), NEG)
                bias_ref[u, 0:ns_c, :] = bias.astype(BF16)
                if ns_c < n_sel:
                    bias_ref[u, ns_c:, :] = jnp.full((n_sel - ns_c, nq), NEG, BF16)

    def prepare(u):
        q0, q_slc, q_win = qs_[u]["q0"], qs_[u]["q_slc"], qs_[u]["q_win"]
        o_c = oc_ref[u]

        wstart = pl.multiple_of(jnp.maximum(q0 - WINDOW, 0), nq)
        woff = pl.multiple_of(jnp.maximum(WINDOW - q0, 0), nq)
        s_w = jnp.dot(kp_ref[0, pl.ds(wstart, wlen), :], q_win, preferred_element_type=F32)
        s_w = s_w + jnp.concatenate([wpat_ref[pl.ds(woff, wlen), :]] * NSA_GROUP, axis=1)
        p_w = jnp.exp2(s_w - jnp.max(s_w, axis=0, keepdims=True)).astype(BF16)
        acc_w = jnp.dot(vwt_ref[0, :, pl.ds(wstart, wlen)], p_w, preferred_element_type=F32)
        o_w = acc_w[0:HEAD_DIM] / acc_w[HEAD_DIM:HEAD_DIM + 1]

        for slot in range(2):
            w_ref[slot, u, 0:kw, :] = q_slc
            w_ref[slot, u, kw:, :] = jnp.zeros((w_ref.shape[2] - kw, ncol), BF16)
        return q0, q_slc, o_c, o_w

    prepared = [prepare(u) for u in range(nsub)]

    def tile_start(j):
        jc = jnp.minimum(j, n_tiles_total - 1)
        return jc, pl.multiple_of(jc * SEL_TK, SEL_TK)

    def produce(j, slot):
        jc, k0 = tile_start(j)
        lhs = jnp.concatenate([kp_ref[0, pl.ds(k0, SEL_TK), :], e_ref[...]], axis=1)
        mts = []
        for u in range(nsub):
            b16 = bias_ref[u, pl.ds(pl.multiple_of(jc * bpt, bpt), bpt), :]
            w_ref[slot, u, kw:kw + bpt, :] = jnp.concatenate([b16] * NSA_GROUP, axis=1)
            s = jnp.dot(lhs, w_ref[slot, u], preferred_element_type=F32)
            sbuf_ref[slot, u] = s
            mts.append(jnp.max(s, axis=0, keepdims=True))
        return tuple(mts)

    def consume(j, slot, mts, state):
        _, k0 = tile_start(j)
        vt = vst_ref[0, :, pl.ds(k0, SEL_TK)]
        return tuple(_accumulate(sbuf_ref[slot, u], mts[u], vt, *state[u]) for u in range(nsub))

    def pair(i, carry):
        mt_a, state = carry
        mt_b = produce(2 * i + 1, 1)
        state = consume(2 * i, 0, mt_a, state)
        mt_a = produce(2 * i + 2, 0)
        state = consume(2 * i + 1, 1, mt_b, state)
        return mt_a, state

    def park(state):
        for u in range(nsub):
            ms_ref[u] = state[u][0]
            as_ref[u] = state[u][1]

    n_main = jnp.maximum(((qb_first + nsub - 1) * nq + (SEL_TK - 1)) // SEL_TK, 1)
    n_pairs = (n_main + 1) // 2
    state = tuple((jnp.full((1, ncol), NEG, F32), jnp.zeros((vst_ref.shape[1], ncol), F32)) for _ in range(nsub))
    mt_a, state = lax.fori_loop(0, n_pairs - 1, pair, (produce(0, 0), state))
    tail = 2 * (n_pairs - 1)
    odd = (n_main & 1) == 1

    @pl.when(odd)
    def _():
        park(consume(tail, 0, mt_a, state))

    @pl.when(jnp.logical_not(odd))
    def _():
        mt_b = produce(tail + 1, 1)
        park(consume(tail + 1, 1, mt_b, consume(tail, 0, mt_a, state)))

    state = tuple((ms_ref[u], as_ref[u]) for u in range(nsub))

    for u in range(nsub):
        q0, q_slc, o_c, o_w = prepared[u]
        s_d = jnp.dot(kp_ref[0, pl.ds(q0, nq), :], q_slc, preferred_element_type=F32) + tri_ref[...]
        _, acc_s = _accumulate(s_d, jnp.max(s_d, axis=0, keepdims=True), vst_ref[0, :, pl.ds(q0, nq)], *state[u])
        o_s = acc_s[0:HEAD_DIM] / acc_s[HEAD_DIM:HEAD_DIM + 1]
        gts = jax.nn.sigmoid(ng_ref[0, :, u * nq:(u + 1) * nq])
        rows = []
        for r in range(NSA_GROUP):
            sl = slice(r * nq, (r + 1) * nq)
            rows.append(gts[3 * r:3 * r + 1, :] * o_c[:, sl] + gts[3 * r + 1:3 * r + 2, :] * o_s[:, sl]
                        + gts[3 * r + 2:3 * r + 3, :] * o_w[:, sl])
        y_t = jnp.concatenate(rows, axis=0)
        o_ref[0, u * nq:(u + 1) * nq, :] = y_t.T.astype(o_ref.dtype)


def _nsa_constants(ncp):
    nq = Q_BLOCK
    ql = jnp.arange(nq)[None, :]
    r = jnp.arange(SEL_TK)[:, None]
    e = (r // SLC_BLOCK == jnp.arange(2 * HEAD_DIM)[None, :]).astype(BF16)
    rd = jnp.arange(nq)[:, None]
    tri = jnp.tile(jnp.where(rd <= ql, 0.0, NEG).astype(F32), (1, NSA_GROUP))
    rw = jnp.arange(2 * WINDOW + nq)[:, None]
    wpat = jnp.where((rw > ql) & (rw <= ql + WINDOW), 0.0, NEG).astype(F32)
    rc = jnp.arange(2 * ncp)[:, None] - ncp
    cpat = jnp.where(CMP_STRIDE * rc + (CMP_LEN - 1) <= ql, 0.0, NEG).astype(F32)
    return e, tri, wpat, cpat


def nsa_attention(proj_t, kc, vct, hn, vt_pack):
    b, _, t = proj_t.shape
    ncp = kc.shape[2]
    n_sel = t // SLC_BLOCK
    gq = NSA_GROUP * HEAD_DIM
    qstep = NSA_QSUB * Q_BLOCK
    e, tri, wpat, cpat = _nsa_constants(ncp)
    const2 = lambda i, g, q: (0, 0)
    return pl.pallas_call(
        _nsa_kernel,
        grid=(b, NSA_KV_GROUPS, t // qstep),
        in_specs=[
            pl.BlockSpec((1, gq, qstep), lambda i, g, q: (i, g, q)),
            pl.BlockSpec((1, 16, qstep), lambda i, g, q: (i, TR_NG // 16 + g, q)),
            pl.BlockSpec((1, 1, ncp, HEAD_DIM), lambda i, g, q: (i, g, 0, 0)),
            pl.BlockSpec((1, 1, HEAD_DIM, ncp), lambda i, g, q: (i, g, 0, 0)),
            pl.BlockSpec((1, t, 2 * HEAD_DIM), lambda i, g, q: (i, 0, NAT_KP // (2 * HEAD_DIM) + g)),
            pl.BlockSpec((1, VT_ROWS, t), lambda i, g, q: (i, g, 0)),
            pl.BlockSpec((1, VT_ROWS, t), lambda i, g, q: (i, NSA_KV_GROUPS + g, 0)),
            pl.BlockSpec(e.shape, const2),
            pl.BlockSpec(tri.shape, const2),
            pl.BlockSpec(wpat.shape, const2),
            pl.BlockSpec(cpat.shape, const2),
        ],
        out_specs=pl.BlockSpec((1, qstep, gq), lambda i, g, q: (i, q, g)),
        out_shape=jax.ShapeDtypeStruct((b, t, NSA_HEADS * HEAD_DIM), BF16),
        scratch_shapes=[pltpu.VMEM((NSA_QSUB, n_sel, Q_BLOCK), BF16),
                        pltpu.VMEM((2, NSA_QSUB, 4 * HEAD_DIM, NSA_GROUP * Q_BLOCK), BF16),
                        pltpu.VMEM((2, NSA_QSUB, SEL_TK, NSA_GROUP * Q_BLOCK), F32),
                        pltpu.VMEM((NSA_QSUB, HEAD_DIM, NSA_GROUP * Q_BLOCK), F32),
                        pltpu.VMEM((NSA_QSUB, PG_PAD + ncp, Q_BLOCK), F32),
                        pltpu.VMEM((NSA_QSUB, 1, NSA_GROUP * Q_BLOCK), F32),
                        pltpu.VMEM((NSA_QSUB, VT_ROWS, NSA_GROUP * Q_BLOCK), F32)],
        compiler_params=_cparams(3),
        name="nsa_attention",
    )(proj_t, proj_t, kc, vct, hn, vt_pack, vt_pack, e, tri, wpat, cpat)


def _ret_kernel(rq_ref, rk_ref, rv_ref, rg_ref, cos_ref, sin_ref, dec_ref, qd_ref, kd_ref, cd_ref,
                gg_ref, gb_ref, o_ref, st_ref):
    c = RET_CHUNK
    hd = RET_HEAD_DIM

    @pl.when(pl.program_id(1) == 0)
    def _():
        st_ref[...] = jnp.zeros_like(st_ref)

    n_chunks = rq_ref.shape[1] // c

    def chunk(ci, _):
        r0 = pl.multiple_of(ci * c, c)
        cos = cos_ref[pl.ds(r0, c), :]
        sin = sin_ref[pl.ds(r0, c), :]
        for h in range(RET_HEADS):
            hs = slice(h * hd, (h + 1) * hd)
            q = rq_ref[0, pl.ds(r0, c), hs].astype(F32)
            k = rk_ref[0, pl.ds(r0, c), hs].astype(F32)
            v = rv_ref[0, pl.ds(r0, c), hs]
            q = q * cos + pltpu.roll(q, hd // 2, 1) * sin
            k = (k * cos + pltpu.roll(k, hd // 2, 1) * sin) * (hd ** -0.5)
            qb = q.astype(BF16)
            kb = k.astype(BF16)
            vb = v.astype(BF16)
            inner = lax.dot_general(qb, kb, (((1,), (1,)), ((), ())), preferred_element_type=F32) * dec_ref[h]
            st = st_ref[h]
            o = (jnp.dot(inner.astype(BF16), vb, preferred_element_type=F32)
                 + jnp.dot(qb, st.astype(BF16), preferred_element_type=F32) * qd_ref[h])
            kdt = (k * kd_ref[h]).T.astype(BF16)
            st_ref[h] = st * cd_ref[h] + jnp.dot(kdt, vb, preferred_element_type=F32)
            mu = jnp.mean(o, axis=-1, keepdims=True)
            d = o - mu
            var = jnp.mean(d * d, axis=-1, keepdims=True)
            y = d * lax.rsqrt(var + LN_EPS) * gg_ref[:, hs] + gb_ref[:, hs]
            o_ref[0, pl.ds(r0, c), hs] = (jax.nn.silu(rg_ref[0, pl.ds(r0, c), hs].astype(F32)) * y).astype(o_ref.dtype)
        return 0

    lax.fori_loop(0, n_chunks, chunk, 0)


def retention_branch(hn, gn_g, gn_b, tr):
    b, t, _ = hn.shape
    w = BRANCH_WIDTH
    hd = RET_HEAD_DIM
    c = RET_CHUNK
    inv = ROPE_BASE ** (-jnp.arange(0, hd, 2, dtype=F32) / hd)
    ang = jnp.arange(t, dtype=F32)[:, None] * inv[None, :]
    cos2 = jnp.concatenate([jnp.cos(ang), jnp.cos(ang)], axis=1)
    sin2 = jnp.concatenate([-jnp.sin(ang), jnp.sin(ang)], axis=1)
    log_g = jnp.log(1.0 - 2.0 ** (-5.0 - jnp.arange(RET_HEADS, dtype=F32)))
    idx = jnp.arange(c, dtype=F32)
    diff = idx[:, None] - idx[None, :]
    decay_in = jnp.where(diff >= 0, jnp.exp(log_g[:, None, None] * jnp.maximum(diff, 0.0)), 0.0)
    ones = jnp.ones((RET_HEADS, c, hd), F32)
    q_dec = jnp.exp(log_g[:, None] * (idx + 1.0))[:, :, None] * ones
    k_dec = jnp.exp(log_g[:, None] * (c - 1.0 - idx))[:, :, None] * ones
    c_dec = jnp.exp(log_g * c)[:, None, None] * jnp.ones((RET_HEADS, hd, hd), F32)
    col = lambda cc: (lambda i, j: (i, j, cc // w))
    full3 = lambda i, j: (0, 0, 0)
    return pl.pallas_call(
        _ret_kernel,
        grid=(b, t // tr),
        in_specs=[
            pl.BlockSpec((1, tr, w), col(NAT_RQ)),
            pl.BlockSpec((1, tr, w), col(NAT_RK)),
            pl.BlockSpec((1, tr, w), col(NAT_RV)),
            pl.BlockSpec((1, tr, w), col(NAT_RG)),
            pl.BlockSpec((tr, hd), lambda i, j: (j, 0)),
            pl.BlockSpec((tr, hd), lambda i, j: (j, 0)),
            pl.BlockSpec((RET_HEADS, c, c), full3),
            pl.BlockSpec((RET_HEADS, c, hd), full3),
            pl.BlockSpec((RET_HEADS, c, hd), full3),
            pl.BlockSpec((RET_HEADS, hd, hd), full3),
            pl.BlockSpec((1, w), lambda i, j: (0, 0)),
            pl.BlockSpec((1, w), lambda i, j: (0, 0)),
        ],
        out_specs=pl.BlockSpec((1, tr, w), lambda i, j: (i, j, 0)),
        out_shape=jax.ShapeDtypeStruct((b, t, w), BF16),
        scratch_shapes=[pltpu.VMEM((RET_HEADS, hd, hd), F32)],
        compiler_params=_cparams(2),
        name="retention",
    )(hn, hn, hn, hn, cos2, sin2, decay_in, q_dec, k_dec, c_dec, gn_g.reshape(1, w), gn_b.reshape(1, w))


def _layer_norm(z, g, b):
    mu = jnp.mean(z, axis=-1, keepdims=True)
    d = z - mu
    var = jnp.mean(d * d, axis=-1, keepdims=True)
    return d * lax.rsqrt(var + LN_EPS) * g + b


def _merge_kernel(tiles_per_seq, x_ref, ya_ref, cb_ref, cc_ref, ch_ref, pc_ref, ph_ref, cw_ref, yc_ref, mg_ref,
                  wbr_ref, wout_ref, g_ref, b_ref, o_ref, ob_ref):
    first = (pl.program_id(0) % tiles_per_seq) == 0
    u = cc_ref[...].astype(F32) * ch_ref[...].astype(F32)
    prev = pc_ref[...].astype(F32) * ph_ref[...].astype(F32)
    prev = jnp.where(first, 0.0, prev)
    row = lax.broadcasted_iota(jnp.int32, u.shape, 0)
    p1 = prev[CONV_HALO - 1:CONV_HALO, :]
    p2 = prev[CONV_HALO - 2:CONV_HALO - 1, :]
    u1 = jnp.where(row == 0, p1, pltpu.roll(u, 1, 0))
    u2 = jnp.where(row == 0, p2, jnp.where(row == 1, p1, pltpu.roll(u, 2, 0)))
    cw = cw_ref[...]
    y_b = cb_ref[...].astype(F32) * (cw[0:1, :] * u2 + cw[1:2, :] * u1 + cw[2:3, :] * u)

    mix = None
    for c, y in enumerate((ya_ref[...], y_b.astype(BF16), yc_ref[...])):
        proj = jnp.dot(y, wbr_ref[c], preferred_element_type=F32)
        gate = jax.nn.sigmoid(mg_ref[:, c * D_MODEL:(c + 1) * D_MODEL].astype(F32))
        mix = gate * proj if mix is None else mix + gate * proj
    m = jnp.dot(mix.astype(BF16), wout_ref[...], preferred_element_type=F32)
    out = _layer_norm(ALPHA * x_ref[...] + m, g_ref[...], b_ref[...])
    o_ref[...] = out
    ob_ref[...] = out.astype(BF16)


def merge_branches(x, ya, yc, hn, conv_w, w_br, w_out, g, b, tm, t):
    m = x.shape[0]
    d = D_MODEL
    w = BRANCH_WIDTH
    row = lambda i: (i, 0)
    col = lambda c: (lambda i: (i, c // w))
    hal = lambda c: (lambda i: (jnp.maximum(i * (tm // CONV_HALO) - 1, 0), c // w))
    wpad = jnp.zeros((8, w), F32).at[:CONV_K].set(conv_w)
    return pl.pallas_call(
        functools.partial(_merge_kernel, t // tm),
        grid=(m // tm,),
        in_specs=[
            pl.BlockSpec((tm, d), row),
            pl.BlockSpec((tm, w), row),
            pl.BlockSpec((tm, w), col(NAT_CB)),
            pl.BlockSpec((tm, w), col(NAT_CC)),
            pl.BlockSpec((tm, w), col(NAT_CH)),
            pl.BlockSpec((CONV_HALO, w), hal(NAT_CC)),
            pl.BlockSpec((CONV_HALO, w), hal(NAT_CH)),
            pl.BlockSpec((8, w), lambda i: (0, 0)),
            pl.BlockSpec((tm, w), row),
            pl.BlockSpec((tm, 3 * d), row),
            pl.BlockSpec((3, w, d), lambda i: (0, 0, 0)),
            pl.BlockSpec((d, d), lambda i: (0, 0)),
            pl.BlockSpec((1, d), lambda i: (0, 0)),
            pl.BlockSpec((1, d), lambda i: (0, 0)),
        ],
        out_specs=[pl.BlockSpec((tm, d), row), pl.BlockSpec((tm, d), row)],
        out_shape=[jax.ShapeDtypeStruct((m, d), F32), jax.ShapeDtypeStruct((m, d), BF16)],
        compiler_params=_cparams(1),
        name="merge",
    )(x, ya, hn, hn, hn, hn, hn, wpad, yc, hn, w_br.astype(BF16), w_out.astype(BF16), g.reshape(1, d),
      b.reshape(1, d))


def _cast3_kernel(g_ref, u_ref, d_ref, og_ref, ou_ref, od_ref):
    og_ref[...] = g_ref[0].astype(og_ref.dtype)
    ou_ref[...] = u_ref[0].astype(ou_ref.dtype)
    od_ref[...] = d_ref[0].astype(od_ref.dtype)


def ffn_weights_bf16(w_gate, w_up, w_down, layer):
    _, e, d, ff = w_gate.shape
    rg, rd = d // CAST_STEPS, ff // CAST_STEPS
    up_spec = pl.BlockSpec((1, 1, rg, ff), lambda i, j: (layer, i, j, 0))
    return pl.pallas_call(
        _cast3_kernel,
        grid=(e, CAST_STEPS),
        in_specs=[up_spec, up_spec, pl.BlockSpec((1, 1, rd, d), lambda i, j: (layer, i, j, 0))],
        out_specs=[pl.BlockSpec((1, rg, ff), lambda i, j: (i, j, 0)), pl.BlockSpec((1, rg, ff), lambda i, j: (i, j, 0)),
                   pl.BlockSpec((1, rd, d), lambda i, j: (i, j, 0))],
        out_shape=[jax.ShapeDtypeStruct((e, d, ff), BF16), jax.ShapeDtypeStruct((e, d, ff), BF16),
                   jax.ShapeDtypeStruct((e, ff, d), BF16)],
        compiler_params=_cparams(2),
        name="cast_bf16",
    )(w_gate, w_up, w_down)


def _swiglu_chunks(x, wg_ref, wu_ref, wd_ref, tf):
    acc = None
    for c in range(wg_ref.shape[2] // tf):
        sl = slice(c * tf, (c + 1) * tf)
        gte = jnp.dot(x, wg_ref[0, :, sl], preferred_element_type=F32)
        up = jnp.dot(x, wu_ref[0, :, sl], preferred_element_type=F32)
        h = (jax.nn.silu(gte) * up).astype(BF16)
        part = jnp.dot(h, wd_ref[0, sl, :], preferred_element_type=F32)
        acc = part if acc is None else acc + part
    return acc


def _mlp_kernel(tf, te_ref, tv_ref, x_ref, wg_ref, wu_ref, wd_ref, o_ref):
    i = pl.program_id(0)

    @pl.when(tv_ref[i] > 0)
    def _():
        o_ref[...] = _swiglu_chunks(x_ref[...], wg_ref, wu_ref, wd_ref, tf).astype(o_ref.dtype)

    @pl.when(tv_ref[i] == 0)
    def _():
        o_ref[...] = jnp.zeros_like(o_ref)


def _resident(block_shape, index_map):
    return pl.BlockSpec(block_shape, index_map, pipeline_mode=pl.Buffered(1))


def grouped_swiglu(xs, w_gate, w_up, w_down, tile_expert, tile_valid, tm, tf):
    n, d = xs.shape
    ff = w_gate.shape[2]
    grid_spec = pltpu.PrefetchScalarGridSpec(
        num_scalar_prefetch=2,
        grid=(n // tm,),
        in_specs=[
            pl.BlockSpec((tm, d), lambda i, te, tv: (i, 0)),
            _resident((1, d, ff), lambda i, te, tv: (te[i], 0, 0)),
            _resident((1, d, ff), lambda i, te, tv: (te[i], 0, 0)),
            _resident((1, ff, d), lambda i, te, tv: (te[i], 0, 0)),
        ],
        out_specs=pl.BlockSpec((tm, d), lambda i, te, tv: (i, 0)),
    )
    return pl.pallas_call(
        functools.partial(_mlp_kernel, tf),
        grid_spec=grid_spec,
        out_shape=jax.ShapeDtypeStruct((n, d), BF16),
        compiler_params=_cparams(1, BIG_VMEM_LIMIT),
        name="grouped_swiglu",
    )(tile_expert, tile_valid, xs, w_gate, w_up, w_down)


def _add_ln_kernel(*refs):
    x_ref, sc_ref, *adds, g_ref, b_ref, o_ref, ob_ref = refs
    z = ALPHA * x_ref[...]
    for k, a in enumerate(adds):
        z = z + sc_ref[:, k:k + 1] * a[...].astype(F32)
    out = _layer_norm(z, g_ref[...], b_ref[...])
    o_ref[...] = out
    ob_ref[...] = out.astype(BF16)


def add_layer_norm(x, adds, scales, g, b, tm):
    m, d = x.shape
    row = lambda i: (i, 0)
    return pl.pallas_call(
        _add_ln_kernel,
        grid=(m // tm,),
        in_specs=([pl.BlockSpec((tm, d), row), pl.BlockSpec((tm, len(adds)), row)]
                  + [pl.BlockSpec((tm, d), row)] * len(adds) + [pl.BlockSpec((1, d), lambda i: (0, 0))] * 2),
        out_specs=[pl.BlockSpec((tm, d), row)] * 2,
        out_shape=[jax.ShapeDtypeStruct((m, d), F32), jax.ShapeDtypeStruct((m, d), BF16)],
        compiler_params=_cparams(1),
        name="add_layer_norm",
    )(x, scales, *adds, g.reshape(1, d), b.reshape(1, d))


def _router_kernel(x_ref, wh_ref, wl_ref, o_ref):
    x = x_ref[...]
    xh = x.astype(BF16)
    xl = (x - xh.astype(F32)).astype(BF16)
    wh = wh_ref[...]
    o_ref[...] = (jnp.dot(xh, wh, preferred_element_type=F32) + jnp.dot(xl, wh, preferred_element_type=F32)
                  + jnp.dot(xh, wl_ref[...], preferred_element_type=F32))


def router_logits(x, w_router, tm):
    m, d = x.shape
    wp = jnp.zeros((d, 128), F32).at[:, :N_EXPERTS].set(w_router)
    wh = wp.astype(BF16)
    wl = (wp - wh.astype(F32)).astype(BF16)
    return pl.pallas_call(
        _router_kernel,
        grid=(m // tm,),
        in_specs=[pl.BlockSpec((tm, d), lambda i: (i, 0)), pl.BlockSpec((d, 128), lambda i: (0, 0)),
                  pl.BlockSpec((d, 128), lambda i: (0, 0))],
        out_specs=pl.BlockSpec((tm, 128), lambda i: (i, 0)),
        out_shape=jax.ShapeDtypeStruct((m, 128), F32),
        compiler_params=_cparams(1),
        name="router",
    )(x, wh, wl)


def _dense_ffn_kernel(tf, xb_ref, x_ref, wg_ref, wu_ref, wd_ref, g_ref, b_ref, o_ref, ob_ref):
    acc = _swiglu_chunks(xb_ref[...], wg_ref, wu_ref, wd_ref, tf)
    out = _layer_norm(ALPHA * x_ref[...] + acc, g_ref[...], b_ref[...])
    o_ref[...] = out
    ob_ref[...] = out.astype(BF16)


def dense_ffn(x, xb, w_gate, w_up, w_down, g, b):
    m, d = x.shape
    ff = w_gate.shape[2]
    tm = min(FFN_TM, m)
    row = lambda i: (i, 0)
    return pl.pallas_call(
        functools.partial(_dense_ffn_kernel, FFN_TF),
        grid=(m // tm,),
        in_specs=[
            pl.BlockSpec((tm, d), row),
            pl.BlockSpec((tm, d), row),
            _resident((1, d, ff), lambda i: (0, 0, 0)),
            _resident((1, d, ff), lambda i: (0, 0, 0)),
            _resident((1, ff, d), lambda i: (0, 0, 0)),
            pl.BlockSpec((1, d), lambda i: (0, 0)),
            pl.BlockSpec((1, d), lambda i: (0, 0)),
        ],
        out_specs=[pl.BlockSpec((tm, d), row)] * 2,
        out_shape=[jax.ShapeDtypeStruct((m, d), F32), jax.ShapeDtypeStruct((m, d), BF16)],
        compiler_params=_cparams(1, BIG_VMEM_LIMIT),
        name="dense_ffn",
    )(xb, x, w_gate, w_up, w_down, g.reshape(1, d), b.reshape(1, d))


def _rank_kernel(e_ref, u_ref, rank_ref, cnt_ref, carry_ref):
    @pl.when(pl.program_id(0) == 0)
    def _():
        carry_ref[...] = jnp.zeros_like(carry_ref)

    rows, width = e_ref.shape
    expert = lax.broadcasted_iota(jnp.int32, (N_EXPERTS, width), 0)
    carry = carry_ref[...]
    for r in range(rows):
        hot = e_ref[r:r + 1, :] == expert
        prefix = jnp.dot(hot.astype(BF16), u_ref[...], preferred_element_type=F32) + carry
        rank_ref[r:r + 1, :] = (jnp.sum(jnp.where(hot, prefix, 0.0), axis=0, keepdims=True) - 1.0).astype(jnp.int32)
        carry = prefix[:, width - 1:width]
    carry_ref[...] = carry
    cnt_ref[...] = jnp.broadcast_to(carry, cnt_ref.shape).astype(jnp.int32)


def routing_ranks(e_flat):
    n = e_flat.shape[0]
    width = RANK_WIDTH
    rows = 8
    e2 = e_flat.reshape(n // width, width)
    upper = (jnp.arange(width)[:, None] <= jnp.arange(width)[None, :]).astype(BF16)
    rank, cnt = pl.pallas_call(
        _rank_kernel,
        grid=(n // (rows * width),),
        in_specs=[pl.BlockSpec((rows, width), lambda i: (i, 0)), pl.BlockSpec((width, width), lambda i: (0, 0))],
        out_specs=[pl.BlockSpec((rows, width), lambda i: (i, 0)), pl.BlockSpec((N_EXPERTS, 128), lambda i: (0, 0))],
        out_shape=[jax.ShapeDtypeStruct((n // width, width), jnp.int32),
                   jax.ShapeDtypeStruct((N_EXPERTS, 128), jnp.int32)],
        scratch_shapes=[pltpu.VMEM((N_EXPERTS, 1), F32)],
        compiler_params=_cparams(1),
        name="routing_ranks",
    )(e2, upper)
    return rank.reshape(n), cnt[:, 0]


def moe_ffn(x, xb, w_router, w_gate, w_up, w_down, g, b):
    m = x.shape[0]
    tm = FFN_TM
    logits = router_logits(x, w_router, min(ROW_TILE, m))[:, :N_EXPERTS]
    top_v, top_i = lax.top_k(logits, TOP_K)
    probs = jax.nn.softmax(top_v, axis=-1)
    e_flat = top_i.reshape(-1)
    rank, counts = routing_ranks(e_flat)
    padded = ((counts + tm - 1) // tm) * tm
    ends = jnp.cumsum(padded)
    starts = ends - padded
    pos = starts[e_flat] + rank
    n_slots = TOP_K * m + N_EXPERTS * tm
    n_tiles = n_slots // tm
    tile_start = jnp.arange(n_tiles, dtype=jnp.int32) * tm
    tile_expert = jnp.minimum(jnp.searchsorted(ends, tile_start, side="right"), N_EXPERTS - 1).astype(jnp.int32)
    tile_valid = (tile_start < ends[-1]).astype(jnp.int32)
    n_flat = TOP_K * m
    order = jnp.sort(e_flat * n_flat + jnp.arange(n_flat, dtype=jnp.int32)) % n_flat
    first_entry = jnp.cumsum(counts) - counts
    r_in_group = tile_start - starts[tile_expert]
    slot_rank = (r_in_group[:, None] + jnp.arange(tm, dtype=jnp.int32)[None, :]).reshape(-1)
    slot_expert = jnp.repeat(tile_expert, tm)
    entry = jnp.clip(first_entry[slot_expert] + slot_rank, 0, n_flat - 1)
    src = jnp.where(slot_rank < counts[slot_expert], order[entry] // TOP_K, 0)
    xs = jnp.take(xb, src, axis=0)
    ys = grouped_swiglu(xs, w_gate, w_up, w_down,
                        tile_expert, tile_valid, tm, FFN_TF)
    pos2 = pos.reshape(m, TOP_K)
    y0 = jnp.take(ys, pos2[:, 0], axis=0)
    y1 = jnp.take(ys, pos2[:, 1], axis=0)
    return add_layer_norm(x, [y0, y1], probs, g, b, min(ROW_TILE, m))


def _pack_in_weights(w):
    d = w.shape[0]
    o = 0
    q = w[:, o:o + 512]; o += 512
    kv = w[:, o:o + 768].reshape(d, 3, 2, NSA_KV_GROUPS, HEAD_DIM); o += 768
    ng = w[:, o:o + 24]; o += 24
    rest = w[:, o:o + 7 * 512]; o += 7 * 512
    mg = w[:, o:]
    kpack = jnp.concatenate([kv[:, 1, 0], kv[:, 2, 0]], axis=-1).reshape(d, 2 * NSA_KV_GROUPS * HEAD_DIM)
    kc = kv[:, 0, 0].reshape(d, NSA_KV_GROUPS * HEAD_DIM)
    vc = kv[:, 0, 1].reshape(d, NSA_KV_GROUPS * HEAD_DIM)
    w_nat = jnp.concatenate([mg, rest, kpack, kc, vc], axis=1).astype(BF16)
    vs = kv[:, 1, 1].reshape(d, NSA_KV_GROUPS * HEAD_DIM)
    vw = kv[:, 2, 1].reshape(d, NSA_KV_GROUPS * HEAD_DIM)
    ngp = jnp.pad(ng.reshape(d, NSA_KV_GROUPS, NSA_GROUP * 3), ((0, 0), (0, 0), (0, 4))).reshape(d, 32)
    w_t = jnp.concatenate([q, vs, vw, ngp], axis=1).T.astype(BF16)
    return w_nat, w_t


def token_mixer_ln(x, xb, w_in, cmp_pos, cmp_w1, cmp_b1, cmp_w2, cmp_b2, conv_w, gn_g, gn_b, w_br, w_out, ln_g, ln_b):
    b, t, d = x.shape
    m = b * t
    w_nat, w_t = _pack_in_weights(w_in)
    x2 = x.reshape(m, d)
    hn, proj_t = input_projection(xb, w_nat, w_t, min(PROJ_TM, t), PROJ_TN)
    hn = hn.reshape(b, t, NAT_COLS)
    nch = t // CMP_STRIDE
    raw = hn[:, :, NAT_KC:NAT_KC + 2 * NSA_KV_GROUPS * HEAD_DIM].reshape(b, t, 2, NSA_KV_GROUPS, HEAD_DIM)
    chunks = raw.transpose(2, 0, 3, 1, 4).reshape(2, b, NSA_KV_GROUPS, nch, CMP_STRIDE * HEAD_DIM)
    cn, ct = nsa_compress_kv(chunks, cmp_pos, cmp_w1, cmp_b1, cmp_w2, cmp_b2)
    kc = cn[0].astype(BF16)
    vct = ct[1].astype(BF16)
    vt4 = proj_t[:, TR_VS:TR_NG, :].astype(BF16).reshape(b, 4, HEAD_DIM, t)
    ones_pad = jnp.zeros((b, 4, VT_ROWS - HEAD_DIM, t), BF16).at[:, :, 0].set(1.0)
    vt_pack = jnp.concatenate([vt4, ones_pad], axis=2).reshape(b, 4 * VT_ROWS, t)
    y_a = nsa_attention(proj_t, kc, vct, hn, vt_pack)
    y_c = retention_branch(hn, gn_g, gn_b, min(ROW_TILE, t))
    return merge_branches(x2, y_a.reshape(m, -1), y_c.reshape(m, -1), hn.reshape(m, NAT_COLS), conv_w,
                          w_br, w_out, ln_g, ln_b, min(MERGE_TM, t), t)


def kernel(x, w_in, cmp_pos, cmp_w1, cmp_b1, cmp_w2, cmp_b2, conv_w, ret_gn_g, ret_gn_b, w_br, w_out,
           ln1_g, ln1_b, ln2_g, ln2_b, ffn_gate, ffn_up, ffn_down, moe_router, moe_gate, moe_up, moe_down):
    b, t, d = x.shape
    xb = x.astype(BF16)
    for l in range(DEPTH):
        x2, x2b = token_mixer_ln(x, xb, w_in[l], cmp_pos[l], cmp_w1[l], cmp_b1[l], cmp_w2[l], cmp_b2[l], conv_w[l],
                                 ret_gn_g[l], ret_gn_b[l], w_br[l], w_out[l], ln1_g[l], ln1_b[l])
        if l % 2 == 0:
            wg, wu, wd = ffn_weights_bf16(ffn_gate[:, None], ffn_up[:, None], ffn_down[:, None], l // 2)
            x2, x2b = dense_ffn(x2, x2b, wg, wu, wd, ln2_g[l], ln2_b[l])
        else:
            wg, wu, wd = ffn_weights_bf16(moe_gate, moe_up, moe_down, l // 2)
            x2, x2b = moe_ffn(x2, x2b, moe_router[l // 2], wg, wu, wd, ln2_g[l], ln2_b[l])
        x, xb = x2.reshape(b, t, d), x2b.reshape(b, t, d)
    return x
```

```python
import functools

import jax
import jax.numpy as jnp
from jax import lax
from jax.experimental import pallas as pl
from jax.experimental.pallas import tpu as pltpu

F32 = jnp.float32
BF16 = jnp.bfloat16

D_MODEL = 1024
DEPTH = 4
BRANCH_WIDTH = 512
HEAD_DIM = 64
NSA_HEADS = 8
NSA_KV_GROUPS = 2
NSA_GROUP = 4
CMP_LEN = 32
CMP_STRIDE = 16
SLC_BLOCK = 64
SLC_TOPN = 16
WINDOW = 512
CMP_HIDDEN = 256
Q_BLOCK = 128
CONV_K = 3
CONV_HALO = 16
RET_HEADS = 4
RET_HEAD_DIM = 128
RET_CHUNK = 256
ROPE_BASE = 10000.0
D_FF = 3584
N_EXPERTS = 8
TOP_K = 2
LN_EPS = 1e-5
ALPHA = (2.0 * DEPTH) ** 0.25

NAT_MG = 0
NAT_CB = 3072
NAT_CC = 3584
NAT_CH = 4096
NAT_RQ = 4608
NAT_RK = 5120
NAT_RV = 5632
NAT_RG = 6144
NAT_KP = 6656
NAT_KC = 6912
NAT_VC = 7040
NAT_COLS = 7168
TR_Q = 0
TR_VS = 512
TR_VW = 640
TR_NG = 768
TR_ROWS = 800

NEG = -1e30
SEL_TK = 1024
LOG2E = 1.4426950408889634
NSA_QSUB = 2
CMP_CLASS_ROWS = 128
PG_PAD = 8
PICKED = -3e38
VT_ROWS = 80
RANK_WIDTH = 1024
ROW_TILE = 1024
MERGE_TM = 512
PROJ_TM = 2048
PROJ_TN = 1024
FFN_TM = 512
FFN_TF = 1792
CAST_STEPS = 4
VMEM_LIMIT = 48 * 1024 * 1024
BIG_VMEM_LIMIT = 56 * 1024 * 1024


def _cparams(n_axes, vmem=VMEM_LIMIT):
    return pltpu.CompilerParams(dimension_semantics=("arbitrary",) * n_axes, vmem_limit_bytes=vmem)


def _proj_kernel(x_ref, w_ref, wt_ref, o_ref, ot_ref):
    @pl.when(pl.program_id(1) == 0)
    def _():
        ot_ref[0] = lax.dot_general(wt_ref[...], x_ref[...], (((1,), (1,)), ((), ())), preferred_element_type=F32)

    o_ref[...] = jnp.dot(x_ref[...], w_ref[...], preferred_element_type=F32).astype(o_ref.dtype)


def input_projection(x, w_nat, w_t, tm, tn):
    b, t, k = x.shape
    m = b * t
    n = w_nat.shape[1]
    r = w_t.shape[0]
    tps = t // tm
    return pl.pallas_call(
        _proj_kernel,
        grid=(m // tm, n // tn),
        in_specs=[pl.BlockSpec((tm, k), lambda i, j: (i, 0)), pl.BlockSpec((k, tn), lambda i, j: (0, j)),
                  pl.BlockSpec((r, k), lambda i, j: (0, 0))],
        out_specs=[pl.BlockSpec((tm, tn), lambda i, j: (i, j)),
                   pl.BlockSpec((1, r, tm), lambda i, j: (i // tps, 0, i % tps))],
        out_shape=[jax.ShapeDtypeStruct((m, n), BF16), jax.ShapeDtypeStruct((b, r, t), F32)],
        compiler_params=_cparams(2, BIG_VMEM_LIMIT),
        name="input_projection",
    )(x.reshape(m, k), w_nat, w_t)


def _cmp_kernel(c_ref, pos_ref, w1_ref, b1_ref, w2_ref, w2t_ref, b2_ref, b2t_ref, on_ref, ot_ref):
    half = CMP_STRIDE * HEAD_DIM
    c = c_ref[0, 0, 0].astype(BF16)
    w1 = w1_ref[0]
    a = jnp.dot(c, w1[:half], preferred_element_type=F32)
    bm = jnp.dot(c, w1[half:], preferred_element_type=F32)
    nch = a.shape[0]
    bm = pltpu.roll(bm, nch - 1, 0)
    posb = jnp.dot(pos_ref[0].astype(BF16), w1, preferred_element_type=F32)[0:1]
    hid = jax.nn.gelu(a + bm + posb + b1_ref[0])
    hb = hid.astype(BF16)
    on_ref[0, 0, 0] = jnp.dot(hb, w2_ref[0], preferred_element_type=F32) + b2_ref[0]
    ot_ref[0, 0, 0] = lax.dot_general(w2t_ref[0], hb, (((1,), (1,)), ((), ())),
                                      preferred_element_type=F32) + b2t_ref[0]


def nsa_compress_kv(chunks, pos, w1, b1, w2, b2):
    _, b, g, nch, cw = chunks.shape
    pos_flat = jnp.broadcast_to(pos.reshape(2, 1, CMP_LEN * HEAD_DIM), (2, 8, CMP_LEN * HEAD_DIM))
    w1b = w1.astype(BF16)
    w2b = w2.astype(BF16)
    w2t = jnp.swapaxes(w2, 1, 2).astype(BF16)
    b1r = b1.reshape(2, 1, CMP_HIDDEN)
    b2r = b2.reshape(2, 1, HEAD_DIM)
    b2t = b2.reshape(2, HEAD_DIM, 1)
    sel = lambda kv, i, j: (kv, 0, 0)
    return pl.pallas_call(
        _cmp_kernel,
        grid=(2, b, g),
        in_specs=[
            pl.BlockSpec((1, 1, 1, nch, cw), lambda kv, i, j: (kv, i, j, 0, 0)),
            pl.BlockSpec((1, 8, CMP_LEN * HEAD_DIM), sel),
            pl.BlockSpec((1, CMP_LEN * HEAD_DIM, CMP_HIDDEN), sel),
            pl.BlockSpec((1, 1, CMP_HIDDEN), sel),
            pl.BlockSpec((1, CMP_HIDDEN, HEAD_DIM), sel),
            pl.BlockSpec((1, HEAD_DIM, CMP_HIDDEN), sel),
            pl.BlockSpec((1, 1, HEAD_DIM), sel),
            pl.BlockSpec((1, HEAD_DIM, 1), sel),
        ],
        out_specs=[
            pl.BlockSpec((1, 1, 1, nch, HEAD_DIM), lambda kv, i, j: (kv, i, j, 0, 0)),
            pl.BlockSpec((1, 1, 1, HEAD_DIM, nch), lambda kv, i, j: (kv, i, j, 0, 0)),
        ],
        out_shape=[
            jax.ShapeDtypeStruct((2, b, g, nch, HEAD_DIM), F32),
            jax.ShapeDtypeStruct((2, b, g, HEAD_DIM, nch), F32),
        ],
        compiler_params=_cparams(3),
        name="nsa_compress",
    )(chunks, pos_flat, w1b, b1r, w2b, w2t, b2r, b2t)


def _accumulate(s, mt, vt, m, acc):
    m_new = jnp.maximum(m, mt)
    alpha = jnp.exp2(m - m_new)
    p = jnp.exp2(s - m_new).astype(BF16)
    return m_new, alpha * acc + jnp.dot(vt, p, preferred_element_type=F32)


def _nsa_kernel(qt_ref, ng_ref, kc_ref, vct_ref, kp_ref, vst_ref, vwt_ref, e_ref, tri_ref, wpat_ref,
                cpat_ref, o_ref, bias_ref, w_ref, sbuf_ref, oc_ref, pg_ref, ms_ref, as_ref):
    nq = Q_BLOCK
    nsub = qt_ref.shape[2] // nq
    ncol = NSA_GROUP * nq
    n_sel = bias_ref.shape[1]
    ncp = kc_ref.shape[2]
    kw = 2 * HEAD_DIM
    bpt = SEL_TK // SLC_BLOCK
    n_tiles_total = kp_ref.shape[1] // SEL_TK
    qb_first = pl.program_id(2) * nsub
    wlen = WINDOW + nq

    def query_setup(u):
        qb = qb_first + u
        q0 = pl.multiple_of(qb * nq, nq)
        qt = qt_ref[0, :, u * nq:(u + 1) * nq] * (HEAD_DIM ** -0.5 * LOG2E)
        qs = jnp.concatenate([qt[r * HEAD_DIM:(r + 1) * HEAD_DIM, :] for r in range(NSA_GROUP)], axis=1)
        zq = jnp.zeros_like(qs)
        tq1 = q0 + lax.broadcasted_iota(jnp.int32, (1, nq), 1)
        return dict(
            q0=q0, q_c=qs.astype(BF16),
            q_slc=jnp.concatenate([qs, zq], axis=0).astype(BF16),
            q_win=jnp.concatenate([zq, qs], axis=0).astype(BF16),
            tq=q0 + (lax.broadcasted_iota(jnp.int32, (1, ncol), 1) & (nq - 1)),
            coff=pl.multiple_of(ncp - qb * (nq // CMP_STRIDE), nq // CMP_STRIDE),
            cur=jnp.right_shift(tq1, 6).astype(F32),
            first_diag=(2 * qb).astype(F32))

    qs_ = [query_setup(u) for u in range(nsub)]

    crow = min(CMP_CLASS_ROWS, ncp)
    qb_per_class = crow // (nq // CMP_STRIDE)
    assert qb_per_class % nsub == 0
    cls = qb_first // qb_per_class
    ratio = SLC_BLOCK // CMP_STRIDE
    for c in range(ncp // crow):
        nc_c = crow * (c + 1)
        ns_c = nc_c * CMP_STRIDE // SLC_BLOCK

        @pl.when(cls == c)
        def _(nc_c=nc_c, ns_c=ns_c):
            j_io = lax.broadcasted_iota(jnp.int32, (ns_c, nq), 0).astype(F32)
            scores = []
            for u, q in enumerate(qs_):
                pg_ref[u, 0:PG_PAD, :] = jnp.zeros((PG_PAD, nq), F32)
                s_c = (jnp.dot(kc_ref[0, 0, 0:nc_c, :], q["q_c"], preferred_element_type=F32)
                       + jnp.concatenate([cpat_ref[pl.ds(q["coff"], nc_c), :]] * NSA_GROUP, axis=1))
                m_c = jnp.max(s_c, axis=0, keepdims=True)
                e_c = jnp.exp2(s_c - m_c)
                l_c = jnp.sum(e_c, axis=0, keepdims=True)
                p_c = e_c * jnp.where(q["tq"] >= CMP_LEN - 1, 1.0 / l_c, 0.0)
                oc_ref[u] = jnp.dot(vct_ref[0, 0, :, 0:nc_c], p_c.astype(BF16), preferred_element_type=F32)

                p_grp = p_c[:, 0:nq]
                for r in range(1, NSA_GROUP):
                    p_grp = p_grp + p_c[:, r * nq:(r + 1) * nq]
                pg_ref[u, PG_PAD:PG_PAD + nc_c, :] = p_grp
                taps = [pg_ref[u, pl.ds(PG_PAD - 1 + k, ns_c, stride=ratio), :] for k in range(ratio + 1)]
                score = taps[0] + 2.0 * (taps[1] + taps[2] + taps[3]) + taps[4]
                score = jnp.where(j_io <= q["cur"], score, -1.0)
                score = jnp.where(j_io == 0.0, PICKED, score)
                score = jnp.where(j_io == q["cur"], PICKED, score)
                score = jnp.where(j_io == q["cur"] - 1.0, PICKED, score)
                scores.append(score)
            for _ in range(SLC_TOPN - 3):
                for u in range(nsub):
                    mx = jnp.max(scores[u], axis=0, keepdims=True)
                    first = jnp.min(jnp.where(scores[u] == mx, j_io, float(ns_c)), axis=0, keepdims=True)
                    scores[u] = jnp.where(j_io == first, PICKED, scores[u])
            for u, q in enumerate(qs_):
                bias = jnp.where(j_io < q["first_diag"], jnp.where(scores[u] == PICKED, 0.0, NEG), NEG)
                bias_ref[u, 0:ns_c, :] = bias.astype(BF16)
                if ns_c < n_sel:
                    bias_ref[u, ns_c:, :] = jnp.full((n_sel - ns_c, nq), NEG, BF16)

    def prepare(u):
        q0, q_slc, q_win = qs_[u]["q0"], qs_[u]["q_slc"], qs_[u]["q_win"]
        o_c = oc_ref[u]

        wstart = pl.multiple_of(jnp.maximum(q0 - WINDOW, 0), nq)
        woff = pl.multiple_of(jnp.maximum(WINDOW - q0, 0), nq)
        s_w = jnp.dot(kp_ref[0, pl.ds(wstart, wlen), :], q_win, preferred_element_type=F32)
        s_w = s_w + jnp.concatenate([wpat_ref[pl.ds(woff, wlen), :]] * NSA_GROUP, axis=1)
        p_w = jnp.exp2(s_w - jnp.max(s_w, axis=0, keepdims=True)).astype(BF16)
        acc_w = jnp.dot(vwt_ref[0, :, pl.ds(wstart, wlen)], p_w, preferred_element_type=F32)
        o_w = acc_w[0:HEAD_DIM] / acc_w[HEAD_DIM:HEAD_DIM + 1]

        for slot in range(2):
            w_ref[slot, u, 0:kw, :] = q_slc
            w_ref[slot, u, kw:, :] = jnp.zeros((w_ref.shape[2] - kw, ncol), BF16)
        return q0, q_slc, o_c, o_w

    prepared = [prepare(u) for u in range(nsub)]

    def tile_start(j):
        jc = jnp.minimum(j, n_tiles_total - 1)
        return jc, pl.multiple_of(jc * SEL_TK, SEL_TK)

    def produce(j, slot):
        jc, k0 = tile_start(j)
        lhs = jnp.concatenate([kp_ref[0, pl.ds(k0, SEL_TK), :], e_ref[...]], axis=1)
        mts = []
        for u in range(nsub):
            b16 = bias_ref[u, pl.ds(pl.multiple_of(jc * bpt, bpt), bpt), :]
            w_ref[slot, u, kw:kw + bpt, :] = jnp.concatenate([b16] * NSA_GROUP, axis=1)
            s = jnp.dot(lhs, w_ref[slot, u], preferred_element_type=F32)
            sbuf_ref[slot, u] = s
            mts.append(jnp.max(s, axis=0, keepdims=True))
        return tuple(mts)

    def consume(j, slot, mts, state):
        _, k0 = tile_start(j)
        vt = vst_ref[0, :, pl.ds(k0, SEL_TK)]
        return tuple(_accumulate(sbuf_ref[slot, u], mts[u], vt, *state[u]) for u in range(nsub))

    def pair(i, carry):
        mt_a, state = carry
        mt_b = produce(2 * i + 1, 1)
        state = consume(2 * i, 0, mt_a, state)
        mt_a = produce(2 * i + 2, 0)
        state = consume(2 * i + 1, 1, mt_b, state)
        return mt_a, state

    def park(state):
        for u in range(nsub):
            ms_ref[u] = state[u][0]
            as_ref[u] = state[u][1]

    n_main = jnp.maximum(((qb_first + nsub - 1) * nq + (SEL_TK - 1)) // SEL_TK, 1)
    n_pairs = (n_main + 1) // 2
    state = tuple((jnp.full((1, ncol), NEG, F32), jnp.zeros((vst_ref.shape[1], ncol), F32)) for _ in range(nsub))
    mt_a, state = lax.fori_loop(0, n_pairs - 1, pair, (produce(0, 0), state))
    tail = 2 * (n_pairs - 1)
    odd = (n_main & 1) == 1

    @pl.when(odd)
    def _():
        park(consume(tail, 0, mt_a, state))

    @pl.when(jnp.logical_not(odd))
    def _():
        mt_b = produce(tail + 1, 1)
        park(consume(tail + 1, 1, mt_b, consume(tail, 0, mt_a, state)))

    state = tuple((ms_ref[u], as_ref[u]) for u in range(nsub))

    for u in range(nsub):
        q0, q_slc, o_c, o_w = prepared[u]
        s_d = jnp.dot(kp_ref[0, pl.ds(q0, nq), :], q_slc, preferred_element_type=F32) + tri_ref[...]
        _, acc_s = _accumulate(s_d, jnp.max(s_d, axis=0, keepdims=True), vst_ref[0, :, pl.ds(q0, nq)], *state[u])
        o_s = acc_s[0:HEAD_DIM] / acc_s[HEAD_DIM:HEAD_DIM + 1]
        gts = jax.nn.sigmoid(ng_ref[0, :, u * nq:(u + 1) * nq])
        rows = []
        for r in range(NSA_GROUP):
            sl = slice(r * nq, (r + 1) * nq)
            rows.append(gts[3 * r:3 * r + 1, :] * o_c[:, sl] + gts[3 * r + 1:3 * r + 2, :] * o_s[:, sl]
                        + gts[3 * r + 2:3 * r + 3, :] * o_w[:, sl])
        y_t = jnp.concatenate(rows, axis=0)
        o_ref[0, u * nq:(u + 1) * nq, :] = y_t.T.astype(o_ref.dtype)


def _nsa_constants(ncp):
    nq = Q_BLOCK
    ql = jnp.arange(nq)[None, :]
    r = jnp.arange(SEL_TK)[:, None]
    e = (r // SLC_BLOCK == jnp.arange(2 * HEAD_DIM)[None, :]).astype(BF16)
    rd = jnp.arange(nq)[:, None]
    tri = jnp.tile(jnp.where(rd <= ql, 0.0, NEG).astype(F32), (1, NSA_GROUP))
    rw = jnp.arange(2 * WINDOW + nq)[:, None]
    wpat = jnp.where((rw > ql) & (rw <= ql + WINDOW), 0.0, NEG).astype(F32)
    rc = jnp.arange(2 * ncp)[:, None] - ncp
    cpat = jnp.where(CMP_STRIDE * rc + (CMP_LEN - 1) <= ql, 0.0, NEG).astype(F32)
    return e, tri, wpat, cpat


def nsa_attention(proj_t, kc, vct, hn, vt_pack):
    b, _, t = proj_t.shape
    ncp = kc.shape[2]
    n_sel = t // SLC_BLOCK
    gq = NSA_GROUP * HEAD_DIM
    qstep = NSA_QSUB * Q_BLOCK
    e, tri, wpat, cpat = _nsa_constants(ncp)
    const2 = lambda i, g, q: (0, 0)
    return pl.pallas_call(
        _nsa_kernel,
        grid=(b, NSA_KV_GROUPS, t // qstep),
        in_specs=[
            pl.BlockSpec((1, gq, qstep), lambda i, g, q: (i, g, q)),
            pl.BlockSpec((1, 16, qstep), lambda i, g, q: (i, TR_NG // 16 + g, q)),
            pl.BlockSpec((1, 1, ncp, HEAD_DIM), lambda i, g, q: (i, g, 0, 0)),
            pl.BlockSpec((1, 1, HEAD_DIM, ncp), lambda i, g, q: (i, g, 0, 0)),
            pl.BlockSpec((1, t, 2 * HEAD_DIM), lambda i, g, q: (i, 0, NAT_KP // (2 * HEAD_DIM) + g)),
            pl.BlockSpec((1, VT_ROWS, t), lambda i, g, q: (i, g, 0)),
            pl.BlockSpec((1, VT_ROWS, t), lambda i, g, q: (i, NSA_KV_GROUPS + g, 0)),
            pl.BlockSpec(e.shape, const2),
            pl.BlockSpec(tri.shape, const2),
            pl.BlockSpec(wpat.shape, const2),
            pl.BlockSpec(cpat.shape, const2),
        ],
        out_specs=pl.BlockSpec((1, qstep, gq), lambda i, g, q: (i, q, g)),
        out_shape=jax.ShapeDtypeStruct((b, t, NSA_HEADS * HEAD_DIM), BF16),
        scratch_shapes=[pltpu.VMEM((NSA_QSUB, n_sel, Q_BLOCK), BF16),
                        pltpu.VMEM((2, NSA_QSUB, 4 * HEAD_DIM, NSA_GROUP * Q_BLOCK), BF16),
                        pltpu.VMEM((2, NSA_QSUB, SEL_TK, NSA_GROUP * Q_BLOCK), F32),
                        pltpu.VMEM((NSA_QSUB, HEAD_DIM, NSA_GROUP * Q_BLOCK), F32),
                        pltpu.VMEM((NSA_QSUB, PG_PAD + ncp, Q_BLOCK), F32),
                        pltpu.VMEM((NSA_QSUB, 1, NSA_GROUP * Q_BLOCK), F32),
                        pltpu.VMEM((NSA_QSUB, VT_ROWS, NSA_GROUP * Q_BLOCK), F32)],
        compiler_params=_cparams(3),
        name="nsa_attention",
    )(proj_t, proj_t, kc, vct, hn, vt_pack, vt_pack, e, tri, wpat, cpat)


def _ret_kernel(rq_ref, rk_ref, rv_ref, rg_ref, cos_ref, sin_ref, dec_ref, qd_ref, kd_ref, cd_ref,
                gg_ref, gb_ref, o_ref, st_ref):
    c = RET_CHUNK
    hd = RET_HEAD_DIM

    @pl.when(pl.program_id(1) == 0)
    def _():
        st_ref[...] = jnp.zeros_like(st_ref)

    n_chunks = rq_ref.shape[1] // c

    def chunk(ci, _):
        r0 = pl.multiple_of(ci * c, c)
        cos = cos_ref[pl.ds(r0, c), :]
        sin = sin_ref[pl.ds(r0, c), :]
        for h in range(RET_HEADS):
            hs = slice(h * hd, (h + 1) * hd)
            q = rq_ref[0, pl.ds(r0, c), hs].astype(F32)
            k = rk_ref[0, pl.ds(r0, c), hs].astype(F32)
            v = rv_ref[0, pl.ds(r0, c), hs]
            q = q * cos + pltpu.roll(q, hd // 2, 1) * sin
            k = (k * cos + pltpu.roll(k, hd // 2, 1) * sin) * (hd ** -0.5)
            qb = q.astype(BF16)
            kb = k.astype(BF16)
            vb = v.astype(BF16)
            inner = lax.dot_general(qb, kb, (((1,), (1,)), ((), ())), preferred_element_type=F32) * dec_ref[h]
            st = st_ref[h]
            o = (jnp.dot(inner.astype(BF16), vb, preferred_element_type=F32)
                 + jnp.dot(qb, st.astype(BF16), preferred_element_type=F32) * qd_ref[h])
            kdt = (k * kd_ref[h]).T.astype(BF16)
            st_ref[h] = st * cd_ref[h] + jnp.dot(kdt, vb, preferred_element_type=F32)
            mu = jnp.mean(o, axis=-1, keepdims=True)
            d = o - mu
            var = jnp.mean(d * d, axis=-1, keepdims=True)
            y = d * lax.rsqrt(var + LN_EPS) * gg_ref[:, hs] + gb_ref[:, hs]
            o_ref[0, pl.ds(r0, c), hs] = (jax.nn.silu(rg_ref[0, pl.ds(r0, c), hs].astype(F32)) * y).astype(o_ref.dtype)
        return 0

    lax.fori_loop(0, n_chunks, chunk, 0)


def retention_branch(hn, gn_g, gn_b, tr):
    b, t, _ = hn.shape
    w = BRANCH_WIDTH
    hd = RET_HEAD_DIM
    c = RET_CHUNK
    inv = ROPE_BASE ** (-jnp.arange(0, hd, 2, dtype=F32) / hd)
    ang = jnp.arange(t, dtype=F32)[:, None] * inv[None, :]
    cos2 = jnp.concatenate([jnp.cos(ang), jnp.cos(ang)], axis=1)
    sin2 = jnp.concatenate([-jnp.sin(ang), jnp.sin(ang)], axis=1)
    log_g = jnp.log(1.0 - 2.0 ** (-5.0 - jnp.arange(RET_HEADS, dtype=F32)))
    idx = jnp.arange(c, dtype=F32)
    diff = idx[:, None] - idx[None, :]
    decay_in = jnp.where(diff >= 0, jnp.exp(log_g[:, None, None] * jnp.maximum(diff, 0.0)), 0.0)
    ones = jnp.ones((RET_HEADS, c, hd), F32)
    q_dec = jnp.exp(log_g[:, None] * (idx + 1.0))[:, :, None] * ones
    k_dec = jnp.exp(log_g[:, None] * (c - 1.0 - idx))[:, :, None] * ones
    c_dec = jnp.exp(log_g * c)[:, None, None] * jnp.ones((RET_HEADS, hd, hd), F32)
    col = lambda cc: (lambda i, j: (i, j, cc // w))
    full3 = lambda i, j: (0, 0, 0)
    return pl.pallas_call(
        _ret_kernel,
        grid=(b, t // tr),
        in_specs=[
            pl.BlockSpec((1, tr, w), col(NAT_RQ)),
            pl.BlockSpec((1, tr, w), col(NAT_RK)),
            pl.BlockSpec((1, tr, w), col(NAT_RV)),
            pl.BlockSpec((1, tr, w), col(NAT_RG)),
            pl.BlockSpec((tr, hd), lambda i, j: (j, 0)),
            pl.BlockSpec((tr, hd), lambda i, j: (j, 0)),
            pl.BlockSpec((RET_HEADS, c, c), full3),
            pl.BlockSpec((RET_HEADS, c, hd), full3),
            pl.BlockSpec((RET_HEADS, c, hd), full3),
            pl.BlockSpec((RET_HEADS, hd, hd), full3),
            pl.BlockSpec((1, w), lambda i, j: (0, 0)),
            pl.BlockSpec((1, w), lambda i, j: (0, 0)),
        ],
        out_specs=pl.BlockSpec((1, tr, w), lambda i, j: (i, j, 0)),
        out_shape=jax.ShapeDtypeStruct((b, t, w), BF16),
        scratch_shapes=[pltpu.VMEM((RET_HEADS, hd, hd), F32)],
        compiler_params=_cparams(2),
        name="retention",
    )(hn, hn, hn, hn, cos2, sin2, decay_in, q_dec, k_dec, c_dec, gn_g.reshape(1, w), gn_b.reshape(1, w))


def _layer_norm(z, g, b):
    mu = jnp.mean(z, axis=-1, keepdims=True)
    d = z - mu
    var = jnp.mean(d * d, axis=-1, keepdims=True)
    return d * lax.rsqrt(var + LN_EPS) * g + b


def _merge_kernel(tiles_per_seq, x_ref, ya_ref, cb_ref, cc_ref, ch_ref, pc_ref, ph_ref, cw_ref, yc_ref, mg_ref,
                  wbr_ref, wout_ref, g_ref, b_ref, o_ref, ob_ref):
    first = (pl.program_id(0) % tiles_per_seq) == 0
    u = cc_ref[...].astype(F32) * ch_ref[...].astype(F32)
    prev = pc_ref[...].astype(F32) * ph_ref[...].astype(F32)
    prev = jnp.where(first, 0.0, prev)
    row = lax.broadcasted_iota(jnp.int32, u.shape, 0)
    p1 = prev[CONV_HALO - 1:CONV_HALO, :]
    p2 = prev[CONV_HALO - 2:CONV_HALO - 1, :]
    u1 = jnp.where(row == 0, p1, pltpu.roll(u, 1, 0))
    u2 = jnp.where(row == 0, p2, jnp.where(row == 1, p1, pltpu.roll(u, 2, 0)))
    cw = cw_ref[...]
    y_b = cb_ref[...].astype(F32) * (cw[0:1, :] * u2 + cw[1:2, :] * u1 + cw[2:3, :] * u)

    mix = None
    for c, y in enumerate((ya_ref[...], y_b.astype(BF16), yc_ref[...])):
        proj = jnp.dot(y, wbr_ref[c], preferred_element_type=F32)
        gate = jax.nn.sigmoid(mg_ref[:, c * D_MODEL:(c + 1) * D_MODEL].astype(F32))
        mix = gate * proj if mix is None else mix + gate * proj
    m = jnp.dot(mix.astype(BF16), wout_ref[...], preferred_element_type=F32)
    out = _layer_norm(ALPHA * x_ref[...] + m, g_ref[...], b_ref[...])
    o_ref[...] = out
    ob_ref[...] = out.astype(BF16)


def merge_branches(x, ya, yc, hn, conv_w, w_br, w_out, g, b, tm, t):
    m = x.shape[0]
    d = D_MODEL
    w = BRANCH_WIDTH
    row = lambda i: (i, 0)
    col = lambda c: (lambda i: (i, c // w))
    hal = lambda c: (lambda i: (jnp.maximum(i * (tm // CONV_HALO) - 1, 0), c // w))
    wpad = jnp.zeros((8, w), F32).at[:CONV_K].set(conv_w)
    return pl.pallas_call(
        functools.partial(_merge_kernel, t // tm),
        grid=(m // tm,),
        in_specs=[
            pl.BlockSpec((tm, d), row),
            pl.BlockSpec((tm, w), row),
            pl.BlockSpec((tm, w), col(NAT_CB)),
            pl.BlockSpec((tm, w), col(NAT_CC)),
            pl.BlockSpec((tm, w), col(NAT_CH)),
            pl.BlockSpec((CONV_HALO, w), hal(NAT_CC)),
            pl.BlockSpec((CONV_HALO, w), hal(NAT_CH)),
            pl.BlockSpec((8, w), lambda i: (0, 0)),
            pl.BlockSpec((tm, w), row),
            pl.BlockSpec((tm, 3 * d), row),
            pl.BlockSpec((3, w, d), lambda i: (0, 0, 0)),
            pl.BlockSpec((d, d), lambda i: (0, 0)),
            pl.BlockSpec((1, d), lambda i: (0, 0)),
            pl.BlockSpec((1, d), lambda i: (0, 0)),
        ],
        out_specs=[pl.BlockSpec((tm, d), row), pl.BlockSpec((tm, d), row)],
        out_shape=[jax.ShapeDtypeStruct((m, d), F32), jax.ShapeDtypeStruct((m, d), BF16)],
        compiler_params=_cparams(1),
        name="merge",
    )(x, ya, hn, hn, hn, hn, hn, wpad, yc, hn, w_br.astype(BF16), w_out.astype(BF16), g.reshape(1, d),
      b.reshape(1, d))


def _cast3_kernel(g_ref, u_ref, d_ref, og_ref, ou_ref, od_ref):
    og_ref[...] = g_ref[0].astype(og_ref.dtype)
    ou_ref[...] = u_ref[0].astype(ou_ref.dtype)
    od_ref[...] = d_ref[0].astype(od_ref.dtype)


def ffn_weights_bf16(w_gate, w_up, w_down, layer):
    _, e, d, ff = w_gate.shape
    rg, rd = d // CAST_STEPS, ff // CAST_STEPS
    up_spec = pl.BlockSpec((1, 1, rg, ff), lambda i, j: (layer, i, j, 0))
    return pl.pallas_call(
        _cast3_kernel,
        grid=(e, CAST_STEPS),
        in_specs=[up_spec, up_spec, pl.BlockSpec((1, 1, rd, d), lambda i, j: (layer, i, j, 0))],
        out_specs=[pl.BlockSpec((1, rg, ff), lambda i, j: (i, j, 0)), pl.BlockSpec((1, rg, ff), lambda i, j: (i, j, 0)),
                   pl.BlockSpec((1, rd, d), lambda i, j: (i, j, 0))],
        out_shape=[jax.ShapeDtypeStruct((e, d, ff), BF16), jax.ShapeDtypeStruct((e, d, ff), BF16),
                   jax.ShapeDtypeStruct((e, ff, d), BF16)],
        compiler_params=_cparams(2),
        name="cast_bf16",
    )(w_gate, w_up, w_down)


def _swiglu_chunks(x, wg_ref, wu_ref, wd_ref, tf):
    acc = None
    for c in range(wg_ref.shape[2] // tf):
        sl = slice(c * tf, (c + 1) * tf)
        gte = jnp.dot(x, wg_ref[0, :, sl], preferred_element_type=F32)
        up = jnp.dot(x, wu_ref[0, :, sl], preferred_element_type=F32)
        h = (jax.nn.silu(gte) * up).astype(BF16)
        part = jnp.dot(h, wd_ref[0, sl, :], preferred_element_type=F32)
        acc = part if acc is None else acc + part
    return acc


def _mlp_kernel(tf, te_ref, tv_ref, x_ref, wg_ref, wu_ref, wd_ref, o_ref):
    i = pl.program_id(0)

    @pl.when(tv_ref[i] > 0)
    def _():
        o_ref[...] = _swiglu_chunks(x_ref[...], wg_ref, wu_ref, wd_ref, tf).astype(o_ref.dtype)

    @pl.when(tv_ref[i] == 0)
    def _():
        o_ref[...] = jnp.zeros_like(o_ref)


def _resident(block_shape, index_map):
    return pl.BlockSpec(block_shape, index_map, pipeline_mode=pl.Buffered(1))


def grouped_swiglu(xs, w_gate, w_up, w_down, tile_expert, tile_valid, tm, tf):
    n, d = xs.shape
    ff = w_gate.shape[2]
    grid_spec = pltpu.PrefetchScalarGridSpec(
        num_scalar_prefetch=2,
        grid=(n // tm,),
        in_specs=[
            pl.BlockSpec((tm, d), lambda i, te, tv: (i, 0)),
            _resident((1, d, ff), lambda i, te, tv: (te[i], 0, 0)),
            _resident((1, d, ff), lambda i, te, tv: (te[i], 0, 0)),
            _resident((1, ff, d), lambda i, te, tv: (te[i], 0, 0)),
        ],
        out_specs=pl.BlockSpec((tm, d), lambda i, te, tv: (i, 0)),
    )
    return pl.pallas_call(
        functools.partial(_mlp_kernel, tf),
        grid_spec=grid_spec,
        out_shape=jax.ShapeDtypeStruct((n, d), BF16),
        compiler_params=_cparams(1, BIG_VMEM_LIMIT),
        name="grouped_swiglu",
    )(tile_expert, tile_valid, xs, w_gate, w_up, w_down)


def _add_ln_kernel(*refs):
    x_ref, sc_ref, *adds, g_ref, b_ref, o_ref, ob_ref = refs
    z = ALPHA * x_ref[...]
    for k, a in enumerate(adds):
        z = z + sc_ref[:, k:k + 1] * a[...].astype(F32)
    out = _layer_norm(z, g_ref[...], b_ref[...])
    o_ref[...] = out
    ob_ref[...] = out.astype(BF16)


def add_layer_norm(x, adds, scales, g, b, tm):
    m, d = x.shape
    row = lambda i: (i, 0)
    return pl.pallas_call(
        _add_ln_kernel,
        grid=(m // tm,),
        in_specs=([pl.BlockSpec((tm, d), row), pl.BlockSpec((tm, len(adds)), row)]
                  + [pl.BlockSpec((tm, d), row)] * len(adds) + [pl.BlockSpec((1, d), lambda i: (0, 0))] * 2),
        out_specs=[pl.BlockSpec((tm, d), row)] * 2,
        out_shape=[jax.ShapeDtypeStruct((m, d), F32), jax.ShapeDtypeStruct((m, d), BF16)],
        compiler_params=_cparams(1),
        name="add_layer_norm",
    )(x, scales, *adds, g.reshape(1, d), b.reshape(1, d))


def _router_kernel(x_ref, wh_ref, wl_ref, idx_ref, p_ref):
    x = x_ref[...]
    xh = x.astype(BF16)
    xl = (x - xh.astype(F32)).astype(BF16)
    wh = wh_ref[...]
    logits = (jnp.dot(xh, wh, preferred_element_type=F32) + jnp.dot(xl, wh, preferred_element_type=F32)
              + jnp.dot(xh, wl_ref[...], preferred_element_type=F32))
    lg = logits.T[0:N_EXPERTS, :]
    row = lax.broadcasted_iota(jnp.int32, lg.shape, 0).astype(F32)
    m1 = jnp.max(lg, axis=0, keepdims=True)
    i1 = jnp.min(jnp.where(lg == m1, row, float(N_EXPERTS)), axis=0, keepdims=True)
    lg2 = jnp.where(row == i1, -jnp.inf, lg)
    m2 = jnp.max(lg2, axis=0, keepdims=True)
    i2 = jnp.min(jnp.where(lg2 == m2, row, float(N_EXPERTS)), axis=0, keepdims=True)
    e2 = jnp.exp(m2 - m1)
    den = 1.0 + e2
    idx_ref[...] = jnp.concatenate([i1, i2], axis=0).astype(jnp.int32)
    p_ref[...] = jnp.concatenate([1.0 / den, e2 / den], axis=0)


def router_top2(x, w_router, tm):
    m, d = x.shape
    wp = jnp.zeros((d, 128), F32).at[:, :N_EXPERTS].set(w_router)
    wh = wp.astype(BF16)
    wl = (wp - wh.astype(F32)).astype(BF16)
    return pl.pallas_call(
        _router_kernel,
        grid=(m // tm,),
        in_specs=[pl.BlockSpec((tm, d), lambda i: (i, 0)), pl.BlockSpec((d, 128), lambda i: (0, 0)),
                  pl.BlockSpec((d, 128), lambda i: (0, 0))],
        out_specs=[pl.BlockSpec((TOP_K, tm), lambda i: (0, i))] * 2,
        out_shape=[jax.ShapeDtypeStruct((TOP_K, m), jnp.int32), jax.ShapeDtypeStruct((TOP_K, m), F32)],
        compiler_params=_cparams(1),
        name="router",
    )(x, wh, wl)


def _dense_ffn_kernel(tf, xb_ref, x_ref, wg_ref, wu_ref, wd_ref, g_ref, b_ref, o_ref, ob_ref):
    acc = _swiglu_chunks(xb_ref[...], wg_ref, wu_ref, wd_ref, tf)
    out = _layer_norm(ALPHA * x_ref[...] + acc, g_ref[...], b_ref[...])
    o_ref[...] = out
    ob_ref[...] = out.astype(BF16)


def dense_ffn(x, xb, w_gate, w_up, w_down, g, b):
    m, d = x.shape
    ff = w_gate.shape[2]
    tm = min(FFN_TM, m)
    row = lambda i: (i, 0)
    return pl.pallas_call(
        functools.partial(_dense_ffn_kernel, FFN_TF),
        grid=(m // tm,),
        in_specs=[
            pl.BlockSpec((tm, d), row),
            pl.BlockSpec((tm, d), row),
            _resident((1, d, ff), lambda i: (0, 0, 0)),
            _resident((1, d, ff), lambda i: (0, 0, 0)),
            _resident((1, ff, d), lambda i: (0, 0, 0)),
            pl.BlockSpec((1, d), lambda i: (0, 0)),
            pl.BlockSpec((1, d), lambda i: (0, 0)),
        ],
        out_specs=[pl.BlockSpec((tm, d), row)] * 2,
        out_shape=[jax.ShapeDtypeStruct((m, d), F32), jax.ShapeDtypeStruct((m, d), BF16)],
        compiler_params=_cparams(1, BIG_VMEM_LIMIT),
        name="dense_ffn",
    )(xb, x, w_gate, w_up, w_down, g.reshape(1, d), b.reshape(1, d))


def _rank_kernel(e_ref, u_ref, rank_ref, cnt_ref, carry_ref):
    @pl.when(pl.program_id(0) == 0)
    def _():
        carry_ref[...] = jnp.zeros_like(carry_ref)

    rows, width = e_ref.shape
    expert = lax.broadcasted_iota(jnp.int32, (N_EXPERTS, width), 0)
    carry = carry_ref[...]
    for r in range(rows):
        hot = e_ref[r:r + 1, :] == expert
        prefix = jnp.dot(hot.astype(BF16), u_ref[...], preferred_element_type=F32) + carry
        rank_ref[r:r + 1, :] = (jnp.sum(jnp.where(hot, prefix, 0.0), axis=0, keepdims=True) - 1.0).astype(jnp.int32)
        carry = prefix[:, width - 1:width]
    carry_ref[...] = carry
    cnt_ref[...] = jnp.broadcast_to(carry, cnt_ref.shape).astype(jnp.int32)


def routing_ranks(e_flat):
    n = e_flat.shape[0]
    width = RANK_WIDTH
    rows = 8
    e2 = e_flat.reshape(n // width, width)
    upper = (jnp.arange(width)[:, None] <= jnp.arange(width)[None, :]).astype(BF16)
    rank, cnt = pl.pallas_call(
        _rank_kernel,
        grid=(n // (rows * width),),
        in_specs=[pl.BlockSpec((rows, width), lambda i: (i, 0)), pl.BlockSpec((width, width), lambda i: (0, 0))],
        out_specs=[pl.BlockSpec((rows, width), lambda i: (i, 0)), pl.BlockSpec((N_EXPERTS, 128), lambda i: (0, 0))],
        out_shape=[jax.ShapeDtypeStruct((n // width, width), jnp.int32),
                   jax.ShapeDtypeStruct((N_EXPERTS, 128), jnp.int32)],
        scratch_shapes=[pltpu.VMEM((N_EXPERTS, 1), F32)],
        compiler_params=_cparams(1),
        name="routing_ranks",
    )(e2, upper)
    return rank.reshape(n), cnt[:, 0]


def moe_ffn(x, xb, w_router, w_gate, w_up, w_down, g, b):
    m = x.shape[0]
    tm = FFN_TM
    top_i, top_p = router_top2(x, w_router, min(ROW_TILE, m))
    e_flat = top_i.reshape(-1)
    rank, counts = routing_ranks(e_flat)
    padded = ((counts + tm - 1) // tm) * tm
    ends = jnp.cumsum(padded)
    starts = ends - padded
    pos = starts[e_flat] + rank
    n_slots = TOP_K * m + N_EXPERTS * tm
    n_tiles = n_slots // tm
    tile_start = jnp.arange(n_tiles, dtype=jnp.int32) * tm
    tile_expert = jnp.minimum(jnp.searchsorted(ends, tile_start, side="right"), N_EXPERTS - 1).astype(jnp.int32)
    tile_valid = (tile_start < ends[-1]).astype(jnp.int32)
    n_flat = TOP_K * m
    order = jnp.sort(e_flat * n_flat + jnp.arange(n_flat, dtype=jnp.int32)) % n_flat
    first_entry = jnp.cumsum(counts) - counts
    r_in_group = tile_start - starts[tile_expert]
    slot_rank = (r_in_group[:, None] + jnp.arange(tm, dtype=jnp.int32)[None, :]).reshape(-1)
    slot_expert = jnp.repeat(tile_expert, tm)
    entry = jnp.clip(first_entry[slot_expert] + slot_rank, 0, n_flat - 1)
    src = jnp.where(slot_rank < counts[slot_expert], order[entry] % m, 0)
    xs = jnp.take(xb, src, axis=0)
    ys = grouped_swiglu(xs, w_gate, w_up, w_down,
                        tile_expert, tile_valid, tm, FFN_TF)
    y0 = jnp.take(ys, pos[:m], axis=0)
    y1 = jnp.take(ys, pos[m:], axis=0)
    return add_layer_norm(x, [y0, y1], top_p.T, g, b, min(ROW_TILE, m))


def _pack_in_weights(w):
    d = w.shape[0]
    o = 0
    q = w[:, o:o + 512]; o += 512
    kv = w[:, o:o + 768].reshape(d, 3, 2, NSA_KV_GROUPS, HEAD_DIM); o += 768
    ng = w[:, o:o + 24]; o += 24
    rest = w[:, o:o + 7 * 512]; o += 7 * 512
    mg = w[:, o:]
    kpack = jnp.concatenate([kv[:, 1, 0], kv[:, 2, 0]], axis=-1).reshape(d, 2 * NSA_KV_GROUPS * HEAD_DIM)
    kc = kv[:, 0, 0].reshape(d, NSA_KV_GROUPS * HEAD_DIM)
    vc = kv[:, 0, 1].reshape(d, NSA_KV_GROUPS * HEAD_DIM)
    w_nat = jnp.concatenate([mg, rest, kpack, kc, vc], axis=1).astype(BF16)
    vs = kv[:, 1, 1].reshape(d, NSA_KV_GROUPS * HEAD_DIM)
    vw = kv[:, 2, 1].reshape(d, NSA_KV_GROUPS * HEAD_DIM)
    ngp = jnp.pad(ng.reshape(d, NSA_KV_GROUPS, NSA_GROUP * 3), ((0, 0), (0, 0), (0, 4))).reshape(d, 32)
    w_t = jnp.concatenate([q, vs, vw, ngp], axis=1).T.astype(BF16)
    return w_nat, w_t


def token_mixer_ln(x, xb, w_in, cmp_pos, cmp_w1, cmp_b1, cmp_w2, cmp_b2, conv_w, gn_g, gn_b, w_br, w_out, ln_g, ln_b):
    b, t, d = x.shape
    m = b * t
    w_nat, w_t = _pack_in_weights(w_in)
    x2 = x.reshape(m, d)
    hn, proj_t = input_projection(xb, w_nat, w_t, min(PROJ_TM, t), PROJ_TN)
    hn = hn.reshape(b, t, NAT_COLS)
    nch = t // CMP_STRIDE
    raw = hn[:, :, NAT_KC:NAT_KC + 2 * NSA_KV_GROUPS * HEAD_DIM].reshape(b, t, 2, NSA_KV_GROUPS, HEAD_DIM)
    chunks = raw.transpose(2, 0, 3, 1, 4).reshape(2, b, NSA_KV_GROUPS, nch, CMP_STRIDE * HEAD_DIM)
    cn, ct = nsa_compress_kv(chunks, cmp_pos, cmp_w1, cmp_b1, cmp_w2, cmp_b2)
    kc = cn[0].astype(BF16)
    vct = ct[1].astype(BF16)
    vt4 = proj_t[:, TR_VS:TR_NG, :].astype(BF16).reshape(b, 4, HEAD_DIM, t)
    ones_pad = jnp.zeros((b, 4, VT_ROWS - HEAD_DIM, t), BF16).at[:, :, 0].set(1.0)
    vt_pack = jnp.concatenate([vt4, ones_pad], axis=2).reshape(b, 4 * VT_ROWS, t)
    y_a = nsa_attention(proj_t, kc, vct, hn, vt_pack)
    y_c = retention_branch(hn, gn_g, gn_b, min(ROW_TILE, t))
    return merge_branches(x2, y_a.reshape(m, -1), y_c.reshape(m, -1), hn.reshape(m, NAT_COLS), conv_w,
                          w_br, w_out, ln_g, ln_b, min(MERGE_TM, t), t)


def kernel(x, w_in, cmp_pos, cmp_w1, cmp_b1, cmp_w2, cmp_b2, conv_w, ret_gn_g, ret_gn_b, w_br, w_out,
           ln1_g, ln1_b, ln2_g, ln2_b, ffn_gate, ffn_up, ffn_down, moe_router, moe_gate, moe_up, moe_down):
    b, t, d = x.shape
    xb = x.astype(BF16)
    for l in range(DEPTH):
        x2, x2b = token_mixer_ln(x, xb, w_in[l], cmp_pos[l], cmp_w1[l], cmp_b1[l], cmp_w2[l], cmp_b2[l], conv_w[l],
                                 ret_gn_g[l], ret_gn_b[l], w_br[l], w_out[l], ln1_g[l], ln1_b[l])
        if l % 2 == 0:
            wg, wu, wd = ffn_weights_bf16(ffn_gate[:, None], ffn_up[:, None], ffn_down[:, None], l // 2)
            x2, x2b = dense_ffn(x2, x2b, wg, wu, wd, ln2_g[l], ln2_b[l])
        else:
            wg, wu, wd = ffn_weights_bf16(moe_gate, moe_up, moe_down, l // 2)
            x2, x2b = moe_ffn(x2, x2b, moe_router[l // 2], wg, wu, wd, ln2_g[l], ln2_b[l])
        x, xb = x2.reshape(b, t, d), x2b.reshape(b, t, d)
    return x
```

```python
import functools

import jax
import jax.numpy as jnp
from jax import lax
from jax.experimental import pallas as pl
from jax.experimental.pallas import tpu as pltpu

F32 = jnp.float32
BF16 = jnp.bfloat16

D_MODEL = 1024
DEPTH = 4
BRANCH_WIDTH = 512
HEAD_DIM = 64
NSA_HEADS = 8
NSA_KV_GROUPS = 2
NSA_GROUP = 4
CMP_LEN = 32
CMP_STRIDE = 16
SLC_BLOCK = 64
SLC_TOPN = 16
WINDOW = 512
CMP_HIDDEN = 256
Q_BLOCK = 128
CONV_K = 3
CONV_HALO = 16
RET_HEADS = 4
RET_HEAD_DIM = 128
RET_CHUNK = 256
ROPE_BASE = 10000.0
D_FF = 3584
N_EXPERTS = 8
TOP_K = 2
LN_EPS = 1e-5
ALPHA = (2.0 * DEPTH) ** 0.25

NAT_MG = 0
NAT_CB = 3072
NAT_CC = 3584
NAT_CH = 4096
NAT_RQ = 4608
NAT_RK = 5120
NAT_RV = 5632
NAT_RG = 6144
NAT_KP = 6656
NAT_KC = 6912
NAT_VC = 7040
NAT_COLS = 7168
TR_Q = 0
TR_VS = 512
TR_VW = 640
TR_NG = 768
TR_ROWS = 800

NEG = -1e30
SEL_TK = 1024
LOG2E = 1.4426950408889634
NSA_QSUB = 2
CMP_CLASS_ROWS = 128
PG_PAD = 8
PICKED = -3e38
VT_ROWS = 80
RANK_WIDTH = 1024
ROW_TILE = 1024
MERGE_TM = 512
PROJ_TM = 2048
PROJ_TN = 1024
FFN_TM = 512
FFN_TF = 1792
CAST_STEPS = 4
VMEM_LIMIT = 48 * 1024 * 1024
BIG_VMEM_LIMIT = 56 * 1024 * 1024


def _cparams(n_axes, vmem=VMEM_LIMIT):
    return pltpu.CompilerParams(dimension_semantics=("arbitrary",) * n_axes, vmem_limit_bytes=vmem)


def _proj_kernel(x_ref, w_ref, wt_ref, o_ref, ot_ref):
    @pl.when(pl.program_id(1) == 0)
    def _():
        ot_ref[0] = lax.dot_general(wt_ref[...], x_ref[...], (((1,), (1,)), ((), ())), preferred_element_type=F32)

    o_ref[...] = jnp.dot(x_ref[...], w_ref[...], preferred_element_type=F32).astype(o_ref.dtype)


def input_projection(x, w_nat, w_t, tm, tn):
    b, t, k = x.shape
    m = b * t
    n = w_nat.shape[1]
    r = w_t.shape[0]
    tps = t // tm
    return pl.pallas_call(
        _proj_kernel,
        grid=(m // tm, n // tn),
        in_specs=[pl.BlockSpec((tm, k), lambda i, j: (i, 0)), pl.BlockSpec((k, tn), lambda i, j: (0, j)),
                  pl.BlockSpec((r, k), lambda i, j: (0, 0))],
        out_specs=[pl.BlockSpec((tm, tn), lambda i, j: (i, j)),
                   pl.BlockSpec((1, r, tm), lambda i, j: (i // tps, 0, i % tps))],
        out_shape=[jax.ShapeDtypeStruct((m, n), BF16), jax.ShapeDtypeStruct((b, r, t), F32)],
        compiler_params=_cparams(2, BIG_VMEM_LIMIT),
        name="input_projection",
    )(x.reshape(m, k), w_nat, w_t)


def _cmp_kernel(c_ref, pos_ref, w1_ref, b1_ref, w2_ref, w2t_ref, b2_ref, b2t_ref, on_ref, ot_ref):
    half = CMP_STRIDE * HEAD_DIM
    c = c_ref[0, 0, 0].astype(BF16)
    w1 = w1_ref[0]
    a = jnp.dot(c, w1[:half], preferred_element_type=F32)
    bm = jnp.dot(c, w1[half:], preferred_element_type=F32)
    nch = a.shape[0]
    bm = pltpu.roll(bm, nch - 1, 0)
    posb = jnp.dot(pos_ref[0].astype(BF16), w1, preferred_element_type=F32)[0:1]
    hid = jax.nn.gelu(a + bm + posb + b1_ref[0])
    hb = hid.astype(BF16)
    on_ref[0, 0, 0] = jnp.dot(hb, w2_ref[0], preferred_element_type=F32) + b2_ref[0]
    ot_ref[0, 0, 0] = lax.dot_general(w2t_ref[0], hb, (((1,), (1,)), ((), ())),
                                      preferred_element_type=F32) + b2t_ref[0]


def nsa_compress_kv(chunks, pos, w1, b1, w2, b2):
    _, b, g, nch, cw = chunks.shape
    pos_flat = jnp.broadcast_to(pos.reshape(2, 1, CMP_LEN * HEAD_DIM), (2, 8, CMP_LEN * HEAD_DIM))
    w1b = w1.astype(BF16)
    w2b = w2.astype(BF16)
    w2t = jnp.swapaxes(w2, 1, 2).astype(BF16)
    b1r = b1.reshape(2, 1, CMP_HIDDEN)
    b2r = b2.reshape(2, 1, HEAD_DIM)
    b2t = b2.reshape(2, HEAD_DIM, 1)
    sel = lambda kv, i, j: (kv, 0, 0)
    return pl.pallas_call(
        _cmp_kernel,
        grid=(2, b, g),
        in_specs=[
            pl.BlockSpec((1, 1, 1, nch, cw), lambda kv, i, j: (kv, i, j, 0, 0)),
            pl.BlockSpec((1, 8, CMP_LEN * HEAD_DIM), sel),
            pl.BlockSpec((1, CMP_LEN * HEAD_DIM, CMP_HIDDEN), sel),
            pl.BlockSpec((1, 1, CMP_HIDDEN), sel),
            pl.BlockSpec((1, CMP_HIDDEN, HEAD_DIM), sel),
            pl.BlockSpec((1, HEAD_DIM, CMP_HIDDEN), sel),
            pl.BlockSpec((1, 1, HEAD_DIM), sel),
            pl.BlockSpec((1, HEAD_DIM, 1), sel),
        ],
        out_specs=[
            pl.BlockSpec((1, 1, 1, nch, HEAD_DIM), lambda kv, i, j: (kv, i, j, 0, 0)),
            pl.BlockSpec((1, 1, 1, HEAD_DIM, nch), lambda kv, i, j: (kv, i, j, 0, 0)),
        ],
        out_shape=[
            jax.ShapeDtypeStruct((2, b, g, nch, HEAD_DIM), F32),
            jax.ShapeDtypeStruct((2, b, g, HEAD_DIM, nch), F32),
        ],
        compiler_params=_cparams(3),
        name="nsa_compress",
    )(chunks, pos_flat, w1b, b1r, w2b, w2t, b2r, b2t)


def _accumulate(s, mt, vt, m, acc):
    m_new = jnp.maximum(m, mt)
    alpha = jnp.exp2(m - m_new)
    p = jnp.exp2(s - m_new).astype(BF16)
    return m_new, alpha * acc + jnp.dot(vt, p, preferred_element_type=F32)


def _nsa_kernel(qt_ref, ng_ref, kc_ref, vct_ref, kp_ref, vst_ref, vwt_ref, e_ref, tri_ref, wpat_ref,
                cpat_ref, o_ref, bias_ref, w_ref, sbuf_ref, oc_ref, pg_ref, ms_ref, as_ref):
    nq = Q_BLOCK
    nsub = qt_ref.shape[2] // nq
    ncol = NSA_GROUP * nq
    n_sel = bias_ref.shape[1]
    ncp = kc_ref.shape[2]
    kw = 2 * HEAD_DIM
    bpt = SEL_TK // SLC_BLOCK
    n_tiles_total = kp_ref.shape[1] // SEL_TK
    qb_first = pl.program_id(2) * nsub
    wlen = WINDOW + nq

    def query_setup(u):
        qb = qb_first + u
        q0 = pl.multiple_of(qb * nq, nq)
        qt = qt_ref[0, :, u * nq:(u + 1) * nq] * (HEAD_DIM ** -0.5 * LOG2E)
        qs = jnp.concatenate([qt[r * HEAD_DIM:(r + 1) * HEAD_DIM, :] for r in range(NSA_GROUP)], axis=1)
        zq = jnp.zeros_like(qs)
        tq1 = q0 + lax.broadcasted_iota(jnp.int32, (1, nq), 1)
        return dict(
            q0=q0, q_c=qs.astype(BF16),
            q_slc=jnp.concatenate([qs, zq], axis=0).astype(BF16),
            q_win=jnp.concatenate([zq, qs], axis=0).astype(BF16),
            tq=q0 + (lax.broadcasted_iota(jnp.int32, (1, ncol), 1) & (nq - 1)),
            coff=pl.multiple_of(ncp - qb * (nq // CMP_STRIDE), nq // CMP_STRIDE),
            cur=jnp.right_shift(tq1, 6).astype(F32),
            first_diag=(2 * qb).astype(F32))

    qs_ = [query_setup(u) for u in range(nsub)]

    crow = min(CMP_CLASS_ROWS, ncp)
    qb_per_class = crow // (nq // CMP_STRIDE)
    assert qb_per_class % nsub == 0
    cls = qb_first // qb_per_class
    ratio = SLC_BLOCK // CMP_STRIDE
    for c in range(ncp // crow):
        nc_c = crow * (c + 1)
        ns_c = nc_c * CMP_STRIDE // SLC_BLOCK

        @pl.when(cls == c)
        def _(nc_c=nc_c, ns_c=ns_c):
            j_io = lax.broadcasted_iota(jnp.int32, (ns_c, nq), 0).astype(F32)
            scores = []
            for u, q in enumerate(qs_):
                pg_ref[u, 0:PG_PAD, :] = jnp.zeros((PG_PAD, nq), F32)
                s_c = (jnp.dot(kc_ref[0, 0, 0:nc_c, :], q["q_c"], preferred_element_type=F32)
                       + jnp.concatenate([cpat_ref[pl.ds(q["coff"], nc_c), :]] * NSA_GROUP, axis=1))
                m_c = jnp.max(s_c, axis=0, keepdims=True)
                e_c = jnp.exp2(s_c - m_c)
                l_c = jnp.sum(e_c, axis=0, keepdims=True)
                p_c = e_c * jnp.where(q["tq"] >= CMP_LEN - 1, 1.0 / l_c, 0.0)
                oc_ref[u] = jnp.dot(vct_ref[0, 0, :, 0:nc_c], p_c.astype(BF16), preferred_element_type=F32)

                p_grp = p_c[:, 0:nq]
                for r in range(1, NSA_GROUP):
                    p_grp = p_grp + p_c[:, r * nq:(r + 1) * nq]
                pg_ref[u, PG_PAD:PG_PAD + nc_c, :] = p_grp
                taps = [pg_ref[u, pl.ds(PG_PAD - 1 + k, ns_c, stride=ratio), :] for k in range(ratio + 1)]
                score = taps[0] + 2.0 * (taps[1] + taps[2] + taps[3]) + taps[4]
                score = jnp.where(j_io <= q["cur"], score, -1.0)
                score = jnp.where(j_io == 0.0, PICKED, score)
                score = jnp.where(j_io == q["cur"], PICKED, score)
                score = jnp.where(j_io == q["cur"] - 1.0, PICKED, score)
                scores.append(score)
            for _ in range(SLC_TOPN - 3):
                for u in range(nsub):
                    mx = jnp.max(scores[u], axis=0, keepdims=True)
                    first = jnp.min(jnp.where(scores[u] == mx, j_io, float(ns_c)), axis=0, keepdims=True)
                    scores[u] = jnp.where(j_io == first, PICKED, scores[u])
            for u, q in enumerate(qs_):
                bias = jnp.where(j_io < q["first_diag"], jnp.where(scores[u] == PICKED, 0.0, NEG), NEG)
                bias_ref[u, 0:ns_c, :] = bias.astype(BF16)
                if ns_c < n_sel:
                    bias_ref[u, ns_c:, :] = jnp.full((n_sel - ns_c, nq), NEG, BF16)

    def prepare(u):
        q0, q_slc, q_win = qs_[u]["q0"], qs_[u]["q_slc"], qs_[u]["q_win"]
        o_c = oc_ref[u]

        wstart = pl.multiple_of(jnp.maximum(q0 - WINDOW, 0), nq)
        woff = pl.multiple_of(jnp.maximum(WINDOW - q0, 0), nq)
        s_w = jnp.dot(kp_ref[0, pl.ds(wstart, wlen), :], q_win, preferred_element_type=F32)
        s_w = s_w + jnp.concatenate([wpat_ref[pl.ds(woff, wlen), :]] * NSA_GROUP, axis=1)
        p_w = jnp.exp2(s_w - jnp.max(s_w, axis=0, keepdims=True)).astype(BF16)
        acc_w = jnp.dot(vwt_ref[0, :, pl.ds(wstart, wlen)], p_w, preferred_element_type=F32)
        o_w = acc_w[0:HEAD_DIM] / acc_w[HEAD_DIM:HEAD_DIM + 1]

        for slot in range(2):
            w_ref[slot, u, 0:kw, :] = q_slc
            w_ref[slot, u, kw:, :] = jnp.zeros((w_ref.shape[2] - kw, ncol), BF16)
        return q0, q_slc, o_c, o_w

    prepared = [prepare(u) for u in range(nsub)]

    def tile_start(j):
        jc = jnp.minimum(j, n_tiles_total - 1)
        return jc, pl.multiple_of(jc * SEL_TK, SEL_TK)

    def produce(j, slot):
        jc, k0 = tile_start(j)
        lhs = jnp.concatenate([kp_ref[0, pl.ds(k0, SEL_TK), :], e_ref[...]], axis=1)
        mts = []
        for u in range(nsub):
            b16 = bias_ref[u, pl.ds(pl.multiple_of(jc * bpt, bpt), bpt), :]
            w_ref[slot, u, kw:kw + bpt, :] = jnp.concatenate([b16] * NSA_GROUP, axis=1)
            s = jnp.dot(lhs, w_ref[slot, u], preferred_element_type=F32)
            sbuf_ref[slot, u] = s
            mts.append(jnp.max(s, axis=0, keepdims=True))
        return tuple(mts)

    def consume(j, slot, mts, state):
        _, k0 = tile_start(j)
        vt = vst_ref[0, :, pl.ds(k0, SEL_TK)]
        return tuple(_accumulate(sbuf_ref[slot, u], mts[u], vt, *state[u]) for u in range(nsub))

    def pair(i, carry):
        mt_a, state = carry
        mt_b = produce(2 * i + 1, 1)
        state = consume(2 * i, 0, mt_a, state)
        mt_a = produce(2 * i + 2, 0)
        state = consume(2 * i + 1, 1, mt_b, state)
        return mt_a, state

    def park(state):
        for u in range(nsub):
            ms_ref[u] = state[u][0]
            as_ref[u] = state[u][1]

    n_main = jnp.maximum(((qb_first + nsub - 1) * nq + (SEL_TK - 1)) // SEL_TK, 1)
    n_pairs = (n_main + 1) // 2
    state = tuple((jnp.full((1, ncol), NEG, F32), jnp.zeros((vst_ref.shape[1], ncol), F32)) for _ in range(nsub))
    mt_a, state = lax.fori_loop(0, n_pairs - 1, pair, (produce(0, 0), state))
    tail = 2 * (n_pairs - 1)
    odd = (n_main & 1) == 1

    @pl.when(odd)
    def _():
        park(consume(tail, 0, mt_a, state))

    @pl.when(jnp.logical_not(odd))
    def _():
        mt_b = produce(tail + 1, 1)
        park(consume(tail + 1, 1, mt_b, consume(tail, 0, mt_a, state)))

    state = tuple((ms_ref[u], as_ref[u]) for u in range(nsub))

    for u in range(nsub):
        q0, q_slc, o_c, o_w = prepared[u]
        s_d = jnp.dot(kp_ref[0, pl.ds(q0, nq), :], q_slc, preferred_element_type=F32) + tri_ref[...]
        _, acc_s = _accumulate(s_d, jnp.max(s_d, axis=0, keepdims=True), vst_ref[0, :, pl.ds(q0, nq)], *state[u])
        o_s = acc_s[0:HEAD_DIM] / acc_s[HEAD_DIM:HEAD_DIM + 1]
        gts = jax.nn.sigmoid(ng_ref[0, :, u * nq:(u + 1) * nq])
        rows = []
        for r in range(NSA_GROUP):
            sl = slice(r * nq, (r + 1) * nq)
            rows.append(gts[3 * r:3 * r + 1, :] * o_c[:, sl] + gts[3 * r + 1:3 * r + 2, :] * o_s[:, sl]
                        + gts[3 * r + 2:3 * r + 3, :] * o_w[:, sl])
        y_t = jnp.concatenate(rows, axis=0)
        o_ref[0, u * nq:(u + 1) * nq, :] = y_t.T.astype(o_ref.dtype)


def _nsa_constants(ncp):
    nq = Q_BLOCK
    ql = jnp.arange(nq)[None, :]
    r = jnp.arange(SEL_TK)[:, None]
    e = (r // SLC_BLOCK == jnp.arange(2 * HEAD_DIM)[None, :]).astype(BF16)
    rd = jnp.arange(nq)[:, None]
    tri = jnp.tile(jnp.where(rd <= ql, 0.0, NEG).astype(F32), (1, NSA_GROUP))
    rw = jnp.arange(2 * WINDOW + nq)[:, None]
    wpat = jnp.where((rw > ql) & (rw <= ql + WINDOW), 0.0, NEG).astype(F32)
    rc = jnp.arange(2 * ncp)[:, None] - ncp
    cpat = jnp.where(CMP_STRIDE * rc + (CMP_LEN - 1) <= ql, 0.0, NEG).astype(F32)
    return e, tri, wpat, cpat


def nsa_attention(proj_t, kc, vct, hn, vt_pack):
    b, _, t = proj_t.shape
    ncp = kc.shape[2]
    n_sel = t // SLC_BLOCK
    gq = NSA_GROUP * HEAD_DIM
    qstep = NSA_QSUB * Q_BLOCK
    e, tri, wpat, cpat = _nsa_constants(ncp)
    const2 = lambda i, g, q: (0, 0)
    return pl.pallas_call(
        _nsa_kernel,
        grid=(b, NSA_KV_GROUPS, t // qstep),
        in_specs=[
            pl.BlockSpec((1, gq, qstep), lambda i, g, q: (i, g, q)),
            pl.BlockSpec((1, 16, qstep), lambda i, g, q: (i, TR_NG // 16 + g, q)),
            pl.BlockSpec((1, 1, ncp, HEAD_DIM), lambda i, g, q: (i, g, 0, 0)),
            pl.BlockSpec((1, 1, HEAD_DIM, ncp), lambda i, g, q: (i, g, 0, 0)),
            pl.BlockSpec((1, t, 2 * HEAD_DIM), lambda i, g, q: (i, 0, NAT_KP // (2 * HEAD_DIM) + g)),
            pl.BlockSpec((1, VT_ROWS, t), lambda i, g, q: (i, g, 0)),
            pl.BlockSpec((1, VT_ROWS, t), lambda i, g, q: (i, NSA_KV_GROUPS + g, 0)),
            pl.BlockSpec(e.shape, const2),
            pl.BlockSpec(tri.shape, const2),
            pl.BlockSpec(wpat.shape, const2),
            pl.BlockSpec(cpat.shape, const2),
        ],
        out_specs=pl.BlockSpec((1, qstep, gq), lambda i, g, q: (i, q, g)),
        out_shape=jax.ShapeDtypeStruct((b, t, NSA_HEADS * HEAD_DIM), BF16),
        scratch_shapes=[pltpu.VMEM((NSA_QSUB, n_sel, Q_BLOCK), BF16),
                        pltpu.VMEM((2, NSA_QSUB, 4 * HEAD_DIM, NSA_GROUP * Q_BLOCK), BF16),
                        pltpu.VMEM((2, NSA_QSUB, SEL_TK, NSA_GROUP * Q_BLOCK), F32),
                        pltpu.VMEM((NSA_QSUB, HEAD_DIM, NSA_GROUP * Q_BLOCK), F32),
                        pltpu.VMEM((NSA_QSUB, PG_PAD + ncp, Q_BLOCK), F32),
                        pltpu.VMEM((NSA_QSUB, 1, NSA_GROUP * Q_BLOCK), F32),
                        pltpu.VMEM((NSA_QSUB, VT_ROWS, NSA_GROUP * Q_BLOCK), F32)],
        compiler_params=_cparams(3),
        name="nsa_attention",
    )(proj_t, proj_t, kc, vct, hn, vt_pack, vt_pack, e, tri, wpat, cpat)


def _ret_kernel(rq_ref, rk_ref, rv_ref, rg_ref, cos_ref, sin_ref, dec_ref, qd_ref, kd_ref, cd_ref,
                gg_ref, gb_ref, o_ref, st_ref):
    c = RET_CHUNK
    hd = RET_HEAD_DIM

    @pl.when(pl.program_id(1) == 0)
    def _():
        st_ref[...] = jnp.zeros_like(st_ref)

    n_chunks = rq_ref.shape[1] // c

    def chunk(ci, _):
        r0 = pl.multiple_of(ci * c, c)
        cos = cos_ref[pl.ds(r0, c), :]
        sin = sin_ref[pl.ds(r0, c), :]
        for h in range(RET_HEADS):
            hs = slice(h * hd, (h + 1) * hd)
            q = rq_ref[0, pl.ds(r0, c), hs].astype(F32)
            k = rk_ref[0, pl.ds(r0, c), hs].astype(F32)
            v = rv_ref[0, pl.ds(r0, c), hs]
            q = q * cos + pltpu.roll(q, hd // 2, 1) * sin
            k = (k * cos + pltpu.roll(k, hd // 2, 1) * sin) * (hd ** -0.5)
            qb = q.astype(BF16)
            kb = k.astype(BF16)
            vb = v.astype(BF16)
            inner = lax.dot_general(qb, kb, (((1,), (1,)), ((), ())), preferred_element_type=F32) * dec_ref[h]
            st = st_ref[h]
            o = (jnp.dot(inner.astype(BF16), vb, preferred_element_type=F32)
                 + jnp.dot(qb, st.astype(BF16), preferred_element_type=F32) * qd_ref[h])
            kdt = (k * kd_ref[h]).T.astype(BF16)
            st_ref[h] = st * cd_ref[h] + jnp.dot(kdt, vb, preferred_element_type=F32)
            mu = jnp.mean(o, axis=-1, keepdims=True)
            d = o - mu
            var = jnp.mean(d * d, axis=-1, keepdims=True)
            y = d * lax.rsqrt(var + LN_EPS) * gg_ref[:, hs] + gb_ref[:, hs]
            o_ref[0, pl.ds(r0, c), hs] = (jax.nn.silu(rg_ref[0, pl.ds(r0, c), hs].astype(F32)) * y).astype(o_ref.dtype)
        return 0

    lax.fori_loop(0, n_chunks, chunk, 0)


def retention_branch(hn, gn_g, gn_b, tr):
    b, t, _ = hn.shape
    w = BRANCH_WIDTH
    hd = RET_HEAD_DIM
    c = RET_CHUNK
    inv = ROPE_BASE ** (-jnp.arange(0, hd, 2, dtype=F32) / hd)
    ang = jnp.arange(t, dtype=F32)[:, None] * inv[None, :]
    cos2 = jnp.concatenate([jnp.cos(ang), jnp.cos(ang)], axis=1)
    sin2 = jnp.concatenate([-jnp.sin(ang), jnp.sin(ang)], axis=1)
    log_g = jnp.log(1.0 - 2.0 ** (-5.0 - jnp.arange(RET_HEADS, dtype=F32)))
    idx = jnp.arange(c, dtype=F32)
    diff = idx[:, None] - idx[None, :]
    decay_in = jnp.where(diff >= 0, jnp.exp(log_g[:, None, None] * jnp.maximum(diff, 0.0)), 0.0)
    ones = jnp.ones((RET_HEADS, c, hd), F32)
    q_dec = jnp.exp(log_g[:, None] * (idx + 1.0))[:, :, None] * ones
    k_dec = jnp.exp(log_g[:, None] * (c - 1.0 - idx))[:, :, None] * ones
    c_dec = jnp.exp(log_g * c)[:, None, None] * jnp.ones((RET_HEADS, hd, hd), F32)
    col = lambda cc: (lambda i, j: (i, j, cc // w))
    full3 = lambda i, j: (0, 0, 0)
    return pl.pallas_call(
        _ret_kernel,
        grid=(b, t // tr),
        in_specs=[
            pl.BlockSpec((1, tr, w), col(NAT_RQ)),
            pl.BlockSpec((1, tr, w), col(NAT_RK)),
            pl.BlockSpec((1, tr, w), col(NAT_RV)),
            pl.BlockSpec((1, tr, w), col(NAT_RG)),
            pl.BlockSpec((tr, hd), lambda i, j: (j, 0)),
            pl.BlockSpec((tr, hd), lambda i, j: (j, 0)),
            pl.BlockSpec((RET_HEADS, c, c), full3),
            pl.BlockSpec((RET_HEADS, c, hd), full3),
            pl.BlockSpec((RET_HEADS, c, hd), full3),
            pl.BlockSpec((RET_HEADS, hd, hd), full3),
            pl.BlockSpec((1, w), lambda i, j: (0, 0)),
            pl.BlockSpec((1, w), lambda i, j: (0, 0)),
        ],
        out_specs=pl.BlockSpec((1, tr, w), lambda i, j: (i, j, 0)),
        out_shape=jax.ShapeDtypeStruct((b, t, w), BF16),
        scratch_shapes=[pltpu.VMEM((RET_HEADS, hd, hd), F32)],
        compiler_params=_cparams(2),
        name="retention",
    )(hn, hn, hn, hn, cos2, sin2, decay_in, q_dec, k_dec, c_dec, gn_g.reshape(1, w), gn_b.reshape(1, w))


def _layer_norm(z, g, b):
    mu = jnp.mean(z, axis=-1, keepdims=True)
    d = z - mu
    var = jnp.mean(d * d, axis=-1, keepdims=True)
    return d * lax.rsqrt(var + LN_EPS) * g + b


def _merge_kernel(tiles_per_seq, x_ref, ya_ref, cb_ref, cc_ref, ch_ref, pc_ref, ph_ref, cw_ref, yc_ref, mg_ref,
                  wbr_ref, wout_ref, g_ref, b_ref, o_ref, ob_ref):
    first = (pl.program_id(0) % tiles_per_seq) == 0
    u = cc_ref[...].astype(F32) * ch_ref[...].astype(F32)
    prev = pc_ref[...].astype(F32) * ph_ref[...].astype(F32)
    prev = jnp.where(first, 0.0, prev)
    row = lax.broadcasted_iota(jnp.int32, u.shape, 0)
    p1 = prev[CONV_HALO - 1:CONV_HALO, :]
    p2 = prev[CONV_HALO - 2:CONV_HALO - 1, :]
    u1 = jnp.where(row == 0, p1, pltpu.roll(u, 1, 0))
    u2 = jnp.where(row == 0, p2, jnp.where(row == 1, p1, pltpu.roll(u, 2, 0)))
    cw = cw_ref[...]
    y_b = cb_ref[...].astype(F32) * (cw[0:1, :] * u2 + cw[1:2, :] * u1 + cw[2:3, :] * u)

    mix = None
    for c, y in enumerate((ya_ref[...], y_b.astype(BF16), yc_ref[...])):
        proj = jnp.dot(y, wbr_ref[c], preferred_element_type=F32)
        gate = jax.nn.sigmoid(mg_ref[:, c * D_MODEL:(c + 1) * D_MODEL].astype(F32))
        mix = gate * proj if mix is None else mix + gate * proj
    m = jnp.dot(mix.astype(BF16), wout_ref[...], preferred_element_type=F32)
    out = _layer_norm(ALPHA * x_ref[...] + m, g_ref[...], b_ref[...])
    o_ref[...] = out
    ob_ref[...] = out.astype(BF16)


def merge_branches(x, ya, yc, hn, conv_w, w_br, w_out, g, b, tm, t):
    m = x.shape[0]
    d = D_MODEL
    w = BRANCH_WIDTH
    row = lambda i: (i, 0)
    col = lambda c: (lambda i: (i, c // w))
    hal = lambda c: (lambda i: (jnp.maximum(i * (tm // CONV_HALO) - 1, 0), c // w))
    wpad = jnp.zeros((8, w), F32).at[:CONV_K].set(conv_w)
    return pl.pallas_call(
        functools.partial(_merge_kernel, t // tm),
        grid=(m // tm,),
        in_specs=[
            pl.BlockSpec((tm, d), row),
            pl.BlockSpec((tm, w), row),
            pl.BlockSpec((tm, w), col(NAT_CB)),
            pl.BlockSpec((tm, w), col(NAT_CC)),
            pl.BlockSpec((tm, w), col(NAT_CH)),
            pl.BlockSpec((CONV_HALO, w), hal(NAT_CC)),
            pl.BlockSpec((CONV_HALO, w), hal(NAT_CH)),
            pl.BlockSpec((8, w), lambda i: (0, 0)),
            pl.BlockSpec((tm, w), row),
            pl.BlockSpec((tm, 3 * d), row),
            pl.BlockSpec((3, w, d), lambda i: (0, 0, 0)),
            pl.BlockSpec((d, d), lambda i: (0, 0)),
            pl.BlockSpec((1, d), lambda i: (0, 0)),
            pl.BlockSpec((1, d), lambda i: (0, 0)),
        ],
        out_specs=[pl.BlockSpec((tm, d), row), pl.BlockSpec((tm, d), row)],
        out_shape=[jax.ShapeDtypeStruct((m, d), F32), jax.ShapeDtypeStruct((m, d), BF16)],
        compiler_params=_cparams(1),
        name="merge",
    )(x, ya, hn, hn, hn, hn, hn, wpad, yc, hn, w_br.astype(BF16), w_out.astype(BF16), g.reshape(1, d),
      b.reshape(1, d))


def _cast3_kernel(g_ref, u_ref, d_ref, og_ref, ou_ref, od_ref):
    og_ref[...] = g_ref[0].astype(og_ref.dtype)
    ou_ref[...] = u_ref[0].astype(ou_ref.dtype)
    od_ref[...] = d_ref[0].astype(od_ref.dtype)


def ffn_weights_bf16(w_gate, w_up, w_down, layer):
    _, e, d, ff = w_gate.shape
    rg, rd = d // CAST_STEPS, ff // CAST_STEPS
    up_spec = pl.BlockSpec((1, 1, rg, ff), lambda i, j: (layer, i, j, 0))
    return pl.pallas_call(
        _cast3_kernel,
        grid=(e, CAST_STEPS),
        in_specs=[up_spec, up_spec, pl.BlockSpec((1, 1, rd, d), lambda i, j: (layer, i, j, 0))],
        out_specs=[pl.BlockSpec((1, rg, ff), lambda i, j: (i, j, 0)), pl.BlockSpec((1, rg, ff), lambda i, j: (i, j, 0)),
                   pl.BlockSpec((1, rd, d), lambda i, j: (i, j, 0))],
        out_shape=[jax.ShapeDtypeStruct((e, d, ff), BF16), jax.ShapeDtypeStruct((e, d, ff), BF16),
                   jax.ShapeDtypeStruct((e, ff, d), BF16)],
        compiler_params=_cparams(2),
        name="cast_bf16",
    )(w_gate, w_up, w_down)


def _swiglu_chunks(x, wg_ref, wu_ref, wd_ref, tf):
    acc = None
    for c in range(wg_ref.shape[2] // tf):
        sl = slice(c * tf, (c + 1) * tf)
        gte = jnp.dot(x, wg_ref[0, :, sl], preferred_element_type=F32)
        up = jnp.dot(x, wu_ref[0, :, sl], preferred_element_type=F32)
        h = (jax.nn.silu(gte) * up).astype(BF16)
        part = jnp.dot(h, wd_ref[0, sl, :], preferred_element_type=F32)
        acc = part if acc is None else acc + part
    return acc


def _mlp_kernel(tf, te_ref, tv_ref, x_ref, wg_ref, wu_ref, wd_ref, o_ref):
    i = pl.program_id(0)

    @pl.when(tv_ref[i] > 0)
    def _():
        o_ref[...] = _swiglu_chunks(x_ref[...], wg_ref, wu_ref, wd_ref, tf).astype(o_ref.dtype)

    @pl.when(tv_ref[i] == 0)
    def _():
        o_ref[...] = jnp.zeros_like(o_ref)


def _resident(block_shape, index_map):
    return pl.BlockSpec(block_shape, index_map, pipeline_mode=pl.Buffered(1))


def grouped_swiglu(xs, w_gate, w_up, w_down, tile_expert, tile_valid, tm, tf):
    n, d = xs.shape
    ff = w_gate.shape[2]
    grid_spec = pltpu.PrefetchScalarGridSpec(
        num_scalar_prefetch=2,
        grid=(n // tm,),
        in_specs=[
            pl.BlockSpec((tm, d), lambda i, te, tv: (i, 0)),
            _resident((1, d, ff), lambda i, te, tv: (te[i], 0, 0)),
            _resident((1, d, ff), lambda i, te, tv: (te[i], 0, 0)),
            _resident((1, ff, d), lambda i, te, tv: (te[i], 0, 0)),
        ],
        out_specs=pl.BlockSpec((tm, d), lambda i, te, tv: (i, 0)),
    )
    return pl.pallas_call(
        functools.partial(_mlp_kernel, tf),
        grid_spec=grid_spec,
        out_shape=jax.ShapeDtypeStruct((n, d), BF16),
        compiler_params=_cparams(1, BIG_VMEM_LIMIT),
        name="grouped_swiglu",
    )(tile_expert, tile_valid, xs, w_gate, w_up, w_down)


def _add_ln_kernel(*refs):
    x_ref, sc_ref, *adds, g_ref, b_ref, o_ref, ob_ref = refs
    z = ALPHA * x_ref[...]
    for k, a in enumerate(adds):
        z = z + sc_ref[:, k:k + 1] * a[...].astype(F32)
    out = _layer_norm(z, g_ref[...], b_ref[...])
    o_ref[...] = out
    ob_ref[...] = out.astype(BF16)


def add_layer_norm(x, adds, scales, g, b, tm):
    m, d = x.shape
    row = lambda i: (i, 0)
    return pl.pallas_call(
        _add_ln_kernel,
        grid=(m // tm,),
        in_specs=([pl.BlockSpec((tm, d), row), pl.BlockSpec((tm, len(adds)), row)]
                  + [pl.BlockSpec((tm, d), row)] * len(adds) + [pl.BlockSpec((1, d), lambda i: (0, 0))] * 2),
        out_specs=[pl.BlockSpec((tm, d), row)] * 2,
        out_shape=[jax.ShapeDtypeStruct((m, d), F32), jax.ShapeDtypeStruct((m, d), BF16)],
        compiler_params=_cparams(1),
        name="add_layer_norm",
    )(x, scales, *adds, g.reshape(1, d), b.reshape(1, d))


def _router_kernel(x_ref, wh_ref, wl_ref, idx_ref, p_ref):
    x = x_ref[...]
    xh = x.astype(BF16)
    xl = (x - xh.astype(F32)).astype(BF16)
    wh = wh_ref[...]
    logits = (jnp.dot(xh, wh, preferred_element_type=F32) + jnp.dot(xl, wh, preferred_element_type=F32)
              + jnp.dot(xh, wl_ref[...], preferred_element_type=F32))
    lg = logits.T[0:N_EXPERTS, :]
    row = lax.broadcasted_iota(jnp.int32, lg.shape, 0).astype(F32)
    m1 = jnp.max(lg, axis=0, keepdims=True)
    i1 = jnp.min(jnp.where(lg == m1, row, float(N_EXPERTS)), axis=0, keepdims=True)
    lg2 = jnp.where(row == i1, -jnp.inf, lg)
    m2 = jnp.max(lg2, axis=0, keepdims=True)
    i2 = jnp.min(jnp.where(lg2 == m2, row, float(N_EXPERTS)), axis=0, keepdims=True)
    e2 = jnp.exp(m2 - m1)
    den = 1.0 + e2
    idx_ref[...] = jnp.concatenate([i1, i2], axis=0).astype(jnp.int32)
    p_ref[...] = jnp.concatenate([1.0 / den, e2 / den], axis=0)


def router_top2(x, w_router, tm):
    m, d = x.shape
    wp = jnp.zeros((d, 128), F32).at[:, :N_EXPERTS].set(w_router)
    wh = wp.astype(BF16)
    wl = (wp - wh.astype(F32)).astype(BF16)
    return pl.pallas_call(
        _router_kernel,
        grid=(m // tm,),
        in_specs=[pl.BlockSpec((tm, d), lambda i: (i, 0)), pl.BlockSpec((d, 128), lambda i: (0, 0)),
                  pl.BlockSpec((d, 128), lambda i: (0, 0))],
        out_specs=[pl.BlockSpec((TOP_K, tm), lambda i: (0, i))] * 2,
        out_shape=[jax.ShapeDtypeStruct((TOP_K, m), jnp.int32), jax.ShapeDtypeStruct((TOP_K, m), F32)],
        compiler_params=_cparams(1),
        name="router",
    )(x, wh, wl)


def _dense_ffn_kernel(tf, xb_ref, x_ref, wg_ref, wu_ref, wd_ref, g_ref, b_ref, o_ref, ob_ref):
    acc = _swiglu_chunks(xb_ref[...], wg_ref, wu_ref, wd_ref, tf)
    out = _layer_norm(ALPHA * x_ref[...] + acc, g_ref[...], b_ref[...])
    o_ref[...] = out
    ob_ref[...] = out.astype(BF16)


def dense_ffn(x, xb, w_gate, w_up, w_down, g, b):
    m, d = x.shape
    ff = w_gate.shape[2]
    tm = min(FFN_TM, m)
    row = lambda i: (i, 0)
    return pl.pallas_call(
        functools.partial(_dense_ffn_kernel, FFN_TF),
        grid=(m // tm,),
        in_specs=[
            pl.BlockSpec((tm, d), row),
            pl.BlockSpec((tm, d), row),
            _resident((1, d, ff), lambda i: (0, 0, 0)),
            _resident((1, d, ff), lambda i: (0, 0, 0)),
            _resident((1, ff, d), lambda i: (0, 0, 0)),
            pl.BlockSpec((1, d), lambda i: (0, 0)),
            pl.BlockSpec((1, d), lambda i: (0, 0)),
        ],
        out_specs=[pl.BlockSpec((tm, d), row)] * 2,
        out_shape=[jax.ShapeDtypeStruct((m, d), F32), jax.ShapeDtypeStruct((m, d), BF16)],
        compiler_params=_cparams(1, BIG_VMEM_LIMIT),
        name="dense_ffn",
    )(xb, x, w_gate, w_up, w_down, g.reshape(1, d), b.reshape(1, d))


def _slot_kernel(tile, e_ref, u_ref, pos_ref, cnt_ref, carry_ref, start_ref):
    pas = pl.program_id(0)

    @pl.when(pl.program_id(1) == 0)
    def _():
        @pl.when(pas == 0)
        def _():
            start_ref[...] = jnp.zeros_like(start_ref)

        @pl.when(pas == 1)
        def _():
            counts = carry_ref[...]
            padded = jnp.ceil(counts / tile) * tile
            acc = jnp.zeros((1, 1), F32)
            rows = []
            for e in range(N_EXPERTS):
                rows.append(acc)
                acc = acc + padded[e:e + 1, :]
            start_ref[...] = jnp.concatenate(rows, axis=0)
            cnt_ref[...] = jnp.broadcast_to(counts, cnt_ref.shape).astype(jnp.int32)

        carry_ref[...] = jnp.zeros_like(carry_ref)

    rows, width = e_ref.shape
    expert = lax.broadcasted_iota(jnp.int32, (N_EXPERTS, width), 0)
    carry = carry_ref[...]
    start = start_ref[...]
    for r in range(rows):
        hot = e_ref[r:r + 1, :] == expert
        prefix = jnp.dot(hot.astype(BF16), u_ref[...], preferred_element_type=F32) + carry
        slot = jnp.sum(jnp.where(hot, prefix + start, 0.0), axis=0, keepdims=True) - 1.0
        pos_ref[r:r + 1, :] = slot.astype(jnp.int32)
        carry = prefix[:, width - 1:width]
    carry_ref[...] = carry


def routing_slots(e_flat, tile):
    n = e_flat.shape[0]
    width = RANK_WIDTH
    rows = 8
    e2 = e_flat.reshape(n // width, width)
    upper = (jnp.arange(width)[:, None] <= jnp.arange(width)[None, :]).astype(BF16)
    pos, cnt = pl.pallas_call(
        functools.partial(_slot_kernel, float(tile)),
        grid=(2, n // (rows * width)),
        in_specs=[pl.BlockSpec((rows, width), lambda p, i: (i, 0)),
                  pl.BlockSpec((width, width), lambda p, i: (0, 0))],
        out_specs=[pl.BlockSpec((rows, width), lambda p, i: (i * p, 0)),
                   pl.BlockSpec((N_EXPERTS, 128), lambda p, i: (0, 0))],
        out_shape=[jax.ShapeDtypeStruct((n // width, width), jnp.int32),
                   jax.ShapeDtypeStruct((N_EXPERTS, 128), jnp.int32)],
        scratch_shapes=[pltpu.VMEM((N_EXPERTS, 1), F32), pltpu.VMEM((N_EXPERTS, 1), F32)],
        compiler_params=_cparams(2),
        name="routing_slots",
    )(e2, upper)
    return pos.reshape(n), cnt[:, 0]


def moe_ffn(x, xb, w_router, w_gate, w_up, w_down, g, b):
    m = x.shape[0]
    tm = FFN_TM
    top_i, top_p = router_top2(x, w_router, min(ROW_TILE, m))
    e_flat = top_i.reshape(-1)
    pos, counts = routing_slots(e_flat, tm)
    padded = ((counts + tm - 1) // tm) * tm
    ends = jnp.cumsum(padded)
    starts = ends - padded
    n_slots = TOP_K * m + N_EXPERTS * tm
    n_tiles = n_slots // tm
    tile_start = jnp.arange(n_tiles, dtype=jnp.int32) * tm
    tile_expert = jnp.minimum(jnp.sum(tile_start[:, None] >= ends[None, :], axis=1), N_EXPERTS - 1).astype(jnp.int32)
    tile_valid = (tile_start < ends[-1]).astype(jnp.int32)
    n_flat = TOP_K * m
    order = jnp.sort(e_flat * n_flat + jnp.arange(n_flat, dtype=jnp.int32)) % n_flat
    first_entry = jnp.cumsum(counts) - counts
    r_in_group = tile_start - starts[tile_expert]
    slot_rank = r_in_group[:, None] + jnp.arange(tm, dtype=jnp.int32)[None, :]
    entry = jnp.clip(first_entry[tile_expert][:, None] + slot_rank, 0, n_flat - 1)
    src = jnp.where(slot_rank < counts[tile_expert][:, None], order[entry.reshape(-1)].reshape(n_tiles, tm) % m, 0)
    src = src.reshape(-1)
    xs = jnp.take(xb, src, axis=0)
    ys = grouped_swiglu(xs, w_gate, w_up, w_down,
                        tile_expert, tile_valid, tm, FFN_TF)
    y0 = jnp.take(ys, pos[:m], axis=0)
    y1 = jnp.take(ys, pos[m:], axis=0)
    return add_layer_norm(x, [y0, y1], top_p.T, g, b, min(ROW_TILE, m))


def _pack_in_weights(w):
    d = w.shape[0]
    o = 0
    q = w[:, o:o + 512]; o += 512
    kv = w[:, o:o + 768].reshape(d, 3, 2, NSA_KV_GROUPS, HEAD_DIM); o += 768
    ng = w[:, o:o + 24]; o += 24
    rest = w[:, o:o + 7 * 512]; o += 7 * 512
    mg = w[:, o:]
    kpack = jnp.concatenate([kv[:, 1, 0], kv[:, 2, 0]], axis=-1).reshape(d, 2 * NSA_KV_GROUPS * HEAD_DIM)
    kc = kv[:, 0, 0].reshape(d, NSA_KV_GROUPS * HEAD_DIM)
    vc = kv[:, 0, 1].reshape(d, NSA_KV_GROUPS * HEAD_DIM)
    w_nat = jnp.concatenate([mg, rest, kpack, kc, vc], axis=1).astype(BF16)
    vs = kv[:, 1, 1].reshape(d, NSA_KV_GROUPS * HEAD_DIM)
    vw = kv[:, 2, 1].reshape(d, NSA_KV_GROUPS * HEAD_DIM)
    ngp = jnp.pad(ng.reshape(d, NSA_KV_GROUPS, NSA_GROUP * 3), ((0, 0), (0, 0), (0, 4))).reshape(d, 32)
    w_t = jnp.concatenate([q, vs, vw, ngp], axis=1).T.astype(BF16)
    return w_nat, w_t


def token_mixer_ln(x, xb, w_in, cmp_pos, cmp_w1, cmp_b1, cmp_w2, cmp_b2, conv_w, gn_g, gn_b, w_br, w_out, ln_g, ln_b):
    b, t, d = x.shape
    m = b * t
    w_nat, w_t = _pack_in_weights(w_in)
    x2 = x.reshape(m, d)
    hn, proj_t = input_projection(xb, w_nat, w_t, min(PROJ_TM, t), PROJ_TN)
    hn = hn.reshape(b, t, NAT_COLS)
    nch = t // CMP_STRIDE
    raw = hn[:, :, NAT_KC:NAT_KC + 2 * NSA_KV_GROUPS * HEAD_DIM].reshape(b, t, 2, NSA_KV_GROUPS, HEAD_DIM)
    chunks = raw.transpose(2, 0, 3, 1, 4).reshape(2, b, NSA_KV_GROUPS, nch, CMP_STRIDE * HEAD_DIM)
    cn, ct = nsa_compress_kv(chunks, cmp_pos, cmp_w1, cmp_b1, cmp_w2, cmp_b2)
    kc = cn[0].astype(BF16)
    vct = ct[1].astype(BF16)
    vt4 = proj_t[:, TR_VS:TR_NG, :].astype(BF16).reshape(b, 4, HEAD_DIM, t)
    ones_pad = jnp.zeros((b, 4, VT_ROWS - HEAD_DIM, t), BF16).at[:, :, 0].set(1.0)
    vt_pack = jnp.concatenate([vt4, ones_pad], axis=2).reshape(b, 4 * VT_ROWS, t)
    y_a = nsa_attention(proj_t, kc, vct, hn, vt_pack)
    y_c = retention_branch(hn, gn_g, gn_b, min(ROW_TILE, t))
    return merge_branches(x2, y_a.reshape(m, -1), y_c.reshape(m, -1), hn.reshape(m, NAT_COLS), conv_w,
                          w_br, w_out, ln_g, ln_b, min(MERGE_TM, t), t)


def kernel(x, w_in, cmp_pos, cmp_w1, cmp_b1, cmp_w2, cmp_b2, conv_w, ret_gn_g, ret_gn_b, w_br, w_out,
           ln1_g, ln1_b, ln2_g, ln2_b, ffn_gate, ffn_up, ffn_down, moe_router, moe_gate, moe_up, moe_down):
    b, t, d = x.shape
    xb = x.astype(BF16)
    for l in range(DEPTH):
        x2, x2b = token_mixer_ln(x, xb, w_in[l], cmp_pos[l], cmp_w1[l], cmp_b1[l], cmp_w2[l], cmp_b2[l], conv_w[l],
                                 ret_gn_g[l], ret_gn_b[l], w_br[l], w_out[l], ln1_g[l], ln1_b[l])
        if l % 2 == 0:
            wg, wu, wd = ffn_weights_bf16(ffn_gate[:, None], ffn_up[:, None], ffn_down[:, None], l // 2)
            x2, x2b = dense_ffn(x2, x2b, wg, wu, wd, ln2_g[l], ln2_b[l])
        else:
            wg, wu, wd = ffn_weights_bf16(moe_gate, moe_up, moe_down, l // 2)
            x2, x2b = moe_ffn(x2, x2b, moe_router[l // 2], wg, wu, wd, ln2_g[l], ln2_b[l])
        x, xb = x2.reshape(b, t, d), x2b.reshape(b, t, d)
    return x
```

```python
import functools

import jax
import jax.numpy as jnp
from jax import lax
from jax.experimental import pallas as pl
from jax.experimental.pallas import tpu as pltpu

F32 = jnp.float32
BF16 = jnp.bfloat16

D_MODEL = 1024
DEPTH = 4
BRANCH_WIDTH = 512
HEAD_DIM = 64
NSA_HEADS = 8
NSA_KV_GROUPS = 2
NSA_GROUP = 4
CMP_LEN = 32
CMP_STRIDE = 16
SLC_BLOCK = 64
SLC_TOPN = 16
WINDOW = 512
CMP_HIDDEN = 256
Q_BLOCK = 128
CONV_K = 3
CONV_HALO = 16
RET_HEADS = 4
RET_HEAD_DIM = 128
RET_CHUNK = 256
ROPE_BASE = 10000.0
D_FF = 3584
N_EXPERTS = 8
TOP_K = 2
LN_EPS = 1e-5
ALPHA = (2.0 * DEPTH) ** 0.25

NAT_MG = 0
NAT_CB = 3072
NAT_CC = 3584
NAT_CH = 4096
NAT_RQ = 4608
NAT_RK = 5120
NAT_RV = 5632
NAT_RG = 6144
NAT_KP = 6656
NAT_KC = 6912
NAT_VC = 7040
NAT_COLS = 7168
TR_Q = 0
TR_VS = 512
TR_VW = 640
TR_NG = 768
TR_ROWS = 800

NEG = -1e30
SEL_TK = 1024
LOG2E = 1.4426950408889634
NSA_QSUB = 2
CMP_CLASS_ROWS = 128
PG_PAD = 8
PICKED = -3e38
VT_ROWS = 80
RANK_WIDTH = 1024
ROW_TILE = 1024
MERGE_TM = 512
PROJ_TM = 2048
PROJ_TN = 1024
FFN_TM = 512
FFN_TF = 1792
CAST_STEPS = 4
VMEM_LIMIT = 48 * 1024 * 1024
BIG_VMEM_LIMIT = 56 * 1024 * 1024


def _cparams(n_axes, vmem=VMEM_LIMIT):
    return pltpu.CompilerParams(dimension_semantics=("arbitrary",) * n_axes, vmem_limit_bytes=vmem)


def _proj_kernel(x_ref, w_ref, wt_ref, o_ref, ot_ref):
    @pl.when(pl.program_id(1) == 0)
    def _():
        ot_ref[0] = lax.dot_general(wt_ref[...], x_ref[...], (((1,), (1,)), ((), ())), preferred_element_type=F32)

    o_ref[...] = jnp.dot(x_ref[...], w_ref[...], preferred_element_type=F32).astype(o_ref.dtype)


def input_projection(x, w_nat, w_t, tm, tn):
    b, t, k = x.shape
    m = b * t
    n = w_nat.shape[1]
    r = w_t.shape[0]
    tps = t // tm
    return pl.pallas_call(
        _proj_kernel,
        grid=(m // tm, n // tn),
        in_specs=[pl.BlockSpec((tm, k), lambda i, j: (i, 0)), pl.BlockSpec((k, tn), lambda i, j: (0, j)),
                  pl.BlockSpec((r, k), lambda i, j: (0, 0))],
        out_specs=[pl.BlockSpec((tm, tn), lambda i, j: (i, j)),
                   pl.BlockSpec((1, r, tm), lambda i, j: (i // tps, 0, i % tps))],
        out_shape=[jax.ShapeDtypeStruct((m, n), BF16), jax.ShapeDtypeStruct((b, r, t), F32)],
        compiler_params=_cparams(2, BIG_VMEM_LIMIT),
        name="input_projection",
    )(x.reshape(m, k), w_nat, w_t)


def _cmp_kernel(c_ref, pos_ref, w1_ref, b1_ref, w2_ref, w2t_ref, b2_ref, b2t_ref, on_ref, ot_ref):
    half = CMP_STRIDE * HEAD_DIM
    c = c_ref[0, 0, 0].astype(BF16)
    w1 = w1_ref[0]
    a = jnp.dot(c, w1[:half], preferred_element_type=F32)
    bm = jnp.dot(c, w1[half:], preferred_element_type=F32)
    nch = a.shape[0]
    bm = pltpu.roll(bm, nch - 1, 0)
    posb = jnp.dot(pos_ref[0].astype(BF16), w1, preferred_element_type=F32)[0:1]
    hid = jax.nn.gelu(a + bm + posb + b1_ref[0])
    hb = hid.astype(BF16)
    on_ref[0, 0, 0] = jnp.dot(hb, w2_ref[0], preferred_element_type=F32) + b2_ref[0]
    ot_ref[0, 0, 0] = lax.dot_general(w2t_ref[0], hb, (((1,), (1,)), ((), ())),
                                      preferred_element_type=F32) + b2t_ref[0]


def nsa_compress_kv(chunks, pos, w1, b1, w2, b2):
    _, b, g, nch, cw = chunks.shape
    pos_flat = jnp.broadcast_to(pos.reshape(2, 1, CMP_LEN * HEAD_DIM), (2, 8, CMP_LEN * HEAD_DIM))
    w1b = w1.astype(BF16)
    w2b = w2.astype(BF16)
    w2t = jnp.swapaxes(w2, 1, 2).astype(BF16)
    b1r = b1.reshape(2, 1, CMP_HIDDEN)
    b2r = b2.reshape(2, 1, HEAD_DIM)
    b2t = b2.reshape(2, HEAD_DIM, 1)
    sel = lambda kv, i, j: (kv, 0, 0)
    return pl.pallas_call(
        _cmp_kernel,
        grid=(2, b, g),
        in_specs=[
            pl.BlockSpec((1, 1, 1, nch, cw), lambda kv, i, j: (kv, i, j, 0, 0)),
            pl.BlockSpec((1, 8, CMP_LEN * HEAD_DIM), sel),
            pl.BlockSpec((1, CMP_LEN * HEAD_DIM, CMP_HIDDEN), sel),
            pl.BlockSpec((1, 1, CMP_HIDDEN), sel),
            pl.BlockSpec((1, CMP_HIDDEN, HEAD_DIM), sel),
            pl.BlockSpec((1, HEAD_DIM, CMP_HIDDEN), sel),
            pl.BlockSpec((1, 1, HEAD_DIM), sel),
            pl.BlockSpec((1, HEAD_DIM, 1), sel),
        ],
        out_specs=[
            pl.BlockSpec((1, 1, 1, nch, HEAD_DIM), lambda kv, i, j: (kv, i, j, 0, 0)),
            pl.BlockSpec((1, 1, 1, HEAD_DIM, nch), lambda kv, i, j: (kv, i, j, 0, 0)),
        ],
        out_shape=[
            jax.ShapeDtypeStruct((2, b, g, nch, HEAD_DIM), F32),
            jax.ShapeDtypeStruct((2, b, g, HEAD_DIM, nch), F32),
        ],
        compiler_params=_cparams(3),
        name="nsa_compress",
    )(chunks, pos_flat, w1b, b1r, w2b, w2t, b2r, b2t)


def _accumulate(s, mt, vt, m, acc):
    m_new = jnp.maximum(m, mt)
    alpha = jnp.exp2(m - m_new)
    p = jnp.exp2(s - m_new).astype(BF16)
    return m_new, alpha * acc + jnp.dot(vt, p, preferred_element_type=F32)


def _nsa_kernel(qt_ref, ng_ref, kc_ref, vct_ref, kp_ref, vst_ref, vwt_ref, e_ref, tri_ref, wpat_ref,
                cpat_ref, o_ref, bias_ref, w_ref, sbuf_ref, oc_ref, pg_ref):
    nq = Q_BLOCK
    nsub = qt_ref.shape[2] // nq
    ncol = NSA_GROUP * nq
    n_sel = bias_ref.shape[1]
    ncp = kc_ref.shape[2]
    kw = 2 * HEAD_DIM
    bpt = SEL_TK // SLC_BLOCK
    n_tiles_total = kp_ref.shape[1] // SEL_TK
    qb_first = pl.program_id(2) * nsub
    wlen = WINDOW + nq

    def query_setup(u):
        qb = qb_first + u
        q0 = pl.multiple_of(qb * nq, nq)
        qt = qt_ref[0, :, u * nq:(u + 1) * nq] * (HEAD_DIM ** -0.5 * LOG2E)
        qs = jnp.concatenate([qt[r * HEAD_DIM:(r + 1) * HEAD_DIM, :] for r in range(NSA_GROUP)], axis=1)
        zq = jnp.zeros_like(qs)
        tq1 = q0 + lax.broadcasted_iota(jnp.int32, (1, nq), 1)
        return dict(
            q0=q0, q_c=qs.astype(BF16),
            q_slc=jnp.concatenate([qs, zq], axis=0).astype(BF16),
            q_win=jnp.concatenate([zq, qs], axis=0).astype(BF16),
            tq=q0 + (lax.broadcasted_iota(jnp.int32, (1, ncol), 1) & (nq - 1)),
            coff=pl.multiple_of(ncp - qb * (nq // CMP_STRIDE), nq // CMP_STRIDE),
            cur=jnp.right_shift(tq1, 6).astype(F32),
            first_diag=(2 * qb).astype(F32))

    qs_ = [query_setup(u) for u in range(nsub)]

    crow = min(CMP_CLASS_ROWS, ncp)
    qb_per_class = crow // (nq // CMP_STRIDE)
    assert qb_per_class % nsub == 0
    cls = qb_first // qb_per_class
    ratio = SLC_BLOCK // CMP_STRIDE
    for c in range(ncp // crow):
        nc_c = crow * (c + 1)
        ns_c = nc_c * CMP_STRIDE // SLC_BLOCK

        @pl.when(cls == c)
        def _(nc_c=nc_c, ns_c=ns_c):
            j_io = lax.broadcasted_iota(jnp.int32, (ns_c, nq), 0).astype(F32)
            scores = []
            for u, q in enumerate(qs_):
                pg_ref[u, 0:PG_PAD, :] = jnp.zeros((PG_PAD, nq), F32)
                s_c = (jnp.dot(kc_ref[0, 0, 0:nc_c, :], q["q_c"], preferred_element_type=F32)
                       + jnp.concatenate([cpat_ref[pl.ds(q["coff"], nc_c), :]] * NSA_GROUP, axis=1))
                m_c = jnp.max(s_c, axis=0, keepdims=True)
                e_c = jnp.exp2(s_c - m_c)
                l_c = jnp.sum(e_c, axis=0, keepdims=True)
                p_c = e_c * jnp.where(q["tq"] >= CMP_LEN - 1, 1.0 / l_c, 0.0)
                oc_ref[u] = jnp.dot(vct_ref[0, 0, :, 0:nc_c], p_c.astype(BF16), preferred_element_type=F32)

                p_grp = p_c[:, 0:nq]
                for r in range(1, NSA_GROUP):
                    p_grp = p_grp + p_c[:, r * nq:(r + 1) * nq]
                pg_ref[u, PG_PAD:PG_PAD + nc_c, :] = p_grp
                taps = [pg_ref[u, pl.ds(PG_PAD - 1 + k, ns_c, stride=ratio), :] for k in range(ratio + 1)]
                score = taps[0] + 2.0 * (taps[1] + taps[2] + taps[3]) + taps[4]
                score = jnp.where(j_io <= q["cur"], score, -1.0)
                score = jnp.where(j_io == 0.0, PICKED, score)
                score = jnp.where(j_io == q["cur"], PICKED, score)
                score = jnp.where(j_io == q["cur"] - 1.0, PICKED, score)
                scores.append(score)
            for _ in range(SLC_TOPN - 3):
                for u in range(nsub):
                    mx = jnp.max(scores[u], axis=0, keepdims=True)
                    first = jnp.min(jnp.where(scores[u] == mx, j_io, float(ns_c)), axis=0, keepdims=True)
                    scores[u] = jnp.where(j_io == first, PICKED, scores[u])
            for u, q in enumerate(qs_):
                bias = jnp.where(j_io < q["first_diag"], jnp.where(scores[u] == PICKED, 0.0, NEG), NEG)
                bias_ref[u, 0:ns_c, :] = bias.astype(BF16)
                if ns_c < n_sel:
                    bias_ref[u, ns_c:, :] = jnp.full((n_sel - ns_c, nq), NEG, BF16)

    def prepare(u):
        q0, q_slc, q_win = qs_[u]["q0"], qs_[u]["q_slc"], qs_[u]["q_win"]
        o_c = oc_ref[u]

        wstart = pl.multiple_of(jnp.maximum(q0 - WINDOW, 0), nq)
        woff = pl.multiple_of(jnp.maximum(WINDOW - q0, 0), nq)
        s_w = jnp.dot(kp_ref[0, pl.ds(wstart, wlen), :], q_win, preferred_element_type=F32)
        s_w = s_w + jnp.concatenate([wpat_ref[pl.ds(woff, wlen), :]] * NSA_GROUP, axis=1)
        p_w = jnp.exp2(s_w - jnp.max(s_w, axis=0, keepdims=True)).astype(BF16)
        acc_w = jnp.dot(vwt_ref[0, :, pl.ds(wstart, wlen)], p_w, preferred_element_type=F32)
        o_w = acc_w[0:HEAD_DIM] / acc_w[HEAD_DIM:HEAD_DIM + 1]

        for slot in range(2):
            w_ref[slot, u, 0:kw, :] = q_slc
            w_ref[slot, u, kw:, :] = jnp.zeros((w_ref.shape[2] - kw, ncol), BF16)
        return q0, q_slc, o_c, o_w

    prepared = [prepare(u) for u in range(nsub)]

    def tile_start(j):
        jc = jnp.minimum(j, n_tiles_total - 1)
        return jc, pl.multiple_of(jc * SEL_TK, SEL_TK)

    def produce(j, slot):
        jc, k0 = tile_start(j)
        lhs = jnp.concatenate([kp_ref[0, pl.ds(k0, SEL_TK), :], e_ref[...]], axis=1)
        mts = []
        for u in range(nsub):
            b16 = bias_ref[u, pl.ds(pl.multiple_of(jc * bpt, bpt), bpt), :]
            w_ref[slot, u, kw:kw + bpt, :] = jnp.concatenate([b16] * NSA_GROUP, axis=1)
            s = jnp.dot(lhs, w_ref[slot, u], preferred_element_type=F32)
            sbuf_ref[slot, u] = s
            mts.append(jnp.max(s, axis=0, keepdims=True))
        return tuple(mts)

    def consume(j, slot, mts, state):
        _, k0 = tile_start(j)
        vt = vst_ref[0, :, pl.ds(k0, SEL_TK)]
        return tuple(_accumulate(sbuf_ref[slot, u], mts[u], vt, *state[u]) for u in range(nsub))

    def pair(i, carry):
        mt_a, state = carry
        mt_b = produce(2 * i + 1, 1)
        state = consume(2 * i, 0, mt_a, state)
        mt_a = produce(2 * i + 2, 0)
        state = consume(2 * i + 1, 1, mt_b, state)
        return mt_a, state

    def finish(state):
        for u in range(nsub):
            q0, q_slc, o_c, o_w = prepared[u]
            s_d = (jnp.dot(kp_ref[0, pl.ds(q0, nq), :], q_slc, preferred_element_type=F32)
                   + tri_ref[...])
            _, acc_s = _accumulate(s_d, jnp.max(s_d, axis=0, keepdims=True), vst_ref[0, :, pl.ds(q0, nq)],
                                   *state[u])
            o_s = acc_s[0:HEAD_DIM] / acc_s[HEAD_DIM:HEAD_DIM + 1]
            gts = jax.nn.sigmoid(ng_ref[0, :, u * nq:(u + 1) * nq])
            rows = []
            for r in range(NSA_GROUP):
                sl = slice(r * nq, (r + 1) * nq)
                rows.append(gts[3 * r:3 * r + 1, :] * o_c[:, sl] + gts[3 * r + 1:3 * r + 2, :] * o_s[:, sl]
                            + gts[3 * r + 2:3 * r + 3, :] * o_w[:, sl])
            y_t = jnp.concatenate(rows, axis=0)
            o_ref[0, u * nq:(u + 1) * nq, :] = y_t.T.astype(o_ref.dtype)

    n_main = jnp.maximum(((qb_first + nsub - 1) * nq + (SEL_TK - 1)) // SEL_TK, 1)
    n_pairs = (n_main + 1) // 2
    state = tuple((jnp.full((1, ncol), NEG, F32), jnp.zeros((vst_ref.shape[1], ncol), F32)) for _ in range(nsub))
    mt_a, state = lax.fori_loop(0, n_pairs - 1, pair, (produce(0, 0), state))
    tail = 2 * (n_pairs - 1)
    odd = (n_main & 1) == 1

    @pl.when(odd)
    def _():
        finish(consume(tail, 0, mt_a, state))

    @pl.when(jnp.logical_not(odd))
    def _():
        mt_b = produce(tail + 1, 1)
        finish(consume(tail + 1, 1, mt_b, consume(tail, 0, mt_a, state)))


def _nsa_constants(ncp):
    nq = Q_BLOCK
    ql = jnp.arange(nq)[None, :]
    r = jnp.arange(SEL_TK)[:, None]
    e = (r // SLC_BLOCK == jnp.arange(2 * HEAD_DIM)[None, :]).astype(BF16)
    rd = jnp.arange(nq)[:, None]
    tri = jnp.tile(jnp.where(rd <= ql, 0.0, NEG).astype(F32), (1, NSA_GROUP))
    rw = jnp.arange(2 * WINDOW + nq)[:, None]
    wpat = jnp.where((rw > ql) & (rw <= ql + WINDOW), 0.0, NEG).astype(F32)
    rc = jnp.arange(2 * ncp)[:, None] - ncp
    cpat = jnp.where(CMP_STRIDE * rc + (CMP_LEN - 1) <= ql, 0.0, NEG).astype(F32)
    return e, tri, wpat, cpat


def nsa_attention(proj_t, kc, vct, hn, vt_pack):
    b, _, t = proj_t.shape
    ncp = kc.shape[2]
    n_sel = t // SLC_BLOCK
    gq = NSA_GROUP * HEAD_DIM
    qstep = NSA_QSUB * Q_BLOCK
    e, tri, wpat, cpat = _nsa_constants(ncp)
    const2 = lambda i, g, q: (0, 0)
    return pl.pallas_call(
        _nsa_kernel,
        grid=(b, NSA_KV_GROUPS, t // qstep),
        in_specs=[
            pl.BlockSpec((1, gq, qstep), lambda i, g, q: (i, g, q)),
            pl.BlockSpec((1, 16, qstep), lambda i, g, q: (i, TR_NG // 16 + g, q)),
            pl.BlockSpec((1, 1, ncp, HEAD_DIM), lambda i, g, q: (i, g, 0, 0)),
            pl.BlockSpec((1, 1, HEAD_DIM, ncp), lambda i, g, q: (i, g, 0, 0)),
            pl.BlockSpec((1, t, 2 * HEAD_DIM), lambda i, g, q: (i, 0, NAT_KP // (2 * HEAD_DIM) + g)),
            pl.BlockSpec((1, VT_ROWS, t), lambda i, g, q: (i, g, 0)),
            pl.BlockSpec((1, VT_ROWS, t), lambda i, g, q: (i, NSA_KV_GROUPS + g, 0)),
            pl.BlockSpec(e.shape, const2),
            pl.BlockSpec(tri.shape, const2),
            pl.BlockSpec(wpat.shape, const2),
            pl.BlockSpec(cpat.shape, const2),
        ],
        out_specs=pl.BlockSpec((1, qstep, gq), lambda i, g, q: (i, q, g)),
        out_shape=jax.ShapeDtypeStruct((b, t, NSA_HEADS * HEAD_DIM), BF16),
        scratch_shapes=[pltpu.VMEM((NSA_QSUB, n_sel, Q_BLOCK), BF16),
                        pltpu.VMEM((2, NSA_QSUB, 4 * HEAD_DIM, NSA_GROUP * Q_BLOCK), BF16),
                        pltpu.VMEM((2, NSA_QSUB, SEL_TK, NSA_GROUP * Q_BLOCK), F32),
                        pltpu.VMEM((NSA_QSUB, HEAD_DIM, NSA_GROUP * Q_BLOCK), F32),
                        pltpu.VMEM((NSA_QSUB, PG_PAD + ncp, Q_BLOCK), F32)],
        compiler_params=_cparams(3),
        name="nsa_attention",
    )(proj_t, proj_t, kc, vct, hn, vt_pack, vt_pack, e, tri, wpat, cpat)


def _ret_kernel(rq_ref, rk_ref, rv_ref, rg_ref, cos_ref, sin_ref, dec_ref, qd_ref, kd_ref, cd_ref,
                gg_ref, gb_ref, o_ref, st_ref):
    c = RET_CHUNK
    hd = RET_HEAD_DIM

    @pl.when(pl.program_id(1) == 0)
    def _():
        st_ref[...] = jnp.zeros_like(st_ref)

    n_chunks = rq_ref.shape[1] // c

    def chunk(ci, _):
        r0 = pl.multiple_of(ci * c, c)
        cos = cos_ref[pl.ds(r0, c), :]
        sin = sin_ref[pl.ds(r0, c), :]
        for h in range(RET_HEADS):
            hs = slice(h * hd, (h + 1) * hd)
            q = rq_ref[0, pl.ds(r0, c), hs].astype(F32)
            k = rk_ref[0, pl.ds(r0, c), hs].astype(F32)
            v = rv_ref[0, pl.ds(r0, c), hs]
            q = q * cos + pltpu.roll(q, hd // 2, 1) * sin
            k = (k * cos + pltpu.roll(k, hd // 2, 1) * sin) * (hd ** -0.5)
            qb = q.astype(BF16)
            kb = k.astype(BF16)
            vb = v.astype(BF16)
            inner = lax.dot_general(qb, kb, (((1,), (1,)), ((), ())), preferred_element_type=F32) * dec_ref[h]
            st = st_ref[h]
            o = (jnp.dot(inner.astype(BF16), vb, preferred_element_type=F32)
                 + jnp.dot(qb, st.astype(BF16), preferred_element_type=F32) * qd_ref[h])
            kdt = (k * kd_ref[h]).T.astype(BF16)
            st_ref[h] = st * cd_ref[h] + jnp.dot(kdt, vb, preferred_element_type=F32)
            mu = jnp.mean(o, axis=-1, keepdims=True)
            d = o - mu
            var = jnp.mean(d * d, axis=-1, keepdims=True)
            y = d * lax.rsqrt(var + LN_EPS) * gg_ref[:, hs] + gb_ref[:, hs]
            o_ref[0, pl.ds(r0, c), hs] = (jax.nn.silu(rg_ref[0, pl.ds(r0, c), hs].astype(F32)) * y).astype(o_ref.dtype)
        return 0

    lax.fori_loop(0, n_chunks, chunk, 0)


def retention_branch(hn, gn_g, gn_b, tr):
    b, t, _ = hn.shape
    w = BRANCH_WIDTH
    hd = RET_HEAD_DIM
    c = RET_CHUNK
    inv = ROPE_BASE ** (-jnp.arange(0, hd, 2, dtype=F32) / hd)
    ang = jnp.arange(t, dtype=F32)[:, None] * inv[None, :]
    cos2 = jnp.concatenate([jnp.cos(ang), jnp.cos(ang)], axis=1)
    sin2 = jnp.concatenate([-jnp.sin(ang), jnp.sin(ang)], axis=1)
    log_g = jnp.log(1.0 - 2.0 ** (-5.0 - jnp.arange(RET_HEADS, dtype=F32)))
    idx = jnp.arange(c, dtype=F32)
    diff = idx[:, None] - idx[None, :]
    decay_in = jnp.where(diff >= 0, jnp.exp(log_g[:, None, None] * jnp.maximum(diff, 0.0)), 0.0)
    ones = jnp.ones((RET_HEADS, c, hd), F32)
    q_dec = jnp.exp(log_g[:, None] * (idx + 1.0))[:, :, None] * ones
    k_dec = jnp.exp(log_g[:, None] * (c - 1.0 - idx))[:, :, None] * ones
    c_dec = jnp.exp(log_g * c)[:, None, None] * jnp.ones((RET_HEADS, hd, hd), F32)
    col = lambda cc: (lambda i, j: (i, j, cc // w))
    full3 = lambda i, j: (0, 0, 0)
    return pl.pallas_call(
        _ret_kernel,
        grid=(b, t // tr),
        in_specs=[
            pl.BlockSpec((1, tr, w), col(NAT_RQ)),
            pl.BlockSpec((1, tr, w), col(NAT_RK)),
            pl.BlockSpec((1, tr, w), col(NAT_RV)),
            pl.BlockSpec((1, tr, w), col(NAT_RG)),
            pl.BlockSpec((tr, hd), lambda i, j: (j, 0)),
            pl.BlockSpec((tr, hd), lambda i, j: (j, 0)),
            pl.BlockSpec((RET_HEADS, c, c), full3),
            pl.BlockSpec((RET_HEADS, c, hd), full3),
            pl.BlockSpec((RET_HEADS, c, hd), full3),
            pl.BlockSpec((RET_HEADS, hd, hd), full3),
            pl.BlockSpec((1, w), lambda i, j: (0, 0)),
            pl.BlockSpec((1, w), lambda i, j: (0, 0)),
        ],
        out_specs=pl.BlockSpec((1, tr, w), lambda i, j: (i, j, 0)),
        out_shape=jax.ShapeDtypeStruct((b, t, w), BF16),
        scratch_shapes=[pltpu.VMEM((RET_HEADS, hd, hd), F32)],
        compiler_params=_cparams(2),
        name="retention",
    )(hn, hn, hn, hn, cos2, sin2, decay_in, q_dec, k_dec, c_dec, gn_g.reshape(1, w), gn_b.reshape(1, w))


def _layer_norm(z, g, b):
    mu = jnp.mean(z, axis=-1, keepdims=True)
    d = z - mu
    var = jnp.mean(d * d, axis=-1, keepdims=True)
    return d * lax.rsqrt(var + LN_EPS) * g + b


def _merge_kernel(tiles_per_seq, x_ref, ya_ref, cb_ref, cc_ref, ch_ref, pc_ref, ph_ref, cw_ref, yc_ref, mg_ref,
                  wbr_ref, wout_ref, g_ref, b_ref, o_ref, ob_ref):
    first = (pl.program_id(0) % tiles_per_seq) == 0
    u = cc_ref[...].astype(F32) * ch_ref[...].astype(F32)
    prev = pc_ref[...].astype(F32) * ph_ref[...].astype(F32)
    prev = jnp.where(first, 0.0, prev)
    row = lax.broadcasted_iota(jnp.int32, u.shape, 0)
    p1 = prev[CONV_HALO - 1:CONV_HALO, :]
    p2 = prev[CONV_HALO - 2:CONV_HALO - 1, :]
    u1 = jnp.where(row == 0, p1, pltpu.roll(u, 1, 0))
    u2 = jnp.where(row == 0, p2, jnp.where(row == 1, p1, pltpu.roll(u, 2, 0)))
    cw = cw_ref[...]
    y_b = cb_ref[...].astype(F32) * (cw[0:1, :] * u2 + cw[1:2, :] * u1 + cw[2:3, :] * u)

    mix = None
    for c, y in enumerate((ya_ref[...], y_b.astype(BF16), yc_ref[...])):
        proj = jnp.dot(y, wbr_ref[c], preferred_element_type=F32)
        gate = jax.nn.sigmoid(mg_ref[:, c * D_MODEL:(c + 1) * D_MODEL].astype(F32))
        mix = gate * proj if mix is None else mix + gate * proj
    m = jnp.dot(mix.astype(BF16), wout_ref[...], preferred_element_type=F32)
    out = _layer_norm(ALPHA * x_ref[...] + m, g_ref[...], b_ref[...])
    o_ref[...] = out
    ob_ref[...] = out.astype(BF16)


def merge_branches(x, ya, yc, hn, conv_w, w_br, w_out, g, b, tm, t):
    m = x.shape[0]
    d = D_MODEL
    w = BRANCH_WIDTH
    row = lambda i: (i, 0)
    col = lambda c: (lambda i: (i, c // w))
    hal = lambda c: (lambda i: (jnp.maximum(i * (tm // CONV_HALO) - 1, 0), c // w))
    wpad = jnp.zeros((8, w), F32).at[:CONV_K].set(conv_w)
    return pl.pallas_call(
        functools.partial(_merge_kernel, t // tm),
        grid=(m // tm,),
        in_specs=[
            pl.BlockSpec((tm, d), row),
            pl.BlockSpec((tm, w), row),
            pl.BlockSpec((tm, w), col(NAT_CB)),
            pl.BlockSpec((tm, w), col(NAT_CC)),
            pl.BlockSpec((tm, w), col(NAT_CH)),
            pl.BlockSpec((CONV_HALO, w), hal(NAT_CC)),
            pl.BlockSpec((CONV_HALO, w), hal(NAT_CH)),
            pl.BlockSpec((8, w), lambda i: (0, 0)),
            pl.BlockSpec((tm, w), row),
            pl.BlockSpec((tm, 3 * d), row),
            pl.BlockSpec((3, w, d), lambda i: (0, 0, 0)),
            pl.BlockSpec((d, d), lambda i: (0, 0)),
            pl.BlockSpec((1, d), lambda i: (0, 0)),
            pl.BlockSpec((1, d), lambda i: (0, 0)),
        ],
        out_specs=[pl.BlockSpec((tm, d), row), pl.BlockSpec((tm, d), row)],
        out_shape=[jax.ShapeDtypeStruct((m, d), F32), jax.ShapeDtypeStruct((m, d), BF16)],
        compiler_params=_cparams(1),
        name="merge",
    )(x, ya, hn, hn, hn, hn, hn, wpad, yc, hn, w_br.astype(BF16), w_out.astype(BF16), g.reshape(1, d),
      b.reshape(1, d))


def _cast3_kernel(g_ref, u_ref, d_ref, og_ref, ou_ref, od_ref):
    og_ref[...] = g_ref[0].astype(og_ref.dtype)
    ou_ref[...] = u_ref[0].astype(ou_ref.dtype)
    od_ref[...] = d_ref[0].astype(od_ref.dtype)


def ffn_weights_bf16(w_gate, w_up, w_down, layer):
    _, e, d, ff = w_gate.shape
    rg, rd = d // CAST_STEPS, ff // CAST_STEPS
    up_spec = pl.BlockSpec((1, 1, rg, ff), lambda i, j: (layer, i, j, 0))
    return pl.pallas_call(
        _cast3_kernel,
        grid=(e, CAST_STEPS),
        in_specs=[up_spec, up_spec, pl.BlockSpec((1, 1, rd, d), lambda i, j: (layer, i, j, 0))],
        out_specs=[pl.BlockSpec((1, rg, ff), lambda i, j: (i, j, 0)), pl.BlockSpec((1, rg, ff), lambda i, j: (i, j, 0)),
                   pl.BlockSpec((1, rd, d), lambda i, j: (i, j, 0))],
        out_shape=[jax.ShapeDtypeStruct((e, d, ff), BF16), jax.ShapeDtypeStruct((e, d, ff), BF16),
                   jax.ShapeDtypeStruct((e, ff, d), BF16)],
        compiler_params=_cparams(2),
        name="cast_bf16",
    )(w_gate, w_up, w_down)


def _swiglu_chunks(x, wg_ref, wu_ref, wd_ref, tf):
    acc = None
    for c in range(wg_ref.shape[2] // tf):
        sl = slice(c * tf, (c + 1) * tf)
        gte = jnp.dot(x, wg_ref[0, :, sl], preferred_element_type=F32)
        up = jnp.dot(x, wu_ref[0, :, sl], preferred_element_type=F32)
        h = (jax.nn.silu(gte) * up).astype(BF16)
        part = jnp.dot(h, wd_ref[0, sl, :], preferred_element_type=F32)
        acc = part if acc is None else acc + part
    return acc


def _mlp_kernel(tf, te_ref, tv_ref, x_ref, wg_ref, wu_ref, wd_ref, o_ref):
    i = pl.program_id(0)

    @pl.when(tv_ref[i] > 0)
    def _():
        o_ref[...] = _swiglu_chunks(x_ref[...], wg_ref, wu_ref, wd_ref, tf).astype(o_ref.dtype)

    @pl.when(tv_ref[i] == 0)
    def _():
        o_ref[...] = jnp.zeros_like(o_ref)


def _resident(block_shape, index_map):
    return pl.BlockSpec(block_shape, index_map, pipeline_mode=pl.Buffered(1))


def grouped_swiglu(xs, w_gate, w_up, w_down, tile_expert, tile_valid, tm, tf):
    n, d = xs.shape
    ff = w_gate.shape[2]
    grid_spec = pltpu.PrefetchScalarGridSpec(
        num_scalar_prefetch=2,
        grid=(n // tm,),
        in_specs=[
            pl.BlockSpec((tm, d), lambda i, te, tv: (i, 0)),
            _resident((1, d, ff), lambda i, te, tv: (te[i], 0, 0)),
            _resident((1, d, ff), lambda i, te, tv: (te[i], 0, 0)),
            _resident((1, ff, d), lambda i, te, tv: (te[i], 0, 0)),
        ],
        out_specs=pl.BlockSpec((tm, d), lambda i, te, tv: (i, 0)),
    )
    return pl.pallas_call(
        functools.partial(_mlp_kernel, tf),
        grid_spec=grid_spec,
        out_shape=jax.ShapeDtypeStruct((n, d), BF16),
        compiler_params=_cparams(1, BIG_VMEM_LIMIT),
        name="grouped_swiglu",
    )(tile_expert, tile_valid, xs, w_gate, w_up, w_down)


def _add_ln_kernel(*refs):
    x_ref, sc_ref, *adds, g_ref, b_ref, o_ref, ob_ref = refs
    z = ALPHA * x_ref[...]
    for k, a in enumerate(adds):
        z = z + sc_ref[:, k:k + 1] * a[...].astype(F32)
    out = _layer_norm(z, g_ref[...], b_ref[...])
    o_ref[...] = out
    ob_ref[...] = out.astype(BF16)


def add_layer_norm(x, adds, scales, g, b, tm):
    m, d = x.shape
    row = lambda i: (i, 0)
    return pl.pallas_call(
        _add_ln_kernel,
        grid=(m // tm,),
        in_specs=([pl.BlockSpec((tm, d), row), pl.BlockSpec((tm, len(adds)), row)]
                  + [pl.BlockSpec((tm, d), row)] * len(adds) + [pl.BlockSpec((1, d), lambda i: (0, 0))] * 2),
        out_specs=[pl.BlockSpec((tm, d), row)] * 2,
        out_shape=[jax.ShapeDtypeStruct((m, d), F32), jax.ShapeDtypeStruct((m, d), BF16)],
        compiler_params=_cparams(1),
        name="add_layer_norm",
    )(x, scales, *adds, g.reshape(1, d), b.reshape(1, d))


def _router_kernel(x_ref, wh_ref, wl_ref, idx_ref, p_ref):
    x = x_ref[...]
    xh = x.astype(BF16)
    xl = (x - xh.astype(F32)).astype(BF16)
    wh = wh_ref[...]
    logits = (jnp.dot(xh, wh, preferred_element_type=F32) + jnp.dot(xl, wh, preferred_element_type=F32)
              + jnp.dot(xh, wl_ref[...], preferred_element_type=F32))
    lg = logits.T[0:N_EXPERTS, :]
    row = lax.broadcasted_iota(jnp.int32, lg.shape, 0).astype(F32)
    m1 = jnp.max(lg, axis=0, keepdims=True)
    i1 = jnp.min(jnp.where(lg == m1, row, float(N_EXPERTS)), axis=0, keepdims=True)
    lg2 = jnp.where(row == i1, -jnp.inf, lg)
    m2 = jnp.max(lg2, axis=0, keepdims=True)
    i2 = jnp.min(jnp.where(lg2 == m2, row, float(N_EXPERTS)), axis=0, keepdims=True)
    e2 = jnp.exp(m2 - m1)
    den = 1.0 + e2
    idx_ref[...] = jnp.concatenate([i1, i2], axis=0).astype(jnp.int32)
    p_ref[...] = jnp.concatenate([1.0 / den, e2 / den], axis=0)


def router_top2(x, w_router, tm):
    m, d = x.shape
    wp = jnp.zeros((d, 128), F32).at[:, :N_EXPERTS].set(w_router)
    wh = wp.astype(BF16)
    wl = (wp - wh.astype(F32)).astype(BF16)
    return pl.pallas_call(
        _router_kernel,
        grid=(m // tm,),
        in_specs=[pl.BlockSpec((tm, d), lambda i: (i, 0)), pl.BlockSpec((d, 128), lambda i: (0, 0)),
                  pl.BlockSpec((d, 128), lambda i: (0, 0))],
        out_specs=[pl.BlockSpec((TOP_K, tm), lambda i: (0, i))] * 2,
        out_shape=[jax.ShapeDtypeStruct((TOP_K, m), jnp.int32), jax.ShapeDtypeStruct((TOP_K, m), F32)],
        compiler_params=_cparams(1),
        name="router",
    )(x, wh, wl)


def _dense_ffn_kernel(tf, xb_ref, x_ref, wg_ref, wu_ref, wd_ref, g_ref, b_ref, o_ref, ob_ref):
    acc = _swiglu_chunks(xb_ref[...], wg_ref, wu_ref, wd_ref, tf)
    out = _layer_norm(ALPHA * x_ref[...] + acc, g_ref[...], b_ref[...])
    o_ref[...] = out
    ob_ref[...] = out.astype(BF16)


def dense_ffn(x, xb, w_gate, w_up, w_down, g, b):
    m, d = x.shape
    ff = w_gate.shape[2]
    tm = min(FFN_TM, m)
    row = lambda i: (i, 0)
    return pl.pallas_call(
        functools.partial(_dense_ffn_kernel, FFN_TF),
        grid=(m // tm,),
        in_specs=[
            pl.BlockSpec((tm, d), row),
            pl.BlockSpec((tm, d), row),
            _resident((1, d, ff), lambda i: (0, 0, 0)),
            _resident((1, d, ff), lambda i: (0, 0, 0)),
            _resident((1, ff, d), lambda i: (0, 0, 0)),
            pl.BlockSpec((1, d), lambda i: (0, 0)),
            pl.BlockSpec((1, d), lambda i: (0, 0)),
        ],
        out_specs=[pl.BlockSpec((tm, d), row)] * 2,
        out_shape=[jax.ShapeDtypeStruct((m, d), F32), jax.ShapeDtypeStruct((m, d), BF16)],
        compiler_params=_cparams(1, BIG_VMEM_LIMIT),
        name="dense_ffn",
    )(xb, x, w_gate, w_up, w_down, g.reshape(1, d), b.reshape(1, d))


def _slot_kernel(tile, e_ref, u_ref, pos_ref, cnt_ref, carry_ref, start_ref):
    pas = pl.program_id(0)

    @pl.when(pl.program_id(1) == 0)
    def _():
        @pl.when(pas == 0)
        def _():
            start_ref[...] = jnp.zeros_like(start_ref)

        @pl.when(pas == 1)
        def _():
            counts = carry_ref[...]
            padded = jnp.ceil(counts / tile) * tile
            acc = jnp.zeros((1, 1), F32)
            rows = []
            for e in range(N_EXPERTS):
                rows.append(acc)
                acc = acc + padded[e:e + 1, :]
            start_ref[...] = jnp.concatenate(rows, axis=0)
            cnt_ref[...] = jnp.broadcast_to(counts, cnt_ref.shape).astype(jnp.int32)

        carry_ref[...] = jnp.zeros_like(carry_ref)

    rows, width = e_ref.shape
    expert = lax.broadcasted_iota(jnp.int32, (N_EXPERTS, width), 0)
    carry = carry_ref[...]
    start = start_ref[...]
    for r in range(rows):
        hot = e_ref[r:r + 1, :] == expert
        prefix = jnp.dot(hot.astype(BF16), u_ref[...], preferred_element_type=F32) + carry
        slot = jnp.sum(jnp.where(hot, prefix + start, 0.0), axis=0, keepdims=True) - 1.0
        pos_ref[r:r + 1, :] = slot.astype(jnp.int32)
        carry = prefix[:, width - 1:width]
    carry_ref[...] = carry


def routing_slots(e_flat, tile):
    n = e_flat.shape[0]
    width = RANK_WIDTH
    rows = 8
    e2 = e_flat.reshape(n // width, width)
    upper = (jnp.arange(width)[:, None] <= jnp.arange(width)[None, :]).astype(BF16)
    pos, cnt = pl.pallas_call(
        functools.partial(_slot_kernel, float(tile)),
        grid=(2, n // (rows * width)),
        in_specs=[pl.BlockSpec((rows, width), lambda p, i: (i, 0)),
                  pl.BlockSpec((width, width), lambda p, i: (0, 0))],
        out_specs=[pl.BlockSpec((rows, width), lambda p, i: (i * p, 0)),
                   pl.BlockSpec((N_EXPERTS, 128), lambda p, i: (0, 0))],
        out_shape=[jax.ShapeDtypeStruct((n // width, width), jnp.int32),
                   jax.ShapeDtypeStruct((N_EXPERTS, 128), jnp.int32)],
        scratch_shapes=[pltpu.VMEM((N_EXPERTS, 1), F32), pltpu.VMEM((N_EXPERTS, 1), F32)],
        compiler_params=_cparams(2),
        name="routing_slots",
    )(e2, upper)
    return pos.reshape(n), cnt[:, 0]


def moe_ffn(x, xb, w_router, w_gate, w_up, w_down, g, b):
    m = x.shape[0]
    tm = FFN_TM
    top_i, top_p = router_top2(x, w_router, min(ROW_TILE, m))
    e_flat = top_i.reshape(-1)
    pos, counts = routing_slots(e_flat, tm)
    padded = ((counts + tm - 1) // tm) * tm
    ends = jnp.cumsum(padded)
    starts = ends - padded
    n_slots = TOP_K * m + N_EXPERTS * tm
    n_tiles = n_slots // tm
    tile_start = jnp.arange(n_tiles, dtype=jnp.int32) * tm
    tile_expert = jnp.minimum(jnp.sum(tile_start[:, None] >= ends[None, :], axis=1), N_EXPERTS - 1).astype(jnp.int32)
    tile_valid = (tile_start < ends[-1]).astype(jnp.int32)
    n_flat = TOP_K * m
    order = jnp.sort(e_flat * n_flat + jnp.arange(n_flat, dtype=jnp.int32)) % n_flat
    first_entry = jnp.cumsum(counts) - counts
    r_in_group = tile_start - starts[tile_expert]
    slot_rank = r_in_group[:, None] + jnp.arange(tm, dtype=jnp.int32)[None, :]
    entry = jnp.clip(first_entry[tile_expert][:, None] + slot_rank, 0, n_flat - 1)
    src = jnp.where(slot_rank < counts[tile_expert][:, None], order[entry.reshape(-1)].reshape(n_tiles, tm) % m, 0)
    src = src.reshape(-1)
    xs = jnp.take(xb, src, axis=0)
    ys = grouped_swiglu(xs, w_gate, w_up, w_down,
                        tile_expert, tile_valid, tm, FFN_TF)
    y0 = jnp.take(ys, pos[:m], axis=0)
    y1 = jnp.take(ys, pos[m:], axis=0)
    return add_layer_norm(x, [y0, y1], top_p.T, g, b, min(ROW_TILE, m))


def _pack_in_weights(w):
    d = w.shape[0]
    o = 0
    q = w[:, o:o + 512]; o += 512
    kv = w[:, o:o + 768].reshape(d, 3, 2, NSA_KV_GROUPS, HEAD_DIM); o += 768
    ng = w[:, o:o + 24]; o += 24
    rest = w[:, o:o + 7 * 512]; o += 7 * 512
    mg = w[:, o:]
    kpack = jnp.concatenate([kv[:, 1, 0], kv[:, 2, 0]], axis=-1).reshape(d, 2 * NSA_KV_GROUPS * HEAD_DIM)
    kc = kv[:, 0, 0].reshape(d, NSA_KV_GROUPS * HEAD_DIM)
    vc = kv[:, 0, 1].reshape(d, NSA_KV_GROUPS * HEAD_DIM)
    w_nat = jnp.concatenate([mg, rest, kpack, kc, vc], axis=1).astype(BF16)
    vs = kv[:, 1, 1].reshape(d, NSA_KV_GROUPS * HEAD_DIM)
    vw = kv[:, 2, 1].reshape(d, NSA_KV_GROUPS * HEAD_DIM)
    ngp = jnp.pad(ng.reshape(d, NSA_KV_GROUPS, NSA_GROUP * 3), ((0, 0), (0, 0), (0, 4))).reshape(d, 32)
    w_t = jnp.concatenate([q, vs, vw, ngp], axis=1).T.astype(BF16)
    return w_nat, w_t


def token_mixer_ln(x, xb, w_in, cmp_pos, cmp_w1, cmp_b1, cmp_w2, cmp_b2, conv_w, gn_g, gn_b, w_br, w_out, ln_g, ln_b):
    b, t, d = x.shape
    m = b * t
    w_nat, w_t = _pack_in_weights(w_in)
    x2 = x.reshape(m, d)
    hn, proj_t = input_projection(xb, w_nat, w_t, min(PROJ_TM, t), PROJ_TN)
    hn = hn.reshape(b, t, NAT_COLS)
    nch = t // CMP_STRIDE
    raw = hn[:, :, NAT_KC:NAT_KC + 2 * NSA_KV_GROUPS * HEAD_DIM].reshape(b, t, 2, NSA_KV_GROUPS, HEAD_DIM)
    chunks = raw.transpose(2, 0, 3, 1, 4).reshape(2, b, NSA_KV_GROUPS, nch, CMP_STRIDE * HEAD_DIM)
    cn, ct = nsa_compress_kv(chunks, cmp_pos, cmp_w1, cmp_b1, cmp_w2, cmp_b2)
    kc = cn[0].astype(BF16)
    vct = ct[1].astype(BF16)
    vt4 = proj_t[:, TR_VS:TR_NG, :].astype(BF16).reshape(b, 4, HEAD_DIM, t)
    ones_pad = jnp.zeros((b, 4, VT_ROWS - HEAD_DIM, t), BF16).at[:, :, 0].set(1.0)
    vt_pack = jnp.concatenate([vt4, ones_pad], axis=2).reshape(b, 4 * VT_ROWS, t)
    y_a = nsa_attention(proj_t, kc, vct, hn, vt_pack)
    y_c = retention_branch(hn, gn_g, gn_b, min(ROW_TILE, t))
    return merge_branches(x2, y_a.reshape(m, -1), y_c.reshape(m, -1), hn.reshape(m, NAT_COLS), conv_w,
                          w_br, w_out, ln_g, ln_b, min(MERGE_TM, t), t)


def kernel(x, w_in, cmp_pos, cmp_w1, cmp_b1, cmp_w2, cmp_b2, conv_w, ret_gn_g, ret_gn_b, w_br, w_out,
           ln1_g, ln1_b, ln2_g, ln2_b, ffn_gate, ffn_up, ffn_down, moe_router, moe_gate, moe_up, moe_down):
    b, t, d = x.shape
    xb = x.astype(BF16)
    for l in range(DEPTH):
        x2, x2b = token_mixer_ln(x, xb, w_in[l], cmp_pos[l], cmp_w1[l], cmp_b1[l], cmp_w2[l], cmp_b2[l], conv_w[l],
                                 ret_gn_g[l], ret_gn_b[l], w_br[l], w_out[l], ln1_g[l], ln1_b[l])
        if l % 2 == 0:
            wg, wu, wd = ffn_weights_bf16(ffn_gate[:, None], ffn_up[:, None], ffn_down[:, None], l // 2)
            x2, x2b = dense_ffn(x2, x2b, wg, wu, wd, ln2_g[l], ln2_b[l])
        else:
            wg, wu, wd = ffn_weights_bf16(moe_gate, moe_up, moe_down, l // 2)
            x2, x2b = moe_ffn(x2, x2b, moe_router[l // 2], wg, wu, wd, ln2_g[l], ln2_b[l])
        x, xb = x2.reshape(b, t, d), x2b.reshape(b, t, d)
    return x
```
